```python
import math
import jax
import jax.numpy as jnp
from jax import lax
import numpy as np

D_MODEL = 2048
BATCH = 4
SEQ = 2048
DEPTH = 1

GRID_W = 64
CTX_LEN = 256
EPS = 1e-6
N_MOD = 6

S5_WIDTH = D_MODEL // 2
S5_GROUP = 16
S5_GROUPS = S5_WIDTH // S5_GROUP
S5_STATE = 64
S5_DT_MIN = 0.001
S5_DT_MAX = 0.1

GLA_HEADS = 4
GLA_KEY_WIDTH = D_MODEL // 4
GLA_VAL_WIDTH = D_MODEL // 2
GLA_DK = GLA_KEY_WIDTH // GLA_HEADS
GLA_DV = GLA_VAL_WIDTH // GLA_HEADS
GLA_GATE_RANK = 16
GLA_GATE_NORM = 16.0
GLA_CHUNK = 64

IN_SPLITS = (S5_WIDTH, GLA_KEY_WIDTH, GLA_KEY_WIDTH, GLA_VAL_WIDTH, GLA_VAL_WIDTH, 2 * GLA_GATE_RANK, D_MODEL, D_MODEL)
IN_WIDTH = sum(IN_SPLITS)

N_EXPERTS = 64
TOP_K = 8
N_GROUPS = 8
TOPK_GROUPS = 4
D_EXPERT = D_MODEL // 4
D_SHARED = D_MODEL // 4
ROUTED_SCALE = 2.5
EXPERT_BLOCK = 128

kernel_name = 'hybrid_s5_gla_moe_diffusion_block'


def _flip(t):
    return jnp.flip(t, axis=1)


def _ident(t):
    return t


def rmsnorm(x, w):
    xf = x.astype(jnp.float32)
    y = xf * lax.rsqrt(jnp.mean(xf * xf, axis=-1, keepdims=True) + EPS)
    return (y * w.astype(jnp.float32)).astype(x.dtype)


def modulate(x, w, shift, scale):
    return rmsnorm(x, w) * (1.0 + scale) + shift


def s5_discretize(lam_re, lam_im, log_dt, b_re, b_im):
    lam = lax.complex(lam_re.astype(jnp.float32), lam_im.astype(jnp.float32))
    dt = jnp.exp(log_dt.astype(jnp.float32))[:, None]
    lam_bar = jnp.exp(lam * dt)
    b = lax.complex(b_re.astype(jnp.float32), b_im.astype(jnp.float32))
    b_bar = ((lam_bar - 1.0) / lam)[..., None] * b
    return lam_bar, b_bar


def _linear_recurrence(e1, e2):
    a1, b1 = e1
    a2, b2 = e2
    return a1 * a2, a2 * b1 + b2


def s5_scan(u, lam_bar, b_bar, h0):
    bu = jnp.einsum('blgh,gph->blgp', u.astype(jnp.complex64), b_bar)
    if h0 is not None:
        bu = bu.at[:, 0].add(lam_bar * h0)
    a = jnp.broadcast_to(lam_bar, bu.shape)
    _, hs = lax.associative_scan(_linear_recurrence, (a, bu), axis=1)
    return hs


def s5_readout(hs, c_mat):
    return jnp.einsum('blgp,ghp->blgh', hs, c_mat).real


def s5_bidir(u, uc, lam_re, lam_im, log_dt, b_re, b_im, c_re, c_im, d_skip, ctx_out):
    def grp(t):
        return t.astype(jnp.float32).reshape(t.shape[0], t.shape[1], S5_GROUPS, S5_GROUP)
    ug, ucg = grp(u), grp(uc)
    dsk = d_skip.astype(jnp.float32).reshape(S5_GROUPS, S5_GROUP)
    y = dsk * ug
    yc = dsk * ucg if ctx_out else None
    for dr in range(2):
        flip = _flip if dr else _ident
        lam_bar, b_bar = s5_discretize(lam_re[dr], lam_im[dr], log_dt[dr], b_re[dr], b_im[dr])
        c_mat = lax.complex(c_re[dr].astype(jnp.float32), c_im[dr].astype(jnp.float32))
        hs_c = s5_scan(flip(ucg), lam_bar, b_bar, None)
        hs = s5_scan(flip(ug), lam_bar, b_bar, hs_c[:, -1])
        y = y + flip(s5_readout(hs, c_mat))
        if ctx_out:
            yc = yc + flip(s5_readout(hs_c, c_mat))
    y = y.reshape(u.shape).astype(u.dtype)
    if ctx_out:
        yc = yc.reshape(uc.shape).astype(uc.dtype)
    return y, yc


def s5_glu(y, w):
    y = jax.nn.gelu(y)
    return y * jax.nn.sigmoid(y @ w)


def gla_states(k, v, b, s0):
    b_last = b[:, :, :, -1:, :]
    ds = jnp.einsum('bhnck,bhncv->bhnkv', k * jnp.exp(b_last - b), v)
    decay = jnp.exp(b_last[:, :, :, 0, :])

    def step(s, xs):
        d, dsn = xs
        return d[..., None] * s + dsn, s

    s_final, starts = lax.scan(step, s0, (jnp.moveaxis(decay, 2, 0), jnp.moveaxis(ds, 2, 0)))
    return jnp.moveaxis(starts, 0, 2), s_final


def gla_readout(q, k, v, b, starts, strict):
    qb = q * jnp.exp(b)
    kb = k * jnp.exp(-b)
    scores = jnp.einsum('bhnik,bhnjk->bhnij', qb, kb)
    mask = jnp.tril(jnp.ones((GLA_CHUNK, GLA_CHUNK), bool), k=-1 if strict else 0)
    intra = jnp.einsum('bhnij,bhnjv->bhniv', jnp.where(mask, scores, 0.0), v)
    inter = jnp.einsum('bhnik,bhnkv->bhniv', qb, starts)
    return intra + inter


def gla_bidir(q, k, v, gkd, qc, kc, vc, gkdc, gk_up, gk_b, ctx_out):
    f32 = jnp.float32

    def chunks(t, d, flip):
        t = flip(t.astype(f32))
        b_, l_ = t.shape[:2]
        t = t.reshape(b_, l_, GLA_HEADS, d).transpose(0, 2, 1, 3)
        return t.reshape(b_, GLA_HEADS, l_ // GLA_CHUNK, GLA_CHUNK, d)

    def unchunk(t, flip):
        b_, h_, n_, c_, d_ = t.shape
        t = t.reshape(b_, h_, n_ * c_, d_).transpose(0, 2, 1, 3)
        return flip(t.reshape(b_, n_ * c_, h_ * d_))

    def log_decay(gd, dr):
        z = gd[..., dr * GLA_GATE_RANK:(dr + 1) * GLA_GATE_RANK] @ gk_up[dr] + gk_b[dr]
        return jax.nn.log_sigmoid(z.astype(f32)) / GLA_GATE_NORM

    q = q * GLA_DK ** -0.5
    qc = qc * GLA_DK ** -0.5
    s0 = jnp.zeros((kc.shape[0], GLA_HEADS, GLA_DK, GLA_DV), f32)
    o, oc = 0.0, 0.0
    for dr in range(2):
        flip = _flip if dr else _ident
        strict = dr == 1
        kcn, vcn = chunks(kc, GLA_DK, flip), chunks(vc, GLA_DV, flip)
        bcn = jnp.cumsum(chunks(log_decay(gkdc, dr), GLA_DK, flip), axis=3)
        starts_c, s_ctx = gla_states(kcn, vcn, bcn, s0)
        kn, vn = chunks(k, GLA_DK, flip), chunks(v, GLA_DV, flip)
        bn = jnp.cumsum(chunks(log_decay(gkd, dr), GLA_DK, flip), axis=3)
        starts, _ = gla_states(kn, vn, bn, s_ctx)
        o = o + unchunk(gla_readout(chunks(q, GLA_DK, flip), kn, vn, bn, starts, strict), flip)
        if ctx_out:
            oc = oc + unchunk(gla_readout(chunks(qc, GLA_DK, flip), kcn, vcn, bcn, starts_c, strict), flip)
    return o, (oc if ctx_out else None)


def gla_output(o, go, norm_w):
    b_, l_ = o.shape[:2]
    oh = rmsnorm(o.reshape(b_, l_, GLA_HEADS, GLA_DV), norm_w).reshape(b_, l_, GLA_VAL_WIDTH)
    return oh.astype(go.dtype) * jax.nn.silu(go)


def token_mixer(h, hc, w_in, lam_re, lam_im, log_dt, b_re, b_im, c_re, c_im, d_skip,
                glu_w, gk_up, gk_b, gla_norm_w, w_s5_proj, w_gla_proj, w_out, ctx_out):
    cuts = [int(v) for v in np.cumsum(IN_SPLITS)[:-1]]
    u, q, k, v, go, gkd, gs, gg = jnp.split(h @ w_in, cuts, axis=-1)
    uc, qc, kc, vc, goc, gkdc, gsc, ggc = jnp.split(hc @ w_in, cuts, axis=-1)
    ys, ysc = s5_bidir(u, uc, lam_re, lam_im, log_dt, b_re, b_im, c_re, c_im, d_skip, ctx_out)
    yg, ygc = gla_bidir(q, k, v, gkd, qc, kc, vc, gkdc, gk_up, gk_b, ctx_out)

    def merge(ys_, yg_, go_, gs_, gg_):
        a = s5_glu(ys_, glu_w) @ w_s5_proj
        b = gla_output(yg_, go_, gla_norm_w) @ w_gla_proj
        return (jax.nn.sigmoid(gs_) * a + jax.nn.sigmoid(gg_) * b) @ w_out

    y = merge(ys, yg, go, gs, gg)
    yc = merge(ysc, ygc, goc, gsc, ggc) if ctx_out else None
    return y, yc


def moe(h, router_w, router_bias, w_gate, w_up, w_down, s_gate, s_up, s_down):
    shp = h.shape
    t = h.reshape(-1, D_MODEL)
    n_tok = t.shape[0]
    scores = jax.nn.sigmoid((t @ router_w).astype(jnp.float32))
    choice = scores + router_bias.astype(jnp.float32)
    grp = choice.reshape(n_tok, N_GROUPS, N_EXPERTS // N_GROUPS)
    grp_score = lax.top_k(grp, 2)[0].sum(-1)
    _, gidx = lax.top_k(grp_score, TOPK_GROUPS)
    gmask = jax.nn.one_hot(gidx, N_GROUPS).sum(axis=1) > 0
    emask = jnp.repeat(gmask, N_EXPERTS // N_GROUPS, axis=1)
    _, eidx = lax.top_k(jnp.where(emask, choice, -jnp.inf), TOP_K)
    wts = jnp.take_along_axis(scores, eidx, axis=1)
    wts = wts / jnp.sum(wts, axis=-1, keepdims=True) * ROUTED_SCALE

    n_asg = n_tok * TOP_K
    n_blocks = (n_asg + N_EXPERTS * (EXPERT_BLOCK - 1) + EXPERT_BLOCK - 1) // EXPERT_BLOCK
    flat_e = eidx.reshape(-1)
    order = jnp.argsort(flat_e)
    sorted_e = flat_e[order]
    sizes = jnp.bincount(flat_e, length=N_EXPERTS)
    padded = (sizes + EXPERT_BLOCK - 1) // EXPERT_BLOCK * EXPERT_BLOCK
    pad_ends = jnp.cumsum(padded)
    rank = jnp.arange(n_asg) - (jnp.cumsum(sizes) - sizes)[sorted_e]
    dest = (pad_ends - padded)[sorted_e] + rank
    row_buf = jnp.full((n_blocks * EXPERT_BLOCK,), n_tok, jnp.int32).at[dest].set((order // TOP_K).astype(jnp.int32))
    w_buf = jnp.zeros((n_blocks * EXPERT_BLOCK,), jnp.float32).at[dest].set(wts.reshape(-1)[order])
    block_e = jnp.minimum(jnp.searchsorted(pad_ends, jnp.arange(n_blocks) * EXPERT_BLOCK, side='right'), N_EXPERTS - 1)
    t_pad = jnp.concatenate([t, jnp.zeros((1, D_MODEL), t.dtype)], axis=0)

    def expert_block(args):
        rows, e = args
        xb = t_pad[rows]
        hid = jax.nn.silu(xb @ w_gate[e]) * (xb @ w_up[e])
        return hid @ w_down[e]

    y_buf = lax.map(expert_block, (row_buf.reshape(n_blocks, EXPERT_BLOCK), block_e))
    y_buf = y_buf.reshape(n_blocks * EXPERT_BLOCK, D_MODEL) * w_buf[:, None].astype(t.dtype)
    routed = jax.ops.segment_sum(y_buf, row_buf, num_segments=n_tok + 1)[:n_tok]
    shared = (jax.nn.silu(t @ s_gate) * (t @ s_up)) @ s_down
    return (routed + shared).reshape(shp)


def setup_inputs(seed: int = 0) -> dict:
    key = jax.random.key(seed)
    ks = iter(list(jax.random.split(key, 40)))
    f32 = jnp.float32

    def nrm(shape, scale):
        return jax.random.normal(next(ks), shape, f32) * scale

    L, D, E, F = DEPTH, D_MODEL, N_EXPERTS, D_EXPERT
    G, P, H = S5_GROUPS, S5_STATE, S5_GROUP
    lam_im_init = jnp.pi * jnp.arange(P, dtype=f32)
    return {
        'x': nrm((BATCH, SEQ, D), 1.0),
        'c': nrm((BATCH, D), 1.0),
        'ctx': nrm((BATCH, CTX_LEN, D), 1.0),
        'c_ctx': nrm((D,), 1.0),
        'ada_w': nrm((L, D, N_MOD * D), 0.5 * D ** -0.5),
        'ada_b': nrm((L, N_MOD * D), 0.01),
        'norm1_w': 1.0 + nrm((L, D), 0.02),
        'norm2_w': 1.0 + nrm((L, D), 0.02),
        'w_in': nrm((L, D, IN_WIDTH), D ** -0.5),
        's5_lam_re': -0.5 + nrm((L, 2, G, P), 0.01),
        's5_lam_im': lam_im_init + nrm((L, 2, G, P), 0.01),
        's5_log_dt': jax.random.uniform(next(ks), (L, 2, G), f32, math.log(S5_DT_MIN), math.log(S5_DT_MAX)),
        's5_b_re': nrm((L, 2, G, P, H), (2 * H) ** -0.5),
        's5_b_im': nrm((L, 2, G, P, H), (2 * H) ** -0.5),
        's5_c_re': nrm((L, 2, G, H, P), P ** -0.5),
        's5_c_im': nrm((L, 2, G, H, P), P ** -0.5),
        's5_d': nrm((L, S5_WIDTH), 1.0),
        's5_glu_w': nrm((L, S5_WIDTH, S5_WIDTH), S5_WIDTH ** -0.5),
        'gla_gk_up': nrm((L, 2, GLA_GATE_RANK, GLA_KEY_WIDTH), GLA_GATE_RANK ** -0.5),
        'gla_gk_b': nrm((L, 2, GLA_KEY_WIDTH), 0.1),
        'gla_norm_w': 1.0 + nrm((L, GLA_DV), 0.02),
        'w_s5_proj': nrm((L, S5_WIDTH, D), S5_WIDTH ** -0.5),
        'w_gla_proj': nrm((L, GLA_VAL_WIDTH, D), GLA_VAL_WIDTH ** -0.5),
        'w_out': nrm((L, D, D), D ** -0.5),
        'router_w': nrm((L, D, E), D ** -0.5),
        'router_bias': nrm((L, E), 0.01),
        'exp_w_gate': nrm((L, E, D, F), D ** -0.5),
        'exp_w_up': nrm((L, E, D, F), D ** -0.5),
        'exp_w_down': nrm((L, E, F, D), F ** -0.5),
        'sh_w_gate': nrm((L, D, D_SHARED), D ** -0.5),
        'sh_w_up': nrm((L, D, D_SHARED), D ** -0.5),
        'sh_w_down': nrm((L, D_SHARED, D), D_SHARED ** -0.5),
        'final_norm_w': 1.0 + nrm((D,), 0.02),
    }


def reference(x, c, ctx, c_ctx, ada_w, ada_b, norm1_w, norm2_w, w_in, s5_lam_re, s5_lam_im,
              s5_log_dt, s5_b_re, s5_b_im, s5_c_re, s5_c_im, s5_d, s5_glu_w, gla_gk_up, gla_gk_b,
              gla_norm_w, w_s5_proj, w_gla_proj, w_out, router_w, router_bias, exp_w_gate, exp_w_up,
              exp_w_down, sh_w_gate, sh_w_up, sh_w_down, final_norm_w):
    for i in range(DEPTH):
        last = i == DEPTH - 1
        mod = jax.nn.silu(c) @ ada_w[i] + ada_b[i]
        mod_c = jax.nn.silu(c_ctx) @ ada_w[i] + ada_b[i]
        sh1, sc1, g1, sh2, sc2, g2 = jnp.split(mod[:, None, :], N_MOD, axis=-1)
        csh1, csc1, cg1, csh2, csc2, cg2 = jnp.split(mod_c, N_MOD, axis=-1)
        h = modulate(x, norm1_w[i], sh1, sc1)
        hc = modulate(ctx, norm1_w[i], csh1, csc1)
        y, yc = token_mixer(h, hc, w_in[i], s5_lam_re[i], s5_lam_im[i], s5_log_dt[i], s5_b_re[i],
                            s5_b_im[i], s5_c_re[i], s5_c_im[i], s5_d[i], s5_glu_w[i], gla_gk_up[i],
                            gla_gk_b[i], gla_norm_w[i], w_s5_proj[i], w_gla_proj[i], w_out[i],
                            ctx_out=not last)
        x = x + g1 * y
        x = x + g2 * moe(modulate(x, norm2_w[i], sh2, sc2), router_w[i], router_bias[i],
                         exp_w_gate[i], exp_w_up[i], exp_w_down[i], sh_w_gate[i], sh_w_up[i], sh_w_down[i])
        if not last:
            ctx = ctx + cg1 * yc
            ctx = ctx + cg2 * moe(modulate(ctx, norm2_w[i], csh2, csc2), router_w[i], router_bias[i],
                                  exp_w_gate[i], exp_w_up[i], exp_w_down[i], sh_w_gate[i], sh_w_up[i], sh_w_down[i])
    return rmsnorm(x, final_norm_w)
```

```python
import functools
import math

import jax
import jax.numpy as jnp
from jax import lax
from jax.experimental import pallas as pl
from jax.experimental.pallas import tpu as pltpu

F32 = jnp.float32
BF16 = jnp.bfloat16
U32 = jnp.uint32
I32 = jnp.int32

EPS = 1e-6
N_MOD = 6

S5_GROUP = 16
S5_STATE = 64
S5_CHUNK = 16
S5_GROUPS_PER_STEP = 8

GLA_HEADS = 4
GLA_GATE_RANK = 16
GLA_GATE_NORM = 16.0
GLA_CHUNK = 64

N_EXPERTS = 64
TOP_K = 8
N_GROUPS = 8
TOPK_GROUPS = 4
ROUTED_SCALE = 2.5
EXPERT_ROWS = 256

VMEM_LIMIT = 56 * 1024 * 1024


def _cparams(sem, vmem=VMEM_LIMIT):
    return pltpu.CompilerParams(dimension_semantics=sem, vmem_limit_bytes=vmem)


def _dot(a, b):
    return jnp.dot(a, b, preferred_element_type=F32)


def _sigmoid(x):
    return 1.0 / (1.0 + jnp.exp(-x))


def _silu(x):
    return x * _sigmoid(x)


def _pack_bf16_pair(x):
    n = x.shape[-1] // 2
    bits = lax.bitcast_convert_type(x.astype(BF16).astype(F32), U32)
    return (bits[:, :n] >> 16) | (bits[:, n:] & jnp.uint32(0xFFFF0000))


def _unpack_bf16_pair(w):
    lo = lax.bitcast_convert_type(w << 16, F32)
    hi = lax.bitcast_convert_type(w & jnp.uint32(0xFFFF0000), F32)
    return lo, hi


def _ada_body(c_ref, w_ref, b_ref, o_ref):
    c = c_ref[...]
    s = _silu(c).astype(BF16)
    o_ref[...] = _dot(s, w_ref[...].astype(BF16)) + b_ref[...]


def _ada(c8, ada_w, ada_b):
    d, n = ada_w.shape
    tn = 512
    return pl.pallas_call(
        _ada_body,
        grid=(n // tn,),
        in_specs=[
            pl.BlockSpec((8, d), lambda j: (0, 0)),
            pl.BlockSpec((d, tn), lambda j: (0, j)),
            pl.BlockSpec((1, tn), lambda j: (0, j)),
        ],
        out_specs=pl.BlockSpec((8, tn), lambda j: (0, j)),
        out_shape=jax.ShapeDtypeStruct((8, n), F32),
        compiler_params=_cparams(("parallel",)),
        name="ada",
    )(c8, ada_w, ada_b.reshape(1, n))


def _modulated_norm(x, nw, sh, sc):
    ms = jnp.mean(x * x, axis=-1, keepdims=True)
    y = x * lax.rsqrt(ms + EPS) * nw
    return y * (1.0 + sc) + sh


def _inproj_body(x_ref, sh_ref, sc_ref, nw_ref, w_ref, wg_ref, p_ref, g_ref, h_scr):
    @pl.when(pl.program_id(2) == 0)
    def _():
        h = _modulated_norm(x_ref[0], nw_ref[...], sh_ref[0, 0], sc_ref[0, 0]).astype(BF16)
        h_scr[...] = h
        g_ref[0] = _dot(h, wg_ref[...])

    p_ref[0] = _dot(h_scr[...], w_ref[...]).astype(BF16)


def _inproj(x, mod4, mod_row, nw, w, wg, tm, tn):
    b_, l_, d = x.shape
    n = w.shape[1]
    return pl.pallas_call(
        _inproj_body,
        grid=(b_, l_ // tm, n // tn),
        in_specs=[
            pl.BlockSpec((1, tm, d), lambda b, i, j: (b, i, 0)),
            pl.BlockSpec((1, 1, 1, d), lambda b, i, j: (mod_row(b), 0, 0, 0)),
            pl.BlockSpec((1, 1, 1, d), lambda b, i, j: (mod_row(b), 1, 0, 0)),
            pl.BlockSpec((1, d), lambda b, i, j: (0, 0)),
            pl.BlockSpec((d, tn), lambda b, i, j: (0, j)),
            pl.BlockSpec((d, 128), lambda b, i, j: (0, 0)),
        ],
        out_specs=[
            pl.BlockSpec((1, tm, tn), lambda b, i, j: (b, i, j)),
            pl.BlockSpec((1, tm, 128), lambda b, i, j: (b, i, 0)),
        ],
        out_shape=[
            jax.ShapeDtypeStruct((b_, l_, n), BF16),
            jax.ShapeDtypeStruct((b_, l_, 128), F32),
        ],
        scratch_shapes=[pltpu.VMEM((tm, d), BF16)],
        compiler_params=_cparams(("parallel", "parallel", "arbitrary")),
        name="inproj",
    )(x, mod4, mod4, nw, w, wg)


def _s5_weights(lam_re, lam_im, log_dt, b_re, b_im, c_re, c_im, d_skip):
    hp = lax.Precision.HIGHEST
    tc = S5_CHUNK
    g_, p_ = lam_re.shape[1:]
    h_ = b_re.shape[-1]
    dt = jnp.exp(log_dt)[..., None]
    ar, ai = lam_re * dt, lam_im * dt
    k = jnp.arange(tc + 1, dtype=F32)
    mag = jnp.exp(ar[..., None] * k)
    pw_re = mag * jnp.cos(ai[..., None] * k)
    pw_im = mag * jnp.sin(ai[..., None] * k)
    num_re = jnp.expm1(ar) * jnp.cos(ai) - 2.0 * jnp.sin(0.5 * ai) ** 2
    num_im = jnp.exp(ar) * jnp.sin(ai)
    den = lam_re * lam_re + lam_im * lam_im
    f_re = (num_re * lam_re + num_im * lam_im) / den
    f_im = (num_im * lam_re - num_re * lam_im) / den
    bb_re = f_re[..., None] * b_re - f_im[..., None] * b_im
    bb_im = f_re[..., None] * b_im + f_im[..., None] * b_re

    cp_re = c_re[..., None] * pw_re[:, :, None] - c_im[..., None] * pw_im[:, :, None]
    cp_im = c_re[..., None] * pw_im[:, :, None] + c_im[..., None] * pw_re[:, :, None]
    kk = (jnp.einsum('dgipt,dgpj->dgtij', cp_re, bb_re, precision=hp)
          - jnp.einsum('dgipt,dgpj->dgtij', cp_im, bb_im, precision=hp))

    s_idx = jnp.arange(tc)[:, None]
    t_idx = jnp.arange(tc)[None, :]
    lag_f = t_idx - s_idx
    lag_b = s_idx - t_idx
    kf = jnp.where((lag_f >= 0)[None, :, :, None, None], kk[0][:, jnp.clip(lag_f, 0, tc)], 0.0)
    kb = jnp.where((lag_b >= 0)[None, :, :, None, None], kk[1][:, jnp.clip(lag_b, 0, tc)], 0.0)
    dsk = d_skip.reshape(g_, h_)
    eye_t = jnp.eye(tc, dtype=F32)[None, :, :, None, None]
    eye_h = jnp.eye(h_, dtype=F32)[None, None, None]
    m_all = kf + kb + eye_t * eye_h * dsk[:, None, None, :, None]
    m_all = m_all.transpose(0, 1, 4, 2, 3).reshape(g_, tc * h_, tc * h_)

    def state_w(dr, expo):
        e_re = pw_re[dr][:, :, expo]
        e_im = pw_im[dr][:, :, expo]
        w_re = jnp.einsum('gps,gpj->gsjp', e_re, bb_re[dr], precision=hp) - jnp.einsum(
            'gps,gpj->gsjp', e_im, bb_im[dr], precision=hp)
        w_im = jnp.einsum('gps,gpj->gsjp', e_re, bb_im[dr], precision=hp) + jnp.einsum(
            'gps,gpj->gsjp', e_im, bb_re[dr], precision=hp)
        w_re = w_re.reshape(g_, tc * h_, p_)
        w_im = w_im.reshape(g_, tc * h_, p_)
        return jnp.concatenate([w_re, w_im, w_im, w_re], axis=-1)

    ws_f = state_w(0, tc - 1 - jnp.arange(tc))
    ws_b = state_w(1, jnp.arange(tc))
    wcat = jnp.concatenate([m_all, ws_f, ws_b], axis=-1)

    def out_w(dr, expo):
        q_re = cp_re[dr][..., expo]
        q_im = cp_im[dr][..., expo]
        top = q_re.transpose(0, 2, 3, 1).reshape(g_, p_, tc * h_)
        bot = (-q_im).transpose(0, 2, 3, 1).reshape(g_, p_, tc * h_)
        return jnp.concatenate([top, bot], axis=1)

    wout = jnp.concatenate([out_w(0, 1 + jnp.arange(tc)), out_w(1, tc - jnp.arange(tc))], axis=1)

    a_re, a_im = pw_re[..., tc], pw_im[..., tc]
    zeros = jnp.zeros_like(a_re[0])
    rows = []
    for dr in range(2):
        rows += [jnp.concatenate([a_re[dr], a_re[dr]], -1),
                 jnp.concatenate([-a_im[dr], a_im[dr]], -1),
                 jnp.concatenate([a_im[dr], -a_im[dr]], -1)]
    rows += [jnp.concatenate([zeros, zeros], -1)] * 2
    acoef = jnp.stack(rows, axis=1)
    return wcat.astype(BF16), wout.astype(BF16), acoef


def _s5_body(x_ref, wcat_ref, wout_ref, a_ref, y_ref, y_scr, s_scr, h_scr, *, n_ctx, n_lat, nb):
    gb = x_ref.shape[0]
    pw = 2 * S5_STATE
    for g in range(gb):
        r = _dot(x_ref[g], wcat_ref[g])
        y_scr[g] = r[:, : y_scr.shape[-1]]
        s_scr[g] = r[:, y_scr.shape[-1]:]

    a1f, a2f, a2sf = a_ref[:, 0:1, :], a_ref[:, 1:2, :], a_ref[:, 2:3, :]
    a1b, a2b, a2sb = a_ref[:, 3:4, :], a_ref[:, 4:5, :], a_ref[:, 5:6, :]

    first = lax.broadcasted_iota(I32, (gb, 2 * nb, pw), 1) < nb

    def both(v):
        r = pltpu.roll(v, nb, 1)
        return jnp.where(first, v, r), jnp.where(first, r, v)

    def step(pf, pb, carry):
        hf, hfs, hb, hbs = carry
        rf = pl.ds(pl.multiple_of(pf * 2 * nb, 2 * nb), 2 * nb)
        rb = pl.ds(pl.multiple_of(pb * 2 * nb, 2 * nb), 2 * nb)
        sfa, sfb = both(s_scr[:, rf, 0:pw])
        sfsa, sfsb = both(s_scr[:, rf, pw:2 * pw])
        sba, sbb = both(s_scr[:, rb, 2 * pw:3 * pw])
        sbsa, sbsb = both(s_scr[:, rb, 3 * pw:4 * pw])
        hf1 = a1f * hf + a2f * hfs + sfa
        hfs1 = a1f * hfs + a2sf * hf + sfsa
        h_scr[:, rf, 0:pw] = jnp.where(first, hf, hf1)
        hf2 = a1f * hf1 + a2f * hfs1 + sfb
        hfs2 = a1f * hfs1 + a2sf * hf1 + sfsb
        hb1 = a1b * hb + a2b * hbs + sbb
        hbs1 = a1b * hbs + a2sb * hb + sbsb
        h_scr[:, rb, pw:2 * pw] = jnp.where(first, hb1, hb)
        hb2 = a1b * hb1 + a2b * hbs1 + sba
        hbs2 = a1b * hbs1 + a2sb * hb1 + sbsa
        return hf2, hfs2, hb2, hbs2

    z = jnp.zeros((gb, 2 * nb, pw), F32)
    carry = (z, z, z, z)
    pc, pl_ = n_ctx // 2, n_lat // 2
    carry = lax.fori_loop(0, pc, lambda i, c: step(i, pc - 1 - i, c), carry)
    carry = lax.fori_loop(0, pl_, lambda i, c: step(pc + i, pc + pl_ - 1 - i, c), carry)

    lo = n_ctx * nb
    for g in range(gb):
        y = y_scr[g] + _dot(h_scr[g].astype(BF16), wout_ref[g])
        y_ref[g] = y[lo:, :]


def _s5(x2, wcat, wout, acoef, n_ctx, n_lat, nb):
    g_, rows, kdim = x2.shape
    gb = S5_GROUPS_PER_STEP
    pw = 2 * S5_STATE
    out_rows = n_lat * nb
    return pl.pallas_call(
        functools.partial(_s5_body, n_ctx=n_ctx, n_lat=n_lat, nb=nb),
        grid=(g_ // gb,),
        in_specs=[
            pl.BlockSpec((gb, rows, kdim), lambda i: (i, 0, 0)),
            pl.BlockSpec((gb, kdim, wcat.shape[-1]), lambda i: (i, 0, 0)),
            pl.BlockSpec((gb, 2 * pw, kdim), lambda i: (i, 0, 0)),
            pl.BlockSpec((gb, 8, pw), lambda i: (i, 0, 0)),
        ],
        out_specs=pl.BlockSpec((gb, out_rows, kdim), lambda i: (i, 0, 0)),
        out_shape=jax.ShapeDtypeStruct((g_, out_rows, kdim), F32),
        scratch_shapes=[
            pltpu.VMEM((gb, rows, kdim), F32),
            pltpu.VMEM((gb, rows, 4 * pw), F32),
            pltpu.VMEM((gb, rows, 2 * pw), F32),
        ],
        compiler_params=_cparams(("parallel",)),
        name="s5",
    )(x2, wcat, wout, acoef)


def _log_sigmoid(z):
    return jnp.minimum(z, 0.0) - jnp.log1p(jnp.exp(-jnp.abs(z)))


def _gla_body(q_ref, k_ref, v_ref, kc_ref, vc_ref, gk_ref, gkc_ref, up_ref, bias_ref, o_ref, st_ref,
              *, scale):
    c = GLA_CHUNK
    n_lat = q_ref.shape[1] // c
    n_ctx = kc_ref.shape[1] // c
    row = lax.broadcasted_iota(I32, (c, c), 0)
    col = lax.broadcasted_iota(I32, (c, c), 1)
    nt = (((1,), (1,)), ((), ()))
    tn = (((0,), (0,)), ((), ()))

    for dr in range(2):
        keep = (col <= row) if dr == 0 else (col >= row)
        tri = jnp.where(keep, 1.0, 0.0).astype(BF16)
        smask = (col <= row) if dr == 0 else (col > row)
        up = up_ref[dr]
        bias = bias_ref[dr]
        edge = c - 1 if dr == 0 else 0

        def decay(gk):
            z = _dot(gk.astype(BF16), up) + bias
            g = _log_sigmoid(z) * (1.0 / GLA_GATE_NORM)
            g_hi = g.astype(BF16)
            g_lo = (g - g_hi.astype(F32)).astype(BF16)
            bc = _dot(tri, g_hi) + _dot(tri, g_lo)
            return bc, bc[edge:edge + 1, :]

        def update_state(kk, vv, bc, btot):
            kd = (kk.astype(F32) * jnp.exp(btot - bc)).astype(BF16)
            ds_t = lax.dot_general(vv, kd, tn, preferred_element_type=F32)
            st_ref[...] = jnp.exp(btot) * st_ref[...] + ds_t

        def ctx_step(i, _):
            n = i if dr == 0 else n_ctx - 1 - i
            rs = pl.ds(pl.multiple_of(n * c, c), c)
            bc, btot = decay(gkc_ref[0, rs, :])
            update_state(kc_ref[0, rs, :], vc_ref[0, rs, :], bc, btot)
            return 0

        def lat_step(i, _):
            n = i if dr == 0 else n_lat - 1 - i
            rs = pl.ds(pl.multiple_of(n * c, c), c)
            bc, btot = decay(gk_ref[0, rs, :])
            kk = k_ref[0, rs, :]
            vv = v_ref[0, rs, :]
            qb = (q_ref[0, rs, :].astype(F32) * (scale * jnp.exp(bc))).astype(BF16)
            kb = (kk.astype(F32) * jnp.exp(-bc)).astype(BF16)
            scores = lax.dot_general(qb, kb, nt, preferred_element_type=F32)
            scores = jnp.where(smask, scores, 0.0).astype(BF16)
            o = _dot(scores, vv) + lax.dot_general(qb, st_ref[...].astype(BF16), nt,
                                                   preferred_element_type=F32)
            if dr == 0:
                o_ref[0, rs, :] = o
            else:
                o_ref[0, rs, :] = o_ref[0, rs, :] + o
            update_state(kk, vv, bc, btot)
            return 0

        st_ref[...] = jnp.zeros_like(st_ref)
        lax.fori_loop(0, n_ctx, ctx_step, 0)
        lax.fori_loop(0, n_lat, lat_step, 0)


def _gla(p_lat, p_ctx, gk_lat, gk_ctx, up_pad, bias, dk, dv):
    b_, l_, _ = p_lat.shape
    lc = p_ctx.shape[1]
    hh = GLA_HEADS
    s5w = hh * dv
    q0, k0, v0 = s5w // dk, (s5w + hh * dk) // dk, (s5w + 2 * hh * dk) // dv
    kc0, vc0 = s5w // dk, (s5w + hh * dk) // dv
    return pl.pallas_call(
        functools.partial(_gla_body, scale=dk ** -0.5),
        grid=(b_, hh),
        in_specs=[
            pl.BlockSpec((1, l_, dk), lambda b, h: (b, 0, q0 + h)),
            pl.BlockSpec((1, l_, dk), lambda b, h: (b, 0, k0 + h)),
            pl.BlockSpec((1, l_, dv), lambda b, h: (b, 0, v0 + h)),
            pl.BlockSpec((1, lc, dk), lambda b, h: (b, 0, kc0 + h)),
            pl.BlockSpec((1, lc, dv), lambda b, h: (b, 0, vc0 + h)),
            pl.BlockSpec((1, l_, 128), lambda b, h: (b, 0, 0)),
            pl.BlockSpec((1, lc, 128), lambda b, h: (b, 0, 0)),
            pl.BlockSpec((2, 128, dk), lambda b, h: (0, 0, h)),
            pl.BlockSpec((2, 1, dk), lambda b, h: (0, 0, h)),
        ],
        out_specs=pl.BlockSpec((1, l_, dv), lambda b, h: (b, 0, h)),
        out_shape=jax.ShapeDtypeStruct((b_, l_, hh * dv), F32),
        scratch_shapes=[pltpu.VMEM((dv, dk), F32)],
        compiler_params=_cparams(("parallel", "parallel")),
        name="gla",
    )(p_lat, p_lat, p_lat, p_ctx, p_ctx, gk_lat, gk_ctx, up_pad, bias)


def _gelu_tanh(x):
    return 0.5 * x * (1.0 + jnp.tanh(math.sqrt(2.0 / math.pi) * (x + 0.044715 * (x * x * x))))


def _merge_body(ys_ref, yg_ref, go_ref, gs_ref, gg_ref, x_ref, g1_ref, sh2_ref, sc2_ref,
                glu_ref, ws5_ref, wgla_ref, wout_ref, gnw_ref, n2w_ref, rw_ref,
                x1_ref, h2_ref, lg_ref, *, dv):
    a = _gelu_tanh(ys_ref[0])
    a = a * _sigmoid(_dot(a.astype(BF16), glu_ref[...]))
    pa = _dot(a.astype(BF16), ws5_ref[...])

    yg = yg_ref[0]
    parts = []
    for h in range(GLA_HEADS):
        oh = yg[:, h * dv:(h + 1) * dv]
        ms = jnp.mean(oh * oh, axis=-1, keepdims=True)
        parts.append(oh * lax.rsqrt(ms + EPS) * gnw_ref[...])
    gl = jnp.concatenate(parts, axis=-1) * _silu(go_ref[0].astype(F32))
    pb = _dot(gl.astype(BF16), wgla_ref[...])

    m = _sigmoid(gs_ref[0].astype(F32)) * pa + _sigmoid(gg_ref[0].astype(F32)) * pb
    y = _dot(m.astype(BF16), wout_ref[...])
    x1 = x_ref[0] + g1_ref[0, 0] * y
    x1_ref[0] = x1

    h2 = _modulated_norm(x1, n2w_ref[...], sh2_ref[0, 0], sc2_ref[0, 0])
    h_hi = h2.astype(BF16)
    h_lo = (h2 - h_hi.astype(F32)).astype(BF16)
    h2_ref[0] = _pack_bf16_pair(h2)
    lg_ref[0] = _dot(h_hi, rw_ref[0]) + _dot(h_lo, rw_ref[0]) + _dot(h_hi, rw_ref[1])


def _resident(shape):
    nd = len(shape)
    return pl.BlockSpec(shape, lambda b, i: (0,) * nd, pipeline_mode=pl.Buffered(1))


def _merge(ys, yg, p_lat, x, mod4, glu_w, ws5, wgla, wout, gnw, n2w, rw2, dv, tm):
    b_, l_, d = x.shape
    w5 = ys.shape[-1]
    go0 = (p_lat.shape[-1] - 2 * d - w5) // w5
    gs0 = (p_lat.shape[-1] - 2 * d) // d
    modspec = lambda r: pl.BlockSpec((1, 1, 1, d), lambda b, i: (b, r, 0, 0))
    return pl.pallas_call(
        functools.partial(_merge_body, dv=dv),
        grid=(b_, l_ // tm),
        in_specs=[
            pl.BlockSpec((1, tm, w5), lambda b, i: (b, i, 0)),
            pl.BlockSpec((1, tm, w5), lambda b, i: (b, i, 0)),
            pl.BlockSpec((1, tm, w5), lambda b, i: (b, i, go0)),
            pl.BlockSpec((1, tm, d), lambda b, i: (b, i, gs0)),
            pl.BlockSpec((1, tm, d), lambda b, i: (b, i, gs0 + 1)),
            pl.BlockSpec((1, tm, d), lambda b, i: (b, i, 0)),
            modspec(2), modspec(3), modspec(4),
            _resident(glu_w.shape), _resident(ws5.shape), _resident(wgla.shape), _resident(wout.shape),
            _resident(gnw.shape), _resident(n2w.shape), _resident(rw2.shape),
        ],
        out_specs=[
            pl.BlockSpec((1, tm, d), lambda b, i: (b, i, 0)),
            pl.BlockSpec((1, tm, d // 2), lambda b, i: (b, i, 0)),
            pl.BlockSpec((1, tm, 128), lambda b, i: (b, i, 0)),
        ],
        out_shape=[
            jax.ShapeDtypeStruct((b_, l_, d), F32),
            jax.ShapeDtypeStruct((b_, l_, d // 2), U32),
            jax.ShapeDtypeStruct((b_, l_, 128), F32),
        ],
        compiler_params=_cparams(("parallel", "parallel")),
        name="merge",
    )(ys, yg, p_lat, p_lat, p_lat, x, mod4, mod4, mod4, glu_w, ws5, wgla, wout, gnw, n2w, rw2)


def _route_body(lg_ref, bias_ref, eidx_ref, w_ref, rank_ref, cnt_ref, carry_ref):
    ne, tt = lg_ref.shape
    gsz = ne // N_GROUPS
    neg = -jnp.inf

    @pl.when(pl.program_id(0) == 0)
    def _():
        carry_ref[...] = jnp.zeros_like(carry_ref)

    sc = _sigmoid(lg_ref[...])
    ch = sc + bias_ref[...]

    def first_max(v, idx, big):
        m = jnp.max(v, axis=0, keepdims=True)
        i = jnp.min(jnp.where(v == m, idx, big), axis=0, keepdims=True)
        return m, i

    midx = lax.broadcasted_iota(I32, (gsz, tt), 0).astype(F32)
    gs_rows = []
    for g in range(N_GROUPS):
        v = ch[g * gsz:(g + 1) * gsz, :]
        m1, i1 = first_max(v, midx, float(gsz))
        m2 = jnp.max(jnp.where(midx == i1, neg, v), axis=0, keepdims=True)
        gs_rows.append(m1 + m2)
    gscore = jnp.concatenate(gs_rows, axis=0)

    gidx = lax.broadcasted_iota(I32, (N_GROUPS, tt), 0).astype(F32)
    gsel = jnp.zeros((N_GROUPS, tt), F32)
    cur = gscore
    for _ in range(TOPK_GROUPS):
        _, i = first_max(cur, gidx, float(N_GROUPS))
        hit = gidx == i
        gsel = jnp.where(hit, 1.0, gsel)
        cur = jnp.where(hit, neg, cur)

    masked = jnp.concatenate(
        [jnp.where(gsel[g:g + 1, :] > 0.0, ch[g * gsz:(g + 1) * gsz, :], neg) for g in range(N_GROUPS)], axis=0)

    eiota = lax.broadcasted_iota(I32, (ne, tt), 0).astype(F32)
    sel = jnp.zeros((ne, tt), F32)
    cur = masked
    idx_rows, s_rows = [], []
    for _ in range(TOP_K):
        _, i = first_max(cur, eiota, float(ne))
        hit = eiota == i
        idx_rows.append(i)
        s_rows.append(jnp.sum(jnp.where(hit, sc, 0.0), axis=0, keepdims=True))
        sel = jnp.where(hit, 1.0, sel)
        cur = jnp.where(hit, neg, cur)
    idx = jnp.concatenate(idx_rows, axis=0)
    s = jnp.concatenate(s_rows, axis=0)
    w_ref[...] = s / jnp.sum(s, axis=0, keepdims=True) * ROUTED_SCALE
    eidx_ref[...] = idx.astype(I32)

    cw = 256
    a_i = lax.broadcasted_iota(I32, (cw, cw), 0)
    b_i = lax.broadcasted_iota(I32, (cw, cw), 1)
    upper = jnp.where(a_i < b_i, 1.0, 0.0).astype(BF16)
    carry = carry_ref[:, 0:1]
    pieces = []
    for c0 in range(0, tt, cw):
        sc_c = sel[:, c0:c0 + cw]
        pieces.append(_dot(sc_c.astype(BF16), upper) + carry)
        carry = carry + jnp.sum(sc_c, axis=1, keepdims=True)
    rank_full = jnp.concatenate(pieces, axis=1)
    carry_ref[...] = jnp.broadcast_to(carry, carry_ref.shape)
    cnt_ref[...] = jnp.broadcast_to(carry, cnt_ref.shape)
    rk = [jnp.sum(jnp.where(eiota == idx_rows[k], rank_full, 0.0), axis=0, keepdims=True) for k in range(TOP_K)]
    rank_ref[...] = jnp.concatenate(rk, axis=0).astype(I32)


def _route(logits_t, bias):
    ne, t = logits_t.shape
    tt = 1024
    return pl.pallas_call(
        _route_body,
        grid=(t // tt,),
        in_specs=[
            pl.BlockSpec((ne, tt), lambda i: (0, i)),
            pl.BlockSpec((ne, 1), lambda i: (0, 0)),
        ],
        out_specs=[
            pl.BlockSpec((TOP_K, tt), lambda i: (0, i)),
            pl.BlockSpec((TOP_K, tt), lambda i: (0, i)),
            pl.BlockSpec((TOP_K, tt), lambda i: (0, i)),
            pl.BlockSpec((ne, 128), lambda i: (0, 0)),
        ],
        out_shape=[
            jax.ShapeDtypeStruct((TOP_K, t), I32),
            jax.ShapeDtypeStruct((TOP_K, t), F32),
            jax.ShapeDtypeStruct((TOP_K, t), I32),
            jax.ShapeDtypeStruct((ne, 128), F32),
        ],
        scratch_shapes=[pltpu.VMEM((ne, 128), F32)],
        compiler_params=_cparams(("arbitrary",)),
        name="route",
    )(logits_t, bias.reshape(ne, 1))


def _expert_body(be_ref, nu_ref, rows_hbm, x_hbm, wg_ref, wu_ref, wd_ref, y_ref,
                 idx_smem, xbuf, wg_s, wu_s, wd_s, isem, gsem):
    i = pl.program_id(0)
    nb = pl.num_programs(0)
    n_used = nu_ref[0]
    r_ = EXPERT_ROWS
    slot = i % 2
    nslot = 1 - slot

    def idx_copy(blk, s):
        return pltpu.make_async_copy(rows_hbm.at[blk], idx_smem.at[s], isem.at[s])

    def row_copy(tok, s, r):
        return pltpu.make_async_copy(x_hbm.at[pl.ds(tok, 1), :], xbuf.at[s, pl.ds(r, 1), :], gsem.at[s])

    def issue_rows(s):
        def body(r, _):
            row_copy(idx_smem[s, r], s, r).start()
            return 0
        lax.fori_loop(0, r_, body, 0)

    def wait_rows(s):
        pltpu.make_async_copy(x_hbm.at[pl.ds(0, r_), :], xbuf.at[s], gsem.at[s]).wait()

    @pl.when(i == 0)
    def _():
        idx_copy(0, 0).start()
        idx_copy(0, 0).wait()
        issue_rows(0)

        @pl.when(n_used > 1)
        def _():
            idx_copy(1, 1).start()

    @pl.when(i + 1 < n_used)
    def _():
        idx_copy(i + 1, nslot).wait()
        issue_rows(nslot)

    @pl.when(i < n_used)
    def _():
        wait_rows(slot)

    @pl.when(i + 2 < n_used)
    def _():
        idx_copy(i + 2, slot).start()

    e_now = be_ref[i]
    e_prev = be_ref[jnp.maximum(i - 1, 0)]

    @pl.when(jnp.logical_or(i == 0, e_now != e_prev))
    def _():
        wg_s[...] = wg_ref[0].astype(BF16)
        wu_s[...] = wu_ref[0].astype(BF16)
        wd_s[...] = wd_ref[0].astype(BF16)

    @pl.when(i < n_used)
    def _():
        lo, hi = _unpack_bf16_pair(xbuf[slot])
        lo = lo.astype(BF16)
        hi = hi.astype(BF16)
        half = lo.shape[-1]
        a = _dot(lo, wg_s[0:half, :]) + _dot(hi, wg_s[half:, :])
        u = _dot(lo, wu_s[0:half, :]) + _dot(hi, wu_s[half:, :])
        hid = (_silu(a) * u).astype(BF16)
        y_ref[...] = _pack_bf16_pair(_dot(hid, wd_s[...]))

    @pl.when(i >= n_used)
    def _():
        y_ref[...] = jnp.zeros_like(y_ref)


def _experts(block_e, n_used, row_buf, h2p, wg, wu, wd):
    nb = block_e.shape[0]
    t, half = h2p.shape
    ne, d, f = wg.shape
    r_ = EXPERT_ROWS
    grid_spec = pltpu.PrefetchScalarGridSpec(
        num_scalar_prefetch=2,
        grid=(nb,),
        in_specs=[
            pl.BlockSpec(memory_space=pl.ANY),
            pl.BlockSpec(memory_space=pl.ANY),
            pl.BlockSpec((1, d, f), lambda i, be, nu: (be[i], 0, 0)),
            pl.BlockSpec((1, d, f), lambda i, be, nu: (be[i], 0, 0)),
            pl.BlockSpec((1, f, d), lambda i, be, nu: (be[i], 0, 0)),
        ],
        out_specs=pl.BlockSpec((r_, half), lambda i, be, nu: (i, 0)),
        scratch_shapes=[
            pltpu.SMEM((2, r_), I32),
            pltpu.VMEM((2, r_, half), U32),
            pltpu.VMEM((d, f), BF16),
            pltpu.VMEM((d, f), BF16),
            pltpu.VMEM((f, d), BF16),
            pltpu.SemaphoreType.DMA((2,)),
            pltpu.SemaphoreType.DMA((2,)),
        ],
    )
    return pl.pallas_call(
        _expert_body,
        grid_spec=grid_spec,
        out_shape=jax.ShapeDtypeStruct((nb * r_, half), U32),
        compiler_params=_cparams(("arbitrary",)),
        name="experts",
    )(block_e, n_used, row_buf.reshape(nb, r_), h2p, wg, wu, wd)


def _combine_body(dest_hbm, y_hbm, w_ref, x1_ref, h2_ref, g2_ref, sg_ref, su_ref, sd_ref, fnw_ref, o_ref,
                  idx_smem, ybuf, isem, gsem):
    i = pl.program_id(0)
    n = pl.num_programs(0)
    tm = x1_ref.shape[0]
    nrow = TOP_K * tm
    slot = i % 2
    nslot = 1 - slot

    def idx_copy(blk, s):
        return pltpu.make_async_copy(dest_hbm.at[blk], idx_smem.at[s], isem.at[s])

    def issue_rows(s):
        def body(r, _):
            pltpu.make_async_copy(y_hbm.at[pl.ds(idx_smem[s, r], 1), :], ybuf.at[s, pl.ds(r, 1), :],
                                  gsem.at[s]).start()
            return 0
        lax.fori_loop(0, nrow, body, 0)

    def wait_rows(s):
        pltpu.make_async_copy(y_hbm.at[pl.ds(0, nrow), :], ybuf.at[s], gsem.at[s]).wait()

    @pl.when(i == 0)
    def _():
        idx_copy(0, 0).start()
        idx_copy(0, 0).wait()
        issue_rows(0)

        @pl.when(n > 1)
        def _():
            idx_copy(1, 1).start()

    @pl.when(i + 1 < n)
    def _():
        idx_copy(i + 1, nslot).wait()
        issue_rows(nslot)

    wait_rows(slot)

    @pl.when(i + 2 < n)
    def _():
        idx_copy(i + 2, slot).start()

    w = w_ref[...]
    half = ybuf.shape[-1]
    r_lo = jnp.zeros((tm, half), F32)
    r_hi = jnp.zeros((tm, half), F32)
    for k in range(TOP_K):
        lo, hi = _unpack_bf16_pair(ybuf[slot, k * tm:(k + 1) * tm, :])
        wk = w[:, k:k + 1]
        r_lo = r_lo + wk * lo
        r_hi = r_hi + wk * hi
    routed = jnp.concatenate([r_lo, r_hi], axis=-1)

    x_lo, x_hi = _unpack_bf16_pair(h2_ref[...])
    x_lo = x_lo.astype(BF16)
    x_hi = x_hi.astype(BF16)
    a = _dot(x_lo, sg_ref[0:half, :]) + _dot(x_hi, sg_ref[half:, :])
    u = _dot(x_lo, su_ref[0:half, :]) + _dot(x_hi, su_ref[half:, :])
    shared = _dot((_silu(a) * u).astype(BF16), sd_ref[...])

    x2 = x1_ref[...] + g2_ref[0, 0] * (routed + shared)
    ms = jnp.mean(x2 * x2, axis=-1, keepdims=True)
    o_ref[...] = x2 * lax.rsqrt(ms + EPS) * fnw_ref[...]


def _combine(dest_tiles, y_sorted, w_tok, x1, h2p, mod4, sg, su, sd, fnw, tiles_per_batch, tm):
    t, d = x1.shape
    half = d // 2
    nrow = TOP_K * tm
    res = lambda shape: pl.BlockSpec(shape, lambda i: (0,) * len(shape), pipeline_mode=pl.Buffered(1))
    return pl.pallas_call(
        _combine_body,
        grid=(t // tm,),
        in_specs=[
            pl.BlockSpec(memory_space=pl.ANY),
            pl.BlockSpec(memory_space=pl.ANY),
            pl.BlockSpec((tm, TOP_K), lambda i: (i, 0)),
            pl.BlockSpec((tm, d), lambda i: (i, 0)),
            pl.BlockSpec((tm, half), lambda i: (i, 0)),
            pl.BlockSpec((1, 1, 1, d), lambda i: (i // tiles_per_batch, 5, 0, 0)),
            res(sg.shape), res(su.shape), res(sd.shape), res(fnw.shape),
        ],
        out_specs=pl.BlockSpec((tm, d), lambda i: (i, 0)),
        out_shape=jax.ShapeDtypeStruct((t, d), F32),
        scratch_shapes=[
            pltpu.SMEM((2, nrow), I32),
            pltpu.VMEM((2, nrow, half), U32),
            pltpu.SemaphoreType.DMA((2,)),
            pltpu.SemaphoreType.DMA((2,)),
        ],
        compiler_params=_cparams(("arbitrary",)),
        name="combine",
    )(dest_tiles, y_sorted, w_tok, x1, h2p, mod4, sg, su, sd, fnw)


def kernel(x, c, ctx, c_ctx, ada_w, ada_b, norm1_w, norm2_w, w_in, s5_lam_re, s5_lam_im, s5_log_dt,
           s5_b_re, s5_b_im, s5_c_re, s5_c_im, s5_d, s5_glu_w, gla_gk_up, gla_gk_b, gla_norm_w,
           w_s5_proj, w_gla_proj, w_out, router_w, router_bias, exp_w_gate, exp_w_up, exp_w_down,
           sh_w_gate, sh_w_up, sh_w_down, final_norm_w):
    depth = ada_w.shape[0]
    assert depth == 1, "single-layer block: context outputs are never consumed"
    b_, l_, d = x.shape
    lc = ctx.shape[1]
    s5w = s5_d.shape[1]
    kw = gla_gk_up.shape[-1]
    vw = w_gla_proj.shape[1]
    dk, dv = kw // GLA_HEADS, vw // GLA_HEADS
    rank2 = 2 * GLA_GATE_RANK
    li = 0

    c8 = jnp.zeros((8, d), F32).at[:b_].set(c).at[b_].set(c_ctx)
    mod4 = _ada(c8, ada_w[li], ada_b[li]).reshape(8, N_MOD, 1, d)

    cuts = [0, s5w, s5w + kw, s5w + 2 * kw, s5w + 2 * kw + vw, s5w + 2 * kw + 2 * vw]
    c_gkd = cuts[5]
    w = w_in[li]
    w_main = jnp.concatenate([w[:, :c_gkd], w[:, c_gkd + rank2:]], axis=1).astype(BF16)
    w_ctx = jnp.concatenate([w[:, cuts[0]:cuts[1]], w[:, cuts[2]:cuts[4]]], axis=1).astype(BF16)
    w_gkd = jnp.zeros((d, 128), F32).at[:, :rank2].set(w[:, c_gkd:c_gkd + rank2]).astype(BF16)
    nw1 = norm1_w[li].reshape(1, d)
    p_lat, gk_lat = _inproj(x, mod4, lambda b: b, nw1, w_main, w_gkd, tm=512, tn=1024)
    p_ctx, gk_ctx = _inproj(ctx, mod4, lambda b: b_, nw1, w_ctx, w_gkd, tm=lc, tn=w_ctx.shape[1] // 2)

    g_ = s5w // S5_GROUP
    tc = S5_CHUNK
    n_ctx, n_lat = lc // tc, l_ // tc
    u_all = jnp.concatenate([p_ctx[:, :, :s5w], p_lat[:, :, :s5w]], axis=1)
    x2 = u_all.reshape(b_, n_ctx + n_lat, tc, g_, S5_GROUP).transpose(3, 1, 0, 2, 4)
    x2 = x2.reshape(g_, (n_ctx + n_lat) * b_, tc * S5_GROUP)
    wcat, wout5, acoef = _s5_weights(s5_lam_re[li], s5_lam_im[li], s5_log_dt[li], s5_b_re[li], s5_b_im[li],
                                     s5_c_re[li], s5_c_im[li], s5_d[li])
    y2 = _s5(x2, wcat, wout5, acoef, n_ctx, n_lat, b_)
    ys = y2.reshape(g_, n_lat, b_, tc, S5_GROUP).transpose(2, 1, 3, 0, 4).reshape(b_, l_, s5w)

    up_pad = jnp.zeros((2, 128, kw), F32)
    up_pad = up_pad.at[0, :GLA_GATE_RANK].set(gla_gk_up[li, 0]).at[1, GLA_GATE_RANK:rank2].set(gla_gk_up[li, 1])
    yg = _gla(p_lat, p_ctx, gk_lat, gk_ctx, up_pad.astype(BF16), gla_gk_b[li].reshape(2, 1, kw), dk, dv)

    rw = jnp.zeros((d, 128), F32).at[:, :N_EXPERTS].set(router_w[li])
    rw_hi = rw.astype(BF16)
    rw2 = jnp.stack([rw_hi, (rw - rw_hi.astype(F32)).astype(BF16)])
    x1, h2p, logits = _merge(
        ys, yg, p_lat, x, mod4, s5_glu_w[li].astype(BF16), w_s5_proj[li].astype(BF16),
        w_gla_proj[li].astype(BF16), w_out[li].astype(BF16), gla_norm_w[li].reshape(1, dv),
        norm2_w[li].reshape(1, d), rw2, dv, tm=256)

    t = b_ * l_
    logits_t = logits.reshape(t, 128)[:, :N_EXPERTS].T
    eidx, wts, rank, cnt = _route(logits_t, router_bias[li])
    counts = cnt[:, 0].astype(I32)
    nblk = (counts + EXPERT_ROWS - 1) // EXPERT_ROWS
    blk_end = jnp.cumsum(nblk)
    blk_off = blk_end - nblk
    n_used = blk_end[-1]
    nb = (t * TOP_K + N_EXPERTS * (EXPERT_ROWS - 1) + EXPERT_ROWS - 1) // EXPERT_ROWS
    dest = blk_off[eidx] * EXPERT_ROWS + rank
    tok = jnp.broadcast_to(jnp.arange(t, dtype=I32)[None, :], dest.shape)
    row_buf = jnp.zeros((nb * EXPERT_ROWS,), I32).at[dest.reshape(-1)].set(tok.reshape(-1))
    blk = jnp.minimum(jnp.arange(nb, dtype=I32), n_used - 1)
    block_e = jnp.minimum(jnp.searchsorted(blk_end, blk, side='right'), N_EXPERTS - 1).astype(I32)

    h2p_flat = h2p.reshape(t, d // 2)
    y_sorted = _experts(block_e, n_used.reshape(1).astype(I32), row_buf, h2p_flat,
                        exp_w_gate[li], exp_w_up[li], exp_w_down[li])

    tm_c = 128
    dest_tiles = dest.reshape(TOP_K, t // tm_c, tm_c).transpose(1, 0, 2).reshape(t // tm_c, TOP_K * tm_c)
    out = _combine(dest_tiles, y_sorted, wts.T, x1.reshape(t, d), h2p_flat, mod4,
                   sh_w_gate[li].astype(BF16), sh_w_up[li].astype(BF16), sh_w_down[li].astype(BF16),
                   final_norm_w.reshape(1, d), l_ // tm_c, tm_c)
    return out.reshape(b_, l_, d)
```

```python
import functools
import math

import jax
import jax.numpy as jnp
from jax import lax
from jax.experimental import pallas as pl
from jax.experimental.pallas import tpu as pltpu

F32 = jnp.float32
BF16 = jnp.bfloat16
U32 = jnp.uint32
I32 = jnp.int32

EPS = 1e-6
N_MOD = 6

S5_GROUP = 16
S5_STATE = 64
S5_CHUNK = 16
S5_GROUPS_PER_STEP = 8

GLA_HEADS = 4
GLA_GATE_RANK = 16
GLA_GATE_NORM = 16.0
GLA_CHUNK = 64

N_EXPERTS = 64
TOP_K = 8
N_GROUPS = 8
TOPK_GROUPS = 4
ROUTED_SCALE = 2.5
EXPERT_ROWS = 256

VMEM_LIMIT = 56 * 1024 * 1024


def _cparams(sem, vmem=VMEM_LIMIT):
    return pltpu.CompilerParams(dimension_semantics=sem, vmem_limit_bytes=vmem)


def _dot(a, b):
    return jnp.dot(a, b, preferred_element_type=F32)


def _sigmoid(x):
    return 1.0 / (1.0 + jnp.exp(-x))


def _silu(x):
    return x * _sigmoid(x)


def _pack_bf16_pair(x):
    n = x.shape[-1] // 2
    bits = lax.bitcast_convert_type(x.astype(BF16).astype(F32), U32)
    return (bits[:, :n] >> 16) | (bits[:, n:] & jnp.uint32(0xFFFF0000))


def _unpack_bf16_pair(w):
    lo = lax.bitcast_convert_type(w << 16, F32)
    hi = lax.bitcast_convert_type(w & jnp.uint32(0xFFFF0000), F32)
    return lo, hi


def _ada_body(c_ref, w_ref, b_ref, o_ref):
    c = c_ref[...]
    s = _silu(c).astype(BF16)
    o_ref[...] = _dot(s, w_ref[...].astype(BF16)) + b_ref[...]


def _ada(c8, ada_w, ada_b):
    d, n = ada_w.shape
    tn = 512
    return pl.pallas_call(
        _ada_body,
        grid=(n // tn,),
        in_specs=[
            pl.BlockSpec((8, d), lambda j: (0, 0)),
            pl.BlockSpec((d, tn), lambda j: (0, j)),
            pl.BlockSpec((1, tn), lambda j: (0, j)),
        ],
        out_specs=pl.BlockSpec((8, tn), lambda j: (0, j)),
        out_shape=jax.ShapeDtypeStruct((8, n), F32),
        compiler_params=_cparams(("parallel",)),
        name="ada",
    )(c8, ada_w, ada_b.reshape(1, n))


def _modulated_norm(x, nw, sh, sc):
    ms = jnp.mean(x * x, axis=-1, keepdims=True)
    y = x * lax.rsqrt(ms + EPS) * nw
    return y * (1.0 + sc) + sh


def _inproj_body(x_ref, sh_ref, sc_ref, nw_ref, w_ref, wg_ref, p_ref, g_ref, h_scr):
    @pl.when(pl.program_id(2) == 0)
    def _():
        h = _modulated_norm(x_ref[0], nw_ref[...], sh_ref[0, 0], sc_ref[0, 0]).astype(BF16)
        h_scr[...] = h
        g_ref[0] = _dot(h, wg_ref[...])

    p_ref[0] = _dot(h_scr[...], w_ref[...]).astype(BF16)


def _inproj(x, mod4, mod_row, nw, w, wg, tm, tn):
    b_, l_, d = x.shape
    n = w.shape[1]
    return pl.pallas_call(
        _inproj_body,
        grid=(b_, l_ // tm, n // tn),
        in_specs=[
            pl.BlockSpec((1, tm, d), lambda b, i, j: (b, i, 0)),
            pl.BlockSpec((1, 1, 1, d), lambda b, i, j: (mod_row(b), 0, 0, 0)),
            pl.BlockSpec((1, 1, 1, d), lambda b, i, j: (mod_row(b), 1, 0, 0)),
            pl.BlockSpec((1, d), lambda b, i, j: (0, 0)),
            pl.BlockSpec((d, tn), lambda b, i, j: (0, j)),
            pl.BlockSpec((d, 128), lambda b, i, j: (0, 0)),
        ],
        out_specs=[
            pl.BlockSpec((1, tm, tn), lambda b, i, j: (b, i, j)),
            pl.BlockSpec((1, tm, 128), lambda b, i, j: (b, i, 0)),
        ],
        out_shape=[
            jax.ShapeDtypeStruct((b_, l_, n), BF16),
            jax.ShapeDtypeStruct((b_, l_, 128), F32),
        ],
        scratch_shapes=[pltpu.VMEM((tm, d), BF16)],
        compiler_params=_cparams(("parallel", "parallel", "arbitrary")),
        name="inproj",
    )(x, mod4, mod4, nw, w, wg)


def _s5_weights(lam_re, lam_im, log_dt, b_re, b_im, c_re, c_im, d_skip):
    hp = lax.Precision.HIGHEST
    tc = S5_CHUNK
    g_, p_ = lam_re.shape[1:]
    h_ = b_re.shape[-1]
    dt = jnp.exp(log_dt)[..., None]
    ar, ai = lam_re * dt, lam_im * dt
    k = jnp.arange(tc + 1, dtype=F32)
    mag = jnp.exp(ar[..., None] * k)
    pw_re = mag * jnp.cos(ai[..., None] * k)
    pw_im = mag * jnp.sin(ai[..., None] * k)
    num_re = jnp.expm1(ar) * jnp.cos(ai) - 2.0 * jnp.sin(0.5 * ai) ** 2
    num_im = jnp.exp(ar) * jnp.sin(ai)
    den = lam_re * lam_re + lam_im * lam_im
    f_re = (num_re * lam_re + num_im * lam_im) / den
    f_im = (num_im * lam_re - num_re * lam_im) / den
    bb_re = f_re[..., None] * b_re - f_im[..., None] * b_im
    bb_im = f_re[..., None] * b_im + f_im[..., None] * b_re

    cp_re = c_re[..., None] * pw_re[:, :, None] - c_im[..., None] * pw_im[:, :, None]
    cp_im = c_re[..., None] * pw_im[:, :, None] + c_im[..., None] * pw_re[:, :, None]
    kk = (jnp.einsum('dgipt,dgpj->dgtij', cp_re, bb_re, precision=hp)
          - jnp.einsum('dgipt,dgpj->dgtij', cp_im, bb_im, precision=hp))

    s_idx = jnp.arange(tc)[:, None]
    t_idx = jnp.arange(tc)[None, :]
    lag_f = t_idx - s_idx
    lag_b = s_idx - t_idx
    kf = jnp.where((lag_f >= 0)[None, :, :, None, None], kk[0][:, jnp.clip(lag_f, 0, tc)], 0.0)
    kb = jnp.where((lag_b >= 0)[None, :, :, None, None], kk[1][:, jnp.clip(lag_b, 0, tc)], 0.0)
    dsk = d_skip.reshape(g_, h_)
    eye_t = jnp.eye(tc, dtype=F32)[None, :, :, None, None]
    eye_h = jnp.eye(h_, dtype=F32)[None, None, None]
    m_all = kf + kb + eye_t * eye_h * dsk[:, None, None, :, None]
    m_all = m_all.transpose(0, 1, 4, 2, 3).reshape(g_, tc * h_, tc * h_)

    def state_w(dr, expo):
        e_re = pw_re[dr][:, :, expo]
        e_im = pw_im[dr][:, :, expo]
        w_re = jnp.einsum('gps,gpj->gsjp', e_re, bb_re[dr], precision=hp) - jnp.einsum(
            'gps,gpj->gsjp', e_im, bb_im[dr], precision=hp)
        w_im = jnp.einsum('gps,gpj->gsjp', e_re, bb_im[dr], precision=hp) + jnp.einsum(
            'gps,gpj->gsjp', e_im, bb_re[dr], precision=hp)
        w_re = w_re.reshape(g_, tc * h_, p_)
        w_im = w_im.reshape(g_, tc * h_, p_)
        return jnp.concatenate([w_re, w_im, w_im, w_re], axis=-1)

    ws_f = state_w(0, tc - 1 - jnp.arange(tc))
    ws_b = state_w(1, jnp.arange(tc))
    wcat = jnp.concatenate([m_all, ws_f, ws_b], axis=-1)

    def out_w(dr, expo):
        q_re = cp_re[dr][..., expo]
        q_im = cp_im[dr][..., expo]
        top = q_re.transpose(0, 2, 3, 1).reshape(g_, p_, tc * h_)
        bot = (-q_im).transpose(0, 2, 3, 1).reshape(g_, p_, tc * h_)
        return jnp.concatenate([top, bot], axis=1)

    wout = jnp.concatenate([out_w(0, 1 + jnp.arange(tc)), out_w(1, tc - jnp.arange(tc))], axis=1)

    a_re, a_im = pw_re[..., tc], pw_im[..., tc]
    zeros = jnp.zeros_like(a_re[0])
    rows = []
    for dr in range(2):
        rows += [jnp.concatenate([a_re[dr], a_re[dr]], -1),
                 jnp.concatenate([-a_im[dr], a_im[dr]], -1),
                 jnp.concatenate([a_im[dr], -a_im[dr]], -1)]
    rows += [jnp.concatenate([zeros, zeros], -1)] * 2
    acoef = jnp.stack(rows, axis=1)
    return wcat.astype(BF16), wout.astype(BF16), acoef


def _s5_body(x_ref, wcat_ref, wout_ref, a_ref, y_ref, y_scr, s_scr, h_scr, *, n_ctx, n_lat, nb):
    gb = x_ref.shape[0]
    pw = 2 * S5_STATE
    for g in range(gb):
        r = _dot(x_ref[g], wcat_ref[g])
        y_scr[g] = r[:, : y_scr.shape[-1]]
        s_scr[g] = r[:, y_scr.shape[-1]:]

    a1f, a2f, a2sf = a_ref[:, 0:1, :], a_ref[:, 1:2, :], a_ref[:, 2:3, :]
    a1b, a2b, a2sb = a_ref[:, 3:4, :], a_ref[:, 4:5, :], a_ref[:, 5:6, :]

    first = lax.broadcasted_iota(I32, (gb, 2 * nb, pw), 1) < nb

    def both(v):
        r = pltpu.roll(v, nb, 1)
        return jnp.where(first, v, r), jnp.where(first, r, v)

    def step(pf, pb, carry):
        hf, hfs, hb, hbs = carry
        rf = pl.ds(pl.multiple_of(pf * 2 * nb, 2 * nb), 2 * nb)
        rb = pl.ds(pl.multiple_of(pb * 2 * nb, 2 * nb), 2 * nb)
        sfa, sfb = both(s_scr[:, rf, 0:pw])
        sfsa, sfsb = both(s_scr[:, rf, pw:2 * pw])
        sba, sbb = both(s_scr[:, rb, 2 * pw:3 * pw])
        sbsa, sbsb = both(s_scr[:, rb, 3 * pw:4 * pw])
        hf1 = a1f * hf + a2f * hfs + sfa
        hfs1 = a1f * hfs + a2sf * hf + sfsa
        h_scr[:, rf, 0:pw] = jnp.where(first, hf, hf1)
        hf2 = a1f * hf1 + a2f * hfs1 + sfb
        hfs2 = a1f * hfs1 + a2sf * hf1 + sfsb
        hb1 = a1b * hb + a2b * hbs + sbb
        hbs1 = a1b * hbs + a2sb * hb + sbsb
        h_scr[:, rb, pw:2 * pw] = jnp.where(first, hb1, hb)
        hb2 = a1b * hb1 + a2b * hbs1 + sba
        hbs2 = a1b * hbs1 + a2sb * hb1 + sbsa
        return hf2, hfs2, hb2, hbs2

    z = jnp.zeros((gb, 2 * nb, pw), F32)
    carry = (z, z, z, z)
    pc, pl_ = n_ctx // 2, n_lat // 2
    carry = lax.fori_loop(0, pc, lambda i, c: step(i, pc - 1 - i, c), carry)
    carry = lax.fori_loop(0, pl_, lambda i, c: step(pc + i, pc + pl_ - 1 - i, c), carry)

    lo = n_ctx * nb
    for g in range(gb):
        y = y_scr[g] + _dot(h_scr[g].astype(BF16), wout_ref[g])
        y_ref[g] = y[lo:, :]


def _s5(x2, wcat, wout, acoef, n_ctx, n_lat, nb):
    g_, rows, kdim = x2.shape
    gb = S5_GROUPS_PER_STEP
    pw = 2 * S5_STATE
    out_rows = n_lat * nb
    return pl.pallas_call(
        functools.partial(_s5_body, n_ctx=n_ctx, n_lat=n_lat, nb=nb),
        grid=(g_ // gb,),
        in_specs=[
            pl.BlockSpec((gb, rows, kdim), lambda i: (i, 0, 0)),
            pl.BlockSpec((gb, kdim, wcat.shape[-1]), lambda i: (i, 0, 0)),
            pl.BlockSpec((gb, 2 * pw, kdim), lambda i: (i, 0, 0)),
            pl.BlockSpec((gb, 8, pw), lambda i: (i, 0, 0)),
        ],
        out_specs=pl.BlockSpec((gb, out_rows, kdim), lambda i: (i, 0, 0)),
        out_shape=jax.ShapeDtypeStruct((g_, out_rows, kdim), F32),
        scratch_shapes=[
            pltpu.VMEM((gb, rows, kdim), F32),
            pltpu.VMEM((gb, rows, 4 * pw), F32),
            pltpu.VMEM((gb, rows, 2 * pw), F32),
        ],
        compiler_params=_cparams(("parallel",)),
        name="s5",
    )(x2, wcat, wout, acoef)


def _log_sigmoid(z):
    return jnp.minimum(z, 0.0) - jnp.log1p(jnp.exp(-jnp.abs(z)))


def _gla_body(q_ref, k_ref, v_ref, kc_ref, vc_ref, gk_ref, gkc_ref, up_ref, bias_ref, o_ref, st_ref,
              *, scale):
    c = GLA_CHUNK
    n_lat = q_ref.shape[1] // c
    n_ctx = kc_ref.shape[1] // c
    row = lax.broadcasted_iota(I32, (c, c), 0)
    col = lax.broadcasted_iota(I32, (c, c), 1)
    nt = (((1,), (1,)), ((), ()))
    tn = (((0,), (0,)), ((), ()))

    for dr in range(2):
        keep = (col <= row) if dr == 0 else (col >= row)
        tri = jnp.where(keep, 1.0, 0.0).astype(BF16)
        smask = (col <= row) if dr == 0 else (col > row)
        up = up_ref[dr]
        bias = bias_ref[dr]
        edge = c - 1 if dr == 0 else 0

        def decay(gk):
            z = _dot(gk.astype(BF16), up) + bias
            g = _log_sigmoid(z) * (1.0 / GLA_GATE_NORM)
            g_hi = g.astype(BF16)
            g_lo = (g - g_hi.astype(F32)).astype(BF16)
            bc = _dot(tri, g_hi) + _dot(tri, g_lo)
            return bc, bc[edge:edge + 1, :]

        def update_state(kk, vv, bc, btot):
            kd = (kk.astype(F32) * jnp.exp(btot - bc)).astype(BF16)
            ds_t = lax.dot_general(vv, kd, tn, preferred_element_type=F32)
            st_ref[...] = jnp.exp(btot) * st_ref[...] + ds_t

        def ctx_step(i, _):
            n = i if dr == 0 else n_ctx - 1 - i
            rs = pl.ds(pl.multiple_of(n * c, c), c)
            bc, btot = decay(gkc_ref[0, rs, :])
            update_state(kc_ref[0, rs, :], vc_ref[0, rs, :], bc, btot)
            return 0

        def lat_step(i, _):
            n = i if dr == 0 else n_lat - 1 - i
            rs = pl.ds(pl.multiple_of(n * c, c), c)
            bc, btot = decay(gk_ref[0, rs, :])
            kk = k_ref[0, rs, :]
            vv = v_ref[0, rs, :]
            qb = (q_ref[0, rs, :].astype(F32) * (scale * jnp.exp(bc))).astype(BF16)
            kb = (kk.astype(F32) * jnp.exp(-bc)).astype(BF16)
            scores = lax.dot_general(qb, kb, nt, preferred_element_type=F32)
            scores = jnp.where(smask, scores, 0.0).astype(BF16)
            o = _dot(scores, vv) + lax.dot_general(qb, st_ref[...].astype(BF16), nt,
                                                   preferred_element_type=F32)
            if dr == 0:
                o_ref[0, rs, :] = o
            else:
                o_ref[0, rs, :] = o_ref[0, rs, :] + o
            update_state(kk, vv, bc, btot)
            return 0

        st_ref[...] = jnp.zeros_like(st_ref)
        lax.fori_loop(0, n_ctx, ctx_step, 0)
        lax.fori_loop(0, n_lat, lat_step, 0)


def _gla(p_lat, p_ctx, gk_lat, gk_ctx, up_pad, bias, dk, dv):
    b_, l_, _ = p_lat.shape
    lc = p_ctx.shape[1]
    hh = GLA_HEADS
    s5w = hh * dv
    q0, k0, v0 = s5w // dk, (s5w + hh * dk) // dk, (s5w + 2 * hh * dk) // dv
    kc0, vc0 = s5w // dk, (s5w + hh * dk) // dv
    return pl.pallas_call(
        functools.partial(_gla_body, scale=dk ** -0.5),
        grid=(b_, hh),
        in_specs=[
            pl.BlockSpec((1, l_, dk), lambda b, h: (b, 0, q0 + h)),
            pl.BlockSpec((1, l_, dk), lambda b, h: (b, 0, k0 + h)),
            pl.BlockSpec((1, l_, dv), lambda b, h: (b, 0, v0 + h)),
            pl.BlockSpec((1, lc, dk), lambda b, h: (b, 0, kc0 + h)),
            pl.BlockSpec((1, lc, dv), lambda b, h: (b, 0, vc0 + h)),
            pl.BlockSpec((1, l_, 128), lambda b, h: (b, 0, 0)),
            pl.BlockSpec((1, lc, 128), lambda b, h: (b, 0, 0)),
            pl.BlockSpec((2, 128, dk), lambda b, h: (0, 0, h)),
            pl.BlockSpec((2, 1, dk), lambda b, h: (0, 0, h)),
        ],
        out_specs=pl.BlockSpec((1, l_, dv), lambda b, h: (b, 0, h)),
        out_shape=jax.ShapeDtypeStruct((b_, l_, hh * dv), F32),
        scratch_shapes=[pltpu.VMEM((dv, dk), F32)],
        compiler_params=_cparams(("parallel", "parallel")),
        name="gla",
    )(p_lat, p_lat, p_lat, p_ctx, p_ctx, gk_lat, gk_ctx, up_pad, bias)


def _gelu_tanh(x):
    return 0.5 * x * (1.0 + jnp.tanh(math.sqrt(2.0 / math.pi) * (x + 0.044715 * (x * x * x))))


def _merge_body(ys_ref, yg_ref, go_ref, gs_ref, gg_ref, x_ref, g1_ref, sh2_ref, sc2_ref,
                glu_ref, ws5_ref, wgla_ref, wout_ref, gnw_ref, n2w_ref, rw_ref,
                x1_ref, h2_ref, lg_ref, *, dv):
    a = _gelu_tanh(ys_ref[0])
    a = a * _sigmoid(_dot(a.astype(BF16), glu_ref[...]))
    pa = _dot(a.astype(BF16), ws5_ref[...])

    yg = yg_ref[0]
    parts = []
    for h in range(GLA_HEADS):
        oh = yg[:, h * dv:(h + 1) * dv]
        ms = jnp.mean(oh * oh, axis=-1, keepdims=True)
        parts.append(oh * lax.rsqrt(ms + EPS) * gnw_ref[...])
    gl = jnp.concatenate(parts, axis=-1) * _silu(go_ref[0].astype(F32))
    pb = _dot(gl.astype(BF16), wgla_ref[...])

    m = _sigmoid(gs_ref[0].astype(F32)) * pa + _sigmoid(gg_ref[0].astype(F32)) * pb
    y = _dot(m.astype(BF16), wout_ref[...])
    x1 = x_ref[0] + g1_ref[0, 0] * y
    x1_ref[0] = x1

    h2 = _modulated_norm(x1, n2w_ref[...], sh2_ref[0, 0], sc2_ref[0, 0])
    h_hi = h2.astype(BF16)
    h_lo = (h2 - h_hi.astype(F32)).astype(BF16)
    h2_ref[0] = _pack_bf16_pair(h2)
    lg_ref[0] = _dot(h_hi, rw_ref[0]) + _dot(h_lo, rw_ref[0]) + _dot(h_hi, rw_ref[1])


def _resident(shape):
    nd = len(shape)
    return pl.BlockSpec(shape, lambda b, i: (0,) * nd, pipeline_mode=pl.Buffered(1))


def _merge(ys, yg, p_lat, x, mod4, glu_w, ws5, wgla, wout, gnw, n2w, rw2, dv, tm):
    b_, l_, d = x.shape
    w5 = ys.shape[-1]
    go0 = (p_lat.shape[-1] - 2 * d - w5) // w5
    gs0 = (p_lat.shape[-1] - 2 * d) // d
    modspec = lambda r: pl.BlockSpec((1, 1, 1, d), lambda b, i: (b, r, 0, 0))
    return pl.pallas_call(
        functools.partial(_merge_body, dv=dv),
        grid=(b_, l_ // tm),
        in_specs=[
            pl.BlockSpec((1, tm, w5), lambda b, i: (b, i, 0)),
            pl.BlockSpec((1, tm, w5), lambda b, i: (b, i, 0)),
            pl.BlockSpec((1, tm, w5), lambda b, i: (b, i, go0)),
            pl.BlockSpec((1, tm, d), lambda b, i: (b, i, gs0)),
            pl.BlockSpec((1, tm, d), lambda b, i: (b, i, gs0 + 1)),
            pl.BlockSpec((1, tm, d), lambda b, i: (b, i, 0)),
            modspec(2), modspec(3), modspec(4),
            _resident(glu_w.shape), _resident(ws5.shape), _resident(wgla.shape), _resident(wout.shape),
            _resident(gnw.shape), _resident(n2w.shape), _resident(rw2.shape),
        ],
        out_specs=[
            pl.BlockSpec((1, tm, d), lambda b, i: (b, i, 0)),
            pl.BlockSpec((1, tm, d // 2), lambda b, i: (b, i, 0)),
            pl.BlockSpec((1, tm, 128), lambda b, i: (b, i, 0)),
        ],
        out_shape=[
            jax.ShapeDtypeStruct((b_, l_, d), F32),
            jax.ShapeDtypeStruct((b_, l_, d // 2), U32),
            jax.ShapeDtypeStruct((b_, l_, 128), F32),
        ],
        compiler_params=_cparams(("parallel", "parallel")),
        name="merge",
    )(ys, yg, p_lat, p_lat, p_lat, x, mod4, mod4, mod4, glu_w, ws5, wgla, wout, gnw, n2w, rw2)


def _route_body(lg_ref, bias_ref, eidx_ref, w_ref, rank_ref, cnt_ref, carry_ref):
    ne, tt = lg_ref.shape
    gsz = ne // N_GROUPS
    neg = -jnp.inf

    @pl.when(pl.program_id(0) == 0)
    def _():
        carry_ref[...] = jnp.zeros_like(carry_ref)

    sc = _sigmoid(lg_ref[...])
    ch = sc + bias_ref[...]

    def first_max(v, idx, big):
        m = jnp.max(v, axis=0, keepdims=True)
        i = jnp.min(jnp.where(v == m, idx, big), axis=0, keepdims=True)
        return m, i

    midx = lax.broadcasted_iota(I32, (gsz, tt), 0).astype(F32)
    gs_rows = []
    for g in range(N_GROUPS):
        v = ch[g * gsz:(g + 1) * gsz, :]
        m1, i1 = first_max(v, midx, float(gsz))
        m2 = jnp.max(jnp.where(midx == i1, neg, v), axis=0, keepdims=True)
        gs_rows.append(m1 + m2)
    gscore = jnp.concatenate(gs_rows, axis=0)

    gidx = lax.broadcasted_iota(I32, (N_GROUPS, tt), 0).astype(F32)
    gsel = jnp.zeros((N_GROUPS, tt), F32)
    cur = gscore
    for _ in range(TOPK_GROUPS):
        _, i = first_max(cur, gidx, float(N_GROUPS))
        hit = gidx == i
        gsel = jnp.where(hit, 1.0, gsel)
        cur = jnp.where(hit, neg, cur)

    masked = jnp.concatenate(
        [jnp.where(gsel[g:g + 1, :] > 0.0, ch[g * gsz:(g + 1) * gsz, :], neg) for g in range(N_GROUPS)], axis=0)

    eiota = lax.broadcasted_iota(I32, (ne, tt), 0).astype(F32)
    sel = jnp.zeros((ne, tt), F32)
    cur = masked
    idx_rows, s_rows = [], []
    for _ in range(TOP_K):
        _, i = first_max(cur, eiota, float(ne))
        hit = eiota == i
        idx_rows.append(i)
        s_rows.append(jnp.sum(jnp.where(hit, sc, 0.0), axis=0, keepdims=True))
        sel = jnp.where(hit, 1.0, sel)
        cur = jnp.where(hit, neg, cur)
    idx = jnp.concatenate(idx_rows, axis=0)
    s = jnp.concatenate(s_rows, axis=0)
    w_ref[...] = s / jnp.sum(s, axis=0, keepdims=True) * ROUTED_SCALE
    eidx_ref[...] = idx.astype(I32)

    cw = 256
    a_i = lax.broadcasted_iota(I32, (cw, cw), 0)
    b_i = lax.broadcasted_iota(I32, (cw, cw), 1)
    upper = jnp.where(a_i < b_i, 1.0, 0.0).astype(BF16)
    carry = carry_ref[:, 0:1]
    pieces = []
    for c0 in range(0, tt, cw):
        sc_c = sel[:, c0:c0 + cw]
        pieces.append(_dot(sc_c.astype(BF16), upper) + carry)
        carry = carry + jnp.sum(sc_c, axis=1, keepdims=True)
    rank_full = jnp.concatenate(pieces, axis=1)
    carry_ref[...] = jnp.broadcast_to(carry, carry_ref.shape)
    cnt_ref[...] = jnp.broadcast_to(carry, cnt_ref.shape)
    rk = [jnp.sum(jnp.where(eiota == idx_rows[k], rank_full, 0.0), axis=0, keepdims=True) for k in range(TOP_K)]
    rank_ref[...] = jnp.concatenate(rk, axis=0).astype(I32)


def _route(logits_t, bias):
    ne, t = logits_t.shape
    tt = 1024
    return pl.pallas_call(
        _route_body,
        grid=(t // tt,),
        in_specs=[
            pl.BlockSpec((ne, tt), lambda i: (0, i)),
            pl.BlockSpec((ne, 1), lambda i: (0, 0)),
        ],
        out_specs=[
            pl.BlockSpec((TOP_K, tt), lambda i: (0, i)),
            pl.BlockSpec((TOP_K, tt), lambda i: (0, i)),
            pl.BlockSpec((TOP_K, tt), lambda i: (0, i)),
            pl.BlockSpec((ne, 128), lambda i: (0, 0)),
        ],
        out_shape=[
            jax.ShapeDtypeStruct((TOP_K, t), I32),
            jax.ShapeDtypeStruct((TOP_K, t), F32),
            jax.ShapeDtypeStruct((TOP_K, t), I32),
            jax.ShapeDtypeStruct((ne, 128), F32),
        ],
        scratch_shapes=[pltpu.VMEM((ne, 128), F32)],
        compiler_params=_cparams(("arbitrary",)),
        name="route",
    )(logits_t, bias.reshape(ne, 1))


def _slots_body(cnt_ref, eidx_ref, rank_ref, dest_ref, rows_ref, be_ref, first_ref, nxt_ref, nu_ref,
                off_smem, dchunk, sem):
    ne = cnt_ref.shape[0]
    nb = be_ref.shape[0]
    tl = eidx_ref.shape[1]
    r_ = EXPERT_ROWS
    shift = r_.bit_length() - 1
    c = pl.program_id(0)

    @pl.when(c == 0)
    def _():
        def per_expert(e, off):
            cnt = cnt_ref[e]
            n = lax.shift_right_logical(cnt + (r_ - 1), shift)
            off_smem[e] = off

            def fill(j, _):
                be_ref[off + j] = e
                first_ref[off + j] = jnp.where(j == 0, 1, 0)
                return 0
            lax.fori_loop(0, n, fill, 0)

            def pad(j, _):
                s = off * r_ + j
                rows_ref[lax.shift_right_logical(s, shift), s & (r_ - 1)] = 0
                return 0
            lax.fori_loop(cnt, n * r_, pad, 0)
            return off + n

        n_used = lax.fori_loop(0, ne, per_expert, 0)
        nu_ref[0] = n_used
        last_e = be_ref[n_used - 1]

        def tail(j, _):
            be_ref[j] = last_e
            first_ref[j] = 0

            def zero_row(q, _):
                rows_ref[j, q] = 0
                return 0
            lax.fori_loop(0, r_, zero_row, 0)
            return 0
        lax.fori_loop(n_used, nb, tail, 0)

        def back(i, nx):
            j = nb - 1 - i
            nxt_ref[j] = nx
            return jnp.where(first_ref[j] == 1, be_ref[j], nx)
        lax.fori_loop(0, nb, back, -1)

    ei = eidx_ref[...]
    acc = rank_ref[...]
    for e in range(ne):
        acc = acc + jnp.where(ei == e, off_smem[e] * r_, 0)
    dest_ref[...] = acc
    cp = pltpu.make_async_copy(dest_ref, dchunk, sem)
    cp.start()
    cp.wait()
    base = c * tl

    def scatter(j, _):
        for k in range(TOP_K):
            s = dchunk[k, j]
            rows_ref[lax.shift_right_logical(s, shift), s & (r_ - 1)] = base + j
        return 0
    lax.fori_loop(0, tl, scatter, 0)


def _slots(counts, eidx, rank, nb):
    k_, t = eidx.shape
    tl = 1024
    smem = lambda: pl.BlockSpec(memory_space=pltpu.SMEM)
    chunk = lambda: pl.BlockSpec((k_, tl), lambda i: (0, i))
    return pl.pallas_call(
        _slots_body,
        grid=(t // tl,),
        in_specs=[smem(), chunk(), chunk()],
        out_specs=[chunk(), smem(), smem(), smem(), smem(), smem()],
        out_shape=[
            jax.ShapeDtypeStruct((k_, t), I32),
            jax.ShapeDtypeStruct((nb, EXPERT_ROWS), I32),
            jax.ShapeDtypeStruct((nb,), I32),
            jax.ShapeDtypeStruct((nb,), I32),
            jax.ShapeDtypeStruct((nb,), I32),
            jax.ShapeDtypeStruct((1,), I32),
        ],
        scratch_shapes=[
            pltpu.SMEM((counts.shape[0],), I32),
            pltpu.SMEM((k_, tl), I32),
            pltpu.SemaphoreType.DMA,
        ],
        compiler_params=_cparams(("arbitrary",)),
        name="slots",
    )(counts, eidx, rank)


EXPERT_ISSUE_GROUPS = 4


def _expert_body(be_ref, first_ref, nxt_ref, nu_ref, rows_hbm, x_hbm, wg_hbm, wu_hbm, wd_hbm, y_ref,
                 idx_smem, xbuf, wg_f, wu_f, wd_f, wg_s, wu_s, wd_s, isem, gsem, wsem):
    i = pl.program_id(0)
    n_used = nu_ref[0]
    r_ = EXPERT_ROWS
    slot = i % 2
    nslot = 1 - slot

    def idx_copy(blk, s):
        return pltpu.make_async_copy(rows_hbm.at[blk], idx_smem.at[s], isem.at[s])

    def issue_rows(s, lo, hi):
        for r in range(lo, hi):
            pltpu.make_async_copy(x_hbm.at[pl.ds(idx_smem[s, r], 1), :], xbuf.at[s, pl.ds(r, 1), :],
                                  gsem.at[s]).start(priority=r % 2)

    def wait_rows(s):
        pltpu.make_async_copy(x_hbm.at[pl.ds(0, r_), :], xbuf.at[s], gsem.at[s]).wait()

    def weight_copies(e):
        return (pltpu.make_async_copy(wg_hbm.at[e], wg_f, wsem.at[0]),
                pltpu.make_async_copy(wu_hbm.at[e], wu_f, wsem.at[1]),
                pltpu.make_async_copy(wd_hbm.at[e], wd_f, wsem.at[2]))

    @pl.when(i == 0)
    def _():
        for cp in weight_copies(be_ref[0]):
            cp.start()
        idx_copy(0, 0).start()
        idx_copy(0, 0).wait()
        issue_rows(0, 0, r_)
        idx_copy(1, 1).start()

    @pl.when(first_ref[i] == 1)
    def _():
        for cp in weight_copies(be_ref[i]):
            cp.wait()
        wg_s[...] = wg_f[...].astype(BF16)
        wu_s[...] = wu_f[...].astype(BF16)
        wd_s[...] = wd_f[...].astype(BF16)

        @pl.when(nxt_ref[i] >= 0)
        def _():
            for cp in weight_copies(nxt_ref[i]):
                cp.start()

    @pl.when(i < n_used)
    def _():
        idx_copy(i + 1, nslot).wait()
        wait_rows(slot)
        step = r_ // EXPERT_ISSUE_GROUPS
        lo, hi = _unpack_bf16_pair(xbuf[slot])
        lo = lo.astype(BF16)
        hi = hi.astype(BF16)
        half = lo.shape[-1]
        issue_rows(nslot, 0, step)
        a = _dot(lo, wg_s[0:half, :]) + _dot(hi, wg_s[half:, :])
        issue_rows(nslot, step, 2 * step)
        u = _dot(lo, wu_s[0:half, :]) + _dot(hi, wu_s[half:, :])
        issue_rows(nslot, 2 * step, 3 * step)
        hid = (_silu(a) * u).astype(BF16)
        y = _dot(hid, wd_s[...])
        issue_rows(nslot, 3 * step, r_)
        y_ref[...] = _pack_bf16_pair(y)

        @pl.when(i + 1 < n_used)
        def _():
            idx_copy(i + 2, slot).start()

    @pl.when(i == n_used)
    def _():
        wait_rows(slot)

    @pl.when(i >= n_used)
    def _():
        y_ref[...] = jnp.zeros_like(y_ref)


def _experts(block_e, first, nxt, n_used, rows, h2p, wg, wu, wd):
    nb = block_e.shape[0]
    t, half = h2p.shape
    ne, d, f = wg.shape
    r_ = EXPERT_ROWS
    hbm = lambda: pl.BlockSpec(memory_space=pl.ANY)
    grid_spec = pltpu.PrefetchScalarGridSpec(
        num_scalar_prefetch=4,
        grid=(nb,),
        in_specs=[hbm(), hbm(), hbm(), hbm(), hbm()],
        out_specs=pl.BlockSpec((r_, half), lambda i, *_: (i, 0)),
        scratch_shapes=[
            pltpu.SMEM((2, r_), I32),
            pltpu.VMEM((2, r_, half), U32),
            pltpu.VMEM((d, f), F32),
            pltpu.VMEM((d, f), F32),
            pltpu.VMEM((f, d), F32),
            pltpu.VMEM((d, f), BF16),
            pltpu.VMEM((d, f), BF16),
            pltpu.VMEM((f, d), BF16),
            pltpu.SemaphoreType.DMA((2,)),
            pltpu.SemaphoreType.DMA((2,)),
            pltpu.SemaphoreType.DMA((3,)),
        ],
    )
    return pl.pallas_call(
        _expert_body,
        grid_spec=grid_spec,
        out_shape=jax.ShapeDtypeStruct((nb * r_, half), U32),
        compiler_params=_cparams(("arbitrary",)),
        name="experts",
    )(block_e, first, nxt, n_used, rows, h2p, wg, wu, wd)


def _combine_body(dest_hbm, y_hbm, w_ref, x1_ref, h2_ref, g2_ref, sg_ref, su_ref, sd_ref, fnw_ref, o_ref,
                  idx_smem, ybuf, isem, gsem):
    i = pl.program_id(0)
    n = pl.num_programs(0)
    tm = x1_ref.shape[0]
    nrow = TOP_K * tm
    slot = i % 2
    nslot = 1 - slot

    def idx_copy(blk, s):
        cols = pl.ds(pl.multiple_of(blk * tm, tm), tm)
        return pltpu.make_async_copy(dest_hbm.at[:, cols], idx_smem.at[s], isem.at[s])

    def issue_rows(s):
        def body(j, _):
            for k in range(TOP_K):
                pltpu.make_async_copy(y_hbm.at[pl.ds(idx_smem[s, k, j], 1), :],
                                      ybuf.at[s, pl.ds(k * tm + j, 1), :], gsem.at[s]).start(priority=k % 2)
            return 0
        lax.fori_loop(0, tm, body, 0)

    def wait_rows(s):
        pltpu.make_async_copy(y_hbm.at[pl.ds(0, nrow), :], ybuf.at[s], gsem.at[s]).wait()

    @pl.when(i == 0)
    def _():
        idx_copy(0, 0).start()
        idx_copy(0, 0).wait()
        issue_rows(0)

        @pl.when(n > 1)
        def _():
            idx_copy(1, 1).start()

    @pl.when(i + 1 < n)
    def _():
        idx_copy(i + 1, nslot).wait()
        issue_rows(nslot)

    wait_rows(slot)

    @pl.when(i + 2 < n)
    def _():
        idx_copy(i + 2, slot).start()

    w = w_ref[...]
    half = ybuf.shape[-1]
    r_lo = jnp.zeros((tm, half), F32)
    r_hi = jnp.zeros((tm, half), F32)
    for k in range(TOP_K):
        lo, hi = _unpack_bf16_pair(ybuf[slot, k * tm:(k + 1) * tm, :])
        wk = w[:, k:k + 1]
        r_lo = r_lo + wk * lo
        r_hi = r_hi + wk * hi
    routed = jnp.concatenate([r_lo, r_hi], axis=-1)

    x_lo, x_hi = _unpack_bf16_pair(h2_ref[...])
    x_lo = x_lo.astype(BF16)
    x_hi = x_hi.astype(BF16)
    a = _dot(x_lo, sg_ref[0:half, :]) + _dot(x_hi, sg_ref[half:, :])
    u = _dot(x_lo, su_ref[0:half, :]) + _dot(x_hi, su_ref[half:, :])
    shared = _dot((_silu(a) * u).astype(BF16), sd_ref[...])

    x2 = x1_ref[...] + g2_ref[0, 0] * (routed + shared)
    ms = jnp.mean(x2 * x2, axis=-1, keepdims=True)
    o_ref[...] = x2 * lax.rsqrt(ms + EPS) * fnw_ref[...]


def _combine(dest, y_sorted, w_tok, x1, h2p, mod4, sg, su, sd, fnw, tiles_per_batch, tm):
    t, d = x1.shape
    half = d // 2
    nrow = TOP_K * tm
    res = lambda shape: pl.BlockSpec(shape, lambda i: (0,) * len(shape), pipeline_mode=pl.Buffered(1))
    return pl.pallas_call(
        _combine_body,
        grid=(t // tm,),
        in_specs=[
            pl.BlockSpec(memory_space=pl.ANY),
            pl.BlockSpec(memory_space=pl.ANY),
            pl.BlockSpec((tm, TOP_K), lambda i: (i, 0)),
            pl.BlockSpec((tm, d), lambda i: (i, 0)),
            pl.BlockSpec((tm, half), lambda i: (i, 0)),
            pl.BlockSpec((1, 1, 1, d), lambda i: (i // tiles_per_batch, 5, 0, 0)),
            res(sg.shape), res(su.shape), res(sd.shape), res(fnw.shape),
        ],
        out_specs=pl.BlockSpec((tm, d), lambda i: (i, 0)),
        out_shape=jax.ShapeDtypeStruct((t, d), F32),
        scratch_shapes=[
            pltpu.SMEM((2, TOP_K, tm), I32),
            pltpu.VMEM((2, nrow, half), U32),
            pltpu.SemaphoreType.DMA((2,)),
            pltpu.SemaphoreType.DMA((2,)),
        ],
        compiler_params=_cparams(("arbitrary",)),
        name="combine",
    )(dest, y_sorted, w_tok, x1, h2p, mod4, sg, su, sd, fnw)


def kernel(x, c, ctx, c_ctx, ada_w, ada_b, norm1_w, norm2_w, w_in, s5_lam_re, s5_lam_im, s5_log_dt,
           s5_b_re, s5_b_im, s5_c_re, s5_c_im, s5_d, s5_glu_w, gla_gk_up, gla_gk_b, gla_norm_w,
           w_s5_proj, w_gla_proj, w_out, router_w, router_bias, exp_w_gate, exp_w_up, exp_w_down,
           sh_w_gate, sh_w_up, sh_w_down, final_norm_w):
    depth = ada_w.shape[0]
    assert depth == 1, "single-layer block: context outputs are never consumed"
    b_, l_, d = x.shape
    lc = ctx.shape[1]
    s5w = s5_d.shape[1]
    kw = gla_gk_up.shape[-1]
    vw = w_gla_proj.shape[1]
    dk, dv = kw // GLA_HEADS, vw // GLA_HEADS
    rank2 = 2 * GLA_GATE_RANK
    li = 0

    c8 = jnp.zeros((8, d), F32).at[:b_].set(c).at[b_].set(c_ctx)
    mod4 = _ada(c8, ada_w[li], ada_b[li]).reshape(8, N_MOD, 1, d)

    cuts = [0, s5w, s5w + kw, s5w + 2 * kw, s5w + 2 * kw + vw, s5w + 2 * kw + 2 * vw]
    c_gkd = cuts[5]
    w = w_in[li]
    w_main = jnp.concatenate([w[:, :c_gkd], w[:, c_gkd + rank2:]], axis=1).astype(BF16)
    w_ctx = jnp.concatenate([w[:, cuts[0]:cuts[1]], w[:, cuts[2]:cuts[4]]], axis=1).astype(BF16)
    w_gkd = jnp.zeros((d, 128), F32).at[:, :rank2].set(w[:, c_gkd:c_gkd + rank2]).astype(BF16)
    nw1 = norm1_w[li].reshape(1, d)
    p_lat, gk_lat = _inproj(x, mod4, lambda b: b, nw1, w_main, w_gkd, tm=512, tn=1024)
    p_ctx, gk_ctx = _inproj(ctx, mod4, lambda b: b_, nw1, w_ctx, w_gkd, tm=lc, tn=w_ctx.shape[1] // 2)

    g_ = s5w // S5_GROUP
    tc = S5_CHUNK
    n_ctx, n_lat = lc // tc, l_ // tc
    u_all = jnp.concatenate([p_ctx[:, :, :s5w], p_lat[:, :, :s5w]], axis=1)
    x2 = u_all.reshape(b_, n_ctx + n_lat, tc, g_, S5_GROUP).transpose(3, 1, 0, 2, 4)
    x2 = x2.reshape(g_, (n_ctx + n_lat) * b_, tc * S5_GROUP)
    wcat, wout5, acoef = _s5_weights(s5_lam_re[li], s5_lam_im[li], s5_log_dt[li], s5_b_re[li], s5_b_im[li],
                                     s5_c_re[li], s5_c_im[li], s5_d[li])
    y2 = _s5(x2, wcat, wout5, acoef, n_ctx, n_lat, b_)
    ys = y2.reshape(g_, n_lat, b_, tc, S5_GROUP).transpose(2, 1, 3, 0, 4).reshape(b_, l_, s5w)

    up_pad = jnp.zeros((2, 128, kw), F32)
    up_pad = up_pad.at[0, :GLA_GATE_RANK].set(gla_gk_up[li, 0]).at[1, GLA_GATE_RANK:rank2].set(gla_gk_up[li, 1])
    yg = _gla(p_lat, p_ctx, gk_lat, gk_ctx, up_pad.astype(BF16), gla_gk_b[li].reshape(2, 1, kw), dk, dv)

    rw = jnp.zeros((d, 128), F32).at[:, :N_EXPERTS].set(router_w[li])
    rw_hi = rw.astype(BF16)
    rw2 = jnp.stack([rw_hi, (rw - rw_hi.astype(F32)).astype(BF16)])
    x1, h2p, logits = _merge(
        ys, yg, p_lat, x, mod4, s5_glu_w[li].astype(BF16), w_s5_proj[li].astype(BF16),
        w_gla_proj[li].astype(BF16), w_out[li].astype(BF16), gla_norm_w[li].reshape(1, dv),
        norm2_w[li].reshape(1, d), rw2, dv, tm=256)

    t = b_ * l_
    logits_t = logits.reshape(t, 128)[:, :N_EXPERTS].T
    eidx, wts, rank, cnt = _route(logits_t, router_bias[li])
    nb = (t * TOP_K + N_EXPERTS * (EXPERT_ROWS - 1) + EXPERT_ROWS - 1) // EXPERT_ROWS + 1
    dest, rows, block_e, first, nxt, n_used = _slots(cnt[:, 0].astype(I32), eidx, rank, nb)

    h2p_flat = h2p.reshape(t, d // 2)
    y_sorted = _experts(block_e, first, nxt, n_used, rows, h2p_flat,
                        exp_w_gate[li], exp_w_up[li], exp_w_down[li])

    tm_c = 256
    out = _combine(dest, y_sorted, wts.T, x1.reshape(t, d), h2p_flat, mod4,
                   sh_w_gate[li].astype(BF16), sh_w_up[li].astype(BF16), sh_w_down[li].astype(BF16),
                   final_norm_w.reshape(1, d), l_ // tm_c, tm_c)
    return out.reshape(b_, l_, d)
```

```python
import functools
import math

import jax
import jax.numpy as jnp
from jax import lax
from jax.experimental import pallas as pl
from jax.experimental.pallas import tpu as pltpu

F32 = jnp.float32
BF16 = jnp.bfloat16
U32 = jnp.uint32
I32 = jnp.int32

EPS = 1e-6
N_MOD = 6

S5_GROUP = 16
S5_STATE = 64
S5_CHUNK = 16
S5_GROUPS_PER_STEP = 8

GLA_HEADS = 4
GLA_GATE_RANK = 16
GLA_GATE_NORM = 16.0
GLA_CHUNK = 64

N_EXPERTS = 64
TOP_K = 8
N_GROUPS = 8
TOPK_GROUPS = 4
ROUTED_SCALE = 2.5
EXPERT_ROWS = 256

VMEM_LIMIT = 56 * 1024 * 1024


def _cparams(sem, vmem=VMEM_LIMIT):
    return pltpu.CompilerParams(dimension_semantics=sem, vmem_limit_bytes=vmem)


def _dot(a, b):
    return jnp.dot(a, b, preferred_element_type=F32)


def _sigmoid(x):
    return 1.0 / (1.0 + jnp.exp(-x))


def _silu(x):
    return x * _sigmoid(x)


def _pack_bf16_pair(x):
    n = x.shape[-1] // 2
    bits = lax.bitcast_convert_type(x.astype(BF16).astype(F32), U32)
    return (bits[:, :n] >> 16) | (bits[:, n:] & jnp.uint32(0xFFFF0000))


def _unpack_bf16_pair(w):
    lo = lax.bitcast_convert_type(w << 16, F32)
    hi = lax.bitcast_convert_type(w & jnp.uint32(0xFFFF0000), F32)
    return lo, hi


def _ada_body(c_ref, w_ref, b_ref, o_ref):
    c = c_ref[...]
    s = _silu(c).astype(BF16)
    o_ref[...] = _dot(s, w_ref[...].astype(BF16)) + b_ref[...]


def _ada(c8, ada_w, ada_b):
    d, n = ada_w.shape
    tn = 512
    return pl.pallas_call(
        _ada_body,
        grid=(n // tn,),
        in_specs=[
            pl.BlockSpec((8, d), lambda j: (0, 0)),
            pl.BlockSpec((d, tn), lambda j: (0, j)),
            pl.BlockSpec((1, tn), lambda j: (0, j)),
        ],
        out_specs=pl.BlockSpec((8, tn), lambda j: (0, j)),
        out_shape=jax.ShapeDtypeStruct((8, n), F32),
        compiler_params=_cparams(("parallel",)),
        name="ada",
    )(c8, ada_w, ada_b.reshape(1, n))


def _modulated_norm(x, nw, sh, sc):
    ms = jnp.mean(x * x, axis=-1, keepdims=True)
    y = x * lax.rsqrt(ms + EPS) * nw
    return y * (1.0 + sc) + sh


def _inproj_body(x_ref, sh_ref, sc_ref, nw_ref, w_ref, wg_ref, p_ref, g_ref, h_scr):
    @pl.when(pl.program_id(2) == 0)
    def _():
        h = _modulated_norm(x_ref[0], nw_ref[...], sh_ref[0, 0], sc_ref[0, 0]).astype(BF16)
        h_scr[...] = h
        g_ref[0] = _dot(h, wg_ref[...])

    p_ref[0] = _dot(h_scr[...], w_ref[...]).astype(BF16)


def _inproj(x, mod4, mod_row, nw, w, wg, tm, tn):
    b_, l_, d = x.shape
    n = w.shape[1]
    return pl.pallas_call(
        _inproj_body,
        grid=(b_, l_ // tm, n // tn),
        in_specs=[
            pl.BlockSpec((1, tm, d), lambda b, i, j: (b, i, 0)),
            pl.BlockSpec((1, 1, 1, d), lambda b, i, j: (mod_row(b), 0, 0, 0)),
            pl.BlockSpec((1, 1, 1, d), lambda b, i, j: (mod_row(b), 1, 0, 0)),
            pl.BlockSpec((1, d), lambda b, i, j: (0, 0)),
            pl.BlockSpec((d, tn), lambda b, i, j: (0, j)),
            pl.BlockSpec((d, 128), lambda b, i, j: (0, 0)),
        ],
        out_specs=[
            pl.BlockSpec((1, tm, tn), lambda b, i, j: (b, i, j)),
            pl.BlockSpec((1, tm, 128), lambda b, i, j: (b, i, 0)),
        ],
        out_shape=[
            jax.ShapeDtypeStruct((b_, l_, n), BF16),
            jax.ShapeDtypeStruct((b_, l_, 128), F32),
        ],
        scratch_shapes=[pltpu.VMEM((tm, d), BF16)],
        compiler_params=_cparams(("parallel", "parallel", "arbitrary")),
        name="inproj",
    )(x, mod4, mod4, nw, w, wg)


def _s5_weights(lam_re, lam_im, log_dt, b_re, b_im, c_re, c_im, d_skip):
    hp = lax.Precision.HIGHEST
    tc = S5_CHUNK
    g_, p_ = lam_re.shape[1:]
    h_ = b_re.shape[-1]
    dt = jnp.exp(log_dt)[..., None]
    ar, ai = lam_re * dt, lam_im * dt
    k = jnp.arange(tc + 1, dtype=F32)
    mag = jnp.exp(ar[..., None] * k)
    pw_re = mag * jnp.cos(ai[..., None] * k)
    pw_im = mag * jnp.sin(ai[..., None] * k)
    num_re = jnp.expm1(ar) * jnp.cos(ai) - 2.0 * jnp.sin(0.5 * ai) ** 2
    num_im = jnp.exp(ar) * jnp.sin(ai)
    den = lam_re * lam_re + lam_im * lam_im
    f_re = (num_re * lam_re + num_im * lam_im) / den
    f_im = (num_im * lam_re - num_re * lam_im) / den
    bb_re = f_re[..., None] * b_re - f_im[..., None] * b_im
    bb_im = f_re[..., None] * b_im + f_im[..., None] * b_re

    cp_re = c_re[..., None] * pw_re[:, :, None] - c_im[..., None] * pw_im[:, :, None]
    cp_im = c_re[..., None] * pw_im[:, :, None] + c_im[..., None] * pw_re[:, :, None]
    kk = (jnp.einsum('dgipt,dgpj->dgtij', cp_re, bb_re, precision=hp)
          - jnp.einsum('dgipt,dgpj->dgtij', cp_im, bb_im, precision=hp))

    s_idx = jnp.arange(tc)[:, None]
    t_idx = jnp.arange(tc)[None, :]
    lag_f = t_idx - s_idx
    lag_b = s_idx - t_idx
    kf = jnp.where((lag_f >= 0)[None, :, :, None, None], kk[0][:, jnp.clip(lag_f, 0, tc)], 0.0)
    kb = jnp.where((lag_b >= 0)[None, :, :, None, None], kk[1][:, jnp.clip(lag_b, 0, tc)], 0.0)
    dsk = d_skip.reshape(g_, h_)
    eye_t = jnp.eye(tc, dtype=F32)[None, :, :, None, None]
    eye_h = jnp.eye(h_, dtype=F32)[None, None, None]
    m_all = kf + kb + eye_t * eye_h * dsk[:, None, None, :, None]
    m_all = m_all.transpose(0, 1, 4, 2, 3).reshape(g_, tc * h_, tc * h_)

    def state_w(dr, expo):
        e_re = pw_re[dr][:, :, expo]
        e_im = pw_im[dr][:, :, expo]
        w_re = jnp.einsum('gps,gpj->gsjp', e_re, bb_re[dr], precision=hp) - jnp.einsum(
            'gps,gpj->gsjp', e_im, bb_im[dr], precision=hp)
        w_im = jnp.einsum('gps,gpj->gsjp', e_re, bb_im[dr], precision=hp) + jnp.einsum(
            'gps,gpj->gsjp', e_im, bb_re[dr], precision=hp)
        w_re = w_re.reshape(g_, tc * h_, p_)
        w_im = w_im.reshape(g_, tc * h_, p_)
        return jnp.concatenate([w_re, w_im, w_im, w_re], axis=-1)

    ws_f = state_w(0, tc - 1 - jnp.arange(tc))
    ws_b = state_w(1, jnp.arange(tc))
    wcat = jnp.concatenate([m_all, ws_f, ws_b], axis=-1)

    def out_w(dr, expo):
        q_re = cp_re[dr][..., expo]
        q_im = cp_im[dr][..., expo]
        top = q_re.transpose(0, 2, 3, 1).reshape(g_, p_, tc * h_)
        bot = (-q_im).transpose(0, 2, 3, 1).reshape(g_, p_, tc * h_)
        return jnp.concatenate([top, bot], axis=1)

    wout = jnp.concatenate([out_w(0, 1 + jnp.arange(tc)), out_w(1, tc - jnp.arange(tc))], axis=1)

    a_re, a_im = pw_re[..., tc], pw_im[..., tc]
    zeros = jnp.zeros_like(a_re[0])
    rows = []
    for dr in range(2):
        rows += [jnp.concatenate([a_re[dr], a_re[dr]], -1),
                 jnp.concatenate([-a_im[dr], a_im[dr]], -1),
                 jnp.concatenate([a_im[dr], -a_im[dr]], -1)]
    rows += [jnp.concatenate([zeros, zeros], -1)] * 2
    acoef = jnp.stack(rows, axis=1)
    return wcat.astype(BF16), wout.astype(BF16), acoef


def _s5_body(x_ref, wcat_ref, wout_ref, a_ref, y_ref, y_scr, s_scr, h_scr, *, n_ctx, n_lat, nb):
    gb = x_ref.shape[0]
    pw = 2 * S5_STATE
    for g in range(gb):
        r = _dot(x_ref[g], wcat_ref[g])
        y_scr[g] = r[:, : y_scr.shape[-1]]
        s_scr[g] = r[:, y_scr.shape[-1]:]

    a1f, a2f, a2sf = a_ref[:, 0:1, :], a_ref[:, 1:2, :], a_ref[:, 2:3, :]
    a1b, a2b, a2sb = a_ref[:, 3:4, :], a_ref[:, 4:5, :], a_ref[:, 5:6, :]

    first = lax.broadcasted_iota(I32, (gb, 2 * nb, pw), 1) < nb

    def both(v):
        r = pltpu.roll(v, nb, 1)
        return jnp.where(first, v, r), jnp.where(first, r, v)

    def step(pf, pb, carry):
        hf, hfs, hb, hbs = carry
        rf = pl.ds(pl.multiple_of(pf * 2 * nb, 2 * nb), 2 * nb)
        rb = pl.ds(pl.multiple_of(pb * 2 * nb, 2 * nb), 2 * nb)
        sfa, sfb = both(s_scr[:, rf, 0:pw])
        sfsa, sfsb = both(s_scr[:, rf, pw:2 * pw])
        sba, sbb = both(s_scr[:, rb, 2 * pw:3 * pw])
        sbsa, sbsb = both(s_scr[:, rb, 3 * pw:4 * pw])
        hf1 = a1f * hf + a2f * hfs + sfa
        hfs1 = a1f * hfs + a2sf * hf + sfsa
        h_scr[:, rf, 0:pw] = jnp.where(first, hf, hf1)
        hf2 = a1f * hf1 + a2f * hfs1 + sfb
        hfs2 = a1f * hfs1 + a2sf * hf1 + sfsb
        hb1 = a1b * hb + a2b * hbs + sbb
        hbs1 = a1b * hbs + a2sb * hb + sbsb
        h_scr[:, rb, pw:2 * pw] = jnp.where(first, hb1, hb)
        hb2 = a1b * hb1 + a2b * hbs1 + sba
        hbs2 = a1b * hbs1 + a2sb * hb1 + sbsa
        return hf2, hfs2, hb2, hbs2

    z = jnp.zeros((gb, 2 * nb, pw), F32)
    carry = (z, z, z, z)
    pc, pl_ = n_ctx // 2, n_lat // 2
    carry = lax.fori_loop(0, pc, lambda i, c: step(i, pc - 1 - i, c), carry)
    carry = lax.fori_loop(0, pl_, lambda i, c: step(pc + i, pc + pl_ - 1 - i, c), carry)

    lo = n_ctx * nb
    for g in range(gb):
        y = y_scr[g] + _dot(h_scr[g].astype(BF16), wout_ref[g])
        y_ref[g] = y[lo:, :]


def _s5(x2, wcat, wout, acoef, n_ctx, n_lat, nb):
    g_, rows, kdim = x2.shape
    gb = S5_GROUPS_PER_STEP
    pw = 2 * S5_STATE
    out_rows = n_lat * nb
    return pl.pallas_call(
        functools.partial(_s5_body, n_ctx=n_ctx, n_lat=n_lat, nb=nb),
        grid=(g_ // gb,),
        in_specs=[
            pl.BlockSpec((gb, rows, kdim), lambda i: (i, 0, 0)),
            pl.BlockSpec((gb, kdim, wcat.shape[-1]), lambda i: (i, 0, 0)),
            pl.BlockSpec((gb, 2 * pw, kdim), lambda i: (i, 0, 0)),
            pl.BlockSpec((gb, 8, pw), lambda i: (i, 0, 0)),
        ],
        out_specs=pl.BlockSpec((gb, out_rows, kdim), lambda i: (i, 0, 0)),
        out_shape=jax.ShapeDtypeStruct((g_, out_rows, kdim), F32),
        scratch_shapes=[
            pltpu.VMEM((gb, rows, kdim), F32),
            pltpu.VMEM((gb, rows, 4 * pw), F32),
            pltpu.VMEM((gb, rows, 2 * pw), F32),
        ],
        compiler_params=_cparams(("parallel",)),
        name="s5",
    )(x2, wcat, wout, acoef)


def _log_sigmoid(z):
    return jnp.minimum(z, 0.0) - jnp.log1p(jnp.exp(-jnp.abs(z)))


def _chunk_cumsum(g, pos, reverse):
    n = g.shape[0]
    x = g
    sh = 1
    while sh < GLA_CHUNK:
        if reverse:
            x = x + jnp.where(pos < GLA_CHUNK - sh, pltpu.roll(x, n - sh, 0), 0.0)
        else:
            x = x + jnp.where(pos >= sh, pltpu.roll(x, sh, 0), 0.0)
        sh *= 2
    return x


def _gla_body(q_ref, k_ref, v_ref, kc_ref, vc_ref, gk_ref, gkc_ref, up_ref, bias_ref, o_ref,
              bc_scr, kb_scr, qb_scr, st_ref, *, scale):
    c = GLA_CHUNK
    l_ = q_ref.shape[1]
    lc = kc_ref.shape[1]
    n_lat, n_ctx = l_ // c, lc // c
    row = lax.broadcasted_iota(I32, (c, c), 0)
    col = lax.broadcasted_iota(I32, (c, c), 1)
    nt = (((1,), (1,)), ((), ()))
    tn = (((0,), (0,)), ((), ()))

    for dr in range(2):
        smask = (col <= row) if dr == 0 else (col > row)
        up = up_ref[dr]
        bias = bias_ref[dr]
        edge_tile, edge_row = (c - 8, 7) if dr == 0 else (0, 0)

        for gk_r, k_r, base, n, latent in ((gkc_ref, kc_ref, 0, lc, False), (gk_ref, k_ref, lc, l_, True)):
            z = _dot(gk_r[0].astype(BF16), up) + bias
            g = _log_sigmoid(z) * (1.0 / GLA_GATE_NORM)
            pos = lax.broadcasted_iota(I32, g.shape, 0) & (c - 1)
            bc = _chunk_cumsum(g, pos, reverse=(dr == 1))
            bc_scr[base:base + n, :] = bc
            kb_scr[base:base + n, :] = k_r[0].astype(F32) * jnp.exp(-bc)
            if latent:
                qb_scr[...] = (q_ref[0].astype(F32) * (scale * jnp.exp(bc))).astype(BF16)

        def chunk_rows(n, base):
            r0 = pl.multiple_of(n * c, c)
            tot = bc_scr[pl.ds(base + r0 + edge_tile, 8), :][edge_row:edge_row + 1, :]
            return r0, pl.ds(base + r0, c), jnp.exp(tot)

        def update_state(kb, vv, decay):
            kd = (kb * decay).astype(BF16)
            ds_t = lax.dot_general(vv, kd, tn, preferred_element_type=F32)
            st_ref[...] = decay * st_ref[...] + ds_t

        def ctx_step(i, _):
            n = i if dr == 0 else n_ctx - 1 - i
            r0, rs, decay = chunk_rows(n, 0)
            update_state(kb_scr[rs, :], vc_ref[0, pl.ds(r0, c), :], decay)
            return 0

        def lat_step(i, _):
            n = i if dr == 0 else n_lat - 1 - i
            r0, rs, decay = chunk_rows(n, lc)
            ro = pl.ds(r0, c)
            kb = kb_scr[rs, :]
            vv = v_ref[0, ro, :]
            qb = qb_scr[ro, :]
            scores = lax.dot_general(qb, kb.astype(BF16), nt, preferred_element_type=F32)
            scores = jnp.where(smask, scores, 0.0).astype(BF16)
            o = _dot(scores, vv) + lax.dot_general(qb, st_ref[...].astype(BF16), nt,
                                                   preferred_element_type=F32)
            if dr == 0:
                o_ref[0, ro, :] = o
            else:
                o_ref[0, ro, :] = o_ref[0, ro, :] + o
            update_state(kb, vv, decay)
            return 0

        st_ref[...] = jnp.zeros_like(st_ref)
        lax.fori_loop(0, n_ctx, ctx_step, 0, unroll=2)
        lax.fori_loop(0, n_lat, lat_step, 0, unroll=4)


def _gla(p_lat, p_ctx, gk_lat, gk_ctx, up_pad, bias, dk, dv):
    b_, l_, _ = p_lat.shape
    lc = p_ctx.shape[1]
    hh = GLA_HEADS
    s5w = hh * dv
    q0, k0, v0 = s5w // dk, (s5w + hh * dk) // dk, (s5w + 2 * hh * dk) // dv
    kc0, vc0 = s5w // dk, (s5w + hh * dk) // dv
    return pl.pallas_call(
        functools.partial(_gla_body, scale=dk ** -0.5),
        grid=(b_, hh),
        in_specs=[
            pl.BlockSpec((1, l_, dk), lambda b, h: (b, 0, q0 + h)),
            pl.BlockSpec((1, l_, dk), lambda b, h: (b, 0, k0 + h)),
            pl.BlockSpec((1, l_, dv), lambda b, h: (b, 0, v0 + h)),
            pl.BlockSpec((1, lc, dk), lambda b, h: (b, 0, kc0 + h)),
            pl.BlockSpec((1, lc, dv), lambda b, h: (b, 0, vc0 + h)),
            pl.BlockSpec((1, l_, 128), lambda b, h: (b, 0, 0)),
            pl.BlockSpec((1, lc, 128), lambda b, h: (b, 0, 0)),
            pl.BlockSpec((2, 128, dk), lambda b, h: (0, 0, h)),
            pl.BlockSpec((2, 1, dk), lambda b, h: (0, 0, h)),
        ],
        out_specs=pl.BlockSpec((1, l_, dv), lambda b, h: (b, 0, h)),
        out_shape=jax.ShapeDtypeStruct((b_, l_, hh * dv), F32),
        scratch_shapes=[
            pltpu.VMEM((lc + l_, dk), F32),
            pltpu.VMEM((lc + l_, dk), F32),
            pltpu.VMEM((l_, dk), BF16),
            pltpu.VMEM((dv, dk), F32),
        ],
        compiler_params=_cparams(("parallel", "parallel")),
        name="gla",
    )(p_lat, p_lat, p_lat, p_ctx, p_ctx, gk_lat, gk_ctx, up_pad, bias)


def _gelu_tanh(x):
    return 0.5 * x * (1.0 + jnp.tanh(math.sqrt(2.0 / math.pi) * (x + 0.044715 * (x * x * x))))


def _merge_body(ys_ref, yg_ref, go_ref, gs_ref, gg_ref, x_ref, g1_ref, sh2_ref, sc2_ref,
                glu_ref, ws5_ref, wgla_ref, wout_ref, gnw_ref, n2w_ref, rw_ref,
                x1_ref, h2_ref, lg_ref, *, dv):
    a = _gelu_tanh(ys_ref[0])
    a = a * _sigmoid(_dot(a.astype(BF16), glu_ref[...]))
    pa = _dot(a.astype(BF16), ws5_ref[...])

    yg = yg_ref[0]
    parts = []
    for h in range(GLA_HEADS):
        oh = yg[:, h * dv:(h + 1) * dv]
        ms = jnp.mean(oh * oh, axis=-1, keepdims=True)
        parts.append(oh * lax.rsqrt(ms + EPS) * gnw_ref[...])
    gl = jnp.concatenate(parts, axis=-1) * _silu(go_ref[0].astype(F32))
    pb = _dot(gl.astype(BF16), wgla_ref[...])

    m = _sigmoid(gs_ref[0].astype(F32)) * pa + _sigmoid(gg_ref[0].astype(F32)) * pb
    y = _dot(m.astype(BF16), wout_ref[...])
    x1 = x_ref[0] + g1_ref[0, 0] * y
    x1_ref[0] = x1

    h2 = _modulated_norm(x1, n2w_ref[...], sh2_ref[0, 0], sc2_ref[0, 0])
    h_hi = h2.astype(BF16)
    h_lo = (h2 - h_hi.astype(F32)).astype(BF16)
    h2_ref[0] = _pack_bf16_pair(h2)
    lg_ref[0] = _dot(h_hi, rw_ref[0]) + _dot(h_lo, rw_ref[0]) + _dot(h_hi, rw_ref[1])


def _resident(shape):
    nd = len(shape)
    return pl.BlockSpec(shape, lambda b, i: (0,) * nd, pipeline_mode=pl.Buffered(1))


def _merge(ys, yg, p_lat, x, mod4, glu_w, ws5, wgla, wout, gnw, n2w, rw2, dv, tm):
    b_, l_, d = x.shape
    w5 = ys.shape[-1]
    go0 = (p_lat.shape[-1] - 2 * d - w5) // w5
    gs0 = (p_lat.shape[-1] - 2 * d) // d
    modspec = lambda r: pl.BlockSpec((1, 1, 1, d), lambda b, i: (b, r, 0, 0))
    return pl.pallas_call(
        functools.partial(_merge_body, dv=dv),
        grid=(b_, l_ // tm),
        in_specs=[
            pl.BlockSpec((1, tm, w5), lambda b, i: (b, i, 0)),
            pl.BlockSpec((1, tm, w5), lambda b, i: (b, i, 0)),
            pl.BlockSpec((1, tm, w5), lambda b, i: (b, i, go0)),
            pl.BlockSpec((1, tm, d), lambda b, i: (b, i, gs0)),
            pl.BlockSpec((1, tm, d), lambda b, i: (b, i, gs0 + 1)),
            pl.BlockSpec((1, tm, d), lambda b, i: (b, i, 0)),
            modspec(2), modspec(3), modspec(4),
            _resident(glu_w.shape), _resident(ws5.shape), _resident(wgla.shape), _resident(wout.shape),
            _resident(gnw.shape), _resident(n2w.shape), _resident(rw2.shape),
        ],
        out_specs=[
            pl.BlockSpec((1, tm, d), lambda b, i: (b, i, 0)),
            pl.BlockSpec((1, tm, d // 2), lambda b, i: (b, i, 0)),
            pl.BlockSpec((1, tm, 128), lambda b, i: (b, i, 0)),
        ],
        out_shape=[
            jax.ShapeDtypeStruct((b_, l_, d), F32),
            jax.ShapeDtypeStruct((b_, l_, d // 2), U32),
            jax.ShapeDtypeStruct((b_, l_, 128), F32),
        ],
        compiler_params=_cparams(("parallel", "parallel")),
        name="merge",
    )(ys, yg, p_lat, p_lat, p_lat, x, mod4, mod4, mod4, glu_w, ws5, wgla, wout, gnw, n2w, rw2)


def _route_body(lg_ref, bias_ref, eidx_ref, w_ref, rank_ref, cnt_ref, carry_ref):
    ne, tt = lg_ref.shape
    gsz = ne // N_GROUPS
    neg = -jnp.inf

    @pl.when(pl.program_id(0) == 0)
    def _():
        carry_ref[...] = jnp.zeros_like(carry_ref)

    sc = _sigmoid(lg_ref[...])
    ch = sc + bias_ref[...]

    def first_max(v, idx, big):
        m = jnp.max(v, axis=0, keepdims=True)
        i = jnp.min(jnp.where(v == m, idx, big), axis=0, keepdims=True)
        return m, i

    midx = lax.broadcasted_iota(I32, (gsz, tt), 0).astype(F32)
    gs_rows = []
    for g in range(N_GROUPS):
        v = ch[g * gsz:(g + 1) * gsz, :]
        m1, i1 = first_max(v, midx, float(gsz))
        m2 = jnp.max(jnp.where(midx == i1, neg, v), axis=0, keepdims=True)
        gs_rows.append(m1 + m2)
    gscore = jnp.concatenate(gs_rows, axis=0)

    gidx = lax.broadcasted_iota(I32, (N_GROUPS, tt), 0).astype(F32)
    gsel = jnp.zeros((N_GROUPS, tt), F32)
    cur = gscore
    for _ in range(TOPK_GROUPS):
        _, i = first_max(cur, gidx, float(N_GROUPS))
        hit = gidx == i
        gsel = jnp.where(hit, 1.0, gsel)
        cur = jnp.where(hit, neg, cur)

    masked = jnp.concatenate(
        [jnp.where(gsel[g:g + 1, :] > 0.0, ch[g * gsz:(g + 1) * gsz, :], neg) for g in range(N_GROUPS)], axis=0)

    eiota = lax.broadcasted_iota(I32, (ne, tt), 0).astype(F32)
    sel = jnp.zeros((ne, tt), F32)
    cur = masked
    idx_rows, s_rows = [], []
    for _ in range(TOP_K):
        _, i = first_max(cur, eiota, float(ne))
        hit = eiota == i
        idx_rows.append(i)
        s_rows.append(jnp.sum(jnp.where(hit, sc, 0.0), axis=0, keepdims=True))
        sel = jnp.where(hit, 1.0, sel)
        cur = jnp.where(hit, neg, cur)
    idx = jnp.concatenate(idx_rows, axis=0)
    s = jnp.concatenate(s_rows, axis=0)
    w_ref[...] = s / jnp.sum(s, axis=0, keepdims=True) * ROUTED_SCALE
    eidx_ref[...] = idx.astype(I32)

    cw = 256
    a_i = lax.broadcasted_iota(I32, (cw, cw), 0)
    b_i = lax.broadcasted_iota(I32, (cw, cw), 1)
    upper = jnp.where(a_i < b_i, 1.0, 0.0).astype(BF16)
    carry = carry_ref[:, 0:1]
    pieces = []
    for c0 in range(0, tt, cw):
        sc_c = sel[:, c0:c0 + cw]
        pieces.append(_dot(sc_c.astype(BF16), upper) + carry)
        carry = carry + jnp.sum(sc_c, axis=1, keepdims=True)
    rank_full = jnp.concatenate(pieces, axis=1)
    carry_ref[...] = jnp.broadcast_to(carry, carry_ref.shape)
    cnt_ref[...] = jnp.broadcast_to(carry, cnt_ref.shape)
    rk = [jnp.sum(jnp.where(eiota == idx_rows[k], rank_full, 0.0), axis=0, keepdims=True) for k in range(TOP_K)]
    rank_ref[...] = jnp.concatenate(rk, axis=0).astype(I32)


def _route(logits_t, bias):
    ne, t = logits_t.shape
    tt = 1024
    return pl.pallas_call(
        _route_body,
        grid=(t // tt,),
        in_specs=[
            pl.BlockSpec((ne, tt), lambda i: (0, i)),
            pl.BlockSpec((ne, 1), lambda i: (0, 0)),
        ],
        out_specs=[
            pl.BlockSpec((TOP_K, tt), lambda i: (0, i)),
            pl.BlockSpec((TOP_K, tt), lambda i: (0, i)),
            pl.BlockSpec((TOP_K, tt), lambda i: (0, i)),
            pl.BlockSpec((ne, 128), lambda i: (0, 0)),
        ],
        out_shape=[
            jax.ShapeDtypeStruct((TOP_K, t), I32),
            jax.ShapeDtypeStruct((TOP_K, t), F32),
            jax.ShapeDtypeStruct((TOP_K, t), I32),
            jax.ShapeDtypeStruct((ne, 128), F32),
        ],
        scratch_shapes=[pltpu.VMEM((ne, 128), F32)],
        compiler_params=_cparams(("arbitrary",)),
        name="route",
    )(logits_t, bias.reshape(ne, 1))


def _slots_body(cnt_ref, eidx_ref, rank_ref, dest_ref, rows_ref, be_ref, first_ref, nxt_ref, nu_ref,
                off_smem, dchunk, sem):
    ne = cnt_ref.shape[0]
    nb = be_ref.shape[0]
    tl = eidx_ref.shape[1]
    r_ = EXPERT_ROWS
    shift = r_.bit_length() - 1
    c = pl.program_id(0)

    @pl.when(c == 0)
    def _():
        def per_expert(e, off):
            cnt = cnt_ref[e]
            n = lax.shift_right_logical(cnt + (r_ - 1), shift)
            off_smem[e] = off

            def fill(j, _):
                be_ref[off + j] = e
                first_ref[off + j] = jnp.where(j == 0, 1, 0)
                return 0
            lax.fori_loop(0, n, fill, 0)

            def pad(j, _):
                rows_ref[off * r_ + j] = 0
                return 0
            lax.fori_loop(cnt, n * r_, pad, 0)
            return off + n

        n_used = lax.fori_loop(0, ne, per_expert, 0)
        nu_ref[0] = n_used
        last_e = be_ref[n_used - 1]

        def tail(j, _):
            be_ref[j] = last_e
            first_ref[j] = 0

            def zero_row(q, _):
                rows_ref[j * r_ + q] = 0
                return 0
            lax.fori_loop(0, r_, zero_row, 0)
            return 0
        lax.fori_loop(n_used, nb, tail, 0)

        def back(i, nx):
            j = nb - 1 - i
            nxt_ref[j] = nx
            return jnp.where(first_ref[j] == 1, be_ref[j], nx)
        lax.fori_loop(0, nb, back, -1)

    ei = eidx_ref[...]
    acc = rank_ref[...]
    for e in range(ne):
        acc = acc + jnp.where(ei == e, off_smem[e] * r_, 0)
    dest_ref[...] = acc
    cp = pltpu.make_async_copy(dest_ref, dchunk, sem)
    cp.start()
    cp.wait()
    base = c * tl

    def scatter(j, _):
        for k in range(TOP_K):
            rows_ref[dchunk[k, j]] = base + j
        return 0
    lax.fori_loop(0, tl, scatter, 0)


def _slots(counts, eidx, rank, nb):
    k_, t = eidx.shape
    tl = 1024
    smem = lambda: pl.BlockSpec(memory_space=pltpu.SMEM)
    chunk = lambda: pl.BlockSpec((k_, tl), lambda i: (0, i))
    return pl.pallas_call(
        _slots_body,
        grid=(t // tl,),
        in_specs=[smem(), chunk(), chunk()],
        out_specs=[chunk(), smem(), smem(), smem(), smem(), smem()],
        out_shape=[
            jax.ShapeDtypeStruct((k_, t), I32),
            jax.ShapeDtypeStruct((nb * EXPERT_ROWS,), I32),
            jax.ShapeDtypeStruct((nb,), I32),
            jax.ShapeDtypeStruct((nb,), I32),
            jax.ShapeDtypeStruct((nb,), I32),
            jax.ShapeDtypeStruct((1,), I32),
        ],
        scratch_shapes=[
            pltpu.SMEM((counts.shape[0],), I32),
            pltpu.SMEM((k_, tl), I32),
            pltpu.SemaphoreType.DMA,
        ],
        compiler_params=_cparams(("arbitrary",)),
        name="slots",
    )(counts, eidx, rank)


EXPERT_ISSUE_GROUPS = 4
EXPERT_LOOKAHEAD = 2


def _expert_body(be_ref, first_ref, nxt_ref, nu_ref, rows_hbm, x_hbm, wg_hbm, wu_hbm, wd_hbm, y_ref,
                 idx_smem, xbuf, wg_f, wu_f, wd_f, wg_s, wu_s, wd_s, isem, gsem, wsem):
    i = pl.program_id(0)
    n_used = nu_ref[0]
    r_ = EXPERT_ROWS
    ns = EXPERT_LOOKAHEAD + 1
    slot = lax.rem(i, ns)
    aslot = lax.rem(i + EXPERT_LOOKAHEAD, ns)

    def idx_copy(blk, s):
        return pltpu.make_async_copy(rows_hbm.at[pl.ds(blk, 1)], idx_smem.at[s], isem.at[s])

    def row_copy(s, r):
        return pltpu.make_async_copy(x_hbm.at[pl.ds(idx_smem[s, 0, r], 1), :], xbuf.at[s, pl.ds(r, 1), :],
                                     gsem.at[s])

    def issue_rows(s, lo, hi):
        for r in range(lo, hi):
            row_copy(s, r).start(priority=r % 2)

    def wait_rows(s):
        pltpu.make_async_copy(x_hbm.at[pl.ds(0, r_), :], xbuf.at[s], gsem.at[s]).wait()

    def weight_copies(e):
        return (pltpu.make_async_copy(wg_hbm.at[e], wg_f, wsem.at[0]),
                pltpu.make_async_copy(wu_hbm.at[e], wu_f, wsem.at[1]),
                pltpu.make_async_copy(wd_hbm.at[e], wd_f, wsem.at[2]))

    @pl.when(i == 0)
    def _():
        for cp in weight_copies(be_ref[0]):
            cp.start()
        for b in range(EXPERT_LOOKAHEAD):
            idx_copy(b, b).start()
        for b in range(EXPERT_LOOKAHEAD):
            idx_copy(b, b).wait()

            def body(r, _, b=b):
                row_copy(b, r).start()
                return 0
            lax.fori_loop(0, r_, body, 0)
        idx_copy(EXPERT_LOOKAHEAD, EXPERT_LOOKAHEAD).start()

    @pl.when(first_ref[i] == 1)
    def _():
        for cp in weight_copies(be_ref[i]):
            cp.wait()
        wg_s[...] = wg_f[...].astype(BF16)
        wu_s[...] = wu_f[...].astype(BF16)
        wd_s[...] = wd_f[...].astype(BF16)

        @pl.when(nxt_ref[i] >= 0)
        def _():
            for cp in weight_copies(nxt_ref[i]):
                cp.start()

    @pl.when(i < n_used)
    def _():
        idx_copy(i + EXPERT_LOOKAHEAD, aslot).wait()
        wait_rows(slot)
        step = r_ // EXPERT_ISSUE_GROUPS
        lo, hi = _unpack_bf16_pair(xbuf[slot])
        lo = lo.astype(BF16)
        hi = hi.astype(BF16)
        half = lo.shape[-1]
        issue_rows(aslot, 0, step)
        a = _dot(lo, wg_s[0:half, :]) + _dot(hi, wg_s[half:, :])
        issue_rows(aslot, step, 2 * step)
        u = _dot(lo, wu_s[0:half, :]) + _dot(hi, wu_s[half:, :])
        issue_rows(aslot, 2 * step, 3 * step)
        hid = (_silu(a) * u).astype(BF16)
        y = _dot(hid, wd_s[...])
        issue_rows(aslot, 3 * step, r_)
        y_ref[...] = _pack_bf16_pair(y)

        @pl.when(i + 1 < n_used)
        def _():
            idx_copy(i + EXPERT_LOOKAHEAD + 1, slot).start()

    @pl.when(jnp.logical_and(i >= n_used, i < n_used + EXPERT_LOOKAHEAD))
    def _():
        wait_rows(slot)

    @pl.when(i >= n_used)
    def _():
        y_ref[...] = jnp.zeros_like(y_ref)


def _experts(block_e, first, nxt, n_used, rows, h2p, wg, wu, wd):
    nb = block_e.shape[0]
    t, half = h2p.shape
    ne, d, f = wg.shape
    r_ = EXPERT_ROWS
    hbm = lambda: pl.BlockSpec(memory_space=pl.ANY)
    grid_spec = pltpu.PrefetchScalarGridSpec(
        num_scalar_prefetch=4,
        grid=(nb,),
        in_specs=[hbm(), hbm(), hbm(), hbm(), hbm()],
        out_specs=pl.BlockSpec((r_, half), lambda i, *_: (i, 0)),
        scratch_shapes=[
            pltpu.SMEM((EXPERT_LOOKAHEAD + 1, 1, r_), I32),
            pltpu.VMEM((EXPERT_LOOKAHEAD + 1, r_, half), U32),
            pltpu.VMEM((d, f), F32),
            pltpu.VMEM((d, f), F32),
            pltpu.VMEM((f, d), F32),
            pltpu.VMEM((d, f), BF16),
            pltpu.VMEM((d, f), BF16),
            pltpu.VMEM((f, d), BF16),
            pltpu.SemaphoreType.DMA((EXPERT_LOOKAHEAD + 1,)),
            pltpu.SemaphoreType.DMA((EXPERT_LOOKAHEAD + 1,)),
            pltpu.SemaphoreType.DMA((3,)),
        ],
    )
    return pl.pallas_call(
        _expert_body,
        grid_spec=grid_spec,
        out_shape=jax.ShapeDtypeStruct((nb * r_, half), U32),
        compiler_params=_cparams(("arbitrary",)),
        name="experts",
    )(block_e, first, nxt, n_used, rows, h2p, wg, wu, wd)


def _combine_body(dest_hbm, y_hbm, w_ref, x1_ref, h2_ref, g2_ref, sg_ref, su_ref, sd_ref, fnw_ref, o_ref,
                  idx_smem, ybuf, isem, gsem):
    i = pl.program_id(0)
    n = pl.num_programs(0)
    tm = x1_ref.shape[0]
    nrow = TOP_K * tm
    slot = i % 2
    nslot = 1 - slot

    def idx_copy(blk, s):
        cols = pl.ds(pl.multiple_of(blk * tm, tm), tm)
        return pltpu.make_async_copy(dest_hbm.at[:, cols], idx_smem.at[s], isem.at[s])

    def issue_rows(s):
        def body(j, _):
            for k in range(TOP_K):
                pltpu.make_async_copy(y_hbm.at[pl.ds(idx_smem[s, k, j], 1), :],
                                      ybuf.at[s, pl.ds(k * tm + j, 1), :], gsem.at[s]).start(priority=k % 2)
            return 0
        lax.fori_loop(0, tm, body, 0)

    def wait_rows(s):
        pltpu.make_async_copy(y_hbm.at[pl.ds(0, nrow), :], ybuf.at[s], gsem.at[s]).wait()

    @pl.when(i == 0)
    def _():
        idx_copy(0, 0).start()
        idx_copy(0, 0).wait()
        issue_rows(0)

        @pl.when(n > 1)
        def _():
            idx_copy(1, 1).start()

    @pl.when(i + 1 < n)
    def _():
        idx_copy(i + 1, nslot).wait()
        issue_rows(nslot)

    wait_rows(slot)

    @pl.when(i + 2 < n)
    def _():
        idx_copy(i + 2, slot).start()

    w = w_ref[...]
    half = ybuf.shape[-1]
    r_lo = jnp.zeros((tm, half), F32)
    r_hi = jnp.zeros((tm, half), F32)
    for k in range(TOP_K):
        lo, hi = _unpack_bf16_pair(ybuf[slot, k * tm:(k + 1) * tm, :])
        wk = w[:, k:k + 1]
        r_lo = r_lo + wk * lo
        r_hi = r_hi + wk * hi
    routed = jnp.concatenate([r_lo, r_hi], axis=-1)

    x_lo, x_hi = _unpack_bf16_pair(h2_ref[...])
    x_lo = x_lo.astype(BF16)
    x_hi = x_hi.astype(BF16)
    a = _dot(x_lo, sg_ref[0:half, :]) + _dot(x_hi, sg_ref[half:, :])
    u = _dot(x_lo, su_ref[0:half, :]) + _dot(x_hi, su_ref[half:, :])
    shared = _dot((_silu(a) * u).astype(BF16), sd_ref[...])

    x2 = x1_ref[...] + g2_ref[0, 0] * (routed + shared)
    ms = jnp.mean(x2 * x2, axis=-1, keepdims=True)
    o_ref[...] = x2 * lax.rsqrt(ms + EPS) * fnw_ref[...]


def _combine(dest, y_sorted, w_tok, x1, h2p, mod4, sg, su, sd, fnw, tiles_per_batch, tm):
    t, d = x1.shape
    half = d // 2
    nrow = TOP_K * tm
    res = lambda shape: pl.BlockSpec(shape, lambda i: (0,) * len(shape), pipeline_mode=pl.Buffered(1))
    return pl.pallas_call(
        _combine_body,
        grid=(t // tm,),
        in_specs=[
            pl.BlockSpec(memory_space=pl.ANY),
            pl.BlockSpec(memory_space=pl.ANY),
            pl.BlockSpec((tm, TOP_K), lambda i: (i, 0)),
            pl.BlockSpec((tm, d), lambda i: (i, 0)),
            pl.BlockSpec((tm, half), lambda i: (i, 0)),
            pl.BlockSpec((1, 1, 1, d), lambda i: (i // tiles_per_batch, 5, 0, 0)),
            res(sg.shape), res(su.shape), res(sd.shape), res(fnw.shape),
        ],
        out_specs=pl.BlockSpec((tm, d), lambda i: (i, 0)),
        out_shape=jax.ShapeDtypeStruct((t, d), F32),
        scratch_shapes=[
            pltpu.SMEM((2, TOP_K, tm), I32),
            pltpu.VMEM((2, nrow, half), U32),
            pltpu.SemaphoreType.DMA((2,)),
            pltpu.SemaphoreType.DMA((2,)),
        ],
        compiler_params=_cparams(("arbitrary",)),
        name="combine",
    )(dest, y_sorted, w_tok, x1, h2p, mod4, sg, su, sd, fnw)


def kernel(x, c, ctx, c_ctx, ada_w, ada_b, norm1_w, norm2_w, w_in, s5_lam_re, s5_lam_im, s5_log_dt,
           s5_b_re, s5_b_im, s5_c_re, s5_c_im, s5_d, s5_glu_w, gla_gk_up, gla_gk_b, gla_norm_w,
           w_s5_proj, w_gla_proj, w_out, router_w, router_bias, exp_w_gate, exp_w_up, exp_w_down,
           sh_w_gate, sh_w_up, sh_w_down, final_norm_w):
    depth = ada_w.shape[0]
    assert depth == 1, "single-layer block: context outputs are never consumed"
    b_, l_, d = x.shape
    lc = ctx.shape[1]
    s5w = s5_d.shape[1]
    kw = gla_gk_up.shape[-1]
    vw = w_gla_proj.shape[1]
    dk, dv = kw // GLA_HEADS, vw // GLA_HEADS
    rank2 = 2 * GLA_GATE_RANK
    li = 0

    c8 = jnp.zeros((8, d), F32).at[:b_].set(c).at[b_].set(c_ctx)
    mod4 = _ada(c8, ada_w[li], ada_b[li]).reshape(8, N_MOD, 1, d)

    cuts = [0, s5w, s5w + kw, s5w + 2 * kw, s5w + 2 * kw + vw, s5w + 2 * kw + 2 * vw]
    c_gkd = cuts[5]
    w = w_in[li]
    w_main = jnp.concatenate([w[:, :c_gkd], w[:, c_gkd + rank2:]], axis=1).astype(BF16)
    w_ctx = jnp.concatenate([w[:, cuts[0]:cuts[1]], w[:, cuts[2]:cuts[4]]], axis=1).astype(BF16)
    w_gkd = jnp.zeros((d, 128), F32).at[:, :rank2].set(w[:, c_gkd:c_gkd + rank2]).astype(BF16)
    nw1 = norm1_w[li].reshape(1, d)
    p_lat, gk_lat = _inproj(x, mod4, lambda b: b, nw1, w_main, w_gkd, tm=512, tn=1024)
    p_ctx, gk_ctx = _inproj(ctx, mod4, lambda b: b_, nw1, w_ctx, w_gkd, tm=lc, tn=w_ctx.shape[1] // 2)

    g_ = s5w // S5_GROUP
    tc = S5_CHUNK
    n_ctx, n_lat = lc // tc, l_ // tc
    u_all = jnp.concatenate([p_ctx[:, :, :s5w], p_lat[:, :, :s5w]], axis=1)
    x2 = u_all.reshape(b_, n_ctx + n_lat, tc, g_, S5_GROUP).transpose(3, 1, 0, 2, 4)
    x2 = x2.reshape(g_, (n_ctx + n_lat) * b_, tc * S5_GROUP)
    wcat, wout5, acoef = _s5_weights(s5_lam_re[li], s5_lam_im[li], s5_log_dt[li], s5_b_re[li], s5_b_im[li],
                                     s5_c_re[li], s5_c_im[li], s5_d[li])
    y2 = _s5(x2, wcat, wout5, acoef, n_ctx, n_lat, b_)
    ys = y2.reshape(g_, n_lat, b_, tc, S5_GROUP).transpose(2, 1, 3, 0, 4).reshape(b_, l_, s5w)

    up_pad = jnp.zeros((2, 128, kw), F32)
    up_pad = up_pad.at[0, :GLA_GATE_RANK].set(gla_gk_up[li, 0]).at[1, GLA_GATE_RANK:rank2].set(gla_gk_up[li, 1])
    yg = _gla(p_lat, p_ctx, gk_lat, gk_ctx, up_pad.astype(BF16), gla_gk_b[li].reshape(2, 1, kw), dk, dv)

    rw = jnp.zeros((d, 128), F32).at[:, :N_EXPERTS].set(router_w[li])
    rw_hi = rw.astype(BF16)
    rw2 = jnp.stack([rw_hi, (rw - rw_hi.astype(F32)).astype(BF16)])
    x1, h2p, logits = _merge(
        ys, yg, p_lat, x, mod4, s5_glu_w[li].astype(BF16), w_s5_proj[li].astype(BF16),
        w_gla_proj[li].astype(BF16), w_out[li].astype(BF16), gla_norm_w[li].reshape(1, dv),
        norm2_w[li].reshape(1, d), rw2, dv, tm=256)

    t = b_ * l_
    logits_t = logits.reshape(t, 128)[:, :N_EXPERTS].T
    eidx, wts, rank, cnt = _route(logits_t, router_bias[li])
    nb = (t * TOP_K + N_EXPERTS * (EXPERT_ROWS - 1) + EXPERT_ROWS - 1) // EXPERT_ROWS + EXPERT_LOOKAHEAD
    dest, rows, block_e, first, nxt, n_used = _slots(cnt[:, 0].astype(I32), eidx, rank, nb)

    h2p_flat = h2p.reshape(t, d // 2)
    y_sorted = _experts(block_e, first, nxt, n_used, rows.reshape(nb, EXPERT_ROWS), h2p_flat,
                        exp_w_gate[li], exp_w_up[li], exp_w_down[li])

    tm_c = 256
    out = _combine(dest, y_sorted, wts.T, x1.reshape(t, d), h2p_flat, mod4,
                   sh_w_gate[li].astype(BF16), sh_w_up[li].astype(BF16), sh_w_down[li].astype(BF16),
                   final_norm_w.reshape(1, d), l_ // tm_c, tm_c)
    return out.reshape(b_, l_, d)
```

```python
import functools
import math

import jax
import jax.numpy as jnp
from jax import lax
from jax.experimental import pallas as pl
from jax.experimental.pallas import tpu as pltpu

F32 = jnp.float32
BF16 = jnp.bfloat16
U32 = jnp.uint32
I32 = jnp.int32

EPS = 1e-6
N_MOD = 6

S5_GROUP = 16
S5_STATE = 64
S5_CHUNK = 16
S5_GROUPS_PER_STEP = 8

GLA_HEADS = 4
GLA_GATE_RANK = 16
GLA_GATE_NORM = 16.0
GLA_CHUNK = 64

N_EXPERTS = 64
TOP_K = 8
N_GROUPS = 8
TOPK_GROUPS = 4
ROUTED_SCALE = 2.5
EXPERT_ROWS = 256

VMEM_LIMIT = 56 * 1024 * 1024


def _cparams(sem, vmem=VMEM_LIMIT):
    return pltpu.CompilerParams(dimension_semantics=sem, vmem_limit_bytes=vmem)


def _dot(a, b):
    return jnp.dot(a, b, preferred_element_type=F32)


def _sigmoid(x):
    return 1.0 / (1.0 + jnp.exp(-x))


def _silu(x):
    return x * _sigmoid(x)


def _pack_bf16_pair(lo, hi):
    lo_bits = lax.bitcast_convert_type(lo.astype(BF16).astype(F32), U32)
    hi_bits = lax.bitcast_convert_type(hi.astype(BF16).astype(F32), U32)
    return (lo_bits >> 16) | (hi_bits & jnp.uint32(0xFFFF0000))


def _unpack_bf16_pair(w):
    lo = lax.bitcast_convert_type(w << 16, F32)
    hi = lax.bitcast_convert_type(w & jnp.uint32(0xFFFF0000), F32)
    return lo, hi


def _store_row_tiles(ref, lead, val):
    for kc in range(ref.shape[-2]):
        ref[(*lead, slice(None), kc, slice(None))] = val[:, kc * 128:(kc + 1) * 128]


def _load_row_tiles(ref, lead):
    return jnp.concatenate([ref[(*lead, slice(None), kc, slice(None))] for kc in range(ref.shape[-2])], axis=-1)


def _ada_body(c_ref, w_ref, b_ref, o_ref):
    c = c_ref[...]
    s = _silu(c).astype(BF16)
    o_ref[...] = _dot(s, w_ref[...].astype(BF16)) + b_ref[...]


def _ada(c8, ada_w, ada_b):
    d, n = ada_w.shape
    tn = 512
    return pl.pallas_call(
        _ada_body,
        grid=(n // tn,),
        in_specs=[
            pl.BlockSpec((8, d), lambda j: (0, 0)),
            pl.BlockSpec((d, tn), lambda j: (0, j)),
            pl.BlockSpec((1, tn), lambda j: (0, j)),
        ],
        out_specs=pl.BlockSpec((8, tn), lambda j: (0, j)),
        out_shape=jax.ShapeDtypeStruct((8, n), F32),
        compiler_params=_cparams(("parallel",)),
        name="ada",
    )(c8, ada_w, ada_b.reshape(1, n))


def _modulated_norm(x, nw, sh, sc):
    ms = jnp.mean(x * x, axis=-1, keepdims=True)
    y = x * lax.rsqrt(ms + EPS) * nw
    return y * (1.0 + sc) + sh


def _inproj_body(x_ref, sh_ref, sc_ref, nw_ref, w_ref, wg_ref, p_ref, g_ref, h_scr):
    @pl.when(pl.program_id(2) == 0)
    def _():
        h = _modulated_norm(x_ref[0], nw_ref[...], sh_ref[0, 0], sc_ref[0, 0]).astype(BF16)
        h_scr[...] = h
        g_ref[0] = _dot(h, wg_ref[...])

    p_ref[0] = _dot(h_scr[...], w_ref[...]).astype(BF16)


def _inproj(x, mod4, mod_row, nw, w, wg, tm, tn):
    b_, l_, d = x.shape
    n = w.shape[1]
    return pl.pallas_call(
        _inproj_body,
        grid=(b_, l_ // tm, n // tn),
        in_specs=[
            pl.BlockSpec((1, tm, d), lambda b, i, j: (b, i, 0)),
            pl.BlockSpec((1, 1, 1, d), lambda b, i, j: (mod_row(b), 0, 0, 0)),
            pl.BlockSpec((1, 1, 1, d), lambda b, i, j: (mod_row(b), 1, 0, 0)),
            pl.BlockSpec((1, d), lambda b, i, j: (0, 0)),
            pl.BlockSpec((d, tn), lambda b, i, j: (0, j)),
            pl.BlockSpec((d, 128), lambda b, i, j: (0, 0)),
        ],
        out_specs=[
            pl.BlockSpec((1, tm, tn), lambda b, i, j: (b, i, j)),
            pl.BlockSpec((1, tm, 128), lambda b, i, j: (b, i, 0)),
        ],
        out_shape=[
            jax.ShapeDtypeStruct((b_, l_, n), BF16),
            jax.ShapeDtypeStruct((b_, l_, 128), F32),
        ],
        scratch_shapes=[pltpu.VMEM((tm, d), BF16)],
        compiler_params=_cparams(("parallel", "parallel", "arbitrary")),
        name="inproj",
    )(x, mod4, mod4, nw, w, wg)


def _s5_weights(lam_re, lam_im, log_dt, b_re, b_im, c_re, c_im, d_skip):
    hp = lax.Precision.HIGHEST
    tc = S5_CHUNK
    g_, p_ = lam_re.shape[1:]
    h_ = b_re.shape[-1]
    dt = jnp.exp(log_dt)[..., None]
    ar, ai = lam_re * dt, lam_im * dt
    k = jnp.arange(tc + 1, dtype=F32)
    mag = jnp.exp(ar[..., None] * k)
    pw_re = mag * jnp.cos(ai[..., None] * k)
    pw_im = mag * jnp.sin(ai[..., None] * k)
    num_re = jnp.expm1(ar) * jnp.cos(ai) - 2.0 * jnp.sin(0.5 * ai) ** 2
    num_im = jnp.exp(ar) * jnp.sin(ai)
    den = lam_re * lam_re + lam_im * lam_im
    f_re = (num_re * lam_re + num_im * lam_im) / den
    f_im = (num_im * lam_re - num_re * lam_im) / den
    bb_re = f_re[..., None] * b_re - f_im[..., None] * b_im
    bb_im = f_re[..., None] * b_im + f_im[..., None] * b_re

    cp_re = c_re[..., None] * pw_re[:, :, None] - c_im[..., None] * pw_im[:, :, None]
    cp_im = c_re[..., None] * pw_im[:, :, None] + c_im[..., None] * pw_re[:, :, None]
    kk = (jnp.einsum('dgipt,dgpj->dgtij', cp_re, bb_re, precision=hp)
          - jnp.einsum('dgipt,dgpj->dgtij', cp_im, bb_im, precision=hp))

    s_idx = jnp.arange(tc)[:, None]
    t_idx = jnp.arange(tc)[None, :]
    lag_f = t_idx - s_idx
    lag_b = s_idx - t_idx
    kf = jnp.where((lag_f >= 0)[None, :, :, None, None], kk[0][:, jnp.clip(lag_f, 0, tc)], 0.0)
    kb = jnp.where((lag_b >= 0)[None, :, :, None, None], kk[1][:, jnp.clip(lag_b, 0, tc)], 0.0)
    dsk = d_skip.reshape(g_, h_)
    eye_t = jnp.eye(tc, dtype=F32)[None, :, :, None, None]
    eye_h = jnp.eye(h_, dtype=F32)[None, None, None]
    m_all = kf + kb + eye_t * eye_h * dsk[:, None, None, :, None]
    m_all = m_all.transpose(0, 1, 4, 2, 3).reshape(g_, tc * h_, tc * h_)

    def state_w(dr, expo):
        e_re = pw_re[dr][:, :, expo]
        e_im = pw_im[dr][:, :, expo]
        w_re = jnp.einsum('gps,gpj->gsjp', e_re, bb_re[dr], precision=hp) - jnp.einsum(
            'gps,gpj->gsjp', e_im, bb_im[dr], precision=hp)
        w_im = jnp.einsum('gps,gpj->gsjp', e_re, bb_im[dr], precision=hp) + jnp.einsum(
            'gps,gpj->gsjp', e_im, bb_re[dr], precision=hp)
        w_re = w_re.reshape(g_, tc * h_, p_)
        w_im = w_im.reshape(g_, tc * h_, p_)
        return jnp.concatenate([w_re, w_im, w_im, w_re], axis=-1)

    ws_f = state_w(0, tc - 1 - jnp.arange(tc))
    ws_b = state_w(1, jnp.arange(tc))
    wcat = jnp.concatenate([m_all, ws_f, ws_b], axis=-1)

    def out_w(dr, expo):
        q_re = cp_re[dr][..., expo]
        q_im = cp_im[dr][..., expo]
        top = q_re.transpose(0, 2, 3, 1).reshape(g_, p_, tc * h_)
        bot = (-q_im).transpose(0, 2, 3, 1).reshape(g_, p_, tc * h_)
        return jnp.concatenate([top, bot], axis=1)

    wout = jnp.concatenate([out_w(0, 1 + jnp.arange(tc)), out_w(1, tc - jnp.arange(tc))], axis=1)

    a_re, a_im = pw_re[..., tc], pw_im[..., tc]
    zeros = jnp.zeros_like(a_re[0])
    rows = []
    for dr in range(2):
        rows += [jnp.concatenate([a_re[dr], a_re[dr]], -1),
                 jnp.concatenate([-a_im[dr], a_im[dr]], -1),
                 jnp.concatenate([a_im[dr], -a_im[dr]], -1)]
    rows += [jnp.concatenate([zeros, zeros], -1)] * 2
    acoef = jnp.stack(rows, axis=1)
    return wcat.astype(BF16), wout.astype(BF16), acoef


def _s5_body(x_ref, wcat_ref, wout_ref, a_ref, y_ref, y_scr, s_scr, h_scr, *, n_ctx, n_lat, nb):
    gb = x_ref.shape[0]
    pw = 2 * S5_STATE
    for g in range(gb):
        r = _dot(x_ref[g], wcat_ref[g])
        y_scr[g] = r[:, : y_scr.shape[-1]]
        s_scr[g] = r[:, y_scr.shape[-1]:]

    a1f, a2f, a2sf = a_ref[:, 0:1, :], a_ref[:, 1:2, :], a_ref[:, 2:3, :]
    a1b, a2b, a2sb = a_ref[:, 3:4, :], a_ref[:, 4:5, :], a_ref[:, 5:6, :]

    first = lax.broadcasted_iota(I32, (gb, 2 * nb, pw), 1) < nb

    def both(v):
        r = pltpu.roll(v, nb, 1)
        return jnp.where(first, v, r), jnp.where(first, r, v)

    def step(pf, pb, carry):
        hf, hfs, hb, hbs = carry
        rf = pl.ds(pl.multiple_of(pf * 2 * nb, 2 * nb), 2 * nb)
        rb = pl.ds(pl.multiple_of(pb * 2 * nb, 2 * nb), 2 * nb)
        sfa, sfb = both(s_scr[:, rf, 0:pw])
        sfsa, sfsb = both(s_scr[:, rf, pw:2 * pw])
        sba, sbb = both(s_scr[:, rb, 2 * pw:3 * pw])
        sbsa, sbsb = both(s_scr[:, rb, 3 * pw:4 * pw])
        hf1 = a1f * hf + a2f * hfs + sfa
        hfs1 = a1f * hfs + a2sf * hf + sfsa
        h_scr[:, rf, 0:pw] = jnp.where(first, hf, hf1)
        hf2 = a1f * hf1 + a2f * hfs1 + sfb
        hfs2 = a1f * hfs1 + a2sf * hf1 + sfsb
        hb1 = a1b * hb + a2b * hbs + sbb
        hbs1 = a1b * hbs + a2sb * hb + sbsb
        h_scr[:, rb, pw:2 * pw] = jnp.where(first, hb1, hb)
        hb2 = a1b * hb1 + a2b * hbs1 + sba
        hbs2 = a1b * hbs1 + a2sb * hb1 + sbsa
        return hf2, hfs2, hb2, hbs2

    z = jnp.zeros((gb, 2 * nb, pw), F32)
    carry = (z, z, z, z)
    pc, pl_ = n_ctx // 2, n_lat // 2
    carry = lax.fori_loop(0, pc, lambda i, c: step(i, pc - 1 - i, c), carry)
    carry = lax.fori_loop(0, pl_, lambda i, c: step(pc + i, pc + pl_ - 1 - i, c), carry)

    lo = n_ctx * nb
    for g in range(gb):
        y = y_scr[g] + _dot(h_scr[g].astype(BF16), wout_ref[g])
        y_ref[g] = y[lo:, :]


def _s5(x2, wcat, wout, acoef, n_ctx, n_lat, nb):
    g_, rows, kdim = x2.shape
    gb = S5_GROUPS_PER_STEP
    pw = 2 * S5_STATE
    out_rows = n_lat * nb
    return pl.pallas_call(
        functools.partial(_s5_body, n_ctx=n_ctx, n_lat=n_lat, nb=nb),
        grid=(g_ // gb,),
        in_specs=[
            pl.BlockSpec((gb, rows, kdim), lambda i: (i, 0, 0)),
            pl.BlockSpec((gb, kdim, wcat.shape[-1]), lambda i: (i, 0, 0)),
            pl.BlockSpec((gb, 2 * pw, kdim), lambda i: (i, 0, 0)),
            pl.BlockSpec((gb, 8, pw), lambda i: (i, 0, 0)),
        ],
        out_specs=pl.BlockSpec((gb, out_rows, kdim), lambda i: (i, 0, 0)),
        out_shape=jax.ShapeDtypeStruct((g_, out_rows, kdim), F32),
        scratch_shapes=[
            pltpu.VMEM((gb, rows, kdim), F32),
            pltpu.VMEM((gb, rows, 4 * pw), F32),
            pltpu.VMEM((gb, rows, 2 * pw), F32),
        ],
        compiler_params=_cparams(("parallel",)),
        name="s5",
    )(x2, wcat, wout, acoef)


def _log_sigmoid(z):
    return jnp.minimum(z, 0.0) - jnp.log1p(jnp.exp(-jnp.abs(z)))


def _chunk_cumsum(g, pos, reverse):
    n = g.shape[0]
    x = g
    sh = 1
    while sh < GLA_CHUNK:
        if reverse:
            x = x + jnp.where(pos < GLA_CHUNK - sh, pltpu.roll(x, n - sh, 0), 0.0)
        else:
            x = x + jnp.where(pos >= sh, pltpu.roll(x, sh, 0), 0.0)
        sh *= 2
    return x


def _gla_body(q_ref, k_ref, v_ref, kc_ref, vc_ref, gk_ref, gkc_ref, up_ref, bias_ref, o_ref,
              bc_scr, kb_scr, qb_scr, st_ref, *, scale):
    c = GLA_CHUNK
    l_ = q_ref.shape[1]
    lc = kc_ref.shape[1]
    n_lat, n_ctx = l_ // c, lc // c
    row = lax.broadcasted_iota(I32, (c, c), 0)
    col = lax.broadcasted_iota(I32, (c, c), 1)
    nt = (((1,), (1,)), ((), ()))
    tn = (((0,), (0,)), ((), ()))

    for dr in range(2):
        smask = (col <= row) if dr == 0 else (col > row)
        up = up_ref[dr]
        bias = bias_ref[dr]
        edge_tile, edge_row = (c - 8, 7) if dr == 0 else (0, 0)

        for gk_r, k_r, base, n, latent in ((gkc_ref, kc_ref, 0, lc, False), (gk_ref, k_ref, lc, l_, True)):
            z = _dot(gk_r[0].astype(BF16), up) + bias
            g = _log_sigmoid(z) * (1.0 / GLA_GATE_NORM)
            pos = lax.broadcasted_iota(I32, g.shape, 0) & (c - 1)
            bc = _chunk_cumsum(g, pos, reverse=(dr == 1))
            bc_scr[base:base + n, :] = bc
            kb_scr[base:base + n, :] = k_r[0].astype(F32) * jnp.exp(-bc)
            if latent:
                qb_scr[...] = (q_ref[0].astype(F32) * (scale * jnp.exp(bc))).astype(BF16)

        def chunk_rows(n, base):
            r0 = pl.multiple_of(n * c, c)
            tot = bc_scr[pl.ds(base + r0 + edge_tile, 8), :][edge_row:edge_row + 1, :]
            return r0, pl.ds(base + r0, c), jnp.exp(tot)

        def update_state(kb, vv, decay):
            kd = (kb * decay).astype(BF16)
            ds_t = lax.dot_general(vv, kd, tn, preferred_element_type=F32)
            st_ref[...] = decay * st_ref[...] + ds_t

        def ctx_step(i, _):
            n = i if dr == 0 else n_ctx - 1 - i
            r0, rs, decay = chunk_rows(n, 0)
            update_state(kb_scr[rs, :], vc_ref[0, pl.ds(r0, c), :], decay)
            return 0

        def lat_step(i, _):
            n = i if dr == 0 else n_lat - 1 - i
            r0, rs, decay = chunk_rows(n, lc)
            ro = pl.ds(r0, c)
            kb = kb_scr[rs, :]
            vv = v_ref[0, ro, :]
            qb = qb_scr[ro, :]
            scores = lax.dot_general(qb, kb.astype(BF16), nt, preferred_element_type=F32)
            scores = jnp.where(smask, scores, 0.0).astype(BF16)
            o = _dot(scores, vv) + lax.dot_general(qb, st_ref[...].astype(BF16), nt,
                                                   preferred_element_type=F32)
            if dr == 0:
                o_ref[0, ro, :] = o
            else:
                o_ref[0, ro, :] = o_ref[0, ro, :] + o
            update_state(kb, vv, decay)
            return 0

        st_ref[...] = jnp.zeros_like(st_ref)
        lax.fori_loop(0, n_ctx, ctx_step, 0, unroll=2)
        lax.fori_loop(0, n_lat, lat_step, 0, unroll=4)


def _gla(p_lat, p_ctx, gk_lat, gk_ctx, up_pad, bias, dk, dv):
    b_, l_, _ = p_lat.shape
    lc = p_ctx.shape[1]
    hh = GLA_HEADS
    s5w = hh * dv
    q0, k0, v0 = s5w // dk, (s5w + hh * dk) // dk, (s5w + 2 * hh * dk) // dv
    kc0, vc0 = s5w // dk, (s5w + hh * dk) // dv
    return pl.pallas_call(
        functools.partial(_gla_body, scale=dk ** -0.5),
        grid=(b_, hh),
        in_specs=[
            pl.BlockSpec((1, l_, dk), lambda b, h: (b, 0, q0 + h)),
            pl.BlockSpec((1, l_, dk), lambda b, h: (b, 0, k0 + h)),
            pl.BlockSpec((1, l_, dv), lambda b, h: (b, 0, v0 + h)),
            pl.BlockSpec((1, lc, dk), lambda b, h: (b, 0, kc0 + h)),
            pl.BlockSpec((1, lc, dv), lambda b, h: (b, 0, vc0 + h)),
            pl.BlockSpec((1, l_, 128), lambda b, h: (b, 0, 0)),
            pl.BlockSpec((1, lc, 128), lambda b, h: (b, 0, 0)),
            pl.BlockSpec((2, 128, dk), lambda b, h: (0, 0, h)),
            pl.BlockSpec((2, 1, dk), lambda b, h: (0, 0, h)),
        ],
        out_specs=pl.BlockSpec((1, l_, dv), lambda b, h: (b, 0, h)),
        out_shape=jax.ShapeDtypeStruct((b_, l_, hh * dv), F32),
        scratch_shapes=[
            pltpu.VMEM((lc + l_, dk), F32),
            pltpu.VMEM((lc + l_, dk), F32),
            pltpu.VMEM((l_, dk), BF16),
            pltpu.VMEM((dv, dk), F32),
        ],
        compiler_params=_cparams(("parallel", "parallel")),
        name="gla",
    )(p_lat, p_lat, p_lat, p_ctx, p_ctx, gk_lat, gk_ctx, up_pad, bias)


def _gelu_tanh(x):
    return 0.5 * x * (1.0 + jnp.tanh(math.sqrt(2.0 / math.pi) * (x + 0.044715 * (x * x * x))))


def _merge_body(ys_ref, yg_ref, go_ref, gs_ref, gg_ref, x_ref, g1_ref, sh2_ref, sc2_ref,
                glu_ref, ws5_ref, wgla_ref, wout_ref, gnw_ref, n2w_ref, rw_ref,
                x1_ref, h2_ref, lg_ref, *, dv):
    a = _gelu_tanh(ys_ref[0])
    a = a * _sigmoid(_dot(a.astype(BF16), glu_ref[...]))
    pa = _dot(a.astype(BF16), ws5_ref[...])

    yg = yg_ref[0]
    parts = []
    for h in range(GLA_HEADS):
        oh = yg[:, h * dv:(h + 1) * dv]
        ms = jnp.mean(oh * oh, axis=-1, keepdims=True)
        parts.append(oh * lax.rsqrt(ms + EPS) * gnw_ref[...])
    gl = jnp.concatenate(parts, axis=-1) * _silu(go_ref[0].astype(F32))
    pb = _dot(gl.astype(BF16), wgla_ref[...])

    m = _sigmoid(gs_ref[0].astype(F32)) * pa + _sigmoid(gg_ref[0].astype(F32)) * pb
    y = _dot(m.astype(BF16), wout_ref[...])
    x1 = x_ref[0] + g1_ref[0, 0] * y
    x1_ref[0] = x1

    h2 = _modulated_norm(x1, n2w_ref[...], sh2_ref[0, 0], sc2_ref[0, 0])
    h_hi = h2.astype(BF16)
    h_lo = (h2 - h_hi.astype(F32)).astype(BF16)
    half = h2.shape[-1] // 2
    _store_row_tiles(h2_ref, (0,), _pack_bf16_pair(h2[:, :half], h2[:, half:]))
    lg_ref[0] = _dot(h_hi, rw_ref[0]) + _dot(h_lo, rw_ref[0]) + _dot(h_hi, rw_ref[1])


def _resident(shape):
    nd = len(shape)
    return pl.BlockSpec(shape, lambda b, i: (0,) * nd, pipeline_mode=pl.Buffered(1))


def _merge(ys, yg, p_lat, x, mod4, glu_w, ws5, wgla, wout, gnw, n2w, rw2, dv, tm):
    b_, l_, d = x.shape
    w5 = ys.shape[-1]
    go0 = (p_lat.shape[-1] - 2 * d - w5) // w5
    gs0 = (p_lat.shape[-1] - 2 * d) // d
    modspec = lambda r: pl.BlockSpec((1, 1, 1, d), lambda b, i: (b, r, 0, 0))
    return pl.pallas_call(
        functools.partial(_merge_body, dv=dv),
        grid=(b_, l_ // tm),
        in_specs=[
            pl.BlockSpec((1, tm, w5), lambda b, i: (b, i, 0)),
            pl.BlockSpec((1, tm, w5), lambda b, i: (b, i, 0)),
            pl.BlockSpec((1, tm, w5), lambda b, i: (b, i, go0)),
            pl.BlockSpec((1, tm, d), lambda b, i: (b, i, gs0)),
            pl.BlockSpec((1, tm, d), lambda b, i: (b, i, gs0 + 1)),
            pl.BlockSpec((1, tm, d), lambda b, i: (b, i, 0)),
            modspec(2), modspec(3), modspec(4),
            _resident(glu_w.shape), _resident(ws5.shape), _resident(wgla.shape), _resident(wout.shape),
            _resident(gnw.shape), _resident(n2w.shape), _resident(rw2.shape),
        ],
        out_specs=[
            pl.BlockSpec((1, tm, d), lambda b, i: (b, i, 0)),
            pl.BlockSpec((1, tm, d // 256, 128), lambda b, i: (b, i, 0, 0)),
            pl.BlockSpec((1, tm, 128), lambda b, i: (b, i, 0)),
        ],
        out_shape=[
            jax.ShapeDtypeStruct((b_, l_, d), F32),
            jax.ShapeDtypeStruct((b_, l_, d // 256, 128), U32),
            jax.ShapeDtypeStruct((b_, l_, 128), F32),
        ],
        compiler_params=_cparams(("parallel", "parallel")),
        name="merge",
    )(ys, yg, p_lat, p_lat, p_lat, x, mod4, mod4, mod4, glu_w, ws5, wgla, wout, gnw, n2w, rw2)


def _route_body(lg_ref, bias_ref, eidx_ref, w_ref, rank_ref, cnt_ref, carry_ref):
    ne, tt = lg_ref.shape
    gsz = ne // N_GROUPS
    neg = -jnp.inf

    @pl.when(pl.program_id(0) == 0)
    def _():
        carry_ref[...] = jnp.zeros_like(carry_ref)

    sc = _sigmoid(lg_ref[...])
    ch = sc + bias_ref[...]

    def first_max(v, idx, big):
        m = jnp.max(v, axis=0, keepdims=True)
        i = jnp.min(jnp.where(v == m, idx, big), axis=0, keepdims=True)
        return m, i

    midx = lax.broadcasted_iota(I32, (gsz, tt), 0).astype(F32)
    gs_rows = []
    for g in range(N_GROUPS):
        v = ch[g * gsz:(g + 1) * gsz, :]
        m1, i1 = first_max(v, midx, float(gsz))
        m2 = jnp.max(jnp.where(midx == i1, neg, v), axis=0, keepdims=True)
        gs_rows.append(m1 + m2)
    gscore = jnp.concatenate(gs_rows, axis=0)

    gidx = lax.broadcasted_iota(I32, (N_GROUPS, tt), 0).astype(F32)
    gsel = jnp.zeros((N_GROUPS, tt), F32)
    cur = gscore
    for _ in range(TOPK_GROUPS):
        _, i = first_max(cur, gidx, float(N_GROUPS))
        hit = gidx == i
        gsel = jnp.where(hit, 1.0, gsel)
        cur = jnp.where(hit, neg, cur)

    masked = jnp.concatenate(
        [jnp.where(gsel[g:g + 1, :] > 0.0, ch[g * gsz:(g + 1) * gsz, :], neg) for g in range(N_GROUPS)], axis=0)

    eiota = lax.broadcasted_iota(I32, (ne, tt), 0).astype(F32)
    sel = jnp.zeros((ne, tt), F32)
    cur = masked
    idx_rows, s_rows = [], []
    for _ in range(TOP_K):
        _, i = first_max(cur, eiota, float(ne))
        hit = eiota == i
        idx_rows.append(i)
        s_rows.append(jnp.sum(jnp.where(hit, sc, 0.0), axis=0, keepdims=True))
        sel = jnp.where(hit, 1.0, sel)
        cur = jnp.where(hit, neg, cur)
    idx = jnp.concatenate(idx_rows, axis=0)
    s = jnp.concatenate(s_rows, axis=0)
    w_ref[...] = s / jnp.sum(s, axis=0, keepdims=True) * ROUTED_SCALE
    eidx_ref[...] = idx.astype(I32)

    cw = 256
    a_i = lax.broadcasted_iota(I32, (cw, cw), 0)
    b_i = lax.broadcasted_iota(I32, (cw, cw), 1)
    upper = jnp.where(a_i < b_i, 1.0, 0.0).astype(BF16)
    carry = carry_ref[:, 0:1]
    pieces = []
    for c0 in range(0, tt, cw):
        sc_c = sel[:, c0:c0 + cw]
        pieces.append(_dot(sc_c.astype(BF16), upper) + carry)
        carry = carry + jnp.sum(sc_c, axis=1, keepdims=True)
    rank_full = jnp.concatenate(pieces, axis=1)
    carry_ref[...] = jnp.broadcast_to(carry, carry_ref.shape)
    cnt_ref[...] = jnp.broadcast_to(carry, cnt_ref.shape)
    rk = [jnp.sum(jnp.where(eiota == idx_rows[k], rank_full, 0.0), axis=0, keepdims=True) for k in range(TOP_K)]
    rank_ref[...] = jnp.concatenate(rk, axis=0).astype(I32)


def _route(logits_t, bias):
    ne, t = logits_t.shape
    tt = 1024
    return pl.pallas_call(
        _route_body,
        grid=(t // tt,),
        in_specs=[
            pl.BlockSpec((ne, tt), lambda i: (0, i)),
            pl.BlockSpec((ne, 1), lambda i: (0, 0)),
        ],
        out_specs=[
            pl.BlockSpec((TOP_K, tt), lambda i: (0, i)),
            pl.BlockSpec((TOP_K, tt), lambda i: (0, i)),
            pl.BlockSpec((TOP_K, tt), lambda i: (0, i)),
            pl.BlockSpec((ne, 128), lambda i: (0, 0)),
        ],
        out_shape=[
            jax.ShapeDtypeStruct((TOP_K, t), I32),
            jax.ShapeDtypeStruct((TOP_K, t), F32),
            jax.ShapeDtypeStruct((TOP_K, t), I32),
            jax.ShapeDtypeStruct((ne, 128), F32),
        ],
        scratch_shapes=[pltpu.VMEM((ne, 128), F32)],
        compiler_params=_cparams(("arbitrary",)),
        name="route",
    )(logits_t, bias.reshape(ne, 1))


def _slots_body(cnt_ref, eidx_ref, rank_ref, dest_ref, rows_ref, be_ref, first_ref, nxt_ref, nu_ref,
                off_smem, dchunk, sem):
    ne = cnt_ref.shape[0]
    nb = be_ref.shape[0]
    tl = eidx_ref.shape[1]
    r_ = EXPERT_ROWS
    shift = r_.bit_length() - 1
    c = pl.program_id(0)

    @pl.when(c == 0)
    def _():
        def per_expert(e, off):
            cnt = cnt_ref[e]
            n = lax.shift_right_logical(cnt + (r_ - 1), shift)
            off_smem[e] = off

            def fill(j, _):
                be_ref[off + j] = e
                first_ref[off + j] = jnp.where(j == 0, 1, 0)
                return 0
            lax.fori_loop(0, n, fill, 0)

            def pad(j, _):
                rows_ref[off * r_ + j] = 0
                return 0
            lax.fori_loop(cnt, n * r_, pad, 0)
            return off + n

        n_used = lax.fori_loop(0, ne, per_expert, 0)
        nu_ref[0] = n_used
        last_e = be_ref[n_used - 1]

        def tail(j, _):
            be_ref[j] = last_e
            first_ref[j] = 0

            def zero_row(q, _):
                rows_ref[j * r_ + q] = 0
                return 0
            lax.fori_loop(0, r_, zero_row, 0)
            return 0
        lax.fori_loop(n_used, nb, tail, 0)

        def back(i, nx):
            j = nb - 1 - i
            nxt_ref[j] = nx
            return jnp.where(first_ref[j] == 1, be_ref[j], nx)
        lax.fori_loop(0, nb, back, -1)

    ei = eidx_ref[...]
    acc = rank_ref[...]
    for e in range(ne):
        acc = acc + jnp.where(ei == e, off_smem[e] * r_, 0)
    dest_ref[...] = acc
    cp = pltpu.make_async_copy(dest_ref, dchunk, sem)
    cp.start()
    cp.wait()
    base = c * tl

    def scatter(j, _):
        for k in range(TOP_K):
            rows_ref[dchunk[k, j]] = base + j
        return 0
    lax.fori_loop(0, tl, scatter, 0)


def _slots(counts, eidx, rank, nb):
    k_, t = eidx.shape
    tl = 1024
    smem = lambda: pl.BlockSpec(memory_space=pltpu.SMEM)
    chunk = lambda: pl.BlockSpec((k_, tl), lambda i: (0, i))
    return pl.pallas_call(
        _slots_body,
        grid=(t // tl,),
        in_specs=[smem(), chunk(), chunk()],
        out_specs=[chunk(), smem(), smem(), smem(), smem(), smem()],
        out_shape=[
            jax.ShapeDtypeStruct((k_, t), I32),
            jax.ShapeDtypeStruct((nb * EXPERT_ROWS,), I32),
            jax.ShapeDtypeStruct((nb,), I32),
            jax.ShapeDtypeStruct((nb,), I32),
            jax.ShapeDtypeStruct((nb,), I32),
            jax.ShapeDtypeStruct((1,), I32),
        ],
        scratch_shapes=[
            pltpu.SMEM((counts.shape[0],), I32),
            pltpu.SMEM((k_, tl), I32),
            pltpu.SemaphoreType.DMA,
        ],
        compiler_params=_cparams(("arbitrary",)),
        name="slots",
    )(counts, eidx, rank)


EXPERT_ISSUE_GROUPS = 8
EXPERT_LOOKAHEAD = 2


def _expert_body(be_ref, first_ref, nxt_ref, nu_ref, rows_hbm, x_hbm, wg_hbm, wu_hbm, wd_hbm, y_ref,
                 idx_smem, xbuf, wg_f, wu_f, wd_f, wg_s, wu_s, wd_s, isem, gsem, wsem):
    i = pl.program_id(0)
    n_used = nu_ref[0]
    r_ = EXPERT_ROWS
    ns = EXPERT_LOOKAHEAD + 1
    slot = lax.rem(i, ns)
    aslot = lax.rem(i + EXPERT_LOOKAHEAD, ns)

    def idx_copy(blk, s):
        return pltpu.make_async_copy(rows_hbm.at[pl.ds(blk, 1)], idx_smem.at[s], isem.at[s])

    def row_copy(s, r):
        return pltpu.make_async_copy(x_hbm.at[idx_smem[s, 0, r]], xbuf.at[s, r], gsem.at[s])

    def issue_rows(s, g):
        step = r_ // EXPERT_ISSUE_GROUPS
        for r in range(g * step, (g + 1) * step):
            row_copy(s, r).start(priority=r % 2)

    def wait_rows(s):
        pltpu.make_async_copy(x_hbm.at[pl.ds(0, r_)], xbuf.at[s], gsem.at[s]).wait()

    def weight_copies(e):
        return (pltpu.make_async_copy(wg_hbm.at[e], wg_f, wsem.at[0]),
                pltpu.make_async_copy(wu_hbm.at[e], wu_f, wsem.at[1]),
                pltpu.make_async_copy(wd_hbm.at[e], wd_f, wsem.at[2]))

    @pl.when(i == 0)
    def _():
        for cp in weight_copies(be_ref[0]):
            cp.start()
        for b in range(EXPERT_LOOKAHEAD):
            idx_copy(b, b).start()
        for b in range(EXPERT_LOOKAHEAD):
            idx_copy(b, b).wait()

            def body(r, _, b=b):
                row_copy(b, r).start()
                return 0
            lax.fori_loop(0, r_, body, 0)
        idx_copy(EXPERT_LOOKAHEAD, EXPERT_LOOKAHEAD).start()

    @pl.when(first_ref[i] == 1)
    def _():
        for cp in weight_copies(be_ref[i]):
            cp.wait()
        wg_s[...] = wg_f[...].astype(BF16)
        wu_s[...] = wu_f[...].astype(BF16)
        wd_s[...] = wd_f[...].astype(BF16)

        @pl.when(nxt_ref[i] >= 0)
        def _():
            for cp in weight_copies(nxt_ref[i]):
                cp.start()

    @pl.when(i < n_used)
    def _():
        idx_copy(i + EXPERT_LOOKAHEAD, aslot).wait()
        wait_rows(slot)
        lo, hi = _unpack_bf16_pair(_load_row_tiles(xbuf, (slot,)))
        lo = lo.astype(BF16)
        hi = hi.astype(BF16)
        half = lo.shape[-1]
        issue_rows(aslot, 0)
        a = _dot(lo, wg_s[0:half, :])
        issue_rows(aslot, 1)
        a = a + _dot(hi, wg_s[half:, :])
        issue_rows(aslot, 2)
        u = _dot(lo, wu_s[0:half, :])
        issue_rows(aslot, 3)
        u = u + _dot(hi, wu_s[half:, :])
        issue_rows(aslot, 4)
        hid = (_silu(a) * u).astype(BF16)
        y_lo = _dot(hid, wd_s[:, 0:half])
        issue_rows(aslot, 5)
        y_hi = _dot(hid, wd_s[:, half:])
        issue_rows(aslot, 6)
        _store_row_tiles(y_ref, (), _pack_bf16_pair(y_lo, y_hi))
        issue_rows(aslot, 7)

        @pl.when(i + 1 < n_used)
        def _():
            idx_copy(i + EXPERT_LOOKAHEAD + 1, slot).start()

    @pl.when(jnp.logical_and(i >= n_used, i < n_used + EXPERT_LOOKAHEAD))
    def _():
        wait_rows(slot)

    @pl.when(i >= n_used)
    def _():
        y_ref[...] = jnp.zeros_like(y_ref)


def _experts(block_e, first, nxt, n_used, rows, h2p, wg, wu, wd):
    nb = block_e.shape[0]
    t, tiles, _ = h2p.shape
    ne, d, f = wg.shape
    r_ = EXPERT_ROWS
    hbm = lambda: pl.BlockSpec(memory_space=pl.ANY)
    grid_spec = pltpu.PrefetchScalarGridSpec(
        num_scalar_prefetch=4,
        grid=(nb,),
        in_specs=[hbm(), hbm(), hbm(), hbm(), hbm()],
        out_specs=pl.BlockSpec((r_, tiles, 128), lambda i, *_: (i, 0, 0)),
        scratch_shapes=[
            pltpu.SMEM((EXPERT_LOOKAHEAD + 1, 1, r_), I32),
            pltpu.VMEM((EXPERT_LOOKAHEAD + 1, r_, tiles, 128), U32),
            pltpu.VMEM((d, f), F32),
            pltpu.VMEM((d, f), F32),
            pltpu.VMEM((f, d), F32),
            pltpu.VMEM((d, f), BF16),
            pltpu.VMEM((d, f), BF16),
            pltpu.VMEM((f, d), BF16),
            pltpu.SemaphoreType.DMA((EXPERT_LOOKAHEAD + 1,)),
            pltpu.SemaphoreType.DMA((EXPERT_LOOKAHEAD + 1,)),
            pltpu.SemaphoreType.DMA((3,)),
        ],
    )
    return pl.pallas_call(
        _expert_body,
        grid_spec=grid_spec,
        out_shape=jax.ShapeDtypeStruct((nb * r_, tiles, 128), U32),
        compiler_params=_cparams(("arbitrary",)),
        name="experts",
    )(block_e, first, nxt, n_used, rows, h2p, wg, wu, wd)


def _combine_body(dest_hbm, y_hbm, w_ref, x1_ref, h2_ref, g2_ref, sg_ref, su_ref, sd_ref, fnw_ref, o_ref,
                  idx_smem, ybuf, isem, gsem):
    i = pl.program_id(0)
    n = pl.num_programs(0)
    tm = x1_ref.shape[0]
    nrow = TOP_K * tm
    slot = i % 2
    nslot = 1 - slot

    def idx_copy(blk, s):
        cols = pl.ds(pl.multiple_of(blk * tm, tm), tm)
        return pltpu.make_async_copy(dest_hbm.at[:, cols], idx_smem.at[s], isem.at[s])

    def issue_rows(s):
        def body(j, _):
            for k in range(TOP_K):
                pltpu.make_async_copy(y_hbm.at[idx_smem[s, k, j]], ybuf.at[s, k * tm + j],
                                      gsem.at[s]).start(priority=k % 2)
            return 0
        lax.fori_loop(0, tm, body, 0)

    def wait_rows(s):
        pltpu.make_async_copy(y_hbm.at[pl.ds(0, nrow)], ybuf.at[s], gsem.at[s]).wait()

    @pl.when(i == 0)
    def _():
        idx_copy(0, 0).start()
        idx_copy(0, 0).wait()
        issue_rows(0)

        @pl.when(n > 1)
        def _():
            idx_copy(1, 1).start()

    @pl.when(i + 1 < n)
    def _():
        idx_copy(i + 1, nslot).wait()
        issue_rows(nslot)

    wait_rows(slot)

    @pl.when(i + 2 < n)
    def _():
        idx_copy(i + 2, slot).start()

    w = w_ref[...]
    tiles = ybuf.shape[-2]
    half = tiles * 128
    r_lo = [jnp.zeros((tm, 128), F32) for _ in range(tiles)]
    r_hi = [jnp.zeros((tm, 128), F32) for _ in range(tiles)]
    for k in range(TOP_K):
        wk = w[:, k:k + 1]
        for kc in range(tiles):
            lo, hi = _unpack_bf16_pair(ybuf[slot, k * tm:(k + 1) * tm, kc, :])
            r_lo[kc] = r_lo[kc] + wk * lo
            r_hi[kc] = r_hi[kc] + wk * hi
    routed = jnp.concatenate(r_lo + r_hi, axis=-1)

    x_lo, x_hi = _unpack_bf16_pair(_load_row_tiles(h2_ref, ()))
    x_lo = x_lo.astype(BF16)
    x_hi = x_hi.astype(BF16)
    a = _dot(x_lo, sg_ref[0:half, :]) + _dot(x_hi, sg_ref[half:, :])
    u = _dot(x_lo, su_ref[0:half, :]) + _dot(x_hi, su_ref[half:, :])
    shared = _dot((_silu(a) * u).astype(BF16), sd_ref[...])

    x2 = x1_ref[...] + g2_ref[0, 0] * (routed + shared)
    ms = jnp.mean(x2 * x2, axis=-1, keepdims=True)
    o_ref[...] = x2 * lax.rsqrt(ms + EPS) * fnw_ref[...]


def _combine(dest, y_sorted, w_tok, x1, h2p, mod4, sg, su, sd, fnw, tiles_per_batch, tm):
    t, d = x1.shape
    half = d // 2
    nrow = TOP_K * tm
    res = lambda shape: pl.BlockSpec(shape, lambda i: (0,) * len(shape), pipeline_mode=pl.Buffered(1))
    return pl.pallas_call(
        _combine_body,
        grid=(t // tm,),
        in_specs=[
            pl.BlockSpec(memory_space=pl.ANY),
            pl.BlockSpec(memory_space=pl.ANY),
            pl.BlockSpec((tm, TOP_K), lambda i: (i, 0)),
            pl.BlockSpec((tm, d), lambda i: (i, 0)),
            pl.BlockSpec((tm, half // 128, 128), lambda i: (i, 0, 0)),
            pl.BlockSpec((1, 1, 1, d), lambda i: (i // tiles_per_batch, 5, 0, 0)),
            res(sg.shape), res(su.shape), res(sd.shape), res(fnw.shape),
        ],
        out_specs=pl.BlockSpec((tm, d), lambda i: (i, 0)),
        out_shape=jax.ShapeDtypeStruct((t, d), F32),
        scratch_shapes=[
            pltpu.SMEM((2, TOP_K, tm), I32),
            pltpu.VMEM((2, nrow, half // 128, 128), U32),
            pltpu.SemaphoreType.DMA((2,)),
            pltpu.SemaphoreType.DMA((2,)),
        ],
        compiler_params=_cparams(("arbitrary",)),
        name="combine",
    )(dest, y_sorted, w_tok, x1, h2p, mod4, sg, su, sd, fnw)


def kernel(x, c, ctx, c_ctx, ada_w, ada_b, norm1_w, norm2_w, w_in, s5_lam_re, s5_lam_im, s5_log_dt,
           s5_b_re, s5_b_im, s5_c_re, s5_c_im, s5_d, s5_glu_w, gla_gk_up, gla_gk_b, gla_norm_w,
           w_s5_proj, w_gla_proj, w_out, router_w, router_bias, exp_w_gate, exp_w_up, exp_w_down,
           sh_w_gate, sh_w_up, sh_w_down, final_norm_w):
    depth = ada_w.shape[0]
    assert depth == 1, "single-layer block: context outputs are never consumed"
    b_, l_, d = x.shape
    lc = ctx.shape[1]
    s5w = s5_d.shape[1]
    kw = gla_gk_up.shape[-1]
    vw = w_gla_proj.shape[1]
    dk, dv = kw // GLA_HEADS, vw // GLA_HEADS
    rank2 = 2 * GLA_GATE_RANK
    li = 0

    c8 = jnp.zeros((8, d), F32).at[:b_].set(c).at[b_].set(c_ctx)
    mod4 = _ada(c8, ada_w[li], ada_b[li]).reshape(8, N_MOD, 1, d)

    cuts = [0, s5w, s5w + kw, s5w + 2 * kw, s5w + 2 * kw + vw, s5w + 2 * kw + 2 * vw]
    c_gkd = cuts[5]
    w = w_in[li]
    w_main = jnp.concatenate([w[:, :c_gkd], w[:, c_gkd + rank2:]], axis=1).astype(BF16)
    w_ctx = jnp.concatenate([w[:, cuts[0]:cuts[1]], w[:, cuts[2]:cuts[4]]], axis=1).astype(BF16)
    w_gkd = jnp.zeros((d, 128), F32).at[:, :rank2].set(w[:, c_gkd:c_gkd + rank2]).astype(BF16)
    nw1 = norm1_w[li].reshape(1, d)
    p_lat, gk_lat = _inproj(x, mod4, lambda b: b, nw1, w_main, w_gkd, tm=512, tn=1024)
    p_ctx, gk_ctx = _inproj(ctx, mod4, lambda b: b_, nw1, w_ctx, w_gkd, tm=lc, tn=w_ctx.shape[1] // 2)

    g_ = s5w // S5_GROUP
    tc = S5_CHUNK
    n_ctx, n_lat = lc // tc, l_ // tc
    u_all = jnp.concatenate([p_ctx[:, :, :s5w], p_lat[:, :, :s5w]], axis=1)
    x2 = u_all.reshape(b_, n_ctx + n_lat, tc, g_, S5_GROUP).transpose(3, 1, 0, 2, 4)
    x2 = x2.reshape(g_, (n_ctx + n_lat) * b_, tc * S5_GROUP)
    wcat, wout5, acoef = _s5_weights(s5_lam_re[li], s5_lam_im[li], s5_log_dt[li], s5_b_re[li], s5_b_im[li],
                                     s5_c_re[li], s5_c_im[li], s5_d[li])
    y2 = _s5(x2, wcat, wout5, acoef, n_ctx, n_lat, b_)
    ys = y2.reshape(g_, n_lat, b_, tc, S5_GROUP).transpose(2, 1, 3, 0, 4).reshape(b_, l_, s5w)

    up_pad = jnp.zeros((2, 128, kw), F32)
    up_pad = up_pad.at[0, :GLA_GATE_RANK].set(gla_gk_up[li, 0]).at[1, GLA_GATE_RANK:rank2].set(gla_gk_up[li, 1])
    yg = _gla(p_lat, p_ctx, gk_lat, gk_ctx, up_pad.astype(BF16), gla_gk_b[li].reshape(2, 1, kw), dk, dv)

    rw = jnp.zeros((d, 128), F32).at[:, :N_EXPERTS].set(router_w[li])
    rw_hi = rw.astype(BF16)
    rw2 = jnp.stack([rw_hi, (rw - rw_hi.astype(F32)).astype(BF16)])
    x1, h2p, logits = _merge(
        ys, yg, p_lat, x, mod4, s5_glu_w[li].astype(BF16), w_s5_proj[li].astype(BF16),
        w_gla_proj[li].astype(BF16), w_out[li].astype(BF16), gla_norm_w[li].reshape(1, dv),
        norm2_w[li].reshape(1, d), rw2, dv, tm=256)

    t = b_ * l_
    logits_t = logits.reshape(t, 128)[:, :N_EXPERTS].T
    eidx, wts, rank, cnt = _route(logits_t, router_bias[li])
    nb = (t * TOP_K + N_EXPERTS * (EXPERT_ROWS - 1) + EXPERT_ROWS - 1) // EXPERT_ROWS + EXPERT_LOOKAHEAD
    dest, rows, block_e, first, nxt, n_used = _slots(cnt[:, 0].astype(I32), eidx, rank, nb)

    h2p_flat = h2p.reshape(t, d // 256, 128)
    y_sorted = _experts(block_e, first, nxt, n_used, rows.reshape(nb, EXPERT_ROWS), h2p_flat,
                        exp_w_gate[li], exp_w_up[li], exp_w_down[li])

    tm_c = 256
    out = _combine(dest, y_sorted, wts.T, x1.reshape(t, d), h2p_flat, mod4,
                   sh_w_gate[li].astype(BF16), sh_w_up[li].astype(BF16), sh_w_down[li].astype(BF16),
                   final_norm_w.reshape(1, d), l_ // tm_c, tm_c)
    return out.reshape(b_, l_, d)
```

```python
import functools
import math

import jax
import jax.numpy as jnp
from jax import lax
from jax.experimental import pallas as pl
from jax.experimental.pallas import tpu as pltpu

F32 = jnp.float32
BF16 = jnp.bfloat16
U32 = jnp.uint32
I32 = jnp.int32

EPS = 1e-6
N_MOD = 6

S5_GROUP = 16
S5_STATE = 64
S5_CHUNK = 16
S5_GROUPS_PER_STEP = 8

GLA_HEADS = 4
GLA_GATE_RANK = 16
GLA_GATE_NORM = 16.0
GLA_CHUNK = 64

N_EXPERTS = 64
TOP_K = 8
N_GROUPS = 8
TOPK_GROUPS = 4
ROUTED_SCALE = 2.5
EXPERT_ROWS = 256

VMEM_LIMIT = 56 * 1024 * 1024


def _cparams(sem, vmem=VMEM_LIMIT):
    return pltpu.CompilerParams(dimension_semantics=sem, vmem_limit_bytes=vmem)


def _dot(a, b):
    return jnp.dot(a, b, preferred_element_type=F32)


def _sigmoid(x):
    return 1.0 / (1.0 + jnp.exp(-x))


def _silu(x):
    return x * _sigmoid(x)


def _pack_bf16_pair(lo, hi):
    lo_bits = lax.bitcast_convert_type(lo.astype(BF16).astype(F32), U32)
    hi_bits = lax.bitcast_convert_type(hi.astype(BF16).astype(F32), U32)
    return (lo_bits >> 16) | (hi_bits & jnp.uint32(0xFFFF0000))


def _unpack_bf16_pair(w):
    lo = lax.bitcast_convert_type(w << 16, F32)
    hi = lax.bitcast_convert_type(w & jnp.uint32(0xFFFF0000), F32)
    return lo, hi


def _store_row_tiles(ref, lead, val):
    rows, rt = val.shape[0], val.shape[1] // 128
    for kc in range(rt):
        ref[(*lead, pl.ds(kc, rows, stride=rt), slice(None))] = val[:, kc * 128:(kc + 1) * 128]


def _load_row_tiles(ref, lead, row0, rows, rt):
    return jnp.concatenate(
        [ref[(*lead, pl.ds(row0 * rt + kc, rows, stride=rt), slice(None))] for kc in range(rt)], axis=-1)


def _ada_body(c_ref, w_ref, b_ref, o_ref):
    c = c_ref[...]
    s = _silu(c).astype(BF16)
    o_ref[...] = _dot(s, w_ref[...].astype(BF16)) + b_ref[...]


def _ada(c8, ada_w, ada_b):
    d, n = ada_w.shape
    tn = 512
    return pl.pallas_call(
        _ada_body,
        grid=(n // tn,),
        in_specs=[
            pl.BlockSpec((8, d), lambda j: (0, 0)),
            pl.BlockSpec((d, tn), lambda j: (0, j)),
            pl.BlockSpec((1, tn), lambda j: (0, j)),
        ],
        out_specs=pl.BlockSpec((8, tn), lambda j: (0, j)),
        out_shape=jax.ShapeDtypeStruct((8, n), F32),
        compiler_params=_cparams(("parallel",)),
        name="ada",
    )(c8, ada_w, ada_b.reshape(1, n))


def _modulated_norm(x, nw, sh, sc):
    ms = jnp.mean(x * x, axis=-1, keepdims=True)
    y = x * lax.rsqrt(ms + EPS) * nw
    return y * (1.0 + sc) + sh


def _inproj_body(x_ref, sh_ref, sc_ref, nw_ref, w_ref, wg_ref, p_ref, g_ref, h_scr):
    @pl.when(pl.program_id(2) == 0)
    def _():
        h = _modulated_norm(x_ref[0], nw_ref[...], sh_ref[0, 0], sc_ref[0, 0]).astype(BF16)
        h_scr[...] = h
        g_ref[0] = _dot(h, wg_ref[...])

    p_ref[0] = _dot(h_scr[...], w_ref[...]).astype(BF16)


def _inproj(x, mod4, mod_row, nw, w, wg, tm, tn):
    b_, l_, d = x.shape
    n = w.shape[1]
    return pl.pallas_call(
        _inproj_body,
        grid=(b_, l_ // tm, n // tn),
        in_specs=[
            pl.BlockSpec((1, tm, d), lambda b, i, j: (b, i, 0)),
            pl.BlockSpec((1, 1, 1, d), lambda b, i, j: (mod_row(b), 0, 0, 0)),
            pl.BlockSpec((1, 1, 1, d), lambda b, i, j: (mod_row(b), 1, 0, 0)),
            pl.BlockSpec((1, d), lambda b, i, j: (0, 0)),
            pl.BlockSpec((d, tn), lambda b, i, j: (0, j)),
            pl.BlockSpec((d, 128), lambda b, i, j: (0, 0)),
        ],
        out_specs=[
            pl.BlockSpec((1, tm, tn), lambda b, i, j: (b, i, j)),
            pl.BlockSpec((1, tm, 128), lambda b, i, j: (b, i, 0)),
        ],
        out_shape=[
            jax.ShapeDtypeStruct((b_, l_, n), BF16),
            jax.ShapeDtypeStruct((b_, l_, 128), F32),
        ],
        scratch_shapes=[pltpu.VMEM((tm, d), BF16)],
        compiler_params=_cparams(("parallel", "parallel", "arbitrary")),
        name="inproj",
    )(x, mod4, mod4, nw, w, wg)


def _s5_weights(lam_re, lam_im, log_dt, b_re, b_im, c_re, c_im, d_skip):
    hp = lax.Precision.HIGHEST
    tc = S5_CHUNK
    g_, p_ = lam_re.shape[1:]
    h_ = b_re.shape[-1]
    dt = jnp.exp(log_dt)[..., None]
    ar, ai = lam_re * dt, lam_im * dt
    k = jnp.arange(tc + 1, dtype=F32)
    mag = jnp.exp(ar[..., None] * k)
    pw_re = mag * jnp.cos(ai[..., None] * k)
    pw_im = mag * jnp.sin(ai[..., None] * k)
    num_re = jnp.expm1(ar) * jnp.cos(ai) - 2.0 * jnp.sin(0.5 * ai) ** 2
    num_im = jnp.exp(ar) * jnp.sin(ai)
    den = lam_re * lam_re + lam_im * lam_im
    f_re = (num_re * lam_re + num_im * lam_im) / den
    f_im = (num_im * lam_re - num_re * lam_im) / den
    bb_re = f_re[..., None] * b_re - f_im[..., None] * b_im
    bb_im = f_re[..., None] * b_im + f_im[..., None] * b_re

    cp_re = c_re[..., None] * pw_re[:, :, None] - c_im[..., None] * pw_im[:, :, None]
    cp_im = c_re[..., None] * pw_im[:, :, None] + c_im[..., None] * pw_re[:, :, None]
    kk = (jnp.einsum('dgipt,dgpj->dgtij', cp_re, bb_re, precision=hp)
          - jnp.einsum('dgipt,dgpj->dgtij', cp_im, bb_im, precision=hp))

    s_idx = jnp.arange(tc)[:, None]
    t_idx = jnp.arange(tc)[None, :]
    lag_f = t_idx - s_idx
    lag_b = s_idx - t_idx
    kf = jnp.where((lag_f >= 0)[None, :, :, None, None], kk[0][:, jnp.clip(lag_f, 0, tc)], 0.0)
    kb = jnp.where((lag_b >= 0)[None, :, :, None, None], kk[1][:, jnp.clip(lag_b, 0, tc)], 0.0)
    dsk = d_skip.reshape(g_, h_)
    eye_t = jnp.eye(tc, dtype=F32)[None, :, :, None, None]
    eye_h = jnp.eye(h_, dtype=F32)[None, None, None]
    m_all = kf + kb + eye_t * eye_h * dsk[:, None, None, :, None]
    m_all = m_all.transpose(0, 1, 4, 2, 3).reshape(g_, tc * h_, tc * h_)

    def state_w(dr, expo):
        e_re = pw_re[dr][:, :, expo]
        e_im = pw_im[dr][:, :, expo]
        w_re = jnp.einsum('gps,gpj->gsjp', e_re, bb_re[dr], precision=hp) - jnp.einsum(
            'gps,gpj->gsjp', e_im, bb_im[dr], precision=hp)
        w_im = jnp.einsum('gps,gpj->gsjp', e_re, bb_im[dr], precision=hp) + jnp.einsum(
            'gps,gpj->gsjp', e_im, bb_re[dr], precision=hp)
        w_re = w_re.reshape(g_, tc * h_, p_)
        w_im = w_im.reshape(g_, tc * h_, p_)
        return jnp.concatenate([w_re, w_im, w_im, w_re], axis=-1)

    ws_f = state_w(0, tc - 1 - jnp.arange(tc))
    ws_b = state_w(1, jnp.arange(tc))
    wcat = jnp.concatenate([m_all, ws_f, ws_b], axis=-1)

    def out_w(dr, expo):
        q_re = cp_re[dr][..., expo]
        q_im = cp_im[dr][..., expo]
        top = q_re.transpose(0, 2, 3, 1).reshape(g_, p_, tc * h_)
        bot = (-q_im).transpose(0, 2, 3, 1).reshape(g_, p_, tc * h_)
        return jnp.concatenate([top, bot], axis=1)

    wout = jnp.concatenate([out_w(0, 1 + jnp.arange(tc)), out_w(1, tc - jnp.arange(tc))], axis=1)

    a_re, a_im = pw_re[..., tc], pw_im[..., tc]
    zeros = jnp.zeros_like(a_re[0])
    rows = []
    for dr in range(2):
        rows += [jnp.concatenate([a_re[dr], a_re[dr]], -1),
                 jnp.concatenate([-a_im[dr], a_im[dr]], -1),
                 jnp.concatenate([a_im[dr], -a_im[dr]], -1)]
    rows += [jnp.concatenate([zeros, zeros], -1)] * 2
    acoef = jnp.stack(rows, axis=1)
    return wcat.astype(BF16), wout.astype(BF16), acoef


def _s5_body(x_ref, wcat_ref, wout_ref, a_ref, y_ref, y_scr, s_scr, h_scr, *, n_ctx, n_lat, nb):
    gb = x_ref.shape[0]
    pw = 2 * S5_STATE
    for g in range(gb):
        r = _dot(x_ref[g], wcat_ref[g])
        y_scr[g] = r[:, : y_scr.shape[-1]]
        s_scr[g] = r[:, y_scr.shape[-1]:]

    a1f, a2f, a2sf = a_ref[:, 0:1, :], a_ref[:, 1:2, :], a_ref[:, 2:3, :]
    a1b, a2b, a2sb = a_ref[:, 3:4, :], a_ref[:, 4:5, :], a_ref[:, 5:6, :]

    first = lax.broadcasted_iota(I32, (gb, 2 * nb, pw), 1) < nb

    def both(v):
        r = pltpu.roll(v, nb, 1)
        return jnp.where(first, v, r), jnp.where(first, r, v)

    def step(pf, pb, carry):
        hf, hfs, hb, hbs = carry
        rf = pl.ds(pl.multiple_of(pf * 2 * nb, 2 * nb), 2 * nb)
        rb = pl.ds(pl.multiple_of(pb * 2 * nb, 2 * nb), 2 * nb)
        sfa, sfb = both(s_scr[:, rf, 0:pw])
        sfsa, sfsb = both(s_scr[:, rf, pw:2 * pw])
        sba, sbb = both(s_scr[:, rb, 2 * pw:3 * pw])
        sbsa, sbsb = both(s_scr[:, rb, 3 * pw:4 * pw])
        hf1 = a1f * hf + a2f * hfs + sfa
        hfs1 = a1f * hfs + a2sf * hf + sfsa
        h_scr[:, rf, 0:pw] = jnp.where(first, hf, hf1)
        hf2 = a1f * hf1 + a2f * hfs1 + sfb
        hfs2 = a1f * hfs1 + a2sf * hf1 + sfsb
        hb1 = a1b * hb + a2b * hbs + sbb
        hbs1 = a1b * hbs + a2sb * hb + sbsb
        h_scr[:, rb, pw:2 * pw] = jnp.where(first, hb1, hb)
        hb2 = a1b * hb1 + a2b * hbs1 + sba
        hbs2 = a1b * hbs1 + a2sb * hb1 + sbsa
        return hf2, hfs2, hb2, hbs2

    z = jnp.zeros((gb, 2 * nb, pw), F32)
    carry = (z, z, z, z)
    pc, pl_ = n_ctx // 2, n_lat // 2
    carry = lax.fori_loop(0, pc, lambda i, c: step(i, pc - 1 - i, c), carry)
    carry = lax.fori_loop(0, pl_, lambda i, c: step(pc + i, pc + pl_ - 1 - i, c), carry)

    lo = n_ctx * nb
    for g in range(gb):
        y = y_scr[g] + _dot(h_scr[g].astype(BF16), wout_ref[g])
        y_ref[g] = y[lo:, :]


def _s5(x2, wcat, wout, acoef, n_ctx, n_lat, nb):
    g_, rows, kdim = x2.shape
    gb = S5_GROUPS_PER_STEP
    pw = 2 * S5_STATE
    out_rows = n_lat * nb
    return pl.pallas_call(
        functools.partial(_s5_body, n_ctx=n_ctx, n_lat=n_lat, nb=nb),
        grid=(g_ // gb,),
        in_specs=[
            pl.BlockSpec((gb, rows, kdim), lambda i: (i, 0, 0)),
            pl.BlockSpec((gb, kdim, wcat.shape[-1]), lambda i: (i, 0, 0)),
            pl.BlockSpec((gb, 2 * pw, kdim), lambda i: (i, 0, 0)),
            pl.BlockSpec((gb, 8, pw), lambda i: (i, 0, 0)),
        ],
        out_specs=pl.BlockSpec((gb, out_rows, kdim), lambda i: (i, 0, 0)),
        out_shape=jax.ShapeDtypeStruct((g_, out_rows, kdim), F32),
        scratch_shapes=[
            pltpu.VMEM((gb, rows, kdim), F32),
            pltpu.VMEM((gb, rows, 4 * pw), F32),
            pltpu.VMEM((gb, rows, 2 * pw), F32),
        ],
        compiler_params=_cparams(("parallel",)),
        name="s5",
    )(x2, wcat, wout, acoef)


def _log_sigmoid(z):
    return jnp.minimum(z, 0.0) - jnp.log1p(jnp.exp(-jnp.abs(z)))


def _chunk_cumsum(g, pos, reverse):
    n = g.shape[0]
    x = g
    sh = 1
    while sh < GLA_CHUNK:
        if reverse:
            x = x + jnp.where(pos < GLA_CHUNK - sh, pltpu.roll(x, n - sh, 0), 0.0)
        else:
            x = x + jnp.where(pos >= sh, pltpu.roll(x, sh, 0), 0.0)
        sh *= 2
    return x


def _gla_body(q_ref, k_ref, v_ref, kc_ref, vc_ref, gk_ref, gkc_ref, up_ref, bias_ref, o_ref,
              bc_scr, kb_scr, qb_scr, st_ref, *, scale):
    c = GLA_CHUNK
    l_ = q_ref.shape[1]
    lc = kc_ref.shape[1]
    n_lat, n_ctx = l_ // c, lc // c
    row = lax.broadcasted_iota(I32, (c, c), 0)
    col = lax.broadcasted_iota(I32, (c, c), 1)
    nt = (((1,), (1,)), ((), ()))
    tn = (((0,), (0,)), ((), ()))

    for dr in range(2):
        smask = (col <= row) if dr == 0 else (col > row)
        up = up_ref[dr]
        bias = bias_ref[dr]
        edge_tile, edge_row = (c - 8, 7) if dr == 0 else (0, 0)

        for gk_r, k_r, base, n, latent in ((gkc_ref, kc_ref, 0, lc, False), (gk_ref, k_ref, lc, l_, True)):
            z = _dot(gk_r[0].astype(BF16), up) + bias
            g = _log_sigmoid(z) * (1.0 / GLA_GATE_NORM)
            pos = lax.broadcasted_iota(I32, g.shape, 0) & (c - 1)
            bc = _chunk_cumsum(g, pos, reverse=(dr == 1))
            bc_scr[base:base + n, :] = bc
            kb_scr[base:base + n, :] = k_r[0].astype(F32) * jnp.exp(-bc)
            if latent:
                qb_scr[...] = (q_ref[0].astype(F32) * (scale * jnp.exp(bc))).astype(BF16)

        def chunk_rows(n, base):
            r0 = pl.multiple_of(n * c, c)
            tot = bc_scr[pl.ds(base + r0 + edge_tile, 8), :][edge_row:edge_row + 1, :]
            return r0, pl.ds(base + r0, c), jnp.exp(tot)

        def update_state(kb, vv, decay):
            kd = (kb * decay).astype(BF16)
            ds_t = lax.dot_general(vv, kd, tn, preferred_element_type=F32)
            st_ref[...] = decay * st_ref[...] + ds_t

        def ctx_step(i, _):
            n = i if dr == 0 else n_ctx - 1 - i
            r0, rs, decay = chunk_rows(n, 0)
            update_state(kb_scr[rs, :], vc_ref[0, pl.ds(r0, c), :], decay)
            return 0

        def lat_step(i, _):
            n = i if dr == 0 else n_lat - 1 - i
            r0, rs, decay = chunk_rows(n, lc)
            ro = pl.ds(r0, c)
            kb = kb_scr[rs, :]
            vv = v_ref[0, ro, :]
            qb = qb_scr[ro, :]
            scores = lax.dot_general(qb, kb.astype(BF16), nt, preferred_element_type=F32)
            scores = jnp.where(smask, scores, 0.0).astype(BF16)
            o = _dot(scores, vv) + lax.dot_general(qb, st_ref[...].astype(BF16), nt,
                                                   preferred_element_type=F32)
            if dr == 0:
                o_ref[0, ro, :] = o
            else:
                o_ref[0, ro, :] = o_ref[0, ro, :] + o
            update_state(kb, vv, decay)
            return 0

        st_ref[...] = jnp.zeros_like(st_ref)
        lax.fori_loop(0, n_ctx, ctx_step, 0, unroll=2)
        lax.fori_loop(0, n_lat, lat_step, 0, unroll=4)


def _gla(p_lat, p_ctx, gk_lat, gk_ctx, up_pad, bias, dk, dv):
    b_, l_, _ = p_lat.shape
    lc = p_ctx.shape[1]
    hh = GLA_HEADS
    s5w = hh * dv
    q0, k0, v0 = s5w // dk, (s5w + hh * dk) // dk, (s5w + 2 * hh * dk) // dv
    kc0, vc0 = s5w // dk, (s5w + hh * dk) // dv
    return pl.pallas_call(
        functools.partial(_gla_body, scale=dk ** -0.5),
        grid=(b_, hh),
        in_specs=[
            pl.BlockSpec((1, l_, dk), lambda b, h: (b, 0, q0 + h)),
            pl.BlockSpec((1, l_, dk), lambda b, h: (b, 0, k0 + h)),
            pl.BlockSpec((1, l_, dv), lambda b, h: (b, 0, v0 + h)),
            pl.BlockSpec((1, lc, dk), lambda b, h: (b, 0, kc0 + h)),
            pl.BlockSpec((1, lc, dv), lambda b, h: (b, 0, vc0 + h)),
            pl.BlockSpec((1, l_, 128), lambda b, h: (b, 0, 0)),
            pl.BlockSpec((1, lc, 128), lambda b, h: (b, 0, 0)),
            pl.BlockSpec((2, 128, dk), lambda b, h: (0, 0, h)),
            pl.BlockSpec((2, 1, dk), lambda b, h: (0, 0, h)),
        ],
        out_specs=pl.BlockSpec((1, l_, dv), lambda b, h: (b, 0, h)),
        out_shape=jax.ShapeDtypeStruct((b_, l_, hh * dv), F32),
        scratch_shapes=[
            pltpu.VMEM((lc + l_, dk), F32),
            pltpu.VMEM((lc + l_, dk), F32),
            pltpu.VMEM((l_, dk), BF16),
            pltpu.VMEM((dv, dk), F32),
        ],
        compiler_params=_cparams(("parallel", "parallel")),
        name="gla",
    )(p_lat, p_lat, p_lat, p_ctx, p_ctx, gk_lat, gk_ctx, up_pad, bias)


def _gelu_tanh(x):
    return 0.5 * x * (1.0 + jnp.tanh(math.sqrt(2.0 / math.pi) * (x + 0.044715 * (x * x * x))))


def _merge_body(ys_ref, yg_ref, go_ref, gs_ref, gg_ref, x_ref, g1_ref, sh2_ref, sc2_ref,
                glu_ref, ws5_ref, wgla_ref, wout_ref, gnw_ref, n2w_ref, rw_ref,
                x1_ref, h2_ref, lg_ref, *, dv):
    a = _gelu_tanh(ys_ref[0])
    a = a * _sigmoid(_dot(a.astype(BF16), glu_ref[...]))
    pa = _dot(a.astype(BF16), ws5_ref[...])

    yg = yg_ref[0]
    parts = []
    for h in range(GLA_HEADS):
        oh = yg[:, h * dv:(h + 1) * dv]
        ms = jnp.mean(oh * oh, axis=-1, keepdims=True)
        parts.append(oh * lax.rsqrt(ms + EPS) * gnw_ref[...])
    gl = jnp.concatenate(parts, axis=-1) * _silu(go_ref[0].astype(F32))
    pb = _dot(gl.astype(BF16), wgla_ref[...])

    m = _sigmoid(gs_ref[0].astype(F32)) * pa + _sigmoid(gg_ref[0].astype(F32)) * pb
    y = _dot(m.astype(BF16), wout_ref[...])
    x1 = x_ref[0] + g1_ref[0, 0] * y
    x1_ref[0] = x1

    h2 = _modulated_norm(x1, n2w_ref[...], sh2_ref[0, 0], sc2_ref[0, 0])
    h_hi = h2.astype(BF16)
    h_lo = (h2 - h_hi.astype(F32)).astype(BF16)
    half = h2.shape[-1] // 2
    _store_row_tiles(h2_ref, (0,), _pack_bf16_pair(h2[:, :half], h2[:, half:]))
    lg_ref[0] = _dot(h_hi, rw_ref[0]) + _dot(h_lo, rw_ref[0]) + _dot(h_hi, rw_ref[1])


def _resident(shape):
    nd = len(shape)
    return pl.BlockSpec(shape, lambda b, i: (0,) * nd, pipeline_mode=pl.Buffered(1))


def _merge(ys, yg, p_lat, x, mod4, glu_w, ws5, wgla, wout, gnw, n2w, rw2, dv, tm):
    b_, l_, d = x.shape
    w5 = ys.shape[-1]
    go0 = (p_lat.shape[-1] - 2 * d - w5) // w5
    gs0 = (p_lat.shape[-1] - 2 * d) // d
    modspec = lambda r: pl.BlockSpec((1, 1, 1, d), lambda b, i: (b, r, 0, 0))
    return pl.pallas_call(
        functools.partial(_merge_body, dv=dv),
        grid=(b_, l_ // tm),
        in_specs=[
            pl.BlockSpec((1, tm, w5), lambda b, i: (b, i, 0)),
            pl.BlockSpec((1, tm, w5), lambda b, i: (b, i, 0)),
            pl.BlockSpec((1, tm, w5), lambda b, i: (b, i, go0)),
            pl.BlockSpec((1, tm, d), lambda b, i: (b, i, gs0)),
            pl.BlockSpec((1, tm, d), lambda b, i: (b, i, gs0 + 1)),
            pl.BlockSpec((1, tm, d), lambda b, i: (b, i, 0)),
            modspec(2), modspec(3), modspec(4),
            _resident(glu_w.shape), _resident(ws5.shape), _resident(wgla.shape), _resident(wout.shape),
            _resident(gnw.shape), _resident(n2w.shape), _resident(rw2.shape),
        ],
        out_specs=[
            pl.BlockSpec((1, tm, d), lambda b, i: (b, i, 0)),
            pl.BlockSpec((1, tm * (d // 256), 128), lambda b, i: (b, i, 0)),
            pl.BlockSpec((1, tm, 128), lambda b, i: (b, i, 0)),
        ],
        out_shape=[
            jax.ShapeDtypeStruct((b_, l_, d), F32),
            jax.ShapeDtypeStruct((b_, l_ * (d // 256), 128), U32),
            jax.ShapeDtypeStruct((b_, l_, 128), F32),
        ],
        compiler_params=_cparams(("parallel", "parallel")),
        name="merge",
    )(ys, yg, p_lat, p_lat, p_lat, x, mod4, mod4, mod4, glu_w, ws5, wgla, wout, gnw, n2w, rw2)


def _route_body(lg_ref, bias_ref, eidx_ref, w_ref, rank_ref, cnt_ref, carry_ref):
    ne, tt = lg_ref.shape
    gsz = ne // N_GROUPS
    neg = -jnp.inf

    @pl.when(pl.program_id(0) == 0)
    def _():
        carry_ref[...] = jnp.zeros_like(carry_ref)

    sc = _sigmoid(lg_ref[...])
    ch = sc + bias_ref[...]

    def first_max(v, idx, big):
        m = jnp.max(v, axis=0, keepdims=True)
        i = jnp.min(jnp.where(v == m, idx, big), axis=0, keepdims=True)
        return m, i

    midx = lax.broadcasted_iota(I32, (gsz, tt), 0).astype(F32)
    gs_rows = []
    for g in range(N_GROUPS):
        v = ch[g * gsz:(g + 1) * gsz, :]
        m1, i1 = first_max(v, midx, float(gsz))
        m2 = jnp.max(jnp.where(midx == i1, neg, v), axis=0, keepdims=True)
        gs_rows.append(m1 + m2)
    gscore = jnp.concatenate(gs_rows, axis=0)

    gidx = lax.broadcasted_iota(I32, (N_GROUPS, tt), 0).astype(F32)
    gsel = jnp.zeros((N_GROUPS, tt), F32)
    cur = gscore
    for _ in range(TOPK_GROUPS):
        _, i = first_max(cur, gidx, float(N_GROUPS))
        hit = gidx == i
        gsel = jnp.where(hit, 1.0, gsel)
        cur = jnp.where(hit, neg, cur)

    masked = jnp.concatenate(
        [jnp.where(gsel[g:g + 1, :] > 0.0, ch[g * gsz:(g + 1) * gsz, :], neg) for g in range(N_GROUPS)], axis=0)

    eiota = lax.broadcasted_iota(I32, (ne, tt), 0).astype(F32)
    sel = jnp.zeros((ne, tt), F32)
    cur = masked
    idx_rows, s_rows = [], []
    for _ in range(TOP_K):
        _, i = first_max(cur, eiota, float(ne))
        hit = eiota == i
        idx_rows.append(i)
        s_rows.append(jnp.sum(jnp.where(hit, sc, 0.0), axis=0, keepdims=True))
        sel = jnp.where(hit, 1.0, sel)
        cur = jnp.where(hit, neg, cur)
    idx = jnp.concatenate(idx_rows, axis=0)
    s = jnp.concatenate(s_rows, axis=0)
    w_ref[...] = s / jnp.sum(s, axis=0, keepdims=True) * ROUTED_SCALE
    eidx_ref[...] = idx.astype(I32)

    cw = 256
    a_i = lax.broadcasted_iota(I32, (cw, cw), 0)
    b_i = lax.broadcasted_iota(I32, (cw, cw), 1)
    upper = jnp.where(a_i < b_i, 1.0, 0.0).astype(BF16)
    carry = carry_ref[:, 0:1]
    pieces = []
    for c0 in range(0, tt, cw):
        sc_c = sel[:, c0:c0 + cw]
        pieces.append(_dot(sc_c.astype(BF16), upper) + carry)
        carry = carry + jnp.sum(sc_c, axis=1, keepdims=True)
    rank_full = jnp.concatenate(pieces, axis=1)
    carry_ref[...] = jnp.broadcast_to(carry, carry_ref.shape)
    cnt_ref[...] = jnp.broadcast_to(carry, cnt_ref.shape)
    rk = [jnp.sum(jnp.where(eiota == idx_rows[k], rank_full, 0.0), axis=0, keepdims=True) for k in range(TOP_K)]
    rank_ref[...] = jnp.concatenate(rk, axis=0).astype(I32)


def _route(logits_t, bias):
    ne, t = logits_t.shape
    tt = 1024
    return pl.pallas_call(
        _route_body,
        grid=(t // tt,),
        in_specs=[
            pl.BlockSpec((ne, tt), lambda i: (0, i)),
            pl.BlockSpec((ne, 1), lambda i: (0, 0)),
        ],
        out_specs=[
            pl.BlockSpec((TOP_K, tt), lambda i: (0, i)),
            pl.BlockSpec((TOP_K, tt), lambda i: (0, i)),
            pl.BlockSpec((TOP_K, tt), lambda i: (0, i)),
            pl.BlockSpec((ne, 128), lambda i: (0, 0)),
        ],
        out_shape=[
            jax.ShapeDtypeStruct((TOP_K, t), I32),
            jax.ShapeDtypeStruct((TOP_K, t), F32),
            jax.ShapeDtypeStruct((TOP_K, t), I32),
            jax.ShapeDtypeStruct((ne, 128), F32),
        ],
        scratch_shapes=[pltpu.VMEM((ne, 128), F32)],
        compiler_params=_cparams(("arbitrary",)),
        name="route",
    )(logits_t, bias.reshape(ne, 1))


def _slots_body(cnt_ref, eidx_ref, rank_ref, dest_ref, rows_ref, be_ref, first_ref, nxt_ref, nu_ref,
                off_smem, dchunk, sem):
    ne = cnt_ref.shape[0]
    nb = be_ref.shape[0]
    tl = eidx_ref.shape[1]
    r_ = EXPERT_ROWS
    shift = r_.bit_length() - 1
    c = pl.program_id(0)

    @pl.when(c == 0)
    def _():
        def per_expert(e, off):
            cnt = cnt_ref[e]
            n = lax.shift_right_logical(cnt + (r_ - 1), shift)
            off_smem[e] = off

            def fill(j, _):
                be_ref[off + j] = e
                first_ref[off + j] = jnp.where(j == 0, 1, 0)
                return 0
            lax.fori_loop(0, n, fill, 0)

            def pad(j, _):
                rows_ref[off * r_ + j] = 0
                return 0
            lax.fori_loop(cnt, n * r_, pad, 0)
            return off + n

        n_used = lax.fori_loop(0, ne, per_expert, 0)
        nu_ref[0] = n_used
        last_e = be_ref[n_used - 1]

        def tail(j, _):
            be_ref[j] = last_e
            first_ref[j] = 0

            def zero_row(q, _):
                rows_ref[j * r_ + q] = 0
                return 0
            lax.fori_loop(0, r_, zero_row, 0)
            return 0
        lax.fori_loop(n_used, nb, tail, 0)

        def back(i, nx):
            j = nb - 1 - i
            nxt_ref[j] = nx
            return jnp.where(first_ref[j] == 1, be_ref[j], nx)
        lax.fori_loop(0, nb, back, -1)

    ei = eidx_ref[...]
    acc = rank_ref[...]
    for e in range(ne):
        acc = acc + jnp.where(ei == e, off_smem[e] * r_, 0)
    dest_ref[...] = acc
    cp = pltpu.make_async_copy(dest_ref, dchunk, sem)
    cp.start()
    cp.wait()
    base = c * tl

    def scatter(j, _):
        for k in range(TOP_K):
            rows_ref[dchunk[k, j]] = base + j
        return 0
    lax.fori_loop(0, tl, scatter, 0)


def _slots(counts, eidx, rank, nb):
    k_, t = eidx.shape
    tl = 1024
    smem = lambda: pl.BlockSpec(memory_space=pltpu.SMEM)
    chunk = lambda: pl.BlockSpec((k_, tl), lambda i: (0, i))
    return pl.pallas_call(
        _slots_body,
        grid=(t // tl,),
        in_specs=[smem(), chunk(), chunk()],
        out_specs=[chunk(), smem(), smem(), smem(), smem(), smem()],
        out_shape=[
            jax.ShapeDtypeStruct((k_, t), I32),
            jax.ShapeDtypeStruct((nb * EXPERT_ROWS,), I32),
            jax.ShapeDtypeStruct((nb,), I32),
            jax.ShapeDtypeStruct((nb,), I32),
            jax.ShapeDtypeStruct((nb,), I32),
            jax.ShapeDtypeStruct((1,), I32),
        ],
        scratch_shapes=[
            pltpu.SMEM((counts.shape[0],), I32),
            pltpu.SMEM((k_, tl), I32),
            pltpu.SemaphoreType.DMA,
        ],
        compiler_params=_cparams(("arbitrary",)),
        name="slots",
    )(counts, eidx, rank)


EXPERT_ISSUE_GROUPS = 8
EXPERT_LOOKAHEAD = 2


def _expert_body(be_ref, first_ref, nxt_ref, nu_ref, rows_hbm, x_hbm, wg_hbm, wu_hbm, wd_hbm, y_ref,
                 idx_smem, xbuf, wg_f, wu_f, wd_f, wg_s, wu_s, wd_s, isem, gsem, wsem):
    i = pl.program_id(0)
    n_used = nu_ref[0]
    r_ = EXPERT_ROWS
    tiles = x_hbm.shape[1]
    ns = EXPERT_LOOKAHEAD + 1
    slot = lax.rem(i, ns)
    aslot = lax.rem(i + EXPERT_LOOKAHEAD, ns)

    def idx_copy(blk, s):
        return pltpu.make_async_copy(rows_hbm.at[pl.ds(blk, 1)], idx_smem.at[s], isem.at[s])

    def row_copy(s, r):
        return pltpu.make_async_copy(x_hbm.at[idx_smem[s, 0, r]], xbuf.at[s, pl.ds(r * tiles, tiles)], gsem.at[s])

    def issue_rows(s, g):
        step = r_ // EXPERT_ISSUE_GROUPS
        for r in range(g * step, (g + 1) * step):
            row_copy(s, r).start(priority=r % 2)

    def wait_rows(s):
        pltpu.make_async_copy(xbuf.at[s], xbuf.at[s], gsem.at[s]).wait()

    def weight_copies(e):
        return (pltpu.make_async_copy(wg_hbm.at[e], wg_f, wsem.at[0]),
                pltpu.make_async_copy(wu_hbm.at[e], wu_f, wsem.at[1]),
                pltpu.make_async_copy(wd_hbm.at[e], wd_f, wsem.at[2]))

    @pl.when(i == 0)
    def _():
        for cp in weight_copies(be_ref[0]):
            cp.start()
        for b in range(EXPERT_LOOKAHEAD):
            idx_copy(b, b).start()
        for b in range(EXPERT_LOOKAHEAD):
            idx_copy(b, b).wait()

            def body(r, _, b=b):
                row_copy(b, r).start()
                return 0
            lax.fori_loop(0, r_, body, 0)
        idx_copy(EXPERT_LOOKAHEAD, EXPERT_LOOKAHEAD).start()

    @pl.when(first_ref[i] == 1)
    def _():
        for cp in weight_copies(be_ref[i]):
            cp.wait()
        wg_s[...] = wg_f[...].astype(BF16)
        wu_s[...] = wu_f[...].astype(BF16)
        wd_s[...] = wd_f[...].astype(BF16)

        @pl.when(nxt_ref[i] >= 0)
        def _():
            for cp in weight_copies(nxt_ref[i]):
                cp.start()

    @pl.when(i < n_used)
    def _():
        idx_copy(i + EXPERT_LOOKAHEAD, aslot).wait()
        wait_rows(slot)
        lo, hi = _unpack_bf16_pair(_load_row_tiles(xbuf, (slot,), 0, r_, tiles))
        lo = lo.astype(BF16)
        hi = hi.astype(BF16)
        half = lo.shape[-1]
        issue_rows(aslot, 0)
        a = _dot(lo, wg_s[0:half, :])
        issue_rows(aslot, 1)
        a = a + _dot(hi, wg_s[half:, :])
        issue_rows(aslot, 2)
        u = _dot(lo, wu_s[0:half, :])
        issue_rows(aslot, 3)
        u = u + _dot(hi, wu_s[half:, :])
        issue_rows(aslot, 4)
        hid = (_silu(a) * u).astype(BF16)
        y_lo = _dot(hid, wd_s[:, 0:half])
        issue_rows(aslot, 5)
        y_hi = _dot(hid, wd_s[:, half:])
        issue_rows(aslot, 6)
        _store_row_tiles(y_ref, (), _pack_bf16_pair(y_lo, y_hi))
        issue_rows(aslot, 7)

        @pl.when(i + 1 < n_used)
        def _():
            idx_copy(i + EXPERT_LOOKAHEAD + 1, slot).start()

    @pl.when(jnp.logical_and(i >= n_used, i < n_used + EXPERT_LOOKAHEAD))
    def _():
        wait_rows(slot)

    @pl.when(i >= n_used)
    def _():
        y_ref[...] = jnp.zeros_like(y_ref)


def _experts(block_e, first, nxt, n_used, rows, h2p, wg, wu, wd):
    nb = block_e.shape[0]
    t, tiles, _ = h2p.shape
    ne, d, f = wg.shape
    r_ = EXPERT_ROWS
    hbm = lambda: pl.BlockSpec(memory_space=pl.ANY)
    grid_spec = pltpu.PrefetchScalarGridSpec(
        num_scalar_prefetch=4,
        grid=(nb,),
        in_specs=[hbm(), hbm(), hbm(), hbm(), hbm()],
        out_specs=pl.BlockSpec((r_ * tiles, 128), lambda i, *_: (i, 0)),
        scratch_shapes=[
            pltpu.SMEM((EXPERT_LOOKAHEAD + 1, 1, r_), I32),
            pltpu.VMEM((EXPERT_LOOKAHEAD + 1, r_ * tiles, 128), U32),
            pltpu.VMEM((d, f), F32),
            pltpu.VMEM((d, f), F32),
            pltpu.VMEM((f, d), F32),
            pltpu.VMEM((d, f), BF16),
            pltpu.VMEM((d, f), BF16),
            pltpu.VMEM((f, d), BF16),
            pltpu.SemaphoreType.DMA((EXPERT_LOOKAHEAD + 1,)),
            pltpu.SemaphoreType.DMA((EXPERT_LOOKAHEAD + 1,)),
            pltpu.SemaphoreType.DMA((3,)),
        ],
    )
    return pl.pallas_call(
        _expert_body,
        grid_spec=grid_spec,
        out_shape=jax.ShapeDtypeStruct((nb * r_ * tiles, 128), U32),
        compiler_params=_cparams(("arbitrary",)),
        name="experts",
    )(block_e, first, nxt, n_used, rows, h2p, wg, wu, wd)


def _combine_body(dest_hbm, y_hbm, w_ref, x1_ref, h2_ref, g2_ref, sg_ref, su_ref, sd_ref, fnw_ref, o_ref,
                  idx_smem, ybuf, isem, gsem):
    i = pl.program_id(0)
    n = pl.num_programs(0)
    tm = x1_ref.shape[0]
    tiles = y_hbm.shape[1]
    slot = i % 2
    nslot = 1 - slot

    def idx_copy(blk, s):
        cols = pl.ds(pl.multiple_of(blk * tm, tm), tm)
        return pltpu.make_async_copy(dest_hbm.at[:, cols], idx_smem.at[s], isem.at[s])

    def issue_rows(s):
        def body(j, _):
            for k in range(TOP_K):
                dst = pl.ds(pl.multiple_of((k * tm + j) * tiles, tiles), tiles)
                pltpu.make_async_copy(y_hbm.at[idx_smem[s, k, j]], ybuf.at[s, dst],
                                      gsem.at[s]).start(priority=k % 2)
            return 0
        lax.fori_loop(0, tm, body, 0)

    def wait_rows(s):
        pltpu.make_async_copy(ybuf.at[s], ybuf.at[s], gsem.at[s]).wait()

    @pl.when(i == 0)
    def _():
        idx_copy(0, 0).start()
        idx_copy(0, 0).wait()
        issue_rows(0)

        @pl.when(n > 1)
        def _():
            idx_copy(1, 1).start()

    @pl.when(i + 1 < n)
    def _():
        idx_copy(i + 1, nslot).wait()
        issue_rows(nslot)

    wait_rows(slot)

    @pl.when(i + 2 < n)
    def _():
        idx_copy(i + 2, slot).start()

    w = w_ref[...]
    half = tiles * 128
    r_lo = [jnp.zeros((tm, 128), F32) for _ in range(tiles)]
    r_hi = [jnp.zeros((tm, 128), F32) for _ in range(tiles)]
    for k in range(TOP_K):
        wk = w[:, k:k + 1]
        for kc in range(tiles):
            lo, hi = _unpack_bf16_pair(ybuf[slot, pl.ds(k * tm * tiles + kc, tm, stride=tiles), :])
            r_lo[kc] = r_lo[kc] + wk * lo
            r_hi[kc] = r_hi[kc] + wk * hi
    routed = jnp.concatenate(r_lo + r_hi, axis=-1)

    x_lo, x_hi = _unpack_bf16_pair(_load_row_tiles(h2_ref, (), 0, tm, tiles))
    x_lo = x_lo.astype(BF16)
    x_hi = x_hi.astype(BF16)
    a = _dot(x_lo, sg_ref[0:half, :]) + _dot(x_hi, sg_ref[half:, :])
    u = _dot(x_lo, su_ref[0:half, :]) + _dot(x_hi, su_ref[half:, :])
    shared = _dot((_silu(a) * u).astype(BF16), sd_ref[...])

    x2 = x1_ref[...] + g2_ref[0, 0] * (routed + shared)
    ms = jnp.mean(x2 * x2, axis=-1, keepdims=True)
    o_ref[...] = x2 * lax.rsqrt(ms + EPS) * fnw_ref[...]


def _combine(dest, y_sorted, w_tok, x1, h2p, mod4, sg, su, sd, fnw, tiles_per_batch, tm):
    t, d = x1.shape
    half = d // 2
    nrow = TOP_K * tm
    res = lambda shape: pl.BlockSpec(shape, lambda i: (0,) * len(shape), pipeline_mode=pl.Buffered(1))
    return pl.pallas_call(
        _combine_body,
        grid=(t // tm,),
        in_specs=[
            pl.BlockSpec(memory_space=pl.ANY),
            pl.BlockSpec(memory_space=pl.ANY),
            pl.BlockSpec((tm, TOP_K), lambda i: (i, 0)),
            pl.BlockSpec((tm, d), lambda i: (i, 0)),
            pl.BlockSpec((tm * (half // 128), 128), lambda i: (i, 0)),
            pl.BlockSpec((1, 1, 1, d), lambda i: (i // tiles_per_batch, 5, 0, 0)),
            res(sg.shape), res(su.shape), res(sd.shape), res(fnw.shape),
        ],
        out_specs=pl.BlockSpec((tm, d), lambda i: (i, 0)),
        out_shape=jax.ShapeDtypeStruct((t, d), F32),
        scratch_shapes=[
            pltpu.SMEM((2, TOP_K, tm), I32),
            pltpu.VMEM((2, nrow * (half // 128), 128), U32),
            pltpu.SemaphoreType.DMA((2,)),
            pltpu.SemaphoreType.DMA((2,)),
        ],
        compiler_params=_cparams(("arbitrary",)),
        name="combine",
    )(dest, y_sorted, w_tok, x1, h2p, mod4, sg, su, sd, fnw)


def kernel(x, c, ctx, c_ctx, ada_w, ada_b, norm1_w, norm2_w, w_in, s5_lam_re, s5_lam_im, s5_log_dt,
           s5_b_re, s5_b_im, s5_c_re, s5_c_im, s5_d, s5_glu_w, gla_gk_up, gla_gk_b, gla_norm_w,
           w_s5_proj, w_gla_proj, w_out, router_w, router_bias, exp_w_gate, exp_w_up, exp_w_down,
           sh_w_gate, sh_w_up, sh_w_down, final_norm_w):
    depth = ada_w.shape[0]
    assert depth == 1, "single-layer block: context outputs are never consumed"
    b_, l_, d = x.shape
    lc = ctx.shape[1]
    s5w = s5_d.shape[1]
    kw = gla_gk_up.shape[-1]
    vw = w_gla_proj.shape[1]
    dk, dv = kw // GLA_HEADS, vw // GLA_HEADS
    rank2 = 2 * GLA_GATE_RANK
    li = 0

    c8 = jnp.zeros((8, d), F32).at[:b_].set(c).at[b_].set(c_ctx)
    mod4 = _ada(c8, ada_w[li], ada_b[li]).reshape(8, N_MOD, 1, d)

    cuts = [0, s5w, s5w + kw, s5w + 2 * kw, s5w + 2 * kw + vw, s5w + 2 * kw + 2 * vw]
    c_gkd = cuts[5]
    w = w_in[li]
    w_main = jnp.concatenate([w[:, :c_gkd], w[:, c_gkd + rank2:]], axis=1).astype(BF16)
    w_ctx = jnp.concatenate([w[:, cuts[0]:cuts[1]], w[:, cuts[2]:cuts[4]]], axis=1).astype(BF16)
    w_gkd = jnp.zeros((d, 128), F32).at[:, :rank2].set(w[:, c_gkd:c_gkd + rank2]).astype(BF16)
    nw1 = norm1_w[li].reshape(1, d)
    p_lat, gk_lat = _inproj(x, mod4, lambda b: b, nw1, w_main, w_gkd, tm=512, tn=1024)
    p_ctx, gk_ctx = _inproj(ctx, mod4, lambda b: b_, nw1, w_ctx, w_gkd, tm=lc, tn=w_ctx.shape[1] // 2)

    g_ = s5w // S5_GROUP
    tc = S5_CHUNK
    n_ctx, n_lat = lc // tc, l_ // tc
    u_all = jnp.concatenate([p_ctx[:, :, :s5w], p_lat[:, :, :s5w]], axis=1)
    x2 = u_all.reshape(b_, n_ctx + n_lat, tc, g_, S5_GROUP).transpose(3, 1, 0, 2, 4)
    x2 = x2.reshape(g_, (n_ctx + n_lat) * b_, tc * S5_GROUP)
    wcat, wout5, acoef = _s5_weights(s5_lam_re[li], s5_lam_im[li], s5_log_dt[li], s5_b_re[li], s5_b_im[li],
                                     s5_c_re[li], s5_c_im[li], s5_d[li])
    y2 = _s5(x2, wcat, wout5, acoef, n_ctx, n_lat, b_)
    ys = y2.reshape(g_, n_lat, b_, tc, S5_GROUP).transpose(2, 1, 3, 0, 4).reshape(b_, l_, s5w)

    up_pad = jnp.zeros((2, 128, kw), F32)
    up_pad = up_pad.at[0, :GLA_GATE_RANK].set(gla_gk_up[li, 0]).at[1, GLA_GATE_RANK:rank2].set(gla_gk_up[li, 1])
    yg = _gla(p_lat, p_ctx, gk_lat, gk_ctx, up_pad.astype(BF16), gla_gk_b[li].reshape(2, 1, kw), dk, dv)

    rw = jnp.zeros((d, 128), F32).at[:, :N_EXPERTS].set(router_w[li])
    rw_hi = rw.astype(BF16)
    rw2 = jnp.stack([rw_hi, (rw - rw_hi.astype(F32)).astype(BF16)])
    x1, h2p, logits = _merge(
        ys, yg, p_lat, x, mod4, s5_glu_w[li].astype(BF16), w_s5_proj[li].astype(BF16),
        w_gla_proj[li].astype(BF16), w_out[li].astype(BF16), gla_norm_w[li].reshape(1, dv),
        norm2_w[li].reshape(1, d), rw2, dv, tm=256)

    t = b_ * l_
    logits_t = logits.reshape(t, 128)[:, :N_EXPERTS].T
    eidx, wts, rank, cnt = _route(logits_t, router_bias[li])
    nb = (t * TOP_K + N_EXPERTS * (EXPERT_ROWS - 1) + EXPERT_ROWS - 1) // EXPERT_ROWS + EXPERT_LOOKAHEAD
    dest, rows, block_e, first, nxt, n_used = _slots(cnt[:, 0].astype(I32), eidx, rank, nb)

    tiles = d // 256
    y_sorted = _experts(block_e, first, nxt, n_used, rows.reshape(nb, EXPERT_ROWS), h2p.reshape(t, tiles, 128),
                        exp_w_gate[li], exp_w_up[li], exp_w_down[li])

    tm_c = 256
    out = _combine(dest, y_sorted.reshape(-1, tiles, 128), wts.T, x1.reshape(t, d), h2p.reshape(t * tiles, 128), mod4,
                   sh_w_gate[li].astype(BF16), sh_w_up[li].astype(BF16), sh_w_down[li].astype(BF16),
                   final_norm_w.reshape(1, d), l_ // tm_c, tm_c)
    return out.reshape(b_, l_, d)
```

```python
import functools
import math

import jax
import jax.numpy as jnp
from jax import lax
from jax.experimental import pallas as pl
from jax.experimental.pallas import tpu as pltpu

F32 = jnp.float32
BF16 = jnp.bfloat16
U32 = jnp.uint32
I32 = jnp.int32

EPS = 1e-6
N_MOD = 6

S5_GROUP = 16
S5_STATE = 64
S5_CHUNK = 16
S5_GROUPS_PER_STEP = 8

GLA_HEADS = 4
GLA_GATE_RANK = 16
GLA_GATE_NORM = 16.0
GLA_CHUNK = 64

N_EXPERTS = 64
TOP_K = 8
N_GROUPS = 8
TOPK_GROUPS = 4
ROUTED_SCALE = 2.5
EXPERT_ROWS = 256

VMEM_LIMIT = 56 * 1024 * 1024


def _cparams(sem, vmem=VMEM_LIMIT):
    return pltpu.CompilerParams(dimension_semantics=sem, vmem_limit_bytes=vmem)


def _dot(a, b):
    return jnp.dot(a, b, preferred_element_type=F32)


def _sigmoid(x):
    return 1.0 / (1.0 + jnp.exp(-x))


def _silu(x):
    return x * _sigmoid(x)


def _pack_bf16_pair(lo, hi):
    lo_bits = lax.bitcast_convert_type(lo.astype(BF16).astype(F32), U32)
    hi_bits = lax.bitcast_convert_type(hi.astype(BF16).astype(F32), U32)
    return (lo_bits >> 16) | (hi_bits & jnp.uint32(0xFFFF0000))


def _unpack_bf16_pair(w):
    lo = lax.bitcast_convert_type(w << 16, F32)
    hi = lax.bitcast_convert_type(w & jnp.uint32(0xFFFF0000), F32)
    return lo, hi


def _store_row_tiles(ref, lead, val):
    rows, rt = val.shape[0], val.shape[1] // 128
    for kc in range(rt):
        ref[(*lead, pl.ds(kc, rows, stride=rt), slice(None))] = val[:, kc * 128:(kc + 1) * 128]


def _load_row_tiles(ref, lead, row0, rows, rt):
    return jnp.concatenate(
        [ref[(*lead, pl.ds(row0 * rt + kc, rows, stride=rt), slice(None))] for kc in range(rt)], axis=-1)


def _ada_body(c_ref, w_ref, b_ref, o_ref):
    c = c_ref[...]
    s = _silu(c).astype(BF16)
    o_ref[...] = _dot(s, w_ref[...].astype(BF16)) + b_ref[...]


def _ada(c8, ada_w, ada_b):
    d, n = ada_w.shape
    tn = 512
    return pl.pallas_call(
        _ada_body,
        grid=(n // tn,),
        in_specs=[
            pl.BlockSpec((8, d), lambda j: (0, 0)),
            pl.BlockSpec((d, tn), lambda j: (0, j)),
            pl.BlockSpec((1, tn), lambda j: (0, j)),
        ],
        out_specs=pl.BlockSpec((8, tn), lambda j: (0, j)),
        out_shape=jax.ShapeDtypeStruct((8, n), F32),
        compiler_params=_cparams(("parallel",)),
        name="ada",
    )(c8, ada_w, ada_b.reshape(1, n))


def _modulated_norm(x, nw, sh, sc):
    ms = jnp.mean(x * x, axis=-1, keepdims=True)
    y = x * lax.rsqrt(ms + EPS) * nw
    return y * (1.0 + sc) + sh


def _inproj_body(x_ref, sh_ref, sc_ref, nw_ref, w_ref, wg_ref, p_ref, g_ref, h_scr):
    @pl.when(pl.program_id(2) == 0)
    def _():
        h = _modulated_norm(x_ref[0], nw_ref[...], sh_ref[0, 0], sc_ref[0, 0]).astype(BF16)
        h_scr[...] = h
        g_ref[0] = _dot(h, wg_ref[...])

    p_ref[0] = _dot(h_scr[...], w_ref[...]).astype(BF16)


def _inproj(x, mod4, mod_row, nw, w, wg, tm, tn):
    b_, l_, d = x.shape
    n = w.shape[1]
    return pl.pallas_call(
        _inproj_body,
        grid=(b_, l_ // tm, n // tn),
        in_specs=[
            pl.BlockSpec((1, tm, d), lambda b, i, j: (b, i, 0)),
            pl.BlockSpec((1, 1, 1, d), lambda b, i, j: (mod_row(b), 0, 0, 0)),
            pl.BlockSpec((1, 1, 1, d), lambda b, i, j: (mod_row(b), 1, 0, 0)),
            pl.BlockSpec((1, d), lambda b, i, j: (0, 0)),
            pl.BlockSpec((d, tn), lambda b, i, j: (0, j)),
            pl.BlockSpec((d, 128), lambda b, i, j: (0, 0)),
        ],
        out_specs=[
            pl.BlockSpec((1, tm, tn), lambda b, i, j: (b, i, j)),
            pl.BlockSpec((1, tm, 128), lambda b, i, j: (b, i, 0)),
        ],
        out_shape=[
            jax.ShapeDtypeStruct((b_, l_, n), BF16),
            jax.ShapeDtypeStruct((b_, l_, 128), F32),
        ],
        scratch_shapes=[pltpu.VMEM((tm, d), BF16)],
        compiler_params=_cparams(("parallel", "parallel", "arbitrary")),
        name="inproj",
    )(x, mod4, mod4, nw, w, wg)


def _s5_weights(lam_re, lam_im, log_dt, b_re, b_im, c_re, c_im, d_skip):
    hp = lax.Precision.HIGHEST
    tc = S5_CHUNK
    g_, p_ = lam_re.shape[1:]
    h_ = b_re.shape[-1]
    dt = jnp.exp(log_dt)[..., None]
    ar, ai = lam_re * dt, lam_im * dt
    k = jnp.arange(tc + 1, dtype=F32)
    mag = jnp.exp(ar[..., None] * k)
    pw_re = mag * jnp.cos(ai[..., None] * k)
    pw_im = mag * jnp.sin(ai[..., None] * k)
    num_re = jnp.expm1(ar) * jnp.cos(ai) - 2.0 * jnp.sin(0.5 * ai) ** 2
    num_im = jnp.exp(ar) * jnp.sin(ai)
    den = lam_re * lam_re + lam_im * lam_im
    f_re = (num_re * lam_re + num_im * lam_im) / den
    f_im = (num_im * lam_re - num_re * lam_im) / den
    bb_re = f_re[..., None] * b_re - f_im[..., None] * b_im
    bb_im = f_re[..., None] * b_im + f_im[..., None] * b_re

    cp_re = c_re[..., None] * pw_re[:, :, None] - c_im[..., None] * pw_im[:, :, None]
    cp_im = c_re[..., None] * pw_im[:, :, None] + c_im[..., None] * pw_re[:, :, None]
    kk = (jnp.einsum('dgipt,dgpj->dgtij', cp_re, bb_re, precision=hp)
          - jnp.einsum('dgipt,dgpj->dgtij', cp_im, bb_im, precision=hp))

    s_idx = jnp.arange(tc)[:, None]
    t_idx = jnp.arange(tc)[None, :]
    lag_f = t_idx - s_idx
    lag_b = s_idx - t_idx
    kf = jnp.where((lag_f >= 0)[None, :, :, None, None], kk[0][:, jnp.clip(lag_f, 0, tc)], 0.0)
    kb = jnp.where((lag_b >= 0)[None, :, :, None, None], kk[1][:, jnp.clip(lag_b, 0, tc)], 0.0)
    dsk = d_skip.reshape(g_, h_)
    eye_t = jnp.eye(tc, dtype=F32)[None, :, :, None, None]
    eye_h = jnp.eye(h_, dtype=F32)[None, None, None]
    m_all = kf + kb + eye_t * eye_h * dsk[:, None, None, :, None]
    m_all = m_all.transpose(0, 1, 4, 2, 3).reshape(g_, tc * h_, tc * h_)

    def state_w(dr, expo):
        e_re = pw_re[dr][:, :, expo]
        e_im = pw_im[dr][:, :, expo]
        w_re = jnp.einsum('gps,gpj->gsjp', e_re, bb_re[dr], precision=hp) - jnp.einsum(
            'gps,gpj->gsjp', e_im, bb_im[dr], precision=hp)
        w_im = jnp.einsum('gps,gpj->gsjp', e_re, bb_im[dr], precision=hp) + jnp.einsum(
            'gps,gpj->gsjp', e_im, bb_re[dr], precision=hp)
        w_re = w_re.reshape(g_, tc * h_, p_)
        w_im = w_im.reshape(g_, tc * h_, p_)
        return jnp.concatenate([w_re, w_im, w_im, w_re], axis=-1)

    ws_f = state_w(0, tc - 1 - jnp.arange(tc))
    ws_b = state_w(1, jnp.arange(tc))
    wcat = jnp.concatenate([m_all, ws_f, ws_b], axis=-1)

    def out_w(dr, expo):
        q_re = cp_re[dr][..., expo]
        q_im = cp_im[dr][..., expo]
        top = q_re.transpose(0, 2, 3, 1).reshape(g_, p_, tc * h_)
        bot = (-q_im).transpose(0, 2, 3, 1).reshape(g_, p_, tc * h_)
        return jnp.concatenate([top, bot], axis=1)

    wout = jnp.concatenate([out_w(0, 1 + jnp.arange(tc)), out_w(1, tc - jnp.arange(tc))], axis=1)

    a_re, a_im = pw_re[..., tc], pw_im[..., tc]
    zeros = jnp.zeros_like(a_re[0])
    rows = []
    for dr in range(2):
        rows += [jnp.concatenate([a_re[dr], a_re[dr]], -1),
                 jnp.concatenate([-a_im[dr], a_im[dr]], -1),
                 jnp.concatenate([a_im[dr], -a_im[dr]], -1)]
    rows += [jnp.concatenate([zeros, zeros], -1)] * 2
    acoef = jnp.stack(rows, axis=1)
    return wcat.astype(BF16), wout.astype(BF16), acoef


def _s5_body(x_ref, wcat_ref, wout_ref, a_ref, y_ref, y_scr, s_scr, h_scr, *, n_ctx, n_lat, nb):
    gb = x_ref.shape[0]
    pw = 2 * S5_STATE
    for g in range(gb):
        r = _dot(x_ref[g], wcat_ref[g])
        y_scr[g] = r[:, : y_scr.shape[-1]]
        s_scr[g] = r[:, y_scr.shape[-1]:]

    a1f, a2f, a2sf = a_ref[:, 0:1, :], a_ref[:, 1:2, :], a_ref[:, 2:3, :]
    a1b, a2b, a2sb = a_ref[:, 3:4, :], a_ref[:, 4:5, :], a_ref[:, 5:6, :]

    first = lax.broadcasted_iota(I32, (gb, 2 * nb, pw), 1) < nb

    def both(v):
        r = pltpu.roll(v, nb, 1)
        return jnp.where(first, v, r), jnp.where(first, r, v)

    def step(pf, pb, carry):
        hf, hfs, hb, hbs = carry
        rf = pl.ds(pl.multiple_of(pf * 2 * nb, 2 * nb), 2 * nb)
        rb = pl.ds(pl.multiple_of(pb * 2 * nb, 2 * nb), 2 * nb)
        sfa, sfb = both(s_scr[:, rf, 0:pw])
        sfsa, sfsb = both(s_scr[:, rf, pw:2 * pw])
        sba, sbb = both(s_scr[:, rb, 2 * pw:3 * pw])
        sbsa, sbsb = both(s_scr[:, rb, 3 * pw:4 * pw])
        hf1 = a1f * hf + a2f * hfs + sfa
        hfs1 = a1f * hfs + a2sf * hf + sfsa
        h_scr[:, rf, 0:pw] = jnp.where(first, hf, hf1)
        hf2 = a1f * hf1 + a2f * hfs1 + sfb
        hfs2 = a1f * hfs1 + a2sf * hf1 + sfsb
        hb1 = a1b * hb + a2b * hbs + sbb
        hbs1 = a1b * hbs + a2sb * hb + sbsb
        h_scr[:, rb, pw:2 * pw] = jnp.where(first, hb1, hb)
        hb2 = a1b * hb1 + a2b * hbs1 + sba
        hbs2 = a1b * hbs1 + a2sb * hb1 + sbsa
        return hf2, hfs2, hb2, hbs2

    z = jnp.zeros((gb, 2 * nb, pw), F32)
    carry = (z, z, z, z)
    pc, pl_ = n_ctx // 2, n_lat // 2
    carry = lax.fori_loop(0, pc, lambda i, c: step(i, pc - 1 - i, c), carry)
    carry = lax.fori_loop(0, pl_, lambda i, c: step(pc + i, pc + pl_ - 1 - i, c), carry)

    lo = n_ctx * nb
    for g in range(gb):
        y = y_scr[g] + _dot(h_scr[g].astype(BF16), wout_ref[g])
        y_ref[g] = y[lo:, :]


def _s5(x2, wcat, wout, acoef, n_ctx, n_lat, nb):
    g_, rows, kdim = x2.shape
    gb = S5_GROUPS_PER_STEP
    pw = 2 * S5_STATE
    out_rows = n_lat * nb
    return pl.pallas_call(
        functools.partial(_s5_body, n_ctx=n_ctx, n_lat=n_lat, nb=nb),
        grid=(g_ // gb,),
        in_specs=[
            pl.BlockSpec((gb, rows, kdim), lambda i: (i, 0, 0)),
            pl.BlockSpec((gb, kdim, wcat.shape[-1]), lambda i: (i, 0, 0)),
            pl.BlockSpec((gb, 2 * pw, kdim), lambda i: (i, 0, 0)),
            pl.BlockSpec((gb, 8, pw), lambda i: (i, 0, 0)),
        ],
        out_specs=pl.BlockSpec((gb, out_rows, kdim), lambda i: (i, 0, 0)),
        out_shape=jax.ShapeDtypeStruct((g_, out_rows, kdim), F32),
        scratch_shapes=[
            pltpu.VMEM((gb, rows, kdim), F32),
            pltpu.VMEM((gb, rows, 4 * pw), F32),
            pltpu.VMEM((gb, rows, 2 * pw), F32),
        ],
        compiler_params=_cparams(("parallel",)),
        name="s5",
    )(x2, wcat, wout, acoef)


def _s5_factors(lam_re, lam_im, log_dt, b_re, b_im, c_re, c_im, d_skip):
    tc = S5_CHUNK
    g_, p_ = lam_re.shape[1:]
    h_ = b_re.shape[-1]
    dt = jnp.exp(log_dt)[..., None]
    ar, ai = lam_re * dt, lam_im * dt

    def powers(k):
        k = jnp.asarray(k, F32)
        mag = jnp.exp(ar[..., None] * k)
        return mag * jnp.cos(ai[..., None] * k), mag * jnp.sin(ai[..., None] * k)

    num_re = jnp.expm1(ar) * jnp.cos(ai) - 2.0 * jnp.sin(0.5 * ai) ** 2
    num_im = jnp.exp(ar) * jnp.sin(ai)
    den = lam_re * lam_re + lam_im * lam_im
    f_re = (num_re * lam_re + num_im * lam_im) / den
    f_im = (num_im * lam_re - num_re * lam_im) / den
    bb_re = f_re[..., None] * b_re - f_im[..., None] * b_im
    bb_im = f_re[..., None] * b_im + f_im[..., None] * b_re
    t_idx = jnp.arange(tc)

    def state_w(dr, expo):
        e_re, e_im = powers(expo)
        e_re = e_re[dr].transpose(0, 2, 1)[:, :, None, :]
        e_im = e_im[dr].transpose(0, 2, 1)[:, :, None, :]
        br = bb_re[dr].transpose(0, 2, 1)[:, None, :, :]
        bi = bb_im[dr].transpose(0, 2, 1)[:, None, :, :]
        w_re = (e_re * br - e_im * bi).reshape(g_, tc * h_, p_)
        w_im = (e_re * bi + e_im * br).reshape(g_, tc * h_, p_)
        return jnp.concatenate([w_re, w_im], axis=-1)

    def out_w(dr, expo):
        e_re, e_im = powers(expo)
        e_re = e_re[dr][:, :, :, None]
        e_im = e_im[dr][:, :, :, None]
        cr = c_re[dr].transpose(0, 2, 1)[:, :, None, :]
        ci = c_im[dr].transpose(0, 2, 1)[:, :, None, :]
        q_re = (cr * e_re - ci * e_im).reshape(g_, p_, tc * h_)
        q_im = (cr * e_im + ci * e_re).reshape(g_, p_, tc * h_)
        return jnp.concatenate([q_re, -q_im], axis=1)

    ws = jnp.stack([state_w(0, tc - 1 - t_idx), state_w(1, t_idx)], axis=1)
    wq = jnp.stack([out_w(0, t_idx - (tc - 1)), out_w(1, -t_idx)], axis=1)
    wout = jnp.concatenate([out_w(0, t_idx + 1), out_w(1, tc - t_idx)], axis=1).astype(BF16)

    a_re, a_im = powers(jnp.array([tc]))
    a_re, a_im = a_re[..., 0], a_im[..., 0]
    zeros = jnp.zeros_like(a_re[0])
    rows = []
    for dr in range(2):
        rows += [jnp.concatenate([a_re[dr], a_re[dr]], -1),
                 jnp.concatenate([-a_im[dr], a_im[dr]], -1),
                 jnp.concatenate([a_im[dr], -a_im[dr]], -1)]
    rows += [jnp.concatenate([zeros, zeros], -1)] * 2
    acoef = jnp.stack(rows, axis=1)
    dvec = jnp.tile(d_skip.reshape(g_, h_), (1, tc)).reshape(g_, 1, tc * h_)
    return ws, wq, wout, acoef, dvec


def _split_bf16(x):
    hi = x.astype(BF16)
    return hi, (x - hi.astype(F32)).astype(BF16)


def _s5v2_body(v_ref, ws_ref, wq_ref, wout_ref, a_ref, d_ref, o_ref, y_scr, s_scr, h_scr, *, n_ctx, n_lat, nb):
    gb = ws_ref.shape[0]
    pw = 2 * S5_STATE
    gl = S5_GROUP
    kd = S5_CHUNK * gl
    lanes = 128
    v = v_ref[0]
    width = v.shape[1]

    r_in = lax.broadcasted_iota(I32, (width, kd), 0)
    c_in = lax.broadcasted_iota(I32, (width, kd), 1)
    tgt_in = (r_in // lanes) * gl + (r_in % gl)
    grp_in = (r_in % lanes) // gl
    r_out = lax.broadcasted_iota(I32, (kd, width), 0)
    c_out = lax.broadcasted_iota(I32, (kd, width), 1)
    tgt_out = (c_out // lanes) * gl + (c_out % gl)
    grp_out = (c_out % lanes) // gl

    rs = lax.broadcasted_iota(I32, (kd, kd), 0)
    ct = lax.broadcasted_iota(I32, (kd, kd), 1)
    fmask = ct // gl >= rs // gl
    bmask = ct // gl <= rs // gl
    diag = rs == ct

    for q in range(gb):
        sel = jnp.where(jnp.logical_and(c_in == tgt_in, grp_in == q), 1.0, 0.0).astype(BF16)
        x2 = _dot(v, sel).astype(BF16)
        mm = []
        for dr in range(2):
            a_hi, a_lo = _split_bf16(ws_ref[q, dr])
            b_hi, b_lo = _split_bf16(wq_ref[q, dr])
            mm.append(_dot(a_hi, b_hi) + _dot(a_hi, b_lo) + _dot(a_lo, b_hi))
        m = jnp.where(fmask, mm[0], 0.0) + jnp.where(bmask, mm[1], 0.0) + jnp.where(diag, d_ref[q], 0.0)
        wsf, wsb = ws_ref[q, 0], ws_ref[q, 1]
        wcat = jnp.concatenate(
            [m, wsf, pltpu.roll(wsf, S5_STATE, 1), wsb, pltpu.roll(wsb, S5_STATE, 1)], axis=1).astype(BF16)
        r = _dot(x2, wcat)
        y_scr[q] = r[:, :kd]
        s_scr[q] = r[:, kd:]

    a1f, a2f, a2sf = a_ref[:, 0:1, :], a_ref[:, 1:2, :], a_ref[:, 2:3, :]
    a1b, a2b, a2sb = a_ref[:, 3:4, :], a_ref[:, 4:5, :], a_ref[:, 5:6, :]
    first = lax.broadcasted_iota(I32, (gb, 2 * nb, pw), 1) < nb

    def both(x):
        rr = pltpu.roll(x, nb, 1)
        return jnp.where(first, x, rr), jnp.where(first, rr, x)

    def step(pf, pb, carry):
        hf, hfs, hb, hbs = carry
        rf = pl.ds(pl.multiple_of(pf * 2 * nb, 2 * nb), 2 * nb)
        rb = pl.ds(pl.multiple_of(pb * 2 * nb, 2 * nb), 2 * nb)
        sfa, sfb = both(s_scr[:, rf, 0:pw])
        sfsa, sfsb = both(s_scr[:, rf, pw:2 * pw])
        sba, sbb = both(s_scr[:, rb, 2 * pw:3 * pw])
        sbsa, sbsb = both(s_scr[:, rb, 3 * pw:4 * pw])
        hf1 = a1f * hf + a2f * hfs + sfa
        hfs1 = a1f * hfs + a2sf * hf + sfsa
        h_scr[:, rf, 0:pw] = jnp.where(first, hf, hf1)
        hf2 = a1f * hf1 + a2f * hfs1 + sfb
        hfs2 = a1f * hfs1 + a2sf * hf1 + sfsb
        hb1 = a1b * hb + a2b * hbs + sbb
        hbs1 = a1b * hbs + a2sb * hb + sbsb
        h_scr[:, rb, pw:2 * pw] = jnp.where(first, hb1, hb)
        hb2 = a1b * hb1 + a2b * hbs1 + sba
        hbs2 = a1b * hbs1 + a2sb * hb1 + sbsa
        return hf2, hfs2, hb2, hbs2

    z = jnp.zeros((gb, 2 * nb, pw), F32)
    carry = (z, z, z, z)
    pc, pl_ = n_ctx // 2, n_lat // 2
    carry = lax.fori_loop(0, pc, lambda i, c: step(i, pc - 1 - i, c), carry)
    carry = lax.fori_loop(0, pl_, lambda i, c: step(pc + i, pc + pl_ - 1 - i, c), carry)

    lo = n_ctx * nb
    for q in range(gb):
        y = y_scr[q] + _dot(h_scr[q].astype(BF16), wout_ref[q])
        selt = jnp.where(jnp.logical_and(r_out == tgt_out, grp_out == q), 1.0, 0.0).astype(BF16)
        placed = _dot(y[lo:, :].astype(BF16), selt)
        if q == 0:
            o_ref[0] = placed
        else:
            o_ref[0] = o_ref[0] + placed


def _s5v2(v, ws, wq, wout, acoef, dvec, n_ctx, n_lat, nb):
    nblk, rows, width = v.shape
    g_ = ws.shape[0]
    gb = g_ // nblk
    pw = 2 * S5_STATE
    kd = S5_CHUNK * S5_GROUP
    out_rows = n_lat * nb
    return pl.pallas_call(
        functools.partial(_s5v2_body, n_ctx=n_ctx, n_lat=n_lat, nb=nb),
        grid=(nblk,),
        in_specs=[
            pl.BlockSpec((1, rows, width), lambda i: (i, 0, 0)),
            pl.BlockSpec((gb, 2, kd, pw), lambda i: (i, 0, 0, 0)),
            pl.BlockSpec((gb, 2, pw, kd), lambda i: (i, 0, 0, 0)),
            pl.BlockSpec((gb, 2 * pw, kd), lambda i: (i, 0, 0)),
            pl.BlockSpec((gb, 8, pw), lambda i: (i, 0, 0)),
            pl.BlockSpec((gb, 1, kd), lambda i: (i, 0, 0)),
        ],
        out_specs=pl.BlockSpec((1, out_rows, width), lambda i: (i, 0, 0)),
        out_shape=jax.ShapeDtypeStruct((nblk, out_rows, width), F32),
        scratch_shapes=[
            pltpu.VMEM((gb, rows, kd), F32),
            pltpu.VMEM((gb, rows, 4 * pw), F32),
            pltpu.VMEM((gb, rows, 2 * pw), F32),
        ],
        compiler_params=_cparams(("parallel",)),
        name="s5",
    )(v, ws, wq, wout, acoef, dvec)


def _log_sigmoid(z):
    return jnp.minimum(z, 0.0) - jnp.log1p(jnp.exp(-jnp.abs(z)))


def _chunk_cumsum(g, pos, reverse):
    n = g.shape[0]
    x = g
    sh = 1
    while sh < GLA_CHUNK:
        if reverse:
            x = x + jnp.where(pos < GLA_CHUNK - sh, pltpu.roll(x, n - sh, 0), 0.0)
        else:
            x = x + jnp.where(pos >= sh, pltpu.roll(x, sh, 0), 0.0)
        sh *= 2
    return x


def _gla_body(q_ref, k_ref, v_ref, kc_ref, vc_ref, gk_ref, gkc_ref, up_ref, bias_ref, o_ref,
              bc_scr, kb_scr, qb_scr, st_ref, *, scale):
    c = GLA_CHUNK
    l_ = q_ref.shape[1]
    lc = kc_ref.shape[1]
    n_lat, n_ctx = l_ // c, lc // c
    row = lax.broadcasted_iota(I32, (c, c), 0)
    col = lax.broadcasted_iota(I32, (c, c), 1)
    nt = (((1,), (1,)), ((), ()))
    tn = (((0,), (0,)), ((), ()))

    for dr in range(2):
        smask = (col <= row) if dr == 0 else (col > row)
        up = up_ref[dr]
        bias = bias_ref[dr]
        edge_tile, edge_row = (c - 8, 7) if dr == 0 else (0, 0)

        for gk_r, k_r, base, n, latent in ((gkc_ref, kc_ref, 0, lc, False), (gk_ref, k_ref, lc, l_, True)):
            z = _dot(gk_r[0].astype(BF16), up) + bias
            g = _log_sigmoid(z) * (1.0 / GLA_GATE_NORM)
            pos = lax.broadcasted_iota(I32, g.shape, 0) & (c - 1)
            bc = _chunk_cumsum(g, pos, reverse=(dr == 1))
            bc_scr[base:base + n, :] = bc
            kb_scr[base:base + n, :] = k_r[0].astype(F32) * jnp.exp(-bc)
            if latent:
                qb_scr[...] = (q_ref[0].astype(F32) * (scale * jnp.exp(bc))).astype(BF16)

        def chunk_rows(n, base):
            r0 = pl.multiple_of(n * c, c)
            tot = bc_scr[pl.ds(base + r0 + edge_tile, 8), :][edge_row:edge_row + 1, :]
            return r0, pl.ds(base + r0, c), jnp.exp(tot)

        def update_state(kb, vv, decay):
            kd = (kb * decay).astype(BF16)
            ds_t = lax.dot_general(vv, kd, tn, preferred_element_type=F32)
            st_ref[...] = decay * st_ref[...] + ds_t

        def ctx_step(i, _):
            n = i if dr == 0 else n_ctx - 1 - i
            r0, rs, decay = chunk_rows(n, 0)
            update_state(kb_scr[rs, :], vc_ref[0, pl.ds(r0, c), :], decay)
            return 0

        def lat_step(i, _):
            n = i if dr == 0 else n_lat - 1 - i
            r0, rs, decay = chunk_rows(n, lc)
            ro = pl.ds(r0, c)
            kb = kb_scr[rs, :]
            vv = v_ref[0, ro, :]
            qb = qb_scr[ro, :]
            scores = lax.dot_general(qb, kb.astype(BF16), nt, preferred_element_type=F32)
            scores = jnp.where(smask, scores, 0.0).astype(BF16)
            o = _dot(scores, vv) + lax.dot_general(qb, st_ref[...].astype(BF16), nt,
                                                   preferred_element_type=F32)
            if dr == 0:
                o_ref[0, ro, :] = o
            else:
                o_ref[0, ro, :] = o_ref[0, ro, :] + o
            update_state(kb, vv, decay)
            return 0

        st_ref[...] = jnp.zeros_like(st_ref)
        lax.fori_loop(0, n_ctx, ctx_step, 0, unroll=2)
        lax.fori_loop(0, n_lat, lat_step, 0, unroll=4)


def _gla(p_lat, p_ctx, gk_lat, gk_ctx, up_pad, bias, dk, dv):
    b_, l_, _ = p_lat.shape
    lc = p_ctx.shape[1]
    hh = GLA_HEADS
    s5w = hh * dv
    q0, k0, v0 = s5w // dk, (s5w + hh * dk) // dk, (s5w + 2 * hh * dk) // dv
    kc0, vc0 = s5w // dk, (s5w + hh * dk) // dv
    return pl.pallas_call(
        functools.partial(_gla_body, scale=dk ** -0.5),
        grid=(b_, hh),
        in_specs=[
            pl.BlockSpec((1, l_, dk), lambda b, h: (b, 0, q0 + h)),
            pl.BlockSpec((1, l_, dk), lambda b, h: (b, 0, k0 + h)),
            pl.BlockSpec((1, l_, dv), lambda b, h: (b, 0, v0 + h)),
            pl.BlockSpec((1, lc, dk), lambda b, h: (b, 0, kc0 + h)),
            pl.BlockSpec((1, lc, dv), lambda b, h: (b, 0, vc0 + h)),
            pl.BlockSpec((1, l_, 128), lambda b, h: (b, 0, 0)),
            pl.BlockSpec((1, lc, 128), lambda b, h: (b, 0, 0)),
            pl.BlockSpec((2, 128, dk), lambda b, h: (0, 0, h)),
            pl.BlockSpec((2, 1, dk), lambda b, h: (0, 0, h)),
        ],
        out_specs=pl.BlockSpec((1, l_, dv), lambda b, h: (b, 0, h)),
        out_shape=jax.ShapeDtypeStruct((b_, l_, hh * dv), F32),
        scratch_shapes=[
            pltpu.VMEM((lc + l_, dk), F32),
            pltpu.VMEM((lc + l_, dk), F32),
            pltpu.VMEM((l_, dk), BF16),
            pltpu.VMEM((dv, dk), F32),
        ],
        compiler_params=_cparams(("parallel", "parallel")),
        name="gla",
    )(p_lat, p_lat, p_lat, p_ctx, p_ctx, gk_lat, gk_ctx, up_pad, bias)


def _gelu_tanh(x):
    return 0.5 * x * (1.0 + jnp.tanh(math.sqrt(2.0 / math.pi) * (x + 0.044715 * (x * x * x))))


def _merge_body(ys_ref, yg_ref, go_ref, gs_ref, gg_ref, x_ref, g1_ref, sh2_ref, sc2_ref,
                glu_ref, ws5_ref, wgla_ref, wout_ref, gnw_ref, n2w_ref, rw_ref,
                x1_ref, h2_ref, lg_ref, *, dv):
    a = _gelu_tanh(ys_ref[0])
    a = a * _sigmoid(_dot(a.astype(BF16), glu_ref[...]))
    pa = _dot(a.astype(BF16), ws5_ref[...])

    yg = yg_ref[0]
    parts = []
    for h in range(GLA_HEADS):
        oh = yg[:, h * dv:(h + 1) * dv]
        ms = jnp.mean(oh * oh, axis=-1, keepdims=True)
        parts.append(oh * lax.rsqrt(ms + EPS) * gnw_ref[...])
    gl = jnp.concatenate(parts, axis=-1) * _silu(go_ref[0].astype(F32))
    pb = _dot(gl.astype(BF16), wgla_ref[...])

    m = _sigmoid(gs_ref[0].astype(F32)) * pa + _sigmoid(gg_ref[0].astype(F32)) * pb
    y = _dot(m.astype(BF16), wout_ref[...])
    x1 = x_ref[0] + g1_ref[0, 0] * y
    x1_ref[0] = x1

    h2 = _modulated_norm(x1, n2w_ref[...], sh2_ref[0, 0], sc2_ref[0, 0])
    h_hi = h2.astype(BF16)
    h_lo = (h2 - h_hi.astype(F32)).astype(BF16)
    half = h2.shape[-1] // 2
    _store_row_tiles(h2_ref, (0,), _pack_bf16_pair(h2[:, :half], h2[:, half:]))
    lg_ref[0] = _dot(h_hi, rw_ref[0]) + _dot(h_lo, rw_ref[0]) + _dot(h_hi, rw_ref[1])


def _resident(shape):
    nd = len(shape)
    return pl.BlockSpec(shape, lambda b, i: (0,) * nd, pipeline_mode=pl.Buffered(1))


def _merge(ys, yg, p_lat, x, mod4, glu_w, ws5, wgla, wout, gnw, n2w, rw2, dv, tm):
    b_, l_, d = x.shape
    w5 = ys.shape[-1]
    go0 = (p_lat.shape[-1] - 2 * d - w5) // w5
    gs0 = (p_lat.shape[-1] - 2 * d) // d
    modspec = lambda r: pl.BlockSpec((1, 1, 1, d), lambda b, i: (b, r, 0, 0))
    return pl.pallas_call(
        functools.partial(_merge_body, dv=dv),
        grid=(b_, l_ // tm),
        in_specs=[
            pl.BlockSpec((1, tm, w5), lambda b, i: (b, i, 0)),
            pl.BlockSpec((1, tm, w5), lambda b, i: (b, i, 0)),
            pl.BlockSpec((1, tm, w5), lambda b, i: (b, i, go0)),
            pl.BlockSpec((1, tm, d), lambda b, i: (b, i, gs0)),
            pl.BlockSpec((1, tm, d), lambda b, i: (b, i, gs0 + 1)),
            pl.BlockSpec((1, tm, d), lambda b, i: (b, i, 0)),
            modspec(2), modspec(3), modspec(4),
            _resident(glu_w.shape), _resident(ws5.shape), _resident(wgla.shape), _resident(wout.shape),
            _resident(gnw.shape), _resident(n2w.shape), _resident(rw2.shape),
        ],
        out_specs=[
            pl.BlockSpec((1, tm, d), lambda b, i: (b, i, 0)),
            pl.BlockSpec((1, tm * (d // 256), 128), lambda b, i: (b, i, 0)),
            pl.BlockSpec((1, tm, 128), lambda b, i: (b, i, 0)),
        ],
        out_shape=[
            jax.ShapeDtypeStruct((b_, l_, d), F32),
            jax.ShapeDtypeStruct((b_, l_ * (d // 256), 128), U32),
            jax.ShapeDtypeStruct((b_, l_, 128), F32),
        ],
        compiler_params=_cparams(("parallel", "parallel")),
        name="merge",
    )(ys, yg, p_lat, p_lat, p_lat, x, mod4, mod4, mod4, glu_w, ws5, wgla, wout, gnw, n2w, rw2)


def _route_body(lg_ref, bias_ref, eidx_ref, w_ref, rank_ref, cnt_ref, carry_ref):
    ne, tt = lg_ref.shape
    gsz = ne // N_GROUPS
    neg = -jnp.inf

    @pl.when(pl.program_id(0) == 0)
    def _():
        carry_ref[...] = jnp.zeros_like(carry_ref)

    sc = _sigmoid(lg_ref[...])
    ch = sc + bias_ref[...]

    def first_max(v, idx, big):
        m = jnp.max(v, axis=0, keepdims=True)
        i = jnp.min(jnp.where(v == m, idx, big), axis=0, keepdims=True)
        return m, i

    midx = lax.broadcasted_iota(I32, (gsz, tt), 0).astype(F32)
    gs_rows = []
    for g in range(N_GROUPS):
        v = ch[g * gsz:(g + 1) * gsz, :]
        m1, i1 = first_max(v, midx, float(gsz))
        m2 = jnp.max(jnp.where(midx == i1, neg, v), axis=0, keepdims=True)
        gs_rows.append(m1 + m2)
    gscore = jnp.concatenate(gs_rows, axis=0)

    gidx = lax.broadcasted_iota(I32, (N_GROUPS, tt), 0).astype(F32)
    gsel = jnp.zeros((N_GROUPS, tt), F32)
    cur = gscore
    for _ in range(TOPK_GROUPS):
        _, i = first_max(cur, gidx, float(N_GROUPS))
        hit = gidx == i
        gsel = jnp.where(hit, 1.0, gsel)
        cur = jnp.where(hit, neg, cur)

    masked = jnp.concatenate(
        [jnp.where(gsel[g:g + 1, :] > 0.0, ch[g * gsz:(g + 1) * gsz, :], neg) for g in range(N_GROUPS)], axis=0)

    eiota = lax.broadcasted_iota(I32, (ne, tt), 0).astype(F32)
    sel = jnp.zeros((ne, tt), F32)
    cur = masked
    idx_rows, s_rows = [], []
    for _ in range(TOP_K):
        _, i = first_max(cur, eiota, float(ne))
        hit = eiota == i
        idx_rows.append(i)
        s_rows.append(jnp.sum(jnp.where(hit, sc, 0.0), axis=0, keepdims=True))
        sel = jnp.where(hit, 1.0, sel)
        cur = jnp.where(hit, neg, cur)
    idx = jnp.concatenate(idx_rows, axis=0)
    s = jnp.concatenate(s_rows, axis=0)
    w_ref[...] = s / jnp.sum(s, axis=0, keepdims=True) * ROUTED_SCALE
    eidx_ref[...] = idx.astype(I32)

    cw = 256
    a_i = lax.broadcasted_iota(I32, (cw, cw), 0)
    b_i = lax.broadcasted_iota(I32, (cw, cw), 1)
    upper = jnp.where(a_i < b_i, 1.0, 0.0).astype(BF16)
    carry = carry_ref[:, 0:1]
    pieces = []
    for c0 in range(0, tt, cw):
        sc_c = sel[:, c0:c0 + cw]
        pieces.append(_dot(sc_c.astype(BF16), upper) + carry)
        carry = carry + jnp.sum(sc_c, axis=1, keepdims=True)
    rank_full = jnp.concatenate(pieces, axis=1)
    carry_ref[...] = jnp.broadcast_to(carry, carry_ref.shape)
    cnt_ref[...] = jnp.broadcast_to(carry, cnt_ref.shape)
    rk = [jnp.sum(jnp.where(eiota == idx_rows[k], rank_full, 0.0), axis=0, keepdims=True) for k in range(TOP_K)]
    rank_ref[...] = jnp.concatenate(rk, axis=0).astype(I32)


def _route(logits_t, bias):
    ne, t = logits_t.shape
    tt = 1024
    return pl.pallas_call(
        _route_body,
        grid=(t // tt,),
        in_specs=[
            pl.BlockSpec((ne, tt), lambda i: (0, i)),
            pl.BlockSpec((ne, 1), lambda i: (0, 0)),
        ],
        out_specs=[
            pl.BlockSpec((TOP_K, tt), lambda i: (0, i)),
            pl.BlockSpec((TOP_K, tt), lambda i: (0, i)),
            pl.BlockSpec((TOP_K, tt), lambda i: (0, i)),
            pl.BlockSpec((ne, 128), lambda i: (0, 0)),
        ],
        out_shape=[
            jax.ShapeDtypeStruct((TOP_K, t), I32),
            jax.ShapeDtypeStruct((TOP_K, t), F32),
            jax.ShapeDtypeStruct((TOP_K, t), I32),
            jax.ShapeDtypeStruct((ne, 128), F32),
        ],
        scratch_shapes=[pltpu.VMEM((ne, 128), F32)],
        compiler_params=_cparams(("arbitrary",)),
        name="route",
    )(logits_t, bias.reshape(ne, 1))


def _slots_body(cnt_ref, eidx_ref, rank_ref, dest_ref, rows_ref, be_ref, first_ref, nxt_ref, nu_ref,
                off_smem, dchunk, sem):
    ne = cnt_ref.shape[0]
    nb = be_ref.shape[0]
    tl = eidx_ref.shape[1]
    r_ = EXPERT_ROWS
    shift = r_.bit_length() - 1
    c = pl.program_id(0)

    @pl.when(c == 0)
    def _():
        def per_expert(e, off):
            cnt = cnt_ref[e]
            n = lax.shift_right_logical(cnt + (r_ - 1), shift)
            off_smem[e] = off

            def fill(j, _):
                be_ref[off + j] = e
                first_ref[off + j] = jnp.where(j == 0, 1, 0)
                return 0
            lax.fori_loop(0, n, fill, 0)

            def pad(j, _):
                rows_ref[off * r_ + j] = 0
                return 0
            lax.fori_loop(cnt, n * r_, pad, 0)
            return off + n

        n_used = lax.fori_loop(0, ne, per_expert, 0)
        nu_ref[0] = n_used
        last_e = be_ref[n_used - 1]

        def tail(j, _):
            be_ref[j] = last_e
            first_ref[j] = 0

            def zero_row(q, _):
                rows_ref[j * r_ + q] = 0
                return 0
            lax.fori_loop(0, r_, zero_row, 0)
            return 0
        lax.fori_loop(n_used, nb, tail, 0)

        def back(i, nx):
            j = nb - 1 - i
            nxt_ref[j] = nx
            return jnp.where(first_ref[j] == 1, be_ref[j], nx)
        lax.fori_loop(0, nb, back, -1)

    ei = eidx_ref[...]
    acc = rank_ref[...]
    for e in range(ne):
        acc = acc + jnp.where(ei == e, off_smem[e] * r_, 0)
    dest_ref[...] = acc
    cp = pltpu.make_async_copy(dest_ref, dchunk, sem)
    cp.start()
    cp.wait()
    base = c * tl

    def scatter(j, _):
        for k in range(TOP_K):
            rows_ref[dchunk[k, j]] = base + j
        return 0
    lax.fori_loop(0, tl, scatter, 0)


def _slots(counts, eidx, rank, nb):
    k_, t = eidx.shape
    tl = 1024
    smem = lambda: pl.BlockSpec(memory_space=pltpu.SMEM)
    chunk = lambda: pl.BlockSpec((k_, tl), lambda i: (0, i))
    return pl.pallas_call(
        _slots_body,
        grid=(t // tl,),
        in_specs=[smem(), chunk(), chunk()],
        out_specs=[chunk(), smem(), smem(), smem(), smem(), smem()],
        out_shape=[
            jax.ShapeDtypeStruct((k_, t), I32),
            jax.ShapeDtypeStruct((nb * EXPERT_ROWS,), I32),
            jax.ShapeDtypeStruct((nb,), I32),
            jax.ShapeDtypeStruct((nb,), I32),
            jax.ShapeDtypeStruct((nb,), I32),
            jax.ShapeDtypeStruct((1,), I32),
        ],
        scratch_shapes=[
            pltpu.SMEM((counts.shape[0],), I32),
            pltpu.SMEM((k_, tl), I32),
            pltpu.SemaphoreType.DMA,
        ],
        compiler_params=_cparams(("arbitrary",)),
        name="slots",
    )(counts, eidx, rank)


EXPERT_ISSUE_GROUPS = 8
EXPERT_LOOKAHEAD = 2


def _expert_body(be_ref, first_ref, nxt_ref, nu_ref, rows_hbm, x_hbm, wg_hbm, wu_hbm, wd_hbm, y_ref,
                 idx_smem, xbuf, wg_f, wu_f, wd_f, wg_s, wu_s, wd_s, isem, gsem, wsem):
    i = pl.program_id(0)
    n_used = nu_ref[0]
    r_ = EXPERT_ROWS
    tiles = x_hbm.shape[1]
    ns = EXPERT_LOOKAHEAD + 1
    slot = lax.rem(i, ns)
    aslot = lax.rem(i + EXPERT_LOOKAHEAD, ns)

    def idx_copy(blk, s):
        return pltpu.make_async_copy(rows_hbm.at[pl.ds(blk, 1)], idx_smem.at[s], isem.at[s])

    def row_copy(s, r):
        return pltpu.make_async_copy(x_hbm.at[idx_smem[s, 0, r]], xbuf.at[s, pl.ds(r * tiles, tiles)], gsem.at[s])

    def issue_rows(s, g):
        step = r_ // EXPERT_ISSUE_GROUPS
        for r in range(g * step, (g + 1) * step):
            row_copy(s, r).start()

    def wait_rows(s):
        pltpu.make_async_copy(xbuf.at[s], xbuf.at[s], gsem.at[s]).wait()

    def weight_copies(e):
        return (pltpu.make_async_copy(wg_hbm.at[e], wg_f, wsem.at[0]),
                pltpu.make_async_copy(wu_hbm.at[e], wu_f, wsem.at[1]),
                pltpu.make_async_copy(wd_hbm.at[e], wd_f, wsem.at[2]))

    @pl.when(i == 0)
    def _():
        for cp in weight_copies(be_ref[0]):
            cp.start(priority=1)
        for b in range(EXPERT_LOOKAHEAD):
            idx_copy(b, b).start()
        for b in range(EXPERT_LOOKAHEAD):
            idx_copy(b, b).wait()

            def body(r, _, b=b):
                row_copy(b, r).start()
                return 0
            lax.fori_loop(0, r_, body, 0)
        idx_copy(EXPERT_LOOKAHEAD, EXPERT_LOOKAHEAD).start()

    @pl.when(first_ref[i] == 1)
    def _():
        for cp in weight_copies(be_ref[i]):
            cp.wait()
        wg_s[...] = wg_f[...].astype(BF16)
        wu_s[...] = wu_f[...].astype(BF16)
        wd_s[...] = wd_f[...].astype(BF16)

        @pl.when(nxt_ref[i] >= 0)
        def _():
            for cp in weight_copies(nxt_ref[i]):
                cp.start(priority=1)

    @pl.when(i < n_used)
    def _():
        idx_copy(i + EXPERT_LOOKAHEAD, aslot).wait()
        wait_rows(slot)
        lo, hi = _unpack_bf16_pair(_load_row_tiles(xbuf, (slot,), 0, r_, tiles))
        lo = lo.astype(BF16)
        hi = hi.astype(BF16)
        half = lo.shape[-1]
        issue_rows(aslot, 0)
        a = _dot(lo, wg_s[0:half, :])
        issue_rows(aslot, 1)
        a = a + _dot(hi, wg_s[half:, :])
        issue_rows(aslot, 2)
        u = _dot(lo, wu_s[0:half, :])
        issue_rows(aslot, 3)
        u = u + _dot(hi, wu_s[half:, :])
        issue_rows(aslot, 4)
        hid = (_silu(a) * u).astype(BF16)
        y_lo = _dot(hid, wd_s[:, 0:half])
        issue_rows(aslot, 5)
        y_hi = _dot(hid, wd_s[:, half:])
        issue_rows(aslot, 6)
        _store_row_tiles(y_ref, (), _pack_bf16_pair(y_lo, y_hi))
        issue_rows(aslot, 7)

        @pl.when(i + 1 < n_used)
        def _():
            idx_copy(i + EXPERT_LOOKAHEAD + 1, slot).start()

    @pl.when(jnp.logical_and(i >= n_used, i < n_used + EXPERT_LOOKAHEAD))
    def _():
        wait_rows(slot)

    @pl.when(i >= n_used)
    def _():
        y_ref[...] = jnp.zeros_like(y_ref)


def _experts(block_e, first, nxt, n_used, rows, h2p, wg, wu, wd):
    nb = block_e.shape[0]
    t, tiles, _ = h2p.shape
    ne, d, f = wg.shape
    r_ = EXPERT_ROWS
    hbm = lambda: pl.BlockSpec(memory_space=pl.ANY)
    grid_spec = pltpu.PrefetchScalarGridSpec(
        num_scalar_prefetch=4,
        grid=(nb,),
        in_specs=[hbm(), hbm(), hbm(), hbm(), hbm()],
        out_specs=pl.BlockSpec((r_ * tiles, 128), lambda i, *_: (i, 0)),
        scratch_shapes=[
            pltpu.SMEM((EXPERT_LOOKAHEAD + 1, 1, r_), I32),
            pltpu.VMEM((EXPERT_LOOKAHEAD + 1, r_ * tiles, 128), U32),
            pltpu.VMEM((d, f), F32),
            pltpu.VMEM((d, f), F32),
            pltpu.VMEM((f, d), F32),
            pltpu.VMEM((d, f), BF16),
            pltpu.VMEM((d, f), BF16),
            pltpu.VMEM((f, d), BF16),
            pltpu.SemaphoreType.DMA((EXPERT_LOOKAHEAD + 1,)),
            pltpu.SemaphoreType.DMA((EXPERT_LOOKAHEAD + 1,)),
            pltpu.SemaphoreType.DMA((3,)),
        ],
    )
    return pl.pallas_call(
        _expert_body,
        grid_spec=grid_spec,
        out_shape=jax.ShapeDtypeStruct((nb * r_ * tiles, 128), U32),
        compiler_params=_cparams(("arbitrary",)),
        name="experts",
    )(block_e, first, nxt, n_used, rows, h2p, wg, wu, wd)


def _combine_body(dest_hbm, y_hbm, w_ref, x1_ref, h2_ref, g2_ref, sg_ref, su_ref, sd_ref, fnw_ref, o_ref,
                  idx_smem, ybuf, isem, gsem):
    i = pl.program_id(0)
    n = pl.num_programs(0)
    tm = x1_ref.shape[0]
    tiles = y_hbm.shape[1]
    slot = i % 2
    nslot = 1 - slot

    def idx_copy(blk, s):
        cols = pl.ds(pl.multiple_of(blk * tm, tm), tm)
        return pltpu.make_async_copy(dest_hbm.at[:, cols], idx_smem.at[s], isem.at[s])

    def issue_rows(s):
        def body(j, _):
            for k in range(TOP_K):
                dst = pl.ds(pl.multiple_of((k * tm + j) * tiles, tiles), tiles)
                pltpu.make_async_copy(y_hbm.at[idx_smem[s, k, j]], ybuf.at[s, dst],
                                      gsem.at[s]).start(priority=k % 2)
            return 0
        lax.fori_loop(0, tm, body, 0)

    def wait_rows(s):
        pltpu.make_async_copy(ybuf.at[s], ybuf.at[s], gsem.at[s]).wait()

    @pl.when(i == 0)
    def _():
        idx_copy(0, 0).start()
        idx_copy(0, 0).wait()
        issue_rows(0)

        @pl.when(n > 1)
        def _():
            idx_copy(1, 1).start()

    @pl.when(i + 1 < n)
    def _():
        idx_copy(i + 1, nslot).wait()
        issue_rows(nslot)

    wait_rows(slot)

    @pl.when(i + 2 < n)
    def _():
        idx_copy(i + 2, slot).start()

    w = w_ref[...]
    half = tiles * 128
    r_lo = [jnp.zeros((tm, 128), F32) for _ in range(tiles)]
    r_hi = [jnp.zeros((tm, 128), F32) for _ in range(tiles)]
    for k in range(TOP_K):
        wk = w[:, k:k + 1]
        for kc in range(tiles):
            lo, hi = _unpack_bf16_pair(ybuf[slot, pl.ds(k * tm * tiles + kc, tm, stride=tiles), :])
            r_lo[kc] = r_lo[kc] + wk * lo
            r_hi[kc] = r_hi[kc] + wk * hi
    routed = jnp.concatenate(r_lo + r_hi, axis=-1)

    x_lo, x_hi = _unpack_bf16_pair(_load_row_tiles(h2_ref, (), 0, tm, tiles))
    x_lo = x_lo.astype(BF16)
    x_hi = x_hi.astype(BF16)
    a = _dot(x_lo, sg_ref[0:half, :]) + _dot(x_hi, sg_ref[half:, :])
    u = _dot(x_lo, su_ref[0:half, :]) + _dot(x_hi, su_ref[half:, :])
    shared = _dot((_silu(a) * u).astype(BF16), sd_ref[...])

    x2 = x1_ref[...] + g2_ref[0, 0] * (routed + shared)
    ms = jnp.mean(x2 * x2, axis=-1, keepdims=True)
    o_ref[...] = x2 * lax.rsqrt(ms + EPS) * fnw_ref[...]


def _combine(dest, y_sorted, w_tok, x1, h2p, mod4, sg, su, sd, fnw, tiles_per_batch, tm):
    t, d = x1.shape
    half = d // 2
    nrow = TOP_K * tm
    res = lambda shape: pl.BlockSpec(shape, lambda i: (0,) * len(shape), pipeline_mode=pl.Buffered(1))
    return pl.pallas_call(
        _combine_body,
        grid=(t // tm,),
        in_specs=[
            pl.BlockSpec(memory_space=pl.ANY),
            pl.BlockSpec(memory_space=pl.ANY),
            pl.BlockSpec((tm, TOP_K), lambda i: (i, 0)),
            pl.BlockSpec((tm, d), lambda i: (i, 0)),
            pl.BlockSpec((tm * (half // 128), 128), lambda i: (i, 0)),
            pl.BlockSpec((1, 1, 1, d), lambda i: (i // tiles_per_batch, 5, 0, 0)),
            res(sg.shape), res(su.shape), res(sd.shape), res(fnw.shape),
        ],
        out_specs=pl.BlockSpec((tm, d), lambda i: (i, 0)),
        out_shape=jax.ShapeDtypeStruct((t, d), F32),
        scratch_shapes=[
            pltpu.SMEM((2, TOP_K, tm), I32),
            pltpu.VMEM((2, nrow * (half // 128), 128), U32),
            pltpu.SemaphoreType.DMA((2,)),
            pltpu.SemaphoreType.DMA((2,)),
        ],
        compiler_params=_cparams(("arbitrary",)),
        name="combine",
    )(dest, y_sorted, w_tok, x1, h2p, mod4, sg, su, sd, fnw)


def kernel(x, c, ctx, c_ctx, ada_w, ada_b, norm1_w, norm2_w, w_in, s5_lam_re, s5_lam_im, s5_log_dt,
           s5_b_re, s5_b_im, s5_c_re, s5_c_im, s5_d, s5_glu_w, gla_gk_up, gla_gk_b, gla_norm_w,
           w_s5_proj, w_gla_proj, w_out, router_w, router_bias, exp_w_gate, exp_w_up, exp_w_down,
           sh_w_gate, sh_w_up, sh_w_down, final_norm_w):
    depth = ada_w.shape[0]
    assert depth == 1, "single-layer block: context outputs are never consumed"
    b_, l_, d = x.shape
    lc = ctx.shape[1]
    s5w = s5_d.shape[1]
    kw = gla_gk_up.shape[-1]
    vw = w_gla_proj.shape[1]
    dk, dv = kw // GLA_HEADS, vw // GLA_HEADS
    rank2 = 2 * GLA_GATE_RANK
    li = 0

    c8 = jnp.zeros((8, d), F32).at[:b_].set(c).at[b_].set(c_ctx)
    mod4 = _ada(c8, ada_w[li], ada_b[li]).reshape(8, N_MOD, 1, d)

    cuts = [0, s5w, s5w + kw, s5w + 2 * kw, s5w + 2 * kw + vw, s5w + 2 * kw + 2 * vw]
    c_gkd = cuts[5]
    w = w_in[li]
    w_main = jnp.concatenate([w[:, :c_gkd], w[:, c_gkd + rank2:]], axis=1).astype(BF16)
    w_ctx = jnp.concatenate([w[:, cuts[0]:cuts[1]], w[:, cuts[2]:cuts[4]]], axis=1).astype(BF16)
    w_gkd = jnp.zeros((d, 128), F32).at[:, :rank2].set(w[:, c_gkd:c_gkd + rank2]).astype(BF16)
    nw1 = norm1_w[li].reshape(1, d)
    p_lat, gk_lat = _inproj(x, mod4, lambda b: b, nw1, w_main, w_gkd, tm=512, tn=1024)
    p_ctx, gk_ctx = _inproj(ctx, mod4, lambda b: b_, nw1, w_ctx, w_gkd, tm=lc, tn=w_ctx.shape[1] // 2)

    g_ = s5w // S5_GROUP
    tc = S5_CHUNK
    n_ctx, n_lat = lc // tc, l_ // tc
    nblk = s5w // 128
    u_all = jnp.concatenate([p_ctx[:, :, :s5w], p_lat[:, :, :s5w]], axis=1)
    v = u_all.reshape(b_, n_ctx + n_lat, tc, nblk, 128).transpose(3, 1, 0, 2, 4)
    v = v.reshape(nblk, (n_ctx + n_lat) * b_, tc * 128)
    ws, wq, wout5, acoef, dvec = _s5_factors(s5_lam_re[li], s5_lam_im[li], s5_log_dt[li], s5_b_re[li],
                                             s5_b_im[li], s5_c_re[li], s5_c_im[li], s5_d[li])
    y2 = _s5v2(v, ws, wq, wout5, acoef, dvec, n_ctx, n_lat, b_)
    ys = y2.reshape(nblk, n_lat, b_, tc, 128).transpose(2, 1, 3, 0, 4).reshape(b_, l_, s5w)

    up_pad = jnp.zeros((2, 128, kw), F32)
    up_pad = up_pad.at[0, :GLA_GATE_RANK].set(gla_gk_up[li, 0]).at[1, GLA_GATE_RANK:rank2].set(gla_gk_up[li, 1])
    yg = _gla(p_lat, p_ctx, gk_lat, gk_ctx, up_pad.astype(BF16), gla_gk_b[li].reshape(2, 1, kw), dk, dv)

    rw = jnp.zeros((d, 128), F32).at[:, :N_EXPERTS].set(router_w[li])
    rw_hi = rw.astype(BF16)
    rw2 = jnp.stack([rw_hi, (rw - rw_hi.astype(F32)).astype(BF16)])
    x1, h2p, logits = _merge(
        ys, yg, p_lat, x, mod4, s5_glu_w[li].astype(BF16), w_s5_proj[li].astype(BF16),
        w_gla_proj[li].astype(BF16), w_out[li].astype(BF16), gla_norm_w[li].reshape(1, dv),
        norm2_w[li].reshape(1, d), rw2, dv, tm=256)

    t = b_ * l_
    logits_t = logits.reshape(t, 128)[:, :N_EXPERTS].T
    eidx, wts, rank, cnt = _route(logits_t, router_bias[li])
    nb = (t * TOP_K + N_EXPERTS * (EXPERT_ROWS - 1) + EXPERT_ROWS - 1) // EXPERT_ROWS + EXPERT_LOOKAHEAD
    dest, rows, block_e, first, nxt, n_used = _slots(cnt[:, 0].astype(I32), eidx, rank, nb)

    tiles = d // 256
    y_sorted = _experts(block_e, first, nxt, n_used, rows.reshape(nb, EXPERT_ROWS), h2p.reshape(t, tiles, 128),
                        exp_w_gate[li], exp_w_up[li], exp_w_down[li])

    tm_c = 256
    out = _combine(dest, y_sorted.reshape(-1, tiles, 128), wts.T, x1.reshape(t, d), h2p.reshape(t * tiles, 128), mod4,
                   sh_w_gate[li].astype(BF16), sh_w_up[li].astype(BF16), sh_w_down[li].astype(BF16),
                   final_norm_w.reshape(1, d), l_ // tm_c, tm_c)
    return out.reshape(b_, l_, d)
```

```python
import functools
import math

import jax
import jax.numpy as jnp
from jax import lax
from jax.experimental import pallas as pl
from jax.experimental.pallas import tpu as pltpu

F32 = jnp.float32
BF16 = jnp.bfloat16
U32 = jnp.uint32
I32 = jnp.int32

EPS = 1e-6
N_MOD = 6

S5_GROUP = 16
S5_STATE = 64
S5_CHUNK = 16
S5_GROUPS_PER_STEP = 8

GLA_HEADS = 4
GLA_GATE_RANK = 16
GLA_GATE_NORM = 16.0
GLA_CHUNK = 64

N_EXPERTS = 64
TOP_K = 8
N_GROUPS = 8
TOPK_GROUPS = 4
ROUTED_SCALE = 2.5
EXPERT_ROWS = 256

VMEM_LIMIT = 56 * 1024 * 1024


def _cparams(sem, vmem=VMEM_LIMIT):
    return pltpu.CompilerParams(dimension_semantics=sem, vmem_limit_bytes=vmem)


def _dot(a, b):
    return jnp.dot(a, b, preferred_element_type=F32)


def _sigmoid(x):
    return 1.0 / (1.0 + jnp.exp(-x))


def _silu(x):
    return x * _sigmoid(x)


def _pack_bf16_pair(lo, hi):
    lo_bits = lax.bitcast_convert_type(lo.astype(BF16).astype(F32), U32)
    hi_bits = lax.bitcast_convert_type(hi.astype(BF16).astype(F32), U32)
    return (lo_bits >> 16) | (hi_bits & jnp.uint32(0xFFFF0000))


def _unpack_bf16_pair(w):
    lo = lax.bitcast_convert_type(w << 16, F32)
    hi = lax.bitcast_convert_type(w & jnp.uint32(0xFFFF0000), F32)
    return lo, hi


def _store_row_tiles(ref, lead, val):
    rows, rt = val.shape[0], val.shape[1] // 128
    for kc in range(rt):
        ref[(*lead, pl.ds(kc, rows, stride=rt), slice(None))] = val[:, kc * 128:(kc + 1) * 128]


def _load_row_tiles(ref, lead, row0, rows, rt):
    return jnp.concatenate(
        [ref[(*lead, pl.ds(row0 * rt + kc, rows, stride=rt), slice(None))] for kc in range(rt)], axis=-1)


def _ada_body(c_ref, w_ref, b_ref, o_ref):
    c = c_ref[...]
    s = _silu(c).astype(BF16)
    o_ref[...] = _dot(s, w_ref[...].astype(BF16)) + b_ref[...]


def _ada(c8, ada_w, ada_b):
    d, n = ada_w.shape
    tn = 512
    return pl.pallas_call(
        _ada_body,
        grid=(n // tn,),
        in_specs=[
            pl.BlockSpec((8, d), lambda j: (0, 0)),
            pl.BlockSpec((d, tn), lambda j: (0, j)),
            pl.BlockSpec((1, tn), lambda j: (0, j)),
        ],
        out_specs=pl.BlockSpec((8, tn), lambda j: (0, j)),
        out_shape=jax.ShapeDtypeStruct((8, n), F32),
        compiler_params=_cparams(("parallel",)),
        name="ada",
    )(c8, ada_w, ada_b.reshape(1, n))


def _modulated_norm(x, nw, sh, sc):
    ms = jnp.mean(x * x, axis=-1, keepdims=True)
    y = x * lax.rsqrt(ms + EPS) * nw
    return y * (1.0 + sc) + sh


def _inproj_body(x_ref, sh_ref, sc_ref, nw_ref, w_ref, wg_ref, p_ref, g_ref, h_scr):
    @pl.when(pl.program_id(2) == 0)
    def _():
        h = _modulated_norm(x_ref[0], nw_ref[...], sh_ref[0, 0], sc_ref[0, 0]).astype(BF16)
        h_scr[...] = h
        g_ref[0] = _dot(h, wg_ref[...])

    p_ref[0] = _dot(h_scr[...], w_ref[...]).astype(BF16)


def _inproj(x, mod4, mod_row, nw, w, wg, tm, tn):
    b_, l_, d = x.shape
    n = w.shape[1]
    return pl.pallas_call(
        _inproj_body,
        grid=(b_, l_ // tm, n // tn),
        in_specs=[
            pl.BlockSpec((1, tm, d), lambda b, i, j: (b, i, 0)),
            pl.BlockSpec((1, 1, 1, d), lambda b, i, j: (mod_row(b), 0, 0, 0)),
            pl.BlockSpec((1, 1, 1, d), lambda b, i, j: (mod_row(b), 1, 0, 0)),
            pl.BlockSpec((1, d), lambda b, i, j: (0, 0)),
            pl.BlockSpec((d, tn), lambda b, i, j: (0, j)),
            pl.BlockSpec((d, 128), lambda b, i, j: (0, 0)),
        ],
        out_specs=[
            pl.BlockSpec((1, tm, tn), lambda b, i, j: (b, i, j)),
            pl.BlockSpec((1, tm, 128), lambda b, i, j: (b, i, 0)),
        ],
        out_shape=[
            jax.ShapeDtypeStruct((b_, l_, n), BF16),
            jax.ShapeDtypeStruct((b_, l_, 128), F32),
        ],
        scratch_shapes=[pltpu.VMEM((tm, d), BF16)],
        compiler_params=_cparams(("parallel", "parallel", "arbitrary")),
        name="inproj",
    )(x, mod4, mod4, nw, w, wg)


def _s5_weights(lam_re, lam_im, log_dt, b_re, b_im, c_re, c_im, d_skip):
    hp = lax.Precision.HIGHEST
    tc = S5_CHUNK
    g_, p_ = lam_re.shape[1:]
    h_ = b_re.shape[-1]
    dt = jnp.exp(log_dt)[..., None]
    ar, ai = lam_re * dt, lam_im * dt
    k = jnp.arange(tc + 1, dtype=F32)
    mag = jnp.exp(ar[..., None] * k)
    pw_re = mag * jnp.cos(ai[..., None] * k)
    pw_im = mag * jnp.sin(ai[..., None] * k)
    num_re = jnp.expm1(ar) * jnp.cos(ai) - 2.0 * jnp.sin(0.5 * ai) ** 2
    num_im = jnp.exp(ar) * jnp.sin(ai)
    den = lam_re * lam_re + lam_im * lam_im
    f_re = (num_re * lam_re + num_im * lam_im) / den
    f_im = (num_im * lam_re - num_re * lam_im) / den
    bb_re = f_re[..., None] * b_re - f_im[..., None] * b_im
    bb_im = f_re[..., None] * b_im + f_im[..., None] * b_re

    cp_re = c_re[..., None] * pw_re[:, :, None] - c_im[..., None] * pw_im[:, :, None]
    cp_im = c_re[..., None] * pw_im[:, :, None] + c_im[..., None] * pw_re[:, :, None]
    kk = (jnp.einsum('dgipt,dgpj->dgtij', cp_re, bb_re, precision=hp)
          - jnp.einsum('dgipt,dgpj->dgtij', cp_im, bb_im, precision=hp))

    s_idx = jnp.arange(tc)[:, None]
    t_idx = jnp.arange(tc)[None, :]
    lag_f = t_idx - s_idx
    lag_b = s_idx - t_idx
    kf = jnp.where((lag_f >= 0)[None, :, :, None, None], kk[0][:, jnp.clip(lag_f, 0, tc)], 0.0)
    kb = jnp.where((lag_b >= 0)[None, :, :, None, None], kk[1][:, jnp.clip(lag_b, 0, tc)], 0.0)
    dsk = d_skip.reshape(g_, h_)
    eye_t = jnp.eye(tc, dtype=F32)[None, :, :, None, None]
    eye_h = jnp.eye(h_, dtype=F32)[None, None, None]
    m_all = kf + kb + eye_t * eye_h * dsk[:, None, None, :, None]
    m_all = m_all.transpose(0, 1, 4, 2, 3).reshape(g_, tc * h_, tc * h_)

    def state_w(dr, expo):
        e_re = pw_re[dr][:, :, expo]
        e_im = pw_im[dr][:, :, expo]
        w_re = jnp.einsum('gps,gpj->gsjp', e_re, bb_re[dr], precision=hp) - jnp.einsum(
            'gps,gpj->gsjp', e_im, bb_im[dr], precision=hp)
        w_im = jnp.einsum('gps,gpj->gsjp', e_re, bb_im[dr], precision=hp) + jnp.einsum(
            'gps,gpj->gsjp', e_im, bb_re[dr], precision=hp)
        w_re = w_re.reshape(g_, tc * h_, p_)
        w_im = w_im.reshape(g_, tc * h_, p_)
        return jnp.concatenate([w_re, w_im, w_im, w_re], axis=-1)

    ws_f = state_w(0, tc - 1 - jnp.arange(tc))
    ws_b = state_w(1, jnp.arange(tc))
    wcat = jnp.concatenate([m_all, ws_f, ws_b], axis=-1)

    def out_w(dr, expo):
        q_re = cp_re[dr][..., expo]
        q_im = cp_im[dr][..., expo]
        top = q_re.transpose(0, 2, 3, 1).reshape(g_, p_, tc * h_)
        bot = (-q_im).transpose(0, 2, 3, 1).reshape(g_, p_, tc * h_)
        return jnp.concatenate([top, bot], axis=1)

    wout = jnp.concatenate([out_w(0, 1 + jnp.arange(tc)), out_w(1, tc - jnp.arange(tc))], axis=1)

    a_re, a_im = pw_re[..., tc], pw_im[..., tc]
    zeros = jnp.zeros_like(a_re[0])
    rows = []
    for dr in range(2):
        rows += [jnp.concatenate([a_re[dr], a_re[dr]], -1),
                 jnp.concatenate([-a_im[dr], a_im[dr]], -1),
                 jnp.concatenate([a_im[dr], -a_im[dr]], -1)]
    rows += [jnp.concatenate([zeros, zeros], -1)] * 2
    acoef = jnp.stack(rows, axis=1)
    return wcat.astype(BF16), wout.astype(BF16), acoef


def _s5_body(x_ref, wcat_ref, wout_ref, a_ref, y_ref, y_scr, s_scr, h_scr, *, n_ctx, n_lat, nb):
    gb = x_ref.shape[0]
    pw = 2 * S5_STATE
    for g in range(gb):
        r = _dot(x_ref[g], wcat_ref[g])
        y_scr[g] = r[:, : y_scr.shape[-1]]
        s_scr[g] = r[:, y_scr.shape[-1]:]

    a1f, a2f, a2sf = a_ref[:, 0:1, :], a_ref[:, 1:2, :], a_ref[:, 2:3, :]
    a1b, a2b, a2sb = a_ref[:, 3:4, :], a_ref[:, 4:5, :], a_ref[:, 5:6, :]

    first = lax.broadcasted_iota(I32, (gb, 2 * nb, pw), 1) < nb

    def both(v):
        r = pltpu.roll(v, nb, 1)
        return jnp.where(first, v, r), jnp.where(first, r, v)

    def step(pf, pb, carry):
        hf, hfs, hb, hbs = carry
        rf = pl.ds(pl.multiple_of(pf * 2 * nb, 2 * nb), 2 * nb)
        rb = pl.ds(pl.multiple_of(pb * 2 * nb, 2 * nb), 2 * nb)
        sfa, sfb = both(s_scr[:, rf, 0:pw])
        sfsa, sfsb = both(s_scr[:, rf, pw:2 * pw])
        sba, sbb = both(s_scr[:, rb, 2 * pw:3 * pw])
        sbsa, sbsb = both(s_scr[:, rb, 3 * pw:4 * pw])
        hf1 = a1f * hf + a2f * hfs + sfa
        hfs1 = a1f * hfs + a2sf * hf + sfsa
        h_scr[:, rf, 0:pw] = jnp.where(first, hf, hf1)
        hf2 = a1f * hf1 + a2f * hfs1 + sfb
        hfs2 = a1f * hfs1 + a2sf * hf1 + sfsb
        hb1 = a1b * hb + a2b * hbs + sbb
        hbs1 = a1b * hbs + a2sb * hb + sbsb
        h_scr[:, rb, pw:2 * pw] = jnp.where(first, hb1, hb)
        hb2 = a1b * hb1 + a2b * hbs1 + sba
        hbs2 = a1b * hbs1 + a2sb * hb1 + sbsa
        return hf2, hfs2, hb2, hbs2

    z = jnp.zeros((gb, 2 * nb, pw), F32)
    carry = (z, z, z, z)
    pc, pl_ = n_ctx // 2, n_lat // 2
    carry = lax.fori_loop(0, pc, lambda i, c: step(i, pc - 1 - i, c), carry)
    carry = lax.fori_loop(0, pl_, lambda i, c: step(pc + i, pc + pl_ - 1 - i, c), carry)

    lo = n_ctx * nb
    for g in range(gb):
        y = y_scr[g] + _dot(h_scr[g].astype(BF16), wout_ref[g])
        y_ref[g] = y[lo:, :]


def _s5(x2, wcat, wout, acoef, n_ctx, n_lat, nb):
    g_, rows, kdim = x2.shape
    gb = S5_GROUPS_PER_STEP
    pw = 2 * S5_STATE
    out_rows = n_lat * nb
    return pl.pallas_call(
        functools.partial(_s5_body, n_ctx=n_ctx, n_lat=n_lat, nb=nb),
        grid=(g_ // gb,),
        in_specs=[
            pl.BlockSpec((gb, rows, kdim), lambda i: (i, 0, 0)),
            pl.BlockSpec((gb, kdim, wcat.shape[-1]), lambda i: (i, 0, 0)),
            pl.BlockSpec((gb, 2 * pw, kdim), lambda i: (i, 0, 0)),
            pl.BlockSpec((gb, 8, pw), lambda i: (i, 0, 0)),
        ],
        out_specs=pl.BlockSpec((gb, out_rows, kdim), lambda i: (i, 0, 0)),
        out_shape=jax.ShapeDtypeStruct((g_, out_rows, kdim), F32),
        scratch_shapes=[
            pltpu.VMEM((gb, rows, kdim), F32),
            pltpu.VMEM((gb, rows, 4 * pw), F32),
            pltpu.VMEM((gb, rows, 2 * pw), F32),
        ],
        compiler_params=_cparams(("parallel",)),
        name="s5",
    )(x2, wcat, wout, acoef)


def _s5_factors(lam_re, lam_im, log_dt, b_re, b_im, c_re, c_im, d_skip):
    tc = S5_CHUNK
    g_, p_ = lam_re.shape[1:]
    h_ = b_re.shape[-1]
    dt = jnp.exp(log_dt)[..., None]
    ar, ai = lam_re * dt, lam_im * dt

    def powers(k):
        k = jnp.asarray(k, F32)
        mag = jnp.exp(ar[..., None] * k)
        return mag * jnp.cos(ai[..., None] * k), mag * jnp.sin(ai[..., None] * k)

    num_re = jnp.expm1(ar) * jnp.cos(ai) - 2.0 * jnp.sin(0.5 * ai) ** 2
    num_im = jnp.exp(ar) * jnp.sin(ai)
    den = lam_re * lam_re + lam_im * lam_im
    f_re = (num_re * lam_re + num_im * lam_im) / den
    f_im = (num_im * lam_re - num_re * lam_im) / den
    bb_re = f_re[..., None] * b_re - f_im[..., None] * b_im
    bb_im = f_re[..., None] * b_im + f_im[..., None] * b_re
    t_idx = jnp.arange(tc)

    def state_w(dr, expo):
        e_re, e_im = powers(expo)
        e_re = e_re[dr].transpose(0, 2, 1)[:, :, None, :]
        e_im = e_im[dr].transpose(0, 2, 1)[:, :, None, :]
        br = bb_re[dr].transpose(0, 2, 1)[:, None, :, :]
        bi = bb_im[dr].transpose(0, 2, 1)[:, None, :, :]
        w_re = (e_re * br - e_im * bi).reshape(g_, tc * h_, p_)
        w_im = (e_re * bi + e_im * br).reshape(g_, tc * h_, p_)
        return jnp.concatenate([w_re, w_im], axis=-1)

    def out_w(dr, expo):
        e_re, e_im = powers(expo)
        e_re = e_re[dr][:, :, :, None]
        e_im = e_im[dr][:, :, :, None]
        cr = c_re[dr].transpose(0, 2, 1)[:, :, None, :]
        ci = c_im[dr].transpose(0, 2, 1)[:, :, None, :]
        q_re = (cr * e_re - ci * e_im).reshape(g_, p_, tc * h_)
        q_im = (cr * e_im + ci * e_re).reshape(g_, p_, tc * h_)
        return jnp.concatenate([q_re, -q_im], axis=1)

    ws = jnp.stack([state_w(0, tc - 1 - t_idx), state_w(1, t_idx)], axis=1)
    wq = jnp.stack([out_w(0, t_idx - (tc - 1)), out_w(1, -t_idx)], axis=1)
    wout = jnp.concatenate([out_w(0, t_idx + 1), out_w(1, tc - t_idx)], axis=1).astype(BF16)

    a_re, a_im = powers(jnp.array([tc]))
    a_re, a_im = a_re[..., 0], a_im[..., 0]
    zeros = jnp.zeros_like(a_re[0])
    rows = []
    for dr in range(2):
        rows += [jnp.concatenate([a_re[dr], a_re[dr]], -1),
                 jnp.concatenate([-a_im[dr], a_im[dr]], -1),
                 jnp.concatenate([a_im[dr], -a_im[dr]], -1)]
    rows += [jnp.concatenate([zeros, zeros], -1)] * 2
    acoef = jnp.stack(rows, axis=1)
    dvec = jnp.tile(d_skip.reshape(g_, h_), (1, tc)).reshape(g_, 1, tc * h_)
    return ws, wq, wout, acoef, dvec


def _split_bf16(x):
    hi = x.astype(BF16)
    return hi, (x - hi.astype(F32)).astype(BF16)


def _s5v2_body(v_ref, ws_ref, wq_ref, wout_ref, a_ref, d_ref, o_ref, y_scr, s_scr, h_scr, *, n_ctx, n_lat, nb):
    gb = ws_ref.shape[0]
    pw = 2 * S5_STATE
    gl = S5_GROUP
    kd = S5_CHUNK * gl
    lanes = 128
    v = v_ref[0]
    width = v.shape[1]

    r_in = lax.broadcasted_iota(I32, (width, kd), 0)
    c_in = lax.broadcasted_iota(I32, (width, kd), 1)
    tgt_in = (r_in // lanes) * gl + (r_in % gl)
    grp_in = (r_in % lanes) // gl
    r_out = lax.broadcasted_iota(I32, (kd, width), 0)
    c_out = lax.broadcasted_iota(I32, (kd, width), 1)
    tgt_out = (c_out // lanes) * gl + (c_out % gl)
    grp_out = (c_out % lanes) // gl

    rs = lax.broadcasted_iota(I32, (kd, kd), 0)
    ct = lax.broadcasted_iota(I32, (kd, kd), 1)
    fmask = ct // gl >= rs // gl
    bmask = ct // gl <= rs // gl
    diag = rs == ct

    for q in range(gb):
        sel = jnp.where(jnp.logical_and(c_in == tgt_in, grp_in == q), 1.0, 0.0).astype(BF16)
        x2 = _dot(v, sel).astype(BF16)
        mm = []
        for dr in range(2):
            a_hi, a_lo = _split_bf16(ws_ref[q, dr])
            b_hi, b_lo = _split_bf16(wq_ref[q, dr])
            mm.append(_dot(a_hi, b_hi) + _dot(a_hi, b_lo) + _dot(a_lo, b_hi))
        m = jnp.where(fmask, mm[0], 0.0) + jnp.where(bmask, mm[1], 0.0) + jnp.where(diag, d_ref[q], 0.0)
        wsf, wsb = ws_ref[q, 0], ws_ref[q, 1]
        wcat = jnp.concatenate(
            [m, wsf, pltpu.roll(wsf, S5_STATE, 1), wsb, pltpu.roll(wsb, S5_STATE, 1)], axis=1).astype(BF16)
        r = _dot(x2, wcat)
        y_scr[q] = r[:, :kd]
        s_scr[q] = r[:, kd:]

    a1f, a2f, a2sf = a_ref[:, 0:1, :], a_ref[:, 1:2, :], a_ref[:, 2:3, :]
    a1b, a2b, a2sb = a_ref[:, 3:4, :], a_ref[:, 4:5, :], a_ref[:, 5:6, :]
    first = lax.broadcasted_iota(I32, (gb, 2 * nb, pw), 1) < nb

    def both(x):
        rr = pltpu.roll(x, nb, 1)
        return jnp.where(first, x, rr), jnp.where(first, rr, x)

    def step(pf, pb, carry):
        hf, hfs, hb, hbs = carry
        rf = pl.ds(pl.multiple_of(pf * 2 * nb, 2 * nb), 2 * nb)
        rb = pl.ds(pl.multiple_of(pb * 2 * nb, 2 * nb), 2 * nb)
        sfa, sfb = both(s_scr[:, rf, 0:pw])
        sfsa, sfsb = both(s_scr[:, rf, pw:2 * pw])
        sba, sbb = both(s_scr[:, rb, 2 * pw:3 * pw])
        sbsa, sbsb = both(s_scr[:, rb, 3 * pw:4 * pw])
        hf1 = a1f * hf + a2f * hfs + sfa
        hfs1 = a1f * hfs + a2sf * hf + sfsa
        h_scr[:, rf, 0:pw] = jnp.where(first, hf, hf1)
        hf2 = a1f * hf1 + a2f * hfs1 + sfb
        hfs2 = a1f * hfs1 + a2sf * hf1 + sfsb
        hb1 = a1b * hb + a2b * hbs + sbb
        hbs1 = a1b * hbs + a2sb * hb + sbsb
        h_scr[:, rb, pw:2 * pw] = jnp.where(first, hb1, hb)
        hb2 = a1b * hb1 + a2b * hbs1 + sba
        hbs2 = a1b * hbs1 + a2sb * hb1 + sbsa
        return hf2, hfs2, hb2, hbs2

    z = jnp.zeros((gb, 2 * nb, pw), F32)
    carry = (z, z, z, z)
    pc, pl_ = n_ctx // 2, n_lat // 2
    carry = lax.fori_loop(0, pc, lambda i, c: step(i, pc - 1 - i, c), carry)
    carry = lax.fori_loop(0, pl_, lambda i, c: step(pc + i, pc + pl_ - 1 - i, c), carry)

    lo = n_ctx * nb
    for q in range(gb):
        y = y_scr[q] + _dot(h_scr[q].astype(BF16), wout_ref[q])
        selt = jnp.where(jnp.logical_and(r_out == tgt_out, grp_out == q), 1.0, 0.0).astype(BF16)
        placed = _dot(y[lo:, :].astype(BF16), selt)
        if q == 0:
            o_ref[0] = placed
        else:
            o_ref[0] = o_ref[0] + placed


def _s5v2(v, ws, wq, wout, acoef, dvec, n_ctx, n_lat, nb):
    nblk, rows, width = v.shape
    g_ = ws.shape[0]
    gb = g_ // nblk
    pw = 2 * S5_STATE
    kd = S5_CHUNK * S5_GROUP
    out_rows = n_lat * nb
    return pl.pallas_call(
        functools.partial(_s5v2_body, n_ctx=n_ctx, n_lat=n_lat, nb=nb),
        grid=(nblk,),
        in_specs=[
            pl.BlockSpec((1, rows, width), lambda i: (i, 0, 0)),
            pl.BlockSpec((gb, 2, kd, pw), lambda i: (i, 0, 0, 0)),
            pl.BlockSpec((gb, 2, pw, kd), lambda i: (i, 0, 0, 0)),
            pl.BlockSpec((gb, 2 * pw, kd), lambda i: (i, 0, 0)),
            pl.BlockSpec((gb, 8, pw), lambda i: (i, 0, 0)),
            pl.BlockSpec((gb, 1, kd), lambda i: (i, 0, 0)),
        ],
        out_specs=pl.BlockSpec((1, out_rows, width), lambda i: (i, 0, 0)),
        out_shape=jax.ShapeDtypeStruct((nblk, out_rows, width), F32),
        scratch_shapes=[
            pltpu.VMEM((gb, rows, kd), F32),
            pltpu.VMEM((gb, rows, 4 * pw), F32),
            pltpu.VMEM((gb, rows, 2 * pw), F32),
        ],
        compiler_params=_cparams(("parallel",)),
        name="s5",
    )(v, ws, wq, wout, acoef, dvec)


def _log_sigmoid(z):
    return jnp.minimum(z, 0.0) - jnp.log1p(jnp.exp(-jnp.abs(z)))


def _chunk_cumsum(g, pos, reverse):
    n = g.shape[0]
    x = g
    sh = 1
    while sh < GLA_CHUNK:
        if reverse:
            x = x + jnp.where(pos < GLA_CHUNK - sh, pltpu.roll(x, n - sh, 0), 0.0)
        else:
            x = x + jnp.where(pos >= sh, pltpu.roll(x, sh, 0), 0.0)
        sh *= 2
    return x


def _gla_body(q_ref, k_ref, v_ref, kc_ref, vc_ref, gk_ref, gkc_ref, up_ref, bias_ref, o_ref,
              bc_scr, kb_scr, qb_scr, st_ref, *, scale):
    c = GLA_CHUNK
    l_ = q_ref.shape[1]
    lc = kc_ref.shape[1]
    n_lat, n_ctx = l_ // c, lc // c
    row = lax.broadcasted_iota(I32, (c, c), 0)
    col = lax.broadcasted_iota(I32, (c, c), 1)
    nt = (((1,), (1,)), ((), ()))
    tn = (((0,), (0,)), ((), ()))

    for dr in range(2):
        smask = (col <= row) if dr == 0 else (col > row)
        up = up_ref[dr]
        bias = bias_ref[dr]
        edge_tile, edge_row = (c - 8, 7) if dr == 0 else (0, 0)

        for gk_r, k_r, base, n, latent in ((gkc_ref, kc_ref, 0, lc, False), (gk_ref, k_ref, lc, l_, True)):
            z = _dot(gk_r[0].astype(BF16), up) + bias
            g = _log_sigmoid(z) * (1.0 / GLA_GATE_NORM)
            pos = lax.broadcasted_iota(I32, g.shape, 0) & (c - 1)
            bc = _chunk_cumsum(g, pos, reverse=(dr == 1))
            bc_scr[base:base + n, :] = bc
            kb_scr[base:base + n, :] = k_r[0].astype(F32) * jnp.exp(-bc)
            if latent:
                qb_scr[...] = (q_ref[0].astype(F32) * (scale * jnp.exp(bc))).astype(BF16)

        def chunk_rows(n, base):
            r0 = pl.multiple_of(n * c, c)
            tot = bc_scr[pl.ds(base + r0 + edge_tile, 8), :][edge_row:edge_row + 1, :]
            return r0, pl.ds(base + r0, c), jnp.exp(tot)

        def update_state(kb, vv, decay):
            kd = (kb * decay).astype(BF16)
            ds_t = lax.dot_general(vv, kd, tn, preferred_element_type=F32)
            st_ref[...] = decay * st_ref[...] + ds_t

        def ctx_step(i, _):
            n = i if dr == 0 else n_ctx - 1 - i
            r0, rs, decay = chunk_rows(n, 0)
            update_state(kb_scr[rs, :], vc_ref[0, pl.ds(r0, c), :], decay)
            return 0

        def lat_step(i, _):
            n = i if dr == 0 else n_lat - 1 - i
            r0, rs, decay = chunk_rows(n, lc)
            ro = pl.ds(r0, c)
            kb = kb_scr[rs, :]
            vv = v_ref[0, ro, :]
            qb = qb_scr[ro, :]
            scores = lax.dot_general(qb, kb.astype(BF16), nt, preferred_element_type=F32)
            scores = jnp.where(smask, scores, 0.0).astype(BF16)
            o = _dot(scores, vv) + lax.dot_general(qb, st_ref[...].astype(BF16), nt,
                                                   preferred_element_type=F32)
            if dr == 0:
                o_ref[0, ro, :] = o
            else:
                o_ref[0, ro, :] = o_ref[0, ro, :] + o
            update_state(kb, vv, decay)
            return 0

        st_ref[...] = jnp.zeros_like(st_ref)
        lax.fori_loop(0, n_ctx, ctx_step, 0, unroll=2)
        lax.fori_loop(0, n_lat, lat_step, 0, unroll=4)


def _gla(p_lat, p_ctx, gk_lat, gk_ctx, up_pad, bias, dk, dv):
    b_, l_, _ = p_lat.shape
    lc = p_ctx.shape[1]
    hh = GLA_HEADS
    s5w = hh * dv
    q0, k0, v0 = s5w // dk, (s5w + hh * dk) // dk, (s5w + 2 * hh * dk) // dv
    kc0, vc0 = s5w // dk, (s5w + hh * dk) // dv
    return pl.pallas_call(
        functools.partial(_gla_body, scale=dk ** -0.5),
        grid=(b_, hh),
        in_specs=[
            pl.BlockSpec((1, l_, dk), lambda b, h: (b, 0, q0 + h)),
            pl.BlockSpec((1, l_, dk), lambda b, h: (b, 0, k0 + h)),
            pl.BlockSpec((1, l_, dv), lambda b, h: (b, 0, v0 + h)),
            pl.BlockSpec((1, lc, dk), lambda b, h: (b, 0, kc0 + h)),
            pl.BlockSpec((1, lc, dv), lambda b, h: (b, 0, vc0 + h)),
            pl.BlockSpec((1, l_, 128), lambda b, h: (b, 0, 0)),
            pl.BlockSpec((1, lc, 128), lambda b, h: (b, 0, 0)),
            pl.BlockSpec((2, 128, dk), lambda b, h: (0, 0, h)),
            pl.BlockSpec((2, 1, dk), lambda b, h: (0, 0, h)),
        ],
        out_specs=pl.BlockSpec((1, l_, dv), lambda b, h: (b, 0, h)),
        out_shape=jax.ShapeDtypeStruct((b_, l_, hh * dv), F32),
        scratch_shapes=[
            pltpu.VMEM((lc + l_, dk), F32),
            pltpu.VMEM((lc + l_, dk), F32),
            pltpu.VMEM((l_, dk), BF16),
            pltpu.VMEM((dv, dk), F32),
        ],
        compiler_params=_cparams(("parallel", "parallel")),
        name="gla",
    )(p_lat, p_lat, p_lat, p_ctx, p_ctx, gk_lat, gk_ctx, up_pad, bias)


def _gelu_tanh(x):
    return 0.5 * x * (1.0 + jnp.tanh(math.sqrt(2.0 / math.pi) * (x + 0.044715 * (x * x * x))))


def _merge_body(ys_ref, yg_ref, go_ref, gs_ref, gg_ref, x_ref, g1_ref, sh2_ref, sc2_ref,
                glu_ref, ws5_ref, wgla_ref, wout_ref, gnw_ref, n2w_ref, rw_ref,
                x1_ref, h2_ref, lg_ref, *, dv):
    a = _gelu_tanh(ys_ref[0])
    a = a * _sigmoid(_dot(a.astype(BF16), glu_ref[...]))
    pa = _dot(a.astype(BF16), ws5_ref[...])

    yg = yg_ref[0]
    parts = []
    for h in range(GLA_HEADS):
        oh = yg[:, h * dv:(h + 1) * dv]
        ms = jnp.mean(oh * oh, axis=-1, keepdims=True)
        parts.append(oh * lax.rsqrt(ms + EPS) * gnw_ref[...])
    gl = jnp.concatenate(parts, axis=-1) * _silu(go_ref[0].astype(F32))
    pb = _dot(gl.astype(BF16), wgla_ref[...])

    m = _sigmoid(gs_ref[0].astype(F32)) * pa + _sigmoid(gg_ref[0].astype(F32)) * pb
    y = _dot(m.astype(BF16), wout_ref[...])
    x1 = x_ref[0] + g1_ref[0, 0] * y
    x1_ref[0] = x1

    h2 = _modulated_norm(x1, n2w_ref[...], sh2_ref[0, 0], sc2_ref[0, 0])
    h_hi = h2.astype(BF16)
    h_lo = (h2 - h_hi.astype(F32)).astype(BF16)
    half = h2.shape[-1] // 2
    _store_row_tiles(h2_ref, (0,), _pack_bf16_pair(h2[:, :half], h2[:, half:]))
    lg_ref[0] = _dot(h_hi, rw_ref[0]) + _dot(h_lo, rw_ref[0]) + _dot(h_hi, rw_ref[1])


def _resident(shape):
    nd = len(shape)
    return pl.BlockSpec(shape, lambda b, i: (0,) * nd, pipeline_mode=pl.Buffered(1))


def _merge(ys, yg, p_lat, x, mod4, glu_w, ws5, wgla, wout, gnw, n2w, rw2, dv, tm):
    b_, l_, d = x.shape
    w5 = ys.shape[-1]
    go0 = (p_lat.shape[-1] - 2 * d - w5) // w5
    gs0 = (p_lat.shape[-1] - 2 * d) // d
    modspec = lambda r: pl.BlockSpec((1, 1, 1, d), lambda b, i: (b, r, 0, 0))
    return pl.pallas_call(
        functools.partial(_merge_body, dv=dv),
        grid=(b_, l_ // tm),
        in_specs=[
            pl.BlockSpec((1, tm, w5), lambda b, i: (b, i, 0)),
            pl.BlockSpec((1, tm, w5), lambda b, i: (b, i, 0)),
            pl.BlockSpec((1, tm, w5), lambda b, i: (b, i, go0)),
            pl.BlockSpec((1, tm, d), lambda b, i: (b, i, gs0)),
            pl.BlockSpec((1, tm, d), lambda b, i: (b, i, gs0 + 1)),
            pl.BlockSpec((1, tm, d), lambda b, i: (b, i, 0)),
            modspec(2), modspec(3), modspec(4),
            _resident(glu_w.shape), _resident(ws5.shape), _resident(wgla.shape), _resident(wout.shape),
            _resident(gnw.shape), _resident(n2w.shape), _resident(rw2.shape),
        ],
        out_specs=[
            pl.BlockSpec((1, tm, d), lambda b, i: (b, i, 0)),
            pl.BlockSpec((1, tm * (d // 256), 128), lambda b, i: (b, i, 0)),
            pl.BlockSpec((1, tm, 128), lambda b, i: (b, i, 0)),
        ],
        out_shape=[
            jax.ShapeDtypeStruct((b_, l_, d), F32),
            jax.ShapeDtypeStruct((b_, l_ * (d // 256), 128), U32),
            jax.ShapeDtypeStruct((b_, l_, 128), F32),
        ],
        compiler_params=_cparams(("parallel", "parallel")),
        name="merge",
    )(ys, yg, p_lat, p_lat, p_lat, x, mod4, mod4, mod4, glu_w, ws5, wgla, wout, gnw, n2w, rw2)


def _route_body(lg_ref, bias_ref, eidx_ref, w_ref, rank_ref, cnt_ref, carry_ref):
    ne, tt = lg_ref.shape
    gsz = ne // N_GROUPS
    neg = -jnp.inf

    @pl.when(pl.program_id(0) == 0)
    def _():
        carry_ref[...] = jnp.zeros_like(carry_ref)

    sc = _sigmoid(lg_ref[...])
    ch = sc + bias_ref[...]

    def first_max(v, idx, big):
        m = jnp.max(v, axis=0, keepdims=True)
        i = jnp.min(jnp.where(v == m, idx, big), axis=0, keepdims=True)
        return m, i

    midx = lax.broadcasted_iota(I32, (gsz, tt), 0).astype(F32)
    gs_rows = []
    for g in range(N_GROUPS):
        v = ch[g * gsz:(g + 1) * gsz, :]
        m1, i1 = first_max(v, midx, float(gsz))
        m2 = jnp.max(jnp.where(midx == i1, neg, v), axis=0, keepdims=True)
        gs_rows.append(m1 + m2)
    gscore = jnp.concatenate(gs_rows, axis=0)

    gidx = lax.broadcasted_iota(I32, (N_GROUPS, tt), 0).astype(F32)
    gsel = jnp.zeros((N_GROUPS, tt), F32)
    cur = gscore
    for _ in range(TOPK_GROUPS):
        _, i = first_max(cur, gidx, float(N_GROUPS))
        hit = gidx == i
        gsel = jnp.where(hit, 1.0, gsel)
        cur = jnp.where(hit, neg, cur)

    masked = jnp.concatenate(
        [jnp.where(gsel[g:g + 1, :] > 0.0, ch[g * gsz:(g + 1) * gsz, :], neg) for g in range(N_GROUPS)], axis=0)

    eiota = lax.broadcasted_iota(I32, (ne, tt), 0).astype(F32)
    sel = jnp.zeros((ne, tt), F32)
    cur = masked
    idx_rows, s_rows = [], []
    for _ in range(TOP_K):
        _, i = first_max(cur, eiota, float(ne))
        hit = eiota == i
        idx_rows.append(i)
        s_rows.append(jnp.sum(jnp.where(hit, sc, 0.0), axis=0, keepdims=True))
        sel = jnp.where(hit, 1.0, sel)
        cur = jnp.where(hit, neg, cur)
    idx = jnp.concatenate(idx_rows, axis=0)
    s = jnp.concatenate(s_rows, axis=0)
    w_ref[...] = s / jnp.sum(s, axis=0, keepdims=True) * ROUTED_SCALE
    eidx_ref[...] = idx.astype(I32)

    cw = 256
    a_i = lax.broadcasted_iota(I32, (cw, cw), 0)
    b_i = lax.broadcasted_iota(I32, (cw, cw), 1)
    upper = jnp.where(a_i < b_i, 1.0, 0.0).astype(BF16)
    carry = carry_ref[:, 0:1]
    pieces = []
    for c0 in range(0, tt, cw):
        sc_c = sel[:, c0:c0 + cw]
        pieces.append(_dot(sc_c.astype(BF16), upper) + carry)
        carry = carry + jnp.sum(sc_c, axis=1, keepdims=True)
    rank_full = jnp.concatenate(pieces, axis=1)
    carry_ref[...] = jnp.broadcast_to(carry, carry_ref.shape)
    cnt_ref[...] = jnp.broadcast_to(carry, cnt_ref.shape)
    rk = [jnp.sum(jnp.where(eiota == idx_rows[k], rank_full, 0.0), axis=0, keepdims=True) for k in range(TOP_K)]
    rank_ref[...] = jnp.concatenate(rk, axis=0).astype(I32)


def _route(logits_t, bias):
    ne, t = logits_t.shape
    tt = 1024
    return pl.pallas_call(
        _route_body,
        grid=(t // tt,),
        in_specs=[
            pl.BlockSpec((ne, tt), lambda i: (0, i)),
            pl.BlockSpec((ne, 1), lambda i: (0, 0)),
        ],
        out_specs=[
            pl.BlockSpec((TOP_K, tt), lambda i: (0, i)),
            pl.BlockSpec((TOP_K, tt), lambda i: (0, i)),
            pl.BlockSpec((TOP_K, tt), lambda i: (0, i)),
            pl.BlockSpec((ne, 128), lambda i: (0, 0)),
        ],
        out_shape=[
            jax.ShapeDtypeStruct((TOP_K, t), I32),
            jax.ShapeDtypeStruct((TOP_K, t), F32),
            jax.ShapeDtypeStruct((TOP_K, t), I32),
            jax.ShapeDtypeStruct((ne, 128), F32),
        ],
        scratch_shapes=[pltpu.VMEM((ne, 128), F32)],
        compiler_params=_cparams(("arbitrary",)),
        name="route",
    )(logits_t, bias.reshape(ne, 1))


def _slots_body(cnt_ref, eidx_ref, rank_ref, dest_ref, rows_ref, be_ref, first_ref, nxt_ref, nu_ref,
                off_smem, dchunk, sem):
    ne = cnt_ref.shape[0]
    nb = be_ref.shape[0]
    tl = eidx_ref.shape[1]
    r_ = EXPERT_ROWS
    shift = r_.bit_length() - 1
    c = pl.program_id(0)

    @pl.when(c == 0)
    def _():
        def per_expert(e, off):
            cnt = cnt_ref[e]
            n = lax.shift_right_logical(cnt + (r_ - 1), shift)
            off_smem[e] = off

            def fill(j, _):
                be_ref[off + j] = e
                first_ref[off + j] = jnp.where(j == 0, 1, 0)
                return 0
            lax.fori_loop(0, n, fill, 0)

            def pad(j, _):
                rows_ref[off * r_ + j] = 0
                return 0
            lax.fori_loop(cnt, n * r_, pad, 0)
            return off + n

        n_used = lax.fori_loop(0, ne, per_expert, 0)
        nu_ref[0] = n_used
        last_e = be_ref[n_used - 1]

        def tail(j, _):
            be_ref[j] = last_e
            first_ref[j] = 0

            def zero_row(q, _):
                rows_ref[j * r_ + q] = 0
                return 0
            lax.fori_loop(0, r_, zero_row, 0)
            return 0
        lax.fori_loop(n_used, nb, tail, 0)

        def back(i, nx):
            j = nb - 1 - i
            nxt_ref[j] = nx
            return jnp.where(first_ref[j] == 1, be_ref[j], nx)
        lax.fori_loop(0, nb, back, -1)

    ei = eidx_ref[...]
    acc = rank_ref[...]
    for e in range(ne):
        acc = acc + jnp.where(ei == e, off_smem[e] * r_, 0)
    dest_ref[...] = acc
    cp = pltpu.make_async_copy(dest_ref, dchunk, sem)
    cp.start()
    cp.wait()
    base = c * tl

    def scatter(j, _):
        for k in range(TOP_K):
            rows_ref[dchunk[k, j]] = base + j
        return 0
    lax.fori_loop(0, tl, scatter, 0)


def _slots(counts, eidx, rank, nb):
    k_, t = eidx.shape
    tl = 1024
    smem = lambda: pl.BlockSpec(memory_space=pltpu.SMEM)
    chunk = lambda: pl.BlockSpec((k_, tl), lambda i: (0, i))
    return pl.pallas_call(
        _slots_body,
        grid=(t // tl,),
        in_specs=[smem(), chunk(), chunk()],
        out_specs=[chunk(), smem(), smem(), smem(), smem(), smem()],
        out_shape=[
            jax.ShapeDtypeStruct((k_, t), I32),
            jax.ShapeDtypeStruct((nb * EXPERT_ROWS,), I32),
            jax.ShapeDtypeStruct((nb,), I32),
            jax.ShapeDtypeStruct((nb,), I32),
            jax.ShapeDtypeStruct((nb,), I32),
            jax.ShapeDtypeStruct((1,), I32),
        ],
        scratch_shapes=[
            pltpu.SMEM((counts.shape[0],), I32),
            pltpu.SMEM((k_, tl), I32),
            pltpu.SemaphoreType.DMA,
        ],
        compiler_params=_cparams(("arbitrary",)),
        name="slots",
    )(counts, eidx, rank)


EXPERT_ISSUE_GROUPS = 8
EXPERT_LOOKAHEAD = 2


def _expert_body(be_ref, first_ref, nxt_ref, nu_ref, rows_hbm, x_hbm, wg_hbm, wu_hbm, wd_hbm, y_ref,
                 idx_smem, xbuf, wg_f, wu_f, wd_f, wg_s, wu_s, wd_s, isem, gsem, wsem):
    i = pl.program_id(0)
    n_used = nu_ref[0]
    r_ = EXPERT_ROWS
    tiles = x_hbm.shape[1]
    ns = EXPERT_LOOKAHEAD + 1
    slot = lax.rem(i, ns)
    aslot = lax.rem(i + EXPERT_LOOKAHEAD, ns)

    def idx_copy(blk, s):
        return pltpu.make_async_copy(rows_hbm.at[pl.ds(blk, 1)], idx_smem.at[s], isem.at[s])

    def row_copy(s, r):
        return pltpu.make_async_copy(x_hbm.at[idx_smem[s, 0, r]], xbuf.at[s, pl.ds(r * tiles, tiles)], gsem.at[s])

    def issue_rows(s, g):
        step = r_ // EXPERT_ISSUE_GROUPS
        for r in range(g * step, (g + 1) * step):
            row_copy(s, r).start()

    def wait_rows(s):
        pltpu.make_async_copy(xbuf.at[s], xbuf.at[s], gsem.at[s]).wait()

    def weight_copies(e):
        return (pltpu.make_async_copy(wg_hbm.at[e], wg_f, wsem.at[0]),
                pltpu.make_async_copy(wu_hbm.at[e], wu_f, wsem.at[1]),
                pltpu.make_async_copy(wd_hbm.at[e], wd_f, wsem.at[2]))

    @pl.when(i == 0)
    def _():
        for cp in weight_copies(be_ref[0]):
            cp.start(priority=1)
        for b in range(EXPERT_LOOKAHEAD):
            idx_copy(b, b).start()
        for b in range(EXPERT_LOOKAHEAD):
            idx_copy(b, b).wait()

            def body(r, _, b=b):
                row_copy(b, r).start()
                return 0
            lax.fori_loop(0, r_, body, 0)
        idx_copy(EXPERT_LOOKAHEAD, EXPERT_LOOKAHEAD).start()

    @pl.when(first_ref[i] == 1)
    def _():
        for cp in weight_copies(be_ref[i]):
            cp.wait()
        wg_s[...] = wg_f[...].astype(BF16)
        wu_s[...] = wu_f[...].astype(BF16)
        wd_s[...] = wd_f[...].astype(BF16)

        @pl.when(nxt_ref[i] >= 0)
        def _():
            for cp in weight_copies(nxt_ref[i]):
                cp.start(priority=1)

    @pl.when(i < n_used)
    def _():
        idx_copy(i + EXPERT_LOOKAHEAD, aslot).wait()
        wait_rows(slot)
        lo, hi = _unpack_bf16_pair(_load_row_tiles(xbuf, (slot,), 0, r_, tiles))
        lo = lo.astype(BF16)
        hi = hi.astype(BF16)
        half = lo.shape[-1]
        issue_rows(aslot, 0)
        a = _dot(lo, wg_s[0:half, :])
        issue_rows(aslot, 1)
        a = a + _dot(hi, wg_s[half:, :])
        issue_rows(aslot, 2)
        u = _dot(lo, wu_s[0:half, :])
        issue_rows(aslot, 3)
        u = u + _dot(hi, wu_s[half:, :])
        issue_rows(aslot, 4)
        hid = (_silu(a) * u).astype(BF16)
        y_lo = _dot(hid, wd_s[:, 0:half])
        issue_rows(aslot, 5)
        y_hi = _dot(hid, wd_s[:, half:])
        issue_rows(aslot, 6)
        _store_row_tiles(y_ref, (), _pack_bf16_pair(y_lo, y_hi))
        issue_rows(aslot, 7)

        @pl.when(i + 1 < n_used)
        def _():
            idx_copy(i + EXPERT_LOOKAHEAD + 1, slot).start()

    @pl.when(jnp.logical_and(i >= n_used, i < n_used + EXPERT_LOOKAHEAD))
    def _():
        wait_rows(slot)

    @pl.when(i >= n_used)
    def _():
        y_ref[...] = jnp.zeros_like(y_ref)


def _experts(block_e, first, nxt, n_used, rows, h2p, wg, wu, wd):
    nb = block_e.shape[0]
    t, tiles, _ = h2p.shape
    ne, d, f = wg.shape
    r_ = EXPERT_ROWS
    hbm = lambda: pl.BlockSpec(memory_space=pl.ANY)
    grid_spec = pltpu.PrefetchScalarGridSpec(
        num_scalar_prefetch=4,
        grid=(nb,),
        in_specs=[hbm(), hbm(), hbm(), hbm(), hbm()],
        out_specs=pl.BlockSpec((r_ * tiles, 128), lambda i, *_: (i, 0)),
        scratch_shapes=[
            pltpu.SMEM((EXPERT_LOOKAHEAD + 1, 1, r_), I32),
            pltpu.VMEM((EXPERT_LOOKAHEAD + 1, r_ * tiles, 128), U32),
            pltpu.VMEM((d, f), F32),
            pltpu.VMEM((d, f), F32),
            pltpu.VMEM((f, d), F32),
            pltpu.VMEM((d, f), BF16),
            pltpu.VMEM((d, f), BF16),
            pltpu.VMEM((f, d), BF16),
            pltpu.SemaphoreType.DMA((EXPERT_LOOKAHEAD + 1,)),
            pltpu.SemaphoreType.DMA((EXPERT_LOOKAHEAD + 1,)),
            pltpu.SemaphoreType.DMA((3,)),
        ],
    )
    return pl.pallas_call(
        _expert_body,
        grid_spec=grid_spec,
        out_shape=jax.ShapeDtypeStruct((nb * r_ * tiles, 128), U32),
        compiler_params=_cparams(("arbitrary",)),
        name="experts",
    )(block_e, first, nxt, n_used, rows, h2p, wg, wu, wd)


def _expert_up_body(be_ref, first_ref, nxt_ref, nu_ref, rows_hbm, x_ref, wg_hbm, wu_hbm, hid_ref,
                    idx_smem, xg_a, xg_b, wg_f, wu_f, wg_s, wu_s, isem, wsem):
    i = pl.program_id(0)
    nb = pl.num_programs(0)
    n_used = nu_ref[0]
    r_ = EXPERT_ROWS
    tiles = xg_a.shape[0] // r_
    cur = lax.rem(i, 2)
    nxt = 1 - cur

    def idx_copy(blk, s):
        return pltpu.make_async_copy(rows_hbm.at[pl.ds(blk, 1)], idx_smem.at[s], isem.at[s])

    def weight_copies(e):
        return (pltpu.make_async_copy(wg_hbm.at[e], wg_f, wsem.at[0]),
                pltpu.make_async_copy(wu_hbm.at[e], wu_f, wsem.at[1]))

    def gather(s, dst, lo, hi):
        for r in range(lo, hi):
            tok = idx_smem[s, 0, r]
            dst[pl.ds(r * tiles, tiles), :] = x_ref[pl.ds(pl.multiple_of(tok * tiles, tiles), tiles), :]

    @pl.when(i == 0)
    def _():
        for cp in weight_copies(be_ref[0]):
            cp.start(priority=1)
        idx_copy(0, 0).start()
        idx_copy(0, 0).wait()
        gather(0, xg_a, 0, r_)
        idx_copy(1, 1).start()

    @pl.when(first_ref[i] == 1)
    def _():
        for cp in weight_copies(be_ref[i]):
            cp.wait()
        wg_s[...] = wg_f[...].astype(BF16)
        wu_s[...] = wu_f[...].astype(BF16)

        @pl.when(nxt_ref[i] >= 0)
        def _():
            for cp in weight_copies(nxt_ref[i]):
                cp.start(priority=1)

    def compute(xg_cur, xg_nxt):
        idx_copy(i + 1, nxt).wait()
        step = r_ // 4
        lo, hi = _unpack_bf16_pair(_load_row_tiles(xg_cur, (), 0, r_, tiles))
        lo = lo.astype(BF16)
        hi = hi.astype(BF16)
        half = lo.shape[-1]
        gather(nxt, xg_nxt, 0, step)
        a = _dot(lo, wg_s[0:half, :])
        gather(nxt, xg_nxt, step, 2 * step)
        a = a + _dot(hi, wg_s[half:, :])
        gather(nxt, xg_nxt, 2 * step, 3 * step)
        u = _dot(lo, wu_s[0:half, :])
        gather(nxt, xg_nxt, 3 * step, r_)
        u = u + _dot(hi, wu_s[half:, :])
        hid_ref[...] = (_silu(a) * u).astype(BF16)

        @pl.when(i + 2 < nb)
        def _():
            idx_copy(i + 2, cur).start()

    @pl.when(jnp.logical_and(i < n_used, cur == 0))
    def _():
        compute(xg_a, xg_b)

    @pl.when(jnp.logical_and(i < n_used, cur == 1))
    def _():
        compute(xg_b, xg_a)

    @pl.when(jnp.logical_and(i == n_used, i + 1 < nb))
    def _():
        idx_copy(i + 1, nxt).wait()

    @pl.when(i >= n_used)
    def _():
        hid_ref[...] = jnp.zeros_like(hid_ref)


def _experts_up(block_e, first, nxt, n_used, rows, h2p, wg, wu):
    nb = block_e.shape[0]
    ne, d, f = wg.shape
    r_ = EXPERT_ROWS
    tiles = d // 256
    hbm = lambda: pl.BlockSpec(memory_space=pl.ANY)
    grid_spec = pltpu.PrefetchScalarGridSpec(
        num_scalar_prefetch=4,
        grid=(nb,),
        in_specs=[hbm(), pl.BlockSpec(h2p.shape, lambda i, *_: (0, 0), pipeline_mode=pl.Buffered(1)), hbm(), hbm()],
        out_specs=pl.BlockSpec((r_, f), lambda i, *_: (i, 0)),
        scratch_shapes=[
            pltpu.SMEM((2, 1, r_), I32),
            pltpu.VMEM((r_ * tiles, 128), U32),
            pltpu.VMEM((r_ * tiles, 128), U32),
            pltpu.VMEM((d, f), F32),
            pltpu.VMEM((d, f), F32),
            pltpu.VMEM((d, f), BF16),
            pltpu.VMEM((d, f), BF16),
            pltpu.SemaphoreType.DMA((2,)),
            pltpu.SemaphoreType.DMA((2,)),
        ],
    )
    return pl.pallas_call(
        _expert_up_body,
        grid_spec=grid_spec,
        out_shape=jax.ShapeDtypeStruct((nb * r_, f), BF16),
        compiler_params=_cparams(("arbitrary",)),
        name="experts_up",
    )(block_e, first, nxt, n_used, rows, h2p, wg, wu)


def _expert_down_body(be_ref, nu_ref, hid_ref, wd_ref, y_ref, wd_s):
    i = pl.program_id(0)
    n_used = nu_ref[0]

    @pl.when(jnp.logical_or(i == 0, be_ref[i] != be_ref[jnp.maximum(i - 1, 0)]))
    def _():
        wd_s[...] = wd_ref[0].astype(BF16)

    @pl.when(i < n_used)
    def _():
        half = wd_s.shape[1] // 2
        hid = hid_ref[...]
        y_lo = _dot(hid, wd_s[:, 0:half])
        y_hi = _dot(hid, wd_s[:, half:])
        _store_row_tiles(y_ref, (), _pack_bf16_pair(y_lo, y_hi))

    @pl.when(i >= n_used)
    def _():
        y_ref[...] = jnp.zeros_like(y_ref)


def _experts_down(block_e, n_used, hid, wd):
    nb = block_e.shape[0]
    ne, f, d = wd.shape
    r_ = EXPERT_ROWS
    tiles = d // 256
    grid_spec = pltpu.PrefetchScalarGridSpec(
        num_scalar_prefetch=2,
        grid=(nb,),
        in_specs=[
            pl.BlockSpec((r_, f), lambda i, be, nu: (i, 0)),
            pl.BlockSpec((1, f, d), lambda i, be, nu: (be[i], 0, 0)),
        ],
        out_specs=pl.BlockSpec((r_ * tiles, 128), lambda i, be, nu: (i, 0)),
        scratch_shapes=[pltpu.VMEM((f, d), BF16)],
    )
    return pl.pallas_call(
        _expert_down_body,
        grid_spec=grid_spec,
        out_shape=jax.ShapeDtypeStruct((nb * r_ * tiles, 128), U32),
        compiler_params=_cparams(("arbitrary",)),
        name="experts_down",
    )(block_e, n_used, hid, wd)


def _combine_body(dest_hbm, y_hbm, w_ref, x1_ref, h2_ref, g2_ref, sg_ref, su_ref, sd_ref, fnw_ref, o_ref,
                  idx_smem, ybuf, isem, gsem):
    i = pl.program_id(0)
    n = pl.num_programs(0)
    tm = x1_ref.shape[0]
    tiles = y_hbm.shape[1]
    slot = i % 2
    nslot = 1 - slot

    def idx_copy(blk, s):
        cols = pl.ds(pl.multiple_of(blk * tm, tm), tm)
        return pltpu.make_async_copy(dest_hbm.at[:, cols], idx_smem.at[s], isem.at[s])

    def issue_rows(s):
        def body(j, _):
            for k in range(TOP_K):
                dst = pl.ds(pl.multiple_of((k * tm + j) * tiles, tiles), tiles)
                pltpu.make_async_copy(y_hbm.at[idx_smem[s, k, j]], ybuf.at[s, dst],
                                      gsem.at[s]).start(priority=k % 2)
            return 0
        lax.fori_loop(0, tm, body, 0)

    def wait_rows(s):
        pltpu.make_async_copy(ybuf.at[s], ybuf.at[s], gsem.at[s]).wait()

    @pl.when(i == 0)
    def _():
        idx_copy(0, 0).start()
        idx_copy(0, 0).wait()
        issue_rows(0)

        @pl.when(n > 1)
        def _():
            idx_copy(1, 1).start()

    @pl.when(i + 1 < n)
    def _():
        idx_copy(i + 1, nslot).wait()
        issue_rows(nslot)

    wait_rows(slot)

    @pl.when(i + 2 < n)
    def _():
        idx_copy(i + 2, slot).start()

    w = w_ref[...]
    half = tiles * 128
    r_lo = [jnp.zeros((tm, 128), F32) for _ in range(tiles)]
    r_hi = [jnp.zeros((tm, 128), F32) for _ in range(tiles)]
    for k in range(TOP_K):
        wk = w[:, k:k + 1]
        for kc in range(tiles):
            lo, hi = _unpack_bf16_pair(ybuf[slot, pl.ds(k * tm * tiles + kc, tm, stride=tiles), :])
            r_lo[kc] = r_lo[kc] + wk * lo
            r_hi[kc] = r_hi[kc] + wk * hi
    routed = jnp.concatenate(r_lo + r_hi, axis=-1)

    x_lo, x_hi = _unpack_bf16_pair(_load_row_tiles(h2_ref, (), 0, tm, tiles))
    x_lo = x_lo.astype(BF16)
    x_hi = x_hi.astype(BF16)
    a = _dot(x_lo, sg_ref[0:half, :]) + _dot(x_hi, sg_ref[half:, :])
    u = _dot(x_lo, su_ref[0:half, :]) + _dot(x_hi, su_ref[half:, :])
    shared = _dot((_silu(a) * u).astype(BF16), sd_ref[...])

    x2 = x1_ref[...] + g2_ref[0, 0] * (routed + shared)
    ms = jnp.mean(x2 * x2, axis=-1, keepdims=True)
    o_ref[...] = x2 * lax.rsqrt(ms + EPS) * fnw_ref[...]


def _combine(dest, y_sorted, w_tok, x1, h2p, mod4, sg, su, sd, fnw, tiles_per_batch, tm):
    t, d = x1.shape
    half = d // 2
    nrow = TOP_K * tm
    res = lambda shape: pl.BlockSpec(shape, lambda i: (0,) * len(shape), pipeline_mode=pl.Buffered(1))
    return pl.pallas_call(
        _combine_body,
        grid=(t // tm,),
        in_specs=[
            pl.BlockSpec(memory_space=pl.ANY),
            pl.BlockSpec(memory_space=pl.ANY),
            pl.BlockSpec((tm, TOP_K), lambda i: (i, 0)),
            pl.BlockSpec((tm, d), lambda i: (i, 0)),
            pl.BlockSpec((tm * (half // 128), 128), lambda i: (i, 0)),
            pl.BlockSpec((1, 1, 1, d), lambda i: (i // tiles_per_batch, 5, 0, 0)),
            res(sg.shape), res(su.shape), res(sd.shape), res(fnw.shape),
        ],
        out_specs=pl.BlockSpec((tm, d), lambda i: (i, 0)),
        out_shape=jax.ShapeDtypeStruct((t, d), F32),
        scratch_shapes=[
            pltpu.SMEM((2, TOP_K, tm), I32),
            pltpu.VMEM((2, nrow * (half // 128), 128), U32),
            pltpu.SemaphoreType.DMA((2,)),
            pltpu.SemaphoreType.DMA((2,)),
        ],
        compiler_params=_cparams(("arbitrary",)),
        name="combine",
    )(dest, y_sorted, w_tok, x1, h2p, mod4, sg, su, sd, fnw)


def kernel(x, c, ctx, c_ctx, ada_w, ada_b, norm1_w, norm2_w, w_in, s5_lam_re, s5_lam_im, s5_log_dt,
           s5_b_re, s5_b_im, s5_c_re, s5_c_im, s5_d, s5_glu_w, gla_gk_up, gla_gk_b, gla_norm_w,
           w_s5_proj, w_gla_proj, w_out, router_w, router_bias, exp_w_gate, exp_w_up, exp_w_down,
           sh_w_gate, sh_w_up, sh_w_down, final_norm_w):
    depth = ada_w.shape[0]
    assert depth == 1, "single-layer block: context outputs are never consumed"
    b_, l_, d = x.shape
    lc = ctx.shape[1]
    s5w = s5_d.shape[1]
    kw = gla_gk_up.shape[-1]
    vw = w_gla_proj.shape[1]
    dk, dv = kw // GLA_HEADS, vw // GLA_HEADS
    rank2 = 2 * GLA_GATE_RANK
    li = 0

    c8 = jnp.zeros((8, d), F32).at[:b_].set(c).at[b_].set(c_ctx)
    mod4 = _ada(c8, ada_w[li], ada_b[li]).reshape(8, N_MOD, 1, d)

    cuts = [0, s5w, s5w + kw, s5w + 2 * kw, s5w + 2 * kw + vw, s5w + 2 * kw + 2 * vw]
    c_gkd = cuts[5]
    w = w_in[li]
    w_main = jnp.concatenate([w[:, :c_gkd], w[:, c_gkd + rank2:]], axis=1).astype(BF16)
    w_ctx = jnp.concatenate([w[:, cuts[0]:cuts[1]], w[:, cuts[2]:cuts[4]]], axis=1).astype(BF16)
    w_gkd = jnp.zeros((d, 128), F32).at[:, :rank2].set(w[:, c_gkd:c_gkd + rank2]).astype(BF16)
    nw1 = norm1_w[li].reshape(1, d)
    p_lat, gk_lat = _inproj(x, mod4, lambda b: b, nw1, w_main, w_gkd, tm=512, tn=1024)
    p_ctx, gk_ctx = _inproj(ctx, mod4, lambda b: b_, nw1, w_ctx, w_gkd, tm=lc, tn=w_ctx.shape[1] // 2)

    g_ = s5w // S5_GROUP
    tc = S5_CHUNK
    n_ctx, n_lat = lc // tc, l_ // tc
    nblk = s5w // 128
    u_all = jnp.concatenate([p_ctx[:, :, :s5w], p_lat[:, :, :s5w]], axis=1)
    v = u_all.reshape(b_, n_ctx + n_lat, tc, nblk, 128).transpose(3, 1, 0, 2, 4)
    v = v.reshape(nblk, (n_ctx + n_lat) * b_, tc * 128)
    ws, wq, wout5, acoef, dvec = _s5_factors(s5_lam_re[li], s5_lam_im[li], s5_log_dt[li], s5_b_re[li],
                                             s5_b_im[li], s5_c_re[li], s5_c_im[li], s5_d[li])
    y2 = _s5v2(v, ws, wq, wout5, acoef, dvec, n_ctx, n_lat, b_)
    ys = y2.reshape(nblk, n_lat, b_, tc, 128).transpose(2, 1, 3, 0, 4).reshape(b_, l_, s5w)

    up_pad = jnp.zeros((2, 128, kw), F32)
    up_pad = up_pad.at[0, :GLA_GATE_RANK].set(gla_gk_up[li, 0]).at[1, GLA_GATE_RANK:rank2].set(gla_gk_up[li, 1])
    yg = _gla(p_lat, p_ctx, gk_lat, gk_ctx, up_pad.astype(BF16), gla_gk_b[li].reshape(2, 1, kw), dk, dv)

    rw = jnp.zeros((d, 128), F32).at[:, :N_EXPERTS].set(router_w[li])
    rw_hi = rw.astype(BF16)
    rw2 = jnp.stack([rw_hi, (rw - rw_hi.astype(F32)).astype(BF16)])
    x1, h2p, logits = _merge(
        ys, yg, p_lat, x, mod4, s5_glu_w[li].astype(BF16), w_s5_proj[li].astype(BF16),
        w_gla_proj[li].astype(BF16), w_out[li].astype(BF16), gla_norm_w[li].reshape(1, dv),
        norm2_w[li].reshape(1, d), rw2, dv, tm=256)

    t = b_ * l_
    logits_t = logits.reshape(t, 128)[:, :N_EXPERTS].T
    eidx, wts, rank, cnt = _route(logits_t, router_bias[li])
    nb = (t * TOP_K + N_EXPERTS * (EXPERT_ROWS - 1) + EXPERT_ROWS - 1) // EXPERT_ROWS + EXPERT_LOOKAHEAD
    dest, rows, block_e, first, nxt, n_used = _slots(cnt[:, 0].astype(I32), eidx, rank, nb)

    tiles = d // 256
    hid = _experts_up(block_e, first, nxt, n_used, rows.reshape(nb, EXPERT_ROWS), h2p.reshape(t * tiles, 128),
                      exp_w_gate[li], exp_w_up[li])
    y_sorted = _experts_down(block_e, n_used, hid, exp_w_down[li])

    tm_c = 256
    out = _combine(dest, y_sorted.reshape(-1, tiles, 128), wts.T, x1.reshape(t, d), h2p.reshape(t * tiles, 128), mod4,
                   sh_w_gate[li].astype(BF16), sh_w_up[li].astype(BF16), sh_w_down[li].astype(BF16),
                   final_norm_w.reshape(1, d), l_ // tm_c, tm_c)
    return out.reshape(b_, l_, d)
```

```python
import functools
import math

import jax
import jax.numpy as jnp
from jax import lax
from jax.experimental import pallas as pl
from jax.experimental.pallas import tpu as pltpu

F32 = jnp.float32
BF16 = jnp.bfloat16
U32 = jnp.uint32
I32 = jnp.int32

EPS = 1e-6
N_MOD = 6

S5_GROUP = 16
S5_STATE = 64
S5_CHUNK = 16
S5_GROUPS_PER_STEP = 8

GLA_HEADS = 4
GLA_GATE_RANK = 16
GLA_GATE_NORM = 16.0
GLA_CHUNK = 64

N_EXPERTS = 64
TOP_K = 8
N_GROUPS = 8
TOPK_GROUPS = 4
ROUTED_SCALE = 2.5
EXPERT_ROWS = 256

VMEM_LIMIT = 56 * 1024 * 1024


def _cparams(sem, vmem=VMEM_LIMIT):
    return pltpu.CompilerParams(dimension_semantics=sem, vmem_limit_bytes=vmem)


def _dot(a, b):
    return jnp.dot(a, b, preferred_element_type=F32)


def _sigmoid(x):
    return 1.0 / (1.0 + jnp.exp(-x))


def _silu(x):
    return x * _sigmoid(x)


def _pack_bf16_pair(lo, hi):
    lo_bits = lax.bitcast_convert_type(lo.astype(BF16).astype(F32), U32)
    hi_bits = lax.bitcast_convert_type(hi.astype(BF16).astype(F32), U32)
    return (lo_bits >> 16) | (hi_bits & jnp.uint32(0xFFFF0000))


def _unpack_bf16_pair(w):
    lo = lax.bitcast_convert_type(w << 16, F32)
    hi = lax.bitcast_convert_type(w & jnp.uint32(0xFFFF0000), F32)
    return lo, hi


def _store_row_tiles(ref, lead, val):
    rows, rt = val.shape[0], val.shape[1] // 128
    for kc in range(rt):
        ref[(*lead, pl.ds(kc, rows, stride=rt), slice(None))] = val[:, kc * 128:(kc + 1) * 128]


def _load_row_tiles(ref, lead, row0, rows, rt):
    return jnp.concatenate(
        [ref[(*lead, pl.ds(row0 * rt + kc, rows, stride=rt), slice(None))] for kc in range(rt)], axis=-1)


def _ada_body(c_ref, w_ref, b_ref, o_ref):
    c = c_ref[...]
    s = _silu(c).astype(BF16)
    o_ref[...] = _dot(s, w_ref[...].astype(BF16)) + b_ref[...]


def _ada(c8, ada_w, ada_b):
    d, n = ada_w.shape
    tn = 512
    return pl.pallas_call(
        _ada_body,
        grid=(n // tn,),
        in_specs=[
            pl.BlockSpec((8, d), lambda j: (0, 0)),
            pl.BlockSpec((d, tn), lambda j: (0, j)),
            pl.BlockSpec((1, tn), lambda j: (0, j)),
        ],
        out_specs=pl.BlockSpec((8, tn), lambda j: (0, j)),
        out_shape=jax.ShapeDtypeStruct((8, n), F32),
        compiler_params=_cparams(("parallel",)),
        name="ada",
    )(c8, ada_w, ada_b.reshape(1, n))


def _modulated_norm(x, nw, sh, sc):
    ms = jnp.mean(x * x, axis=-1, keepdims=True)
    y = x * lax.rsqrt(ms + EPS) * nw
    return y * (1.0 + sc) + sh


def _inproj_body(x_ref, sh_ref, sc_ref, nw_ref, w_ref, wg_ref, p_ref, g_ref, v_ref, h_scr, u_scr):
    @pl.when(pl.program_id(2) == 0)
    def _():
        h = _modulated_norm(x_ref[0], nw_ref[...], sh_ref[0, 0], sc_ref[0, 0]).astype(BF16)
        h_scr[...] = h
        g_ref[0] = _dot(h, wg_ref[...])

    r = _dot(h_scr[...], w_ref[...])
    p_ref[0] = r.astype(BF16)

    @pl.when(pl.program_id(2) == 0)
    def _():
        nch = u_scr.shape[1] // S5_CHUNK
        for c in range(u_scr.shape[0]):
            u_scr[c] = r[:, c * 128:(c + 1) * 128]
            for t in range(S5_CHUNK):
                piece = u_scr[c, pl.ds(t, nch, stride=S5_CHUNK), :]
                v_ref[c, 0, :, t * 128:(t + 1) * 128] = piece.astype(BF16)


def _inproj(x, mod4, mod_row, nw, w, wg, tm, tn, s5w):
    b_, l_, d = x.shape
    n = w.shape[1]
    assert tn >= s5w and tm % (16 * S5_CHUNK) == 0
    nblk, vw = s5w // 128, S5_CHUNK * 128
    return pl.pallas_call(
        _inproj_body,
        grid=(b_, l_ // tm, n // tn),
        in_specs=[
            pl.BlockSpec((1, tm, d), lambda b, i, j: (b, i, 0)),
            pl.BlockSpec((1, 1, 1, d), lambda b, i, j: (mod_row(b), 0, 0, 0)),
            pl.BlockSpec((1, 1, 1, d), lambda b, i, j: (mod_row(b), 1, 0, 0)),
            pl.BlockSpec((1, d), lambda b, i, j: (0, 0)),
            pl.BlockSpec((d, tn), lambda b, i, j: (0, j)),
            pl.BlockSpec((d, 128), lambda b, i, j: (0, 0)),
        ],
        out_specs=[
            pl.BlockSpec((1, tm, tn), lambda b, i, j: (b, i, j)),
            pl.BlockSpec((1, tm, 128), lambda b, i, j: (b, i, 0)),
            pl.BlockSpec((nblk, 1, tm // S5_CHUNK, vw), lambda b, i, j: (0, b, i, 0)),
        ],
        out_shape=[
            jax.ShapeDtypeStruct((b_, l_, n), BF16),
            jax.ShapeDtypeStruct((b_, l_, 128), F32),
            jax.ShapeDtypeStruct((nblk, b_, l_ // S5_CHUNK, vw), BF16),
        ],
        scratch_shapes=[pltpu.VMEM((tm, d), BF16), pltpu.VMEM((nblk, tm, 128), F32)],
        compiler_params=_cparams(("parallel", "parallel", "arbitrary")),
        name="inproj",
    )(x, mod4, mod4, nw, w, wg)


def _s5_weights(lam_re, lam_im, log_dt, b_re, b_im, c_re, c_im, d_skip):
    hp = lax.Precision.HIGHEST
    tc = S5_CHUNK
    g_, p_ = lam_re.shape[1:]
    h_ = b_re.shape[-1]
    dt = jnp.exp(log_dt)[..., None]
    ar, ai = lam_re * dt, lam_im * dt
    k = jnp.arange(tc + 1, dtype=F32)
    mag = jnp.exp(ar[..., None] * k)
    pw_re = mag * jnp.cos(ai[..., None] * k)
    pw_im = mag * jnp.sin(ai[..., None] * k)
    num_re = jnp.expm1(ar) * jnp.cos(ai) - 2.0 * jnp.sin(0.5 * ai) ** 2
    num_im = jnp.exp(ar) * jnp.sin(ai)
    den = lam_re * lam_re + lam_im * lam_im
    f_re = (num_re * lam_re + num_im * lam_im) / den
    f_im = (num_im * lam_re - num_re * lam_im) / den
    bb_re = f_re[..., None] * b_re - f_im[..., None] * b_im
    bb_im = f_re[..., None] * b_im + f_im[..., None] * b_re

    cp_re = c_re[..., None] * pw_re[:, :, None] - c_im[..., None] * pw_im[:, :, None]
    cp_im = c_re[..., None] * pw_im[:, :, None] + c_im[..., None] * pw_re[:, :, None]
    kk = (jnp.einsum('dgipt,dgpj->dgtij', cp_re, bb_re, precision=hp)
          - jnp.einsum('dgipt,dgpj->dgtij', cp_im, bb_im, precision=hp))

    s_idx = jnp.arange(tc)[:, None]
    t_idx = jnp.arange(tc)[None, :]
    lag_f = t_idx - s_idx
    lag_b = s_idx - t_idx
    kf = jnp.where((lag_f >= 0)[None, :, :, None, None], kk[0][:, jnp.clip(lag_f, 0, tc)], 0.0)
    kb = jnp.where((lag_b >= 0)[None, :, :, None, None], kk[1][:, jnp.clip(lag_b, 0, tc)], 0.0)
    dsk = d_skip.reshape(g_, h_)
    eye_t = jnp.eye(tc, dtype=F32)[None, :, :, None, None]
    eye_h = jnp.eye(h_, dtype=F32)[None, None, None]
    m_all = kf + kb + eye_t * eye_h * dsk[:, None, None, :, None]
    m_all = m_all.transpose(0, 1, 4, 2, 3).reshape(g_, tc * h_, tc * h_)

    def state_w(dr, expo):
        e_re = pw_re[dr][:, :, expo]
        e_im = pw_im[dr][:, :, expo]
        w_re = jnp.einsum('gps,gpj->gsjp', e_re, bb_re[dr], precision=hp) - jnp.einsum(
            'gps,gpj->gsjp', e_im, bb_im[dr], precision=hp)
        w_im = jnp.einsum('gps,gpj->gsjp', e_re, bb_im[dr], precision=hp) + jnp.einsum(
            'gps,gpj->gsjp', e_im, bb_re[dr], precision=hp)
        w_re = w_re.reshape(g_, tc * h_, p_)
        w_im = w_im.reshape(g_, tc * h_, p_)
        return jnp.concatenate([w_re, w_im, w_im, w_re], axis=-1)

    ws_f = state_w(0, tc - 1 - jnp.arange(tc))
    ws_b = state_w(1, jnp.arange(tc))
    wcat = jnp.concatenate([m_all, ws_f, ws_b], axis=-1)

    def out_w(dr, expo):
        q_re = cp_re[dr][..., expo]
        q_im = cp_im[dr][..., expo]
        top = q_re.transpose(0, 2, 3, 1).reshape(g_, p_, tc * h_)
        bot = (-q_im).transpose(0, 2, 3, 1).reshape(g_, p_, tc * h_)
        return jnp.concatenate([top, bot], axis=1)

    wout = jnp.concatenate([out_w(0, 1 + jnp.arange(tc)), out_w(1, tc - jnp.arange(tc))], axis=1)

    a_re, a_im = pw_re[..., tc], pw_im[..., tc]
    zeros = jnp.zeros_like(a_re[0])
    rows = []
    for dr in range(2):
        rows += [jnp.concatenate([a_re[dr], a_re[dr]], -1),
                 jnp.concatenate([-a_im[dr], a_im[dr]], -1),
                 jnp.concatenate([a_im[dr], -a_im[dr]], -1)]
    rows += [jnp.concatenate([zeros, zeros], -1)] * 2
    acoef = jnp.stack(rows, axis=1)
    return wcat.astype(BF16), wout.astype(BF16), acoef


def _s5_body(x_ref, wcat_ref, wout_ref, a_ref, y_ref, y_scr, s_scr, h_scr, *, n_ctx, n_lat, nb):
    gb = x_ref.shape[0]
    pw = 2 * S5_STATE
    for g in range(gb):
        r = _dot(x_ref[g], wcat_ref[g])
        y_scr[g] = r[:, : y_scr.shape[-1]]
        s_scr[g] = r[:, y_scr.shape[-1]:]

    a1f, a2f, a2sf = a_ref[:, 0:1, :], a_ref[:, 1:2, :], a_ref[:, 2:3, :]
    a1b, a2b, a2sb = a_ref[:, 3:4, :], a_ref[:, 4:5, :], a_ref[:, 5:6, :]

    first = lax.broadcasted_iota(I32, (gb, 2 * nb, pw), 1) < nb

    def both(v):
        r = pltpu.roll(v, nb, 1)
        return jnp.where(first, v, r), jnp.where(first, r, v)

    def step(pf, pb, carry):
        hf, hfs, hb, hbs = carry
        rf = pl.ds(pl.multiple_of(pf * 2 * nb, 2 * nb), 2 * nb)
        rb = pl.ds(pl.multiple_of(pb * 2 * nb, 2 * nb), 2 * nb)
        sfa, sfb = both(s_scr[:, rf, 0:pw])
        sfsa, sfsb = both(s_scr[:, rf, pw:2 * pw])
        sba, sbb = both(s_scr[:, rb, 2 * pw:3 * pw])
        sbsa, sbsb = both(s_scr[:, rb, 3 * pw:4 * pw])
        hf1 = a1f * hf + a2f * hfs + sfa
        hfs1 = a1f * hfs + a2sf * hf + sfsa
        h_scr[:, rf, 0:pw] = jnp.where(first, hf, hf1)
        hf2 = a1f * hf1 + a2f * hfs1 + sfb
        hfs2 = a1f * hfs1 + a2sf * hf1 + sfsb
        hb1 = a1b * hb + a2b * hbs + sbb
        hbs1 = a1b * hbs + a2sb * hb + sbsb
        h_scr[:, rb, pw:2 * pw] = jnp.where(first, hb1, hb)
        hb2 = a1b * hb1 + a2b * hbs1 + sba
        hbs2 = a1b * hbs1 + a2sb * hb1 + sbsa
        return hf2, hfs2, hb2, hbs2

    z = jnp.zeros((gb, 2 * nb, pw), F32)
    carry = (z, z, z, z)
    pc, pl_ = n_ctx // 2, n_lat // 2
    carry = lax.fori_loop(0, pc, lambda i, c: step(i, pc - 1 - i, c), carry)
    carry = lax.fori_loop(0, pl_, lambda i, c: step(pc + i, pc + pl_ - 1 - i, c), carry)

    lo = n_ctx * nb
    for g in range(gb):
        y = y_scr[g] + _dot(h_scr[g].astype(BF16), wout_ref[g])
        y_ref[g] = y[lo:, :]


def _s5(x2, wcat, wout, acoef, n_ctx, n_lat, nb):
    g_, rows, kdim = x2.shape
    gb = S5_GROUPS_PER_STEP
    pw = 2 * S5_STATE
    out_rows = n_lat * nb
    return pl.pallas_call(
        functools.partial(_s5_body, n_ctx=n_ctx, n_lat=n_lat, nb=nb),
        grid=(g_ // gb,),
        in_specs=[
            pl.BlockSpec((gb, rows, kdim), lambda i: (i, 0, 0)),
            pl.BlockSpec((gb, kdim, wcat.shape[-1]), lambda i: (i, 0, 0)),
            pl.BlockSpec((gb, 2 * pw, kdim), lambda i: (i, 0, 0)),
            pl.BlockSpec((gb, 8, pw), lambda i: (i, 0, 0)),
        ],
        out_specs=pl.BlockSpec((gb, out_rows, kdim), lambda i: (i, 0, 0)),
        out_shape=jax.ShapeDtypeStruct((g_, out_rows, kdim), F32),
        scratch_shapes=[
            pltpu.VMEM((gb, rows, kdim), F32),
            pltpu.VMEM((gb, rows, 4 * pw), F32),
            pltpu.VMEM((gb, rows, 2 * pw), F32),
        ],
        compiler_params=_cparams(("parallel",)),
        name="s5",
    )(x2, wcat, wout, acoef)


def _s5_factors(lam_re, lam_im, log_dt, b_re, b_im, c_re, c_im, d_skip):
    tc = S5_CHUNK
    g_, p_ = lam_re.shape[1:]
    h_ = b_re.shape[-1]
    dt = jnp.exp(log_dt)[..., None]
    ar, ai = lam_re * dt, lam_im * dt

    def powers(k):
        k = jnp.asarray(k, F32)
        mag = jnp.exp(ar[..., None] * k)
        return mag * jnp.cos(ai[..., None] * k), mag * jnp.sin(ai[..., None] * k)

    num_re = jnp.expm1(ar) * jnp.cos(ai) - 2.0 * jnp.sin(0.5 * ai) ** 2
    num_im = jnp.exp(ar) * jnp.sin(ai)
    den = lam_re * lam_re + lam_im * lam_im
    f_re = (num_re * lam_re + num_im * lam_im) / den
    f_im = (num_im * lam_re - num_re * lam_im) / den
    bb_re = f_re[..., None] * b_re - f_im[..., None] * b_im
    bb_im = f_re[..., None] * b_im + f_im[..., None] * b_re
    t_idx = jnp.arange(tc)

    def state_w(dr, expo):
        e_re, e_im = powers(expo)
        e_re = e_re[dr].transpose(0, 2, 1)[:, :, None, :]
        e_im = e_im[dr].transpose(0, 2, 1)[:, :, None, :]
        br = bb_re[dr].transpose(0, 2, 1)[:, None, :, :]
        bi = bb_im[dr].transpose(0, 2, 1)[:, None, :, :]
        w_re = (e_re * br - e_im * bi).reshape(g_, tc * h_, p_)
        w_im = (e_re * bi + e_im * br).reshape(g_, tc * h_, p_)
        return jnp.concatenate([w_re, w_im], axis=-1)

    def out_w(dr, expo):
        e_re, e_im = powers(expo)
        e_re = e_re[dr][:, :, :, None]
        e_im = e_im[dr][:, :, :, None]
        cr = c_re[dr].transpose(0, 2, 1)[:, :, None, :]
        ci = c_im[dr].transpose(0, 2, 1)[:, :, None, :]
        q_re = (cr * e_re - ci * e_im).reshape(g_, p_, tc * h_)
        q_im = (cr * e_im + ci * e_re).reshape(g_, p_, tc * h_)
        return jnp.concatenate([q_re, -q_im], axis=1)

    ws = jnp.stack([state_w(0, tc - 1 - t_idx), state_w(1, t_idx)], axis=1)
    wq = jnp.stack([out_w(0, t_idx - (tc - 1)), out_w(1, -t_idx)], axis=1)
    wout = jnp.concatenate([out_w(0, t_idx + 1), out_w(1, tc - t_idx)], axis=1).astype(BF16)

    a_re, a_im = powers(jnp.array([tc]))
    a_re, a_im = a_re[..., 0], a_im[..., 0]
    zeros = jnp.zeros_like(a_re[0])
    rows = []
    for dr in range(2):
        rows += [jnp.concatenate([a_re[dr], a_re[dr]], -1),
                 jnp.concatenate([-a_im[dr], a_im[dr]], -1),
                 jnp.concatenate([a_im[dr], -a_im[dr]], -1)]
    rows += [jnp.concatenate([zeros, zeros], -1)] * 2
    acoef = jnp.stack(rows, axis=1)
    dvec = jnp.tile(d_skip.reshape(g_, h_), (1, tc)).reshape(g_, 1, tc * h_)
    return ws, wq, wout, acoef, dvec


def _split_bf16(x):
    hi = x.astype(BF16)
    return hi, (x - hi.astype(F32)).astype(BF16)


def _s5v2_body(vc_ref, vl_ref, ws_ref, wq_ref, wout_ref, a_ref, d_ref, o_ref, y_scr, s_scr, h_scr,
               *, n_ctx, n_lat, nb):
    gb = ws_ref.shape[0]
    pw = 2 * S5_STATE
    gl = S5_GROUP
    kd = S5_CHUNK * gl
    lanes = 128
    n_tot = n_ctx + n_lat
    v = jnp.concatenate([p for b in range(nb) for p in (vc_ref[0, b], vl_ref[0, b])], axis=0)
    width = v.shape[1]

    r_in = lax.broadcasted_iota(I32, (width, kd), 0)
    c_in = lax.broadcasted_iota(I32, (width, kd), 1)
    tgt_in = (r_in // lanes) * gl + (r_in % gl)
    grp_in = (r_in % lanes) // gl
    r_out = lax.broadcasted_iota(I32, (kd, width), 0)
    c_out = lax.broadcasted_iota(I32, (kd, width), 1)
    tgt_out = (c_out // lanes) * gl + (c_out % gl)
    grp_out = (c_out % lanes) // gl

    rs = lax.broadcasted_iota(I32, (kd, kd), 0)
    ct = lax.broadcasted_iota(I32, (kd, kd), 1)
    fmask = ct // gl >= rs // gl
    bmask = ct // gl <= rs // gl
    diag = rs == ct

    for q in range(gb):
        sel = jnp.where(jnp.logical_and(c_in == tgt_in, grp_in == q), 1.0, 0.0).astype(BF16)
        x2 = _dot(v, sel).astype(BF16)
        mm = []
        for dr in range(2):
            a_hi, a_lo = _split_bf16(ws_ref[q, dr])
            b_hi, b_lo = _split_bf16(wq_ref[q, dr])
            mm.append(_dot(a_hi, b_hi) + _dot(a_hi, b_lo) + _dot(a_lo, b_hi))
        m = jnp.where(fmask, mm[0], 0.0) + jnp.where(bmask, mm[1], 0.0) + jnp.where(diag, d_ref[q], 0.0)
        wsf, wsb = ws_ref[q, 0], ws_ref[q, 1]
        wcat = jnp.concatenate(
            [m, wsf, pltpu.roll(wsf, S5_STATE, 1), wsb, pltpu.roll(wsb, S5_STATE, 1)], axis=1).astype(BF16)
        r = _dot(x2, wcat)
        for b in range(nb):
            rb = r[b * n_tot:(b + 1) * n_tot, :]
            for j in range(kd // lanes):
                y_scr[q, j, pl.ds(b, n_tot, stride=nb), :] = rb[:, j * lanes:(j + 1) * lanes]
            for j in range(4):
                s_scr[q, j, pl.ds(b, n_tot, stride=nb), :] = rb[:, kd + j * pw:kd + (j + 1) * pw]

    a1f, a2f, a2sf = a_ref[:, 0:1, :], a_ref[:, 1:2, :], a_ref[:, 2:3, :]
    a1b, a2b, a2sb = a_ref[:, 3:4, :], a_ref[:, 4:5, :], a_ref[:, 5:6, :]
    first = lax.broadcasted_iota(I32, (gb, 2 * nb, pw), 1) < nb

    def both(x):
        rr = pltpu.roll(x, nb, 1)
        return jnp.where(first, x, rr), jnp.where(first, rr, x)

    def step(pf, pb, carry):
        hf, hfs, hb, hbs = carry
        rf = pl.ds(pl.multiple_of(pf * 2 * nb, 2 * nb), 2 * nb)
        rb = pl.ds(pl.multiple_of(pb * 2 * nb, 2 * nb), 2 * nb)
        sfa, sfb = both(s_scr[:, 0, rf, :])
        sfsa, sfsb = both(s_scr[:, 1, rf, :])
        sba, sbb = both(s_scr[:, 2, rb, :])
        sbsa, sbsb = both(s_scr[:, 3, rb, :])
        hf1 = a1f * hf + a2f * hfs + sfa
        hfs1 = a1f * hfs + a2sf * hf + sfsa
        h_scr[:, 0, rf, :] = jnp.where(first, hf, hf1)
        hf2 = a1f * hf1 + a2f * hfs1 + sfb
        hfs2 = a1f * hfs1 + a2sf * hf1 + sfsb
        hb1 = a1b * hb + a2b * hbs + sbb
        hbs1 = a1b * hbs + a2sb * hb + sbsb
        h_scr[:, 1, rb, :] = jnp.where(first, hb1, hb)
        hb2 = a1b * hb1 + a2b * hbs1 + sba
        hbs2 = a1b * hbs1 + a2sb * hb1 + sbsa
        return hf2, hfs2, hb2, hbs2

    z = jnp.zeros((gb, 2 * nb, pw), F32)
    carry = (z, z, z, z)
    pc, pl_ = n_ctx // 2, n_lat // 2
    carry = lax.fori_loop(0, pc, lambda i, c: step(i, pc - 1 - i, c), carry)
    carry = lax.fori_loop(0, pl_, lambda i, c: step(pc + i, pc + pl_ - 1 - i, c), carry)

    lo = n_ctx * nb
    for q in range(gb):
        h = jnp.concatenate([h_scr[q, 0], h_scr[q, 1]], axis=-1).astype(BF16)
        yq = _dot(h, wout_ref[q])
        for j in range(kd // lanes):
            y_scr[q, j] = y_scr[q, j] + yq[:, j * lanes:(j + 1) * lanes]
        y = jnp.concatenate(
            [jnp.concatenate([y_scr[q, j, pl.ds(lo + b, n_lat, stride=nb), :] for j in range(kd // lanes)], axis=-1)
             for b in range(nb)], axis=0)
        selt = jnp.where(jnp.logical_and(r_out == tgt_out, grp_out == q), 1.0, 0.0).astype(BF16)
        placed = _dot(y.astype(BF16), selt)
        if q == 0:
            o_ref[0] = placed
        else:
            o_ref[0] = o_ref[0] + placed


def _s5v2(v_ctx, v_lat, ws, wq, wout, acoef, dvec):
    nblk, nb, n_ctx, width = v_ctx.shape
    n_lat = v_lat.shape[2]
    rows = (n_ctx + n_lat) * nb
    g_ = ws.shape[0]
    gb = g_ // nblk
    pw = 2 * S5_STATE
    kd = S5_CHUNK * S5_GROUP
    out_rows = n_lat * nb
    return pl.pallas_call(
        functools.partial(_s5v2_body, n_ctx=n_ctx, n_lat=n_lat, nb=nb),
        grid=(nblk,),
        in_specs=[
            pl.BlockSpec((1, nb, n_ctx, width), lambda i: (i, 0, 0, 0)),
            pl.BlockSpec((1, nb, n_lat, width), lambda i: (i, 0, 0, 0)),
            pl.BlockSpec((gb, 2, kd, pw), lambda i: (i, 0, 0, 0)),
            pl.BlockSpec((gb, 2, pw, kd), lambda i: (i, 0, 0, 0)),
            pl.BlockSpec((gb, 2 * pw, kd), lambda i: (i, 0, 0)),
            pl.BlockSpec((gb, 8, pw), lambda i: (i, 0, 0)),
            pl.BlockSpec((gb, 1, kd), lambda i: (i, 0, 0)),
        ],
        out_specs=pl.BlockSpec((1, out_rows, width), lambda i: (i, 0, 0)),
        out_shape=jax.ShapeDtypeStruct((nblk, out_rows, width), F32),
        scratch_shapes=[
            pltpu.VMEM((gb, kd // 128, rows, 128), F32),
            pltpu.VMEM((gb, 4, rows, pw), F32),
            pltpu.VMEM((gb, 2, rows, pw), F32),
        ],
        compiler_params=_cparams(("parallel",)),
        name="s5",
    )(v_ctx, v_lat, ws, wq, wout, acoef, dvec)


def _log_sigmoid(z):
    return jnp.minimum(z, 0.0) - jnp.log1p(jnp.exp(-jnp.abs(z)))


def _chunk_cumsum(g, pos, reverse):
    n = g.shape[0]
    x = g
    sh = 1
    while sh < GLA_CHUNK:
        if reverse:
            x = x + jnp.where(pos < GLA_CHUNK - sh, pltpu.roll(x, n - sh, 0), 0.0)
        else:
            x = x + jnp.where(pos >= sh, pltpu.roll(x, sh, 0), 0.0)
        sh *= 2
    return x


def _gla_body(q_ref, k_ref, v_ref, kc_ref, vc_ref, gk_ref, gkc_ref, up_ref, bias_ref, o_ref,
              bc_scr, kb_scr, qb_scr, st_ref, *, scale):
    c = GLA_CHUNK
    l_ = q_ref.shape[1]
    lc = kc_ref.shape[1]
    n_lat, n_ctx = l_ // c, lc // c
    row = lax.broadcasted_iota(I32, (c, c), 0)
    col = lax.broadcasted_iota(I32, (c, c), 1)
    nt = (((1,), (1,)), ((), ()))
    tn = (((0,), (0,)), ((), ()))

    for dr in range(2):
        smask = (col <= row) if dr == 0 else (col > row)
        up = up_ref[dr]
        bias = bias_ref[dr]
        edge_tile, edge_row = (c - 8, 7) if dr == 0 else (0, 0)

        for gk_r, k_r, base, n, latent in ((gkc_ref, kc_ref, 0, lc, False), (gk_ref, k_ref, lc, l_, True)):
            z = _dot(gk_r[0].astype(BF16), up) + bias
            g = _log_sigmoid(z) * (1.0 / GLA_GATE_NORM)
            pos = lax.broadcasted_iota(I32, g.shape, 0) & (c - 1)
            bc = _chunk_cumsum(g, pos, reverse=(dr == 1))
            bc_scr[base:base + n, :] = bc
            kb_scr[base:base + n, :] = k_r[0].astype(F32) * jnp.exp(-bc)
            if latent:
                qb_scr[...] = (q_ref[0].astype(F32) * (scale * jnp.exp(bc))).astype(BF16)

        def chunk_rows(n, base):
            r0 = pl.multiple_of(n * c, c)
            tot = bc_scr[pl.ds(base + r0 + edge_tile, 8), :][edge_row:edge_row + 1, :]
            return r0, pl.ds(base + r0, c), jnp.exp(tot)

        def update_state(kb, vv, decay):
            kd = (kb * decay).astype(BF16)
            ds_t = lax.dot_general(vv, kd, tn, preferred_element_type=F32)
            st_ref[...] = decay * st_ref[...] + ds_t

        def ctx_step(i, _):
            n = i if dr == 0 else n_ctx - 1 - i
            r0, rs, decay = chunk_rows(n, 0)
            update_state(kb_scr[rs, :], vc_ref[0, pl.ds(r0, c), :], decay)
            return 0

        def lat_step(i, _):
            n = i if dr == 0 else n_lat - 1 - i
            r0, rs, decay = chunk_rows(n, lc)
            ro = pl.ds(r0, c)
            kb = kb_scr[rs, :]
            vv = v_ref[0, ro, :]
            qb = qb_scr[ro, :]
            scores = lax.dot_general(qb, kb.astype(BF16), nt, preferred_element_type=F32)
            scores = jnp.where(smask, scores, 0.0).astype(BF16)
            o = _dot(scores, vv) + lax.dot_general(qb, st_ref[...].astype(BF16), nt,
                                                   preferred_element_type=F32)
            if dr == 0:
                o_ref[0, ro, :] = o
            else:
                o_ref[0, ro, :] = o_ref[0, ro, :] + o
            update_state(kb, vv, decay)
            return 0

        st_ref[...] = jnp.zeros_like(st_ref)
        lax.fori_loop(0, n_ctx, ctx_step, 0, unroll=2)
        lax.fori_loop(0, n_lat, lat_step, 0, unroll=4)


def _gla(p_lat, p_ctx, gk_lat, gk_ctx, up_pad, bias, dk, dv):
    b_, l_, _ = p_lat.shape
    lc = p_ctx.shape[1]
    hh = GLA_HEADS
    s5w = hh * dv
    q0, k0, v0 = s5w // dk, (s5w + hh * dk) // dk, (s5w + 2 * hh * dk) // dv
    kc0, vc0 = s5w // dk, (s5w + hh * dk) // dv
    return pl.pallas_call(
        functools.partial(_gla_body, scale=dk ** -0.5),
        grid=(b_, hh),
        in_specs=[
            pl.BlockSpec((1, l_, dk), lambda b, h: (b, 0, q0 + h)),
            pl.BlockSpec((1, l_, dk), lambda b, h: (b, 0, k0 + h)),
            pl.BlockSpec((1, l_, dv), lambda b, h: (b, 0, v0 + h)),
            pl.BlockSpec((1, lc, dk), lambda b, h: (b, 0, kc0 + h)),
            pl.BlockSpec((1, lc, dv), lambda b, h: (b, 0, vc0 + h)),
            pl.BlockSpec((1, l_, 128), lambda b, h: (b, 0, 0)),
            pl.BlockSpec((1, lc, 128), lambda b, h: (b, 0, 0)),
            pl.BlockSpec((2, 128, dk), lambda b, h: (0, 0, h)),
            pl.BlockSpec((2, 1, dk), lambda b, h: (0, 0, h)),
        ],
        out_specs=pl.BlockSpec((1, l_, dv), lambda b, h: (b, 0, h)),
        out_shape=jax.ShapeDtypeStruct((b_, l_, hh * dv), F32),
        scratch_shapes=[
            pltpu.VMEM((lc + l_, dk), F32),
            pltpu.VMEM((lc + l_, dk), F32),
            pltpu.VMEM((l_, dk), BF16),
            pltpu.VMEM((dv, dk), F32),
        ],
        compiler_params=_cparams(("parallel", "parallel")),
        name="gla",
    )(p_lat, p_lat, p_lat, p_ctx, p_ctx, gk_lat, gk_ctx, up_pad, bias)


def _gelu_tanh(x):
    return 0.5 * x * (1.0 + jnp.tanh(math.sqrt(2.0 / math.pi) * (x + 0.044715 * (x * x * x))))


def _merge_body(ys_ref, yg_ref, go_ref, gs_ref, gg_ref, x_ref, g1_ref, sh2_ref, sc2_ref,
                glu_ref, ws5_ref, wgla_ref, wout_ref, gnw_ref, n2w_ref, rw_ref,
                x1_ref, h2_ref, lg_ref, ys_scr, *, dv):
    nch = ys_ref.shape[1]
    for c in range(ys_ref.shape[0]):
        for t in range(S5_CHUNK):
            ys_scr[c, pl.ds(t, nch, stride=S5_CHUNK), :] = ys_ref[c, :, t * 128:(t + 1) * 128]
    a = _gelu_tanh(jnp.concatenate([ys_scr[c] for c in range(ys_ref.shape[0])], axis=-1))
    a = a * _sigmoid(_dot(a.astype(BF16), glu_ref[...]))
    pa = _dot(a.astype(BF16), ws5_ref[...])

    yg = yg_ref[0]
    parts = []
    for h in range(GLA_HEADS):
        oh = yg[:, h * dv:(h + 1) * dv]
        ms = jnp.mean(oh * oh, axis=-1, keepdims=True)
        parts.append(oh * lax.rsqrt(ms + EPS) * gnw_ref[...])
    gl = jnp.concatenate(parts, axis=-1) * _silu(go_ref[0].astype(F32))
    pb = _dot(gl.astype(BF16), wgla_ref[...])

    m = _sigmoid(gs_ref[0].astype(F32)) * pa + _sigmoid(gg_ref[0].astype(F32)) * pb
    y = _dot(m.astype(BF16), wout_ref[...])
    x1 = x_ref[0] + g1_ref[0, 0] * y
    x1_ref[0] = x1

    h2 = _modulated_norm(x1, n2w_ref[...], sh2_ref[0, 0], sc2_ref[0, 0])
    h_hi = h2.astype(BF16)
    h_lo = (h2 - h_hi.astype(F32)).astype(BF16)
    half = h2.shape[-1] // 2
    _store_row_tiles(h2_ref, (0,), _pack_bf16_pair(h2[:, :half], h2[:, half:]))
    lg_ref[0] = _dot(h_hi, rw_ref[0]) + _dot(h_lo, rw_ref[0]) + _dot(h_hi, rw_ref[1])


def _resident(shape):
    nd = len(shape)
    return pl.BlockSpec(shape, lambda b, i: (0,) * nd, pipeline_mode=pl.Buffered(1))


def _merge(ys, yg, p_lat, x, mod4, glu_w, ws5, wgla, wout, gnw, n2w, rw2, dv, tm):
    b_, l_, d = x.shape
    nblk, _, vw = ys.shape
    w5 = nblk * 128
    nch = tm // S5_CHUNK
    go0 = (p_lat.shape[-1] - 2 * d - w5) // w5
    gs0 = (p_lat.shape[-1] - 2 * d) // d
    modspec = lambda r: pl.BlockSpec((1, 1, 1, d), lambda b, i: (b, r, 0, 0))
    return pl.pallas_call(
        functools.partial(_merge_body, dv=dv),
        grid=(b_, l_ // tm),
        in_specs=[
            pl.BlockSpec((nblk, nch, vw), lambda b, i: (0, b * (l_ // tm) + i, 0)),
            pl.BlockSpec((1, tm, w5), lambda b, i: (b, i, 0)),
            pl.BlockSpec((1, tm, w5), lambda b, i: (b, i, go0)),
            pl.BlockSpec((1, tm, d), lambda b, i: (b, i, gs0)),
            pl.BlockSpec((1, tm, d), lambda b, i: (b, i, gs0 + 1)),
            pl.BlockSpec((1, tm, d), lambda b, i: (b, i, 0)),
            modspec(2), modspec(3), modspec(4),
            _resident(glu_w.shape), _resident(ws5.shape), _resident(wgla.shape), _resident(wout.shape),
            _resident(gnw.shape), _resident(n2w.shape), _resident(rw2.shape),
        ],
        out_specs=[
            pl.BlockSpec((1, tm, d), lambda b, i: (b, i, 0)),
            pl.BlockSpec((1, tm * (d // 256), 128), lambda b, i: (b, i, 0)),
            pl.BlockSpec((1, tm, 128), lambda b, i: (b, i, 0)),
        ],
        out_shape=[
            jax.ShapeDtypeStruct((b_, l_, d), F32),
            jax.ShapeDtypeStruct((b_, l_ * (d // 256), 128), U32),
            jax.ShapeDtypeStruct((b_, l_, 128), F32),
        ],
        scratch_shapes=[pltpu.VMEM((nblk, tm, 128), F32)],
        compiler_params=_cparams(("parallel", "parallel")),
        name="merge",
    )(ys, yg, p_lat, p_lat, p_lat, x, mod4, mod4, mod4, glu_w, ws5, wgla, wout, gnw, n2w, rw2)


def _route_body(lg_ref, bias_ref, eidx_ref, w_ref, rank_ref, cnt_ref, carry_ref):
    ne, tt = lg_ref.shape
    gsz = ne // N_GROUPS
    neg = -jnp.inf

    @pl.when(pl.program_id(0) == 0)
    def _():
        carry_ref[...] = jnp.zeros_like(carry_ref)

    sc = _sigmoid(lg_ref[...])
    ch = sc + bias_ref[...]

    def first_max(v, idx, big):
        m = jnp.max(v, axis=0, keepdims=True)
        i = jnp.min(jnp.where(v == m, idx, big), axis=0, keepdims=True)
        return m, i

    midx = lax.broadcasted_iota(I32, (gsz, tt), 0).astype(F32)
    gs_rows = []
    for g in range(N_GROUPS):
        v = ch[g * gsz:(g + 1) * gsz, :]
        m1, i1 = first_max(v, midx, float(gsz))
        m2 = jnp.max(jnp.where(midx == i1, neg, v), axis=0, keepdims=True)
        gs_rows.append(m1 + m2)
    gscore = jnp.concatenate(gs_rows, axis=0)

    gidx = lax.broadcasted_iota(I32, (N_GROUPS, tt), 0).astype(F32)
    gsel = jnp.zeros((N_GROUPS, tt), F32)
    cur = gscore
    for _ in range(TOPK_GROUPS):
        _, i = first_max(cur, gidx, float(N_GROUPS))
        hit = gidx == i
        gsel = jnp.where(hit, 1.0, gsel)
        cur = jnp.where(hit, neg, cur)

    masked = jnp.concatenate(
        [jnp.where(gsel[g:g + 1, :] > 0.0, ch[g * gsz:(g + 1) * gsz, :], neg) for g in range(N_GROUPS)], axis=0)

    eiota = lax.broadcasted_iota(I32, (ne, tt), 0).astype(F32)
    sel = jnp.zeros((ne, tt), F32)
    cur = masked
    idx_rows, s_rows = [], []
    for _ in range(TOP_K):
        _, i = first_max(cur, eiota, float(ne))
        hit = eiota == i
        idx_rows.append(i)
        s_rows.append(jnp.sum(jnp.where(hit, sc, 0.0), axis=0, keepdims=True))
        sel = jnp.where(hit, 1.0, sel)
        cur = jnp.where(hit, neg, cur)
    idx = jnp.concatenate(idx_rows, axis=0)
    s = jnp.concatenate(s_rows, axis=0)
    w_ref[...] = s / jnp.sum(s, axis=0, keepdims=True) * ROUTED_SCALE
    eidx_ref[...] = idx.astype(I32)

    cw = 256
    a_i = lax.broadcasted_iota(I32, (cw, cw), 0)
    b_i = lax.broadcasted_iota(I32, (cw, cw), 1)
    upper = jnp.where(a_i < b_i, 1.0, 0.0).astype(BF16)
    carry = carry_ref[:, 0:1]
    pieces = []
    for c0 in range(0, tt, cw):
        sc_c = sel[:, c0:c0 + cw]
        pieces.append(_dot(sc_c.astype(BF16), upper) + carry)
        carry = carry + jnp.sum(sc_c, axis=1, keepdims=True)
    rank_full = jnp.concatenate(pieces, axis=1)
    carry_ref[...] = jnp.broadcast_to(carry, carry_ref.shape)
    cnt_ref[...] = jnp.broadcast_to(carry, cnt_ref.shape)
    rk = [jnp.sum(jnp.where(eiota == idx_rows[k], rank_full, 0.0), axis=0, keepdims=True) for k in range(TOP_K)]
    rank_ref[...] = jnp.concatenate(rk, axis=0).astype(I32)


def _route(logits_t, bias):
    ne, t = logits_t.shape
    tt = 1024
    return pl.pallas_call(
        _route_body,
        grid=(t // tt,),
        in_specs=[
            pl.BlockSpec((ne, tt), lambda i: (0, i)),
            pl.BlockSpec((ne, 1), lambda i: (0, 0)),
        ],
        out_specs=[
            pl.BlockSpec((TOP_K, tt), lambda i: (0, i)),
            pl.BlockSpec((TOP_K, tt), lambda i: (0, i)),
            pl.BlockSpec((TOP_K, tt), lambda i: (0, i)),
            pl.BlockSpec((ne, 128), lambda i: (0, 0)),
        ],
        out_shape=[
            jax.ShapeDtypeStruct((TOP_K, t), I32),
            jax.ShapeDtypeStruct((TOP_K, t), F32),
            jax.ShapeDtypeStruct((TOP_K, t), I32),
            jax.ShapeDtypeStruct((ne, 128), F32),
        ],
        scratch_shapes=[pltpu.VMEM((ne, 128), F32)],
        compiler_params=_cparams(("arbitrary",)),
        name="route",
    )(logits_t, bias.reshape(ne, 1))


def _slots_body(cnt_ref, eidx_ref, rank_ref, dest_ref, rows_ref, be_ref, first_ref, nxt_ref, nu_ref,
                off_smem, dchunk, sem):
    ne = cnt_ref.shape[0]
    nb = be_ref.shape[0]
    tl = eidx_ref.shape[1]
    r_ = EXPERT_ROWS
    shift = r_.bit_length() - 1
    c = pl.program_id(0)

    @pl.when(c == 0)
    def _():
        def per_expert(e, off):
            cnt = cnt_ref[e]
            n = lax.shift_right_logical(cnt + (r_ - 1), shift)
            off_smem[e] = off

            def fill(j, _):
                be_ref[off + j] = e
                first_ref[off + j] = jnp.where(j == 0, 1, 0)
                return 0
            lax.fori_loop(0, n, fill, 0)

            def pad(j, _):
                rows_ref[off * r_ + j] = 0
                return 0
            lax.fori_loop(cnt, n * r_, pad, 0)
            return off + n

        n_used = lax.fori_loop(0, ne, per_expert, 0)
        nu_ref[0] = n_used
        last_e = be_ref[n_used - 1]

        def tail(j, _):
            be_ref[j] = last_e
            first_ref[j] = 0

            def zero_row(q, _):
                rows_ref[j * r_ + q] = 0
                return 0
            lax.fori_loop(0, r_, zero_row, 0)
            return 0
        lax.fori_loop(n_used, nb, tail, 0)

        def back(i, nx):
            j = nb - 1 - i
            nxt_ref[j] = nx
            return jnp.where(first_ref[j] == 1, be_ref[j], nx)
        lax.fori_loop(0, nb, back, -1)

    ei = eidx_ref[...]
    acc = rank_ref[...]
    for e in range(ne):
        acc = acc + jnp.where(ei == e, off_smem[e] * r_, 0)
    dest_ref[...] = acc
    cp = pltpu.make_async_copy(dest_ref, dchunk, sem)
    cp.start()
    cp.wait()
    base = c * tl

    def scatter(j, _):
        for k in range(TOP_K):
            rows_ref[dchunk[k, j]] = base + j
        return 0
    lax.fori_loop(0, tl, scatter, 0)


def _slots(counts, eidx, rank, nb):
    k_, t = eidx.shape
    tl = 1024
    smem = lambda: pl.BlockSpec(memory_space=pltpu.SMEM)
    chunk = lambda: pl.BlockSpec((k_, tl), lambda i: (0, i))
    return pl.pallas_call(
        _slots_body,
        grid=(t // tl,),
        in_specs=[smem(), chunk(), chunk()],
        out_specs=[chunk(), smem(), smem(), smem(), smem(), smem()],
        out_shape=[
            jax.ShapeDtypeStruct((k_, t), I32),
            jax.ShapeDtypeStruct((nb * EXPERT_ROWS,), I32),
            jax.ShapeDtypeStruct((nb,), I32),
            jax.ShapeDtypeStruct((nb,), I32),
            jax.ShapeDtypeStruct((nb,), I32),
            jax.ShapeDtypeStruct((1,), I32),
        ],
        scratch_shapes=[
            pltpu.SMEM((counts.shape[0],), I32),
            pltpu.SMEM((k_, tl), I32),
            pltpu.SemaphoreType.DMA,
        ],
        compiler_params=_cparams(("arbitrary",)),
        name="slots",
    )(counts, eidx, rank)


EXPERT_ISSUE_GROUPS = 8
EXPERT_LOOKAHEAD = 2


def _expert_body(be_ref, first_ref, nxt_ref, nu_ref, rows_hbm, x_hbm, wg_hbm, wu_hbm, wd_hbm, y_ref,
                 idx_smem, xbuf, wg_f, wu_f, wd_f, wg_s, wu_s, wd_s, isem, gsem, wsem):
    i = pl.program_id(0)
    n_used = nu_ref[0]
    r_ = EXPERT_ROWS
    tiles = x_hbm.shape[1]
    ns = EXPERT_LOOKAHEAD + 1
    slot = lax.rem(i, ns)
    aslot = lax.rem(i + EXPERT_LOOKAHEAD, ns)

    def idx_copy(blk, s):
        return pltpu.make_async_copy(rows_hbm.at[pl.ds(blk, 1)], idx_smem.at[s], isem.at[s])

    def row_copy(s, r):
        return pltpu.make_async_copy(x_hbm.at[idx_smem[s, 0, r]], xbuf.at[s, pl.ds(r * tiles, tiles)], gsem.at[s])

    def issue_rows(s, g):
        step = r_ // EXPERT_ISSUE_GROUPS
        for r in range(g * step, (g + 1) * step):
            row_copy(s, r).start()

    def wait_rows(s):
        pltpu.make_async_copy(xbuf.at[s], xbuf.at[s], gsem.at[s]).wait()

    def weight_copies(e):
        return (pltpu.make_async_copy(wg_hbm.at[e], wg_f, wsem.at[0]),
                pltpu.make_async_copy(wu_hbm.at[e], wu_f, wsem.at[1]),
                pltpu.make_async_copy(wd_hbm.at[e], wd_f, wsem.at[2]))

    @pl.when(i == 0)
    def _():
        for cp in weight_copies(be_ref[0]):
            cp.start(priority=1)
        for b in range(EXPERT_LOOKAHEAD):
            idx_copy(b, b).start()
        for b in range(EXPERT_LOOKAHEAD):
            idx_copy(b, b).wait()

            def body(r, _, b=b):
                row_copy(b, r).start()
                return 0
            lax.fori_loop(0, r_, body, 0)
        idx_copy(EXPERT_LOOKAHEAD, EXPERT_LOOKAHEAD).start()

    @pl.when(first_ref[i] == 1)
    def _():
        for cp in weight_copies(be_ref[i]):
            cp.wait()
        wg_s[...] = wg_f[...].astype(BF16)
        wu_s[...] = wu_f[...].astype(BF16)
        wd_s[...] = wd_f[...].astype(BF16)

        @pl.when(nxt_ref[i] >= 0)
        def _():
            for cp in weight_copies(nxt_ref[i]):
                cp.start(priority=1)

    @pl.when(i < n_used)
    def _():
        idx_copy(i + EXPERT_LOOKAHEAD, aslot).wait()
        wait_rows(slot)
        lo, hi = _unpack_bf16_pair(_load_row_tiles(xbuf, (slot,), 0, r_, tiles))
        lo = lo.astype(BF16)
        hi = hi.astype(BF16)
        half = lo.shape[-1]
        issue_rows(aslot, 0)
        a = _dot(lo, wg_s[0:half, :])
        issue_rows(aslot, 1)
        a = a + _dot(hi, wg_s[half:, :])
        issue_rows(aslot, 2)
        u = _dot(lo, wu_s[0:half, :])
        issue_rows(aslot, 3)
        u = u + _dot(hi, wu_s[half:, :])
        issue_rows(aslot, 4)
        hid = (_silu(a) * u).astype(BF16)
        y_lo = _dot(hid, wd_s[:, 0:half])
        issue_rows(aslot, 5)
        y_hi = _dot(hid, wd_s[:, half:])
        issue_rows(aslot, 6)
        _store_row_tiles(y_ref, (), _pack_bf16_pair(y_lo, y_hi))
        issue_rows(aslot, 7)

        @pl.when(i + 1 < n_used)
        def _():
            idx_copy(i + EXPERT_LOOKAHEAD + 1, slot).start()

    @pl.when(jnp.logical_and(i >= n_used, i < n_used + EXPERT_LOOKAHEAD))
    def _():
        wait_rows(slot)

    @pl.when(i >= n_used)
    def _():
        y_ref[...] = jnp.zeros_like(y_ref)


def _experts(block_e, first, nxt, n_used, rows, h2p, wg, wu, wd):
    nb = block_e.shape[0]
    t, tiles, _ = h2p.shape
    ne, d, f = wg.shape
    r_ = EXPERT_ROWS
    hbm = lambda: pl.BlockSpec(memory_space=pl.ANY)
    grid_spec = pltpu.PrefetchScalarGridSpec(
        num_scalar_prefetch=4,
        grid=(nb,),
        in_specs=[hbm(), hbm(), hbm(), hbm(), hbm()],
        out_specs=pl.BlockSpec((r_ * tiles, 128), lambda i, *_: (i, 0)),
        scratch_shapes=[
            pltpu.SMEM((EXPERT_LOOKAHEAD + 1, 1, r_), I32),
            pltpu.VMEM((EXPERT_LOOKAHEAD + 1, r_ * tiles, 128), U32),
            pltpu.VMEM((d, f), F32),
            pltpu.VMEM((d, f), F32),
            pltpu.VMEM((f, d), F32),
            pltpu.VMEM((d, f), BF16),
            pltpu.VMEM((d, f), BF16),
            pltpu.VMEM((f, d), BF16),
            pltpu.SemaphoreType.DMA((EXPERT_LOOKAHEAD + 1,)),
            pltpu.SemaphoreType.DMA((EXPERT_LOOKAHEAD + 1,)),
            pltpu.SemaphoreType.DMA((3,)),
        ],
    )
    return pl.pallas_call(
        _expert_body,
        grid_spec=grid_spec,
        out_shape=jax.ShapeDtypeStruct((nb * r_ * tiles, 128), U32),
        compiler_params=_cparams(("arbitrary",)),
        name="experts",
    )(block_e, first, nxt, n_used, rows, h2p, wg, wu, wd)


def _combine_body(dest_hbm, y_hbm, w_ref, x1_ref, h2_ref, g2_ref, sg_ref, su_ref, sd_ref, fnw_ref, o_ref,
                  idx_smem, ybuf, isem, gsem):
    i = pl.program_id(0)
    n = pl.num_programs(0)
    tm = x1_ref.shape[0]
    tiles = y_hbm.shape[1]
    slot = i % 2
    nslot = 1 - slot

    def idx_copy(blk, s):
        cols = pl.ds(pl.multiple_of(blk * tm, tm), tm)
        return pltpu.make_async_copy(dest_hbm.at[:, cols], idx_smem.at[s], isem.at[s])

    def issue_rows(s):
        def body(j, _):
            for k in range(TOP_K):
                dst = pl.ds(pl.multiple_of((k * tm + j) * tiles, tiles), tiles)
                pltpu.make_async_copy(y_hbm.at[idx_smem[s, k, j]], ybuf.at[s, dst],
                                      gsem.at[s]).start(priority=k % 2)
            return 0
        lax.fori_loop(0, tm, body, 0)

    def wait_rows(s):
        pltpu.make_async_copy(ybuf.at[s], ybuf.at[s], gsem.at[s]).wait()

    @pl.when(i == 0)
    def _():
        idx_copy(0, 0).start()
        idx_copy(0, 0).wait()
        issue_rows(0)

        @pl.when(n > 1)
        def _():
            idx_copy(1, 1).start()

    @pl.when(i + 1 < n)
    def _():
        idx_copy(i + 1, nslot).wait()
        issue_rows(nslot)

    wait_rows(slot)

    @pl.when(i + 2 < n)
    def _():
        idx_copy(i + 2, slot).start()

    w = w_ref[...]
    half = tiles * 128
    r_lo = [jnp.zeros((tm, 128), F32) for _ in range(tiles)]
    r_hi = [jnp.zeros((tm, 128), F32) for _ in range(tiles)]
    for k in range(TOP_K):
        wk = w[:, k:k + 1]
        for kc in range(tiles):
            lo, hi = _unpack_bf16_pair(ybuf[slot, pl.ds(k * tm * tiles + kc, tm, stride=tiles), :])
            r_lo[kc] = r_lo[kc] + wk * lo
            r_hi[kc] = r_hi[kc] + wk * hi
    routed = jnp.concatenate(r_lo + r_hi, axis=-1)

    x_lo, x_hi = _unpack_bf16_pair(_load_row_tiles(h2_ref, (), 0, tm, tiles))
    x_lo = x_lo.astype(BF16)
    x_hi = x_hi.astype(BF16)
    a = _dot(x_lo, sg_ref[0:half, :]) + _dot(x_hi, sg_ref[half:, :])
    u = _dot(x_lo, su_ref[0:half, :]) + _dot(x_hi, su_ref[half:, :])
    shared = _dot((_silu(a) * u).astype(BF16), sd_ref[...])

    x2 = x1_ref[...] + g2_ref[0, 0] * (routed + shared)
    ms = jnp.mean(x2 * x2, axis=-1, keepdims=True)
    o_ref[...] = x2 * lax.rsqrt(ms + EPS) * fnw_ref[...]


def _combine(dest, y_sorted, w_tok, x1, h2p, mod4, sg, su, sd, fnw, tiles_per_batch, tm):
    t, d = x1.shape
    half = d // 2
    nrow = TOP_K * tm
    res = lambda shape: pl.BlockSpec(shape, lambda i: (0,) * len(shape), pipeline_mode=pl.Buffered(1))
    return pl.pallas_call(
        _combine_body,
        grid=(t // tm,),
        in_specs=[
            pl.BlockSpec(memory_space=pl.ANY),
            pl.BlockSpec(memory_space=pl.ANY),
            pl.BlockSpec((tm, TOP_K), lambda i: (i, 0)),
            pl.BlockSpec((tm, d), lambda i: (i, 0)),
            pl.BlockSpec((tm * (half // 128), 128), lambda i: (i, 0)),
            pl.BlockSpec((1, 1, 1, d), lambda i: (i // tiles_per_batch, 5, 0, 0)),
            res(sg.shape), res(su.shape), res(sd.shape), res(fnw.shape),
        ],
        out_specs=pl.BlockSpec((tm, d), lambda i: (i, 0)),
        out_shape=jax.ShapeDtypeStruct((t, d), F32),
        scratch_shapes=[
            pltpu.SMEM((2, TOP_K, tm), I32),
            pltpu.VMEM((2, nrow * (half // 128), 128), U32),
            pltpu.SemaphoreType.DMA((2,)),
            pltpu.SemaphoreType.DMA((2,)),
        ],
        compiler_params=_cparams(("arbitrary",)),
        name="combine",
    )(dest, y_sorted, w_tok, x1, h2p, mod4, sg, su, sd, fnw)


def kernel(x, c, ctx, c_ctx, ada_w, ada_b, norm1_w, norm2_w, w_in, s5_lam_re, s5_lam_im, s5_log_dt,
           s5_b_re, s5_b_im, s5_c_re, s5_c_im, s5_d, s5_glu_w, gla_gk_up, gla_gk_b, gla_norm_w,
           w_s5_proj, w_gla_proj, w_out, router_w, router_bias, exp_w_gate, exp_w_up, exp_w_down,
           sh_w_gate, sh_w_up, sh_w_down, final_norm_w):
    depth = ada_w.shape[0]
    assert depth == 1, "single-layer block: context outputs are never consumed"
    b_, l_, d = x.shape
    lc = ctx.shape[1]
    s5w = s5_d.shape[1]
    kw = gla_gk_up.shape[-1]
    vw = w_gla_proj.shape[1]
    dk, dv = kw // GLA_HEADS, vw // GLA_HEADS
    rank2 = 2 * GLA_GATE_RANK
    li = 0

    c8 = jnp.zeros((8, d), F32).at[:b_].set(c).at[b_].set(c_ctx)
    mod4 = _ada(c8, ada_w[li], ada_b[li]).reshape(8, N_MOD, 1, d)

    cuts = [0, s5w, s5w + kw, s5w + 2 * kw, s5w + 2 * kw + vw, s5w + 2 * kw + 2 * vw]
    c_gkd = cuts[5]
    w = w_in[li]
    w_main = jnp.concatenate([w[:, :c_gkd], w[:, c_gkd + rank2:]], axis=1).astype(BF16)
    w_ctx = jnp.concatenate([w[:, cuts[0]:cuts[1]], w[:, cuts[2]:cuts[4]]], axis=1).astype(BF16)
    w_gkd = jnp.zeros((d, 128), F32).at[:, :rank2].set(w[:, c_gkd:c_gkd + rank2]).astype(BF16)
    nw1 = norm1_w[li].reshape(1, d)
    p_lat, gk_lat, v_lat = _inproj(x, mod4, lambda b: b, nw1, w_main, w_gkd, tm=1024, tn=1024, s5w=s5w)
    p_ctx, gk_ctx, v_ctx = _inproj(ctx, mod4, lambda b: b_, nw1, w_ctx, w_gkd, tm=lc, tn=w_ctx.shape[1] // 2,
                                   s5w=s5w)

    ws, wq, wout5, acoef, dvec = _s5_factors(s5_lam_re[li], s5_lam_im[li], s5_log_dt[li], s5_b_re[li],
                                             s5_b_im[li], s5_c_re[li], s5_c_im[li], s5_d[li])
    ys = _s5v2(v_ctx, v_lat, ws, wq, wout5, acoef, dvec)

    up_pad = jnp.zeros((2, 128, kw), F32)
    up_pad = up_pad.at[0, :GLA_GATE_RANK].set(gla_gk_up[li, 0]).at[1, GLA_GATE_RANK:rank2].set(gla_gk_up[li, 1])
    yg = _gla(p_lat, p_ctx, gk_lat, gk_ctx, up_pad.astype(BF16), gla_gk_b[li].reshape(2, 1, kw), dk, dv)

    rw = jnp.zeros((d, 128), F32).at[:, :N_EXPERTS].set(router_w[li])
    rw_hi = rw.astype(BF16)
    rw2 = jnp.stack([rw_hi, (rw - rw_hi.astype(F32)).astype(BF16)])
    x1, h2p, logits = _merge(
        ys, yg, p_lat, x, mod4, s5_glu_w[li].astype(BF16), w_s5_proj[li].astype(BF16),
        w_gla_proj[li].astype(BF16), w_out[li].astype(BF16), gla_norm_w[li].reshape(1, dv),
        norm2_w[li].reshape(1, d), rw2, dv, tm=256)

    t = b_ * l_
    logits_t = logits.reshape(t, 128)[:, :N_EXPERTS].T
    eidx, wts, rank, cnt = _route(logits_t, router_bias[li])
    nb = (t * TOP_K + N_EXPERTS * (EXPERT_ROWS - 1) + EXPERT_ROWS - 1) // EXPERT_ROWS + EXPERT_LOOKAHEAD
    dest, rows, block_e, first, nxt, n_used = _slots(cnt[:, 0].astype(I32), eidx, rank, nb)

    tiles = d // 256
    y_sorted = _experts(block_e, first, nxt, n_used, rows.reshape(nb, EXPERT_ROWS), h2p.reshape(t, tiles, 128),
                        exp_w_gate[li], exp_w_up[li], exp_w_down[li])

    tm_c = 256
    out = _combine(dest, y_sorted.reshape(-1, tiles, 128), wts.T, x1.reshape(t, d), h2p.reshape(t * tiles, 128), mod4,
                   sh_w_gate[li].astype(BF16), sh_w_up[li].astype(BF16), sh_w_down[li].astype(BF16),
                   final_norm_w.reshape(1, d), l_ // tm_c, tm_c)
    return out.reshape(b_, l_, d)
```

```python
import functools
import math

import jax
import jax.numpy as jnp
from jax import lax
from jax.experimental import pallas as pl
from jax.experimental.pallas import tpu as pltpu

F32 = jnp.float32
BF16 = jnp.bfloat16
U32 = jnp.uint32
I32 = jnp.int32

EPS = 1e-6
N_MOD = 6

S5_GROUP = 16
S5_STATE = 64
S5_CHUNK = 16
S5_GROUPS_PER_STEP = 8

GLA_HEADS = 4
GLA_GATE_RANK = 16
GLA_GATE_NORM = 16.0
GLA_CHUNK = 64

N_EXPERTS = 64
TOP_K = 8
N_GROUPS = 8
TOPK_GROUPS = 4
ROUTED_SCALE = 2.5
EXPERT_ROWS = 256

VMEM_LIMIT = 56 * 1024 * 1024


def _cparams(sem, vmem=VMEM_LIMIT):
    return pltpu.CompilerParams(dimension_semantics=sem, vmem_limit_bytes=vmem)


def _dot(a, b):
    return jnp.dot(a, b, preferred_element_type=F32)


def _sigmoid(x):
    return 1.0 / (1.0 + jnp.exp(-x))


def _silu(x):
    return x * _sigmoid(x)


def _pack_bf16_pair(lo, hi):
    lo_bits = lax.bitcast_convert_type(lo.astype(BF16).astype(F32), U32)
    hi_bits = lax.bitcast_convert_type(hi.astype(BF16).astype(F32), U32)
    return (lo_bits >> 16) | (hi_bits & jnp.uint32(0xFFFF0000))


def _unpack_bf16_pair(w):
    lo = lax.bitcast_convert_type(w << 16, F32)
    hi = lax.bitcast_convert_type(w & jnp.uint32(0xFFFF0000), F32)
    return lo, hi


def _store_row_tiles(ref, lead, val):
    rows, rt = val.shape[0], val.shape[1] // 128
    for kc in range(rt):
        ref[(*lead, pl.ds(kc, rows, stride=rt), slice(None))] = val[:, kc * 128:(kc + 1) * 128]


def _load_row_tiles(ref, lead, row0, rows, rt):
    return jnp.concatenate(
        [ref[(*lead, pl.ds(row0 * rt + kc, rows, stride=rt), slice(None))] for kc in range(rt)], axis=-1)


def _ada_body(c_ref, w_ref, b_ref, o_ref):
    c = c_ref[...]
    s = _silu(c).astype(BF16)
    o_ref[...] = _dot(s, w_ref[...].astype(BF16)) + b_ref[...]


def _ada(c8, ada_w, ada_b):
    d, n = ada_w.shape
    tn = 512
    return pl.pallas_call(
        _ada_body,
        grid=(n // tn,),
        in_specs=[
            pl.BlockSpec((8, d), lambda j: (0, 0)),
            pl.BlockSpec((d, tn), lambda j: (0, j)),
            pl.BlockSpec((1, tn), lambda j: (0, j)),
        ],
        out_specs=pl.BlockSpec((8, tn), lambda j: (0, j)),
        out_shape=jax.ShapeDtypeStruct((8, n), F32),
        compiler_params=_cparams(("parallel",)),
        name="ada",
    )(c8, ada_w, ada_b.reshape(1, n))


def _modulated_norm(x, nw, sh, sc):
    ms = jnp.mean(x * x, axis=-1, keepdims=True)
    y = x * lax.rsqrt(ms + EPS) * nw
    return y * (1.0 + sc) + sh


def _inproj_body(x_ref, sh_ref, sc_ref, nw_ref, w_ref, wg_ref, p_ref, g_ref, v_ref, h_scr, u_scr):
    @pl.when(pl.program_id(2) == 0)
    def _():
        h = _modulated_norm(x_ref[0], nw_ref[...], sh_ref[0, 0], sc_ref[0, 0]).astype(BF16)
        h_scr[...] = h
        g_ref[0] = _dot(h, wg_ref[...])

    r = _dot(h_scr[...], w_ref[...])
    p_ref[0] = r.astype(BF16)

    @pl.when(pl.program_id(2) == 0)
    def _():
        nch = u_scr.shape[1] // S5_CHUNK
        for c in range(u_scr.shape[0]):
            u_scr[c] = r[:, c * 128:(c + 1) * 128]
            for t in range(S5_CHUNK):
                piece = u_scr[c, pl.ds(t, nch, stride=S5_CHUNK), :]
                v_ref[c, 0, :, t * 128:(t + 1) * 128] = piece.astype(BF16)


def _inproj(x, mod4, mod_row, nw, w, wg, tm, tn, s5w):
    b_, l_, d = x.shape
    n = w.shape[1]
    assert tn >= s5w and tm % (16 * S5_CHUNK) == 0
    nblk, vw = s5w // 128, S5_CHUNK * 128
    return pl.pallas_call(
        _inproj_body,
        grid=(b_, l_ // tm, n // tn),
        in_specs=[
            pl.BlockSpec((1, tm, d), lambda b, i, j: (b, i, 0)),
            pl.BlockSpec((1, 1, 1, d), lambda b, i, j: (mod_row(b), 0, 0, 0)),
            pl.BlockSpec((1, 1, 1, d), lambda b, i, j: (mod_row(b), 1, 0, 0)),
            pl.BlockSpec((1, d), lambda b, i, j: (0, 0)),
            pl.BlockSpec((d, tn), lambda b, i, j: (0, j)),
            pl.BlockSpec((d, 128), lambda b, i, j: (0, 0)),
        ],
        out_specs=[
            pl.BlockSpec((1, tm, tn), lambda b, i, j: (b, i, j)),
            pl.BlockSpec((1, tm, 128), lambda b, i, j: (b, i, 0)),
            pl.BlockSpec((nblk, 1, tm // S5_CHUNK, vw), lambda b, i, j: (0, b, i, 0)),
        ],
        out_shape=[
            jax.ShapeDtypeStruct((b_, l_, n), BF16),
            jax.ShapeDtypeStruct((b_, l_, 128), F32),
            jax.ShapeDtypeStruct((nblk, b_, l_ // S5_CHUNK, vw), BF16),
        ],
        scratch_shapes=[pltpu.VMEM((tm, d), BF16), pltpu.VMEM((nblk, tm, 128), F32)],
        compiler_params=_cparams(("parallel", "parallel", "arbitrary")),
        name="inproj",
    )(x, mod4, mod4, nw, w, wg)


def _s5_weights(lam_re, lam_im, log_dt, b_re, b_im, c_re, c_im, d_skip):
    hp = lax.Precision.HIGHEST
    tc = S5_CHUNK
    g_, p_ = lam_re.shape[1:]
    h_ = b_re.shape[-1]
    dt = jnp.exp(log_dt)[..., None]
    ar, ai = lam_re * dt, lam_im * dt
    k = jnp.arange(tc + 1, dtype=F32)
    mag = jnp.exp(ar[..., None] * k)
    pw_re = mag * jnp.cos(ai[..., None] * k)
    pw_im = mag * jnp.sin(ai[..., None] * k)
    num_re = jnp.expm1(ar) * jnp.cos(ai) - 2.0 * jnp.sin(0.5 * ai) ** 2
    num_im = jnp.exp(ar) * jnp.sin(ai)
    den = lam_re * lam_re + lam_im * lam_im
    f_re = (num_re * lam_re + num_im * lam_im) / den
    f_im = (num_im * lam_re - num_re * lam_im) / den
    bb_re = f_re[..., None] * b_re - f_im[..., None] * b_im
    bb_im = f_re[..., None] * b_im + f_im[..., None] * b_re

    cp_re = c_re[..., None] * pw_re[:, :, None] - c_im[..., None] * pw_im[:, :, None]
    cp_im = c_re[..., None] * pw_im[:, :, None] + c_im[..., None] * pw_re[:, :, None]
    kk = (jnp.einsum('dgipt,dgpj->dgtij', cp_re, bb_re, precision=hp)
          - jnp.einsum('dgipt,dgpj->dgtij', cp_im, bb_im, precision=hp))

    s_idx = jnp.arange(tc)[:, None]
    t_idx = jnp.arange(tc)[None, :]
    lag_f = t_idx - s_idx
    lag_b = s_idx - t_idx
    kf = jnp.where((lag_f >= 0)[None, :, :, None, None], kk[0][:, jnp.clip(lag_f, 0, tc)], 0.0)
    kb = jnp.where((lag_b >= 0)[None, :, :, None, None], kk[1][:, jnp.clip(lag_b, 0, tc)], 0.0)
    dsk = d_skip.reshape(g_, h_)
    eye_t = jnp.eye(tc, dtype=F32)[None, :, :, None, None]
    eye_h = jnp.eye(h_, dtype=F32)[None, None, None]
    m_all = kf + kb + eye_t * eye_h * dsk[:, None, None, :, None]
    m_all = m_all.transpose(0, 1, 4, 2, 3).reshape(g_, tc * h_, tc * h_)

    def state_w(dr, expo):
        e_re = pw_re[dr][:, :, expo]
        e_im = pw_im[dr][:, :, expo]
        w_re = jnp.einsum('gps,gpj->gsjp', e_re, bb_re[dr], precision=hp) - jnp.einsum(
            'gps,gpj->gsjp', e_im, bb_im[dr], precision=hp)
        w_im = jnp.einsum('gps,gpj->gsjp', e_re, bb_im[dr], precision=hp) + jnp.einsum(
            'gps,gpj->gsjp', e_im, bb_re[dr], precision=hp)
        w_re = w_re.reshape(g_, tc * h_, p_)
        w_im = w_im.reshape(g_, tc * h_, p_)
        return jnp.concatenate([w_re, w_im, w_im, w_re], axis=-1)

    ws_f = state_w(0, tc - 1 - jnp.arange(tc))
    ws_b = state_w(1, jnp.arange(tc))
    wcat = jnp.concatenate([m_all, ws_f, ws_b], axis=-1)

    def out_w(dr, expo):
        q_re = cp_re[dr][..., expo]
        q_im = cp_im[dr][..., expo]
        top = q_re.transpose(0, 2, 3, 1).reshape(g_, p_, tc * h_)
        bot = (-q_im).transpose(0, 2, 3, 1).reshape(g_, p_, tc * h_)
        return jnp.concatenate([top, bot], axis=1)

    wout = jnp.concatenate([out_w(0, 1 + jnp.arange(tc)), out_w(1, tc - jnp.arange(tc))], axis=1)

    a_re, a_im = pw_re[..., tc], pw_im[..., tc]
    zeros = jnp.zeros_like(a_re[0])
    rows = []
    for dr in range(2):
        rows += [jnp.concatenate([a_re[dr], a_re[dr]], -1),
                 jnp.concatenate([-a_im[dr], a_im[dr]], -1),
                 jnp.concatenate([a_im[dr], -a_im[dr]], -1)]
    rows += [jnp.concatenate([zeros, zeros], -1)] * 2
    acoef = jnp.stack(rows, axis=1)
    return wcat.astype(BF16), wout.astype(BF16), acoef


def _s5_body(x_ref, wcat_ref, wout_ref, a_ref, y_ref, y_scr, s_scr, h_scr, *, n_ctx, n_lat, nb):
    gb = x_ref.shape[0]
    pw = 2 * S5_STATE
    for g in range(gb):
        r = _dot(x_ref[g], wcat_ref[g])
        y_scr[g] = r[:, : y_scr.shape[-1]]
        s_scr[g] = r[:, y_scr.shape[-1]:]

    a1f, a2f, a2sf = a_ref[:, 0:1, :], a_ref[:, 1:2, :], a_ref[:, 2:3, :]
    a1b, a2b, a2sb = a_ref[:, 3:4, :], a_ref[:, 4:5, :], a_ref[:, 5:6, :]

    first = lax.broadcasted_iota(I32, (gb, 2 * nb, pw), 1) < nb

    def both(v):
        r = pltpu.roll(v, nb, 1)
        return jnp.where(first, v, r), jnp.where(first, r, v)

    def step(pf, pb, carry):
        hf, hfs, hb, hbs = carry
        rf = pl.ds(pl.multiple_of(pf * 2 * nb, 2 * nb), 2 * nb)
        rb = pl.ds(pl.multiple_of(pb * 2 * nb, 2 * nb), 2 * nb)
        sfa, sfb = both(s_scr[:, rf, 0:pw])
        sfsa, sfsb = both(s_scr[:, rf, pw:2 * pw])
        sba, sbb = both(s_scr[:, rb, 2 * pw:3 * pw])
        sbsa, sbsb = both(s_scr[:, rb, 3 * pw:4 * pw])
        hf1 = a1f * hf + a2f * hfs + sfa
        hfs1 = a1f * hfs + a2sf * hf + sfsa
        h_scr[:, rf, 0:pw] = jnp.where(first, hf, hf1)
        hf2 = a1f * hf1 + a2f * hfs1 + sfb
        hfs2 = a1f * hfs1 + a2sf * hf1 + sfsb
        hb1 = a1b * hb + a2b * hbs + sbb
        hbs1 = a1b * hbs + a2sb * hb + sbsb
        h_scr[:, rb, pw:2 * pw] = jnp.where(first, hb1, hb)
        hb2 = a1b * hb1 + a2b * hbs1 + sba
        hbs2 = a1b * hbs1 + a2sb * hb1 + sbsa
        return hf2, hfs2, hb2, hbs2

    z = jnp.zeros((gb, 2 * nb, pw), F32)
    carry = (z, z, z, z)
    pc, pl_ = n_ctx // 2, n_lat // 2
    carry = lax.fori_loop(0, pc, lambda i, c: step(i, pc - 1 - i, c), carry)
    carry = lax.fori_loop(0, pl_, lambda i, c: step(pc + i, pc + pl_ - 1 - i, c), carry)

    lo = n_ctx * nb
    for g in range(gb):
        y = y_scr[g] + _dot(h_scr[g].astype(BF16), wout_ref[g])
        y_ref[g] = y[lo:, :]


def _s5(x2, wcat, wout, acoef, n_ctx, n_lat, nb):
    g_, rows, kdim = x2.shape
    gb = S5_GROUPS_PER_STEP
    pw = 2 * S5_STATE
    out_rows = n_lat * nb
    return pl.pallas_call(
        functools.partial(_s5_body, n_ctx=n_ctx, n_lat=n_lat, nb=nb),
        grid=(g_ // gb,),
        in_specs=[
            pl.BlockSpec((gb, rows, kdim), lambda i: (i, 0, 0)),
            pl.BlockSpec((gb, kdim, wcat.shape[-1]), lambda i: (i, 0, 0)),
            pl.BlockSpec((gb, 2 * pw, kdim), lambda i: (i, 0, 0)),
            pl.BlockSpec((gb, 8, pw), lambda i: (i, 0, 0)),
        ],
        out_specs=pl.BlockSpec((gb, out_rows, kdim), lambda i: (i, 0, 0)),
        out_shape=jax.ShapeDtypeStruct((g_, out_rows, kdim), F32),
        scratch_shapes=[
            pltpu.VMEM((gb, rows, kdim), F32),
            pltpu.VMEM((gb, rows, 4 * pw), F32),
            pltpu.VMEM((gb, rows, 2 * pw), F32),
        ],
        compiler_params=_cparams(("parallel",)),
        name="s5",
    )(x2, wcat, wout, acoef)


def _s5_factors(lam_re, lam_im, log_dt, b_re, b_im, c_re, c_im, d_skip):
    tc = S5_CHUNK
    g_, p_ = lam_re.shape[1:]
    h_ = b_re.shape[-1]
    dt = jnp.exp(log_dt)[..., None]
    ar, ai = lam_re * dt, lam_im * dt

    def powers(k):
        k = jnp.asarray(k, F32)
        mag = jnp.exp(ar[..., None] * k)
        return mag * jnp.cos(ai[..., None] * k), mag * jnp.sin(ai[..., None] * k)

    num_re = jnp.expm1(ar) * jnp.cos(ai) - 2.0 * jnp.sin(0.5 * ai) ** 2
    num_im = jnp.exp(ar) * jnp.sin(ai)
    den = lam_re * lam_re + lam_im * lam_im
    f_re = (num_re * lam_re + num_im * lam_im) / den
    f_im = (num_im * lam_re - num_re * lam_im) / den
    bb_re = f_re[..., None] * b_re - f_im[..., None] * b_im
    bb_im = f_re[..., None] * b_im + f_im[..., None] * b_re
    t_idx = jnp.arange(tc)

    def halves(a, b):
        return jnp.concatenate([a, b], axis=-1).transpose(1, 0, 2, 3)

    def power_rows(expo_f, expo_b):
        re_f, im_f = powers(expo_f)
        re_b, im_b = powers(expo_b)
        re = jnp.stack([re_f[0], re_b[1]]).transpose(0, 1, 3, 2)
        im = jnp.stack([im_f[0], im_b[1]]).transpose(0, 1, 3, 2)
        return re, im

    e_re, e_im = power_rows(tc - 1 - t_idx, t_idx)
    bt_re, bt_im = bb_re.transpose(0, 1, 3, 2), bb_im.transpose(0, 1, 3, 2)
    ea = jnp.stack([halves(e_re, e_re), halves(-e_im, e_im)], axis=2)
    ba = jnp.stack([halves(bt_re, bt_im), halves(bt_im, bt_re)], axis=2)

    def readout_rows(expo_f, expo_b):
        p_re, p_im = power_rows(expo_f, expo_b)
        return jnp.stack([halves(p_re, -p_im), halves(-p_im, -p_re)], axis=2)

    pq = readout_rows(t_idx - (tc - 1), -t_idx)
    po = readout_rows(t_idx + 1, tc - t_idx)
    ca = jnp.stack([halves(c_re, c_re), halves(c_im, c_im)], axis=2)

    a_re, a_im = powers(jnp.array([tc]))
    a_re, a_im = a_re[..., 0], a_im[..., 0]
    zeros = jnp.zeros_like(a_re[0])
    rows = []
    for dr in range(2):
        rows += [jnp.concatenate([a_re[dr], a_re[dr]], -1),
                 jnp.concatenate([-a_im[dr], a_im[dr]], -1),
                 jnp.concatenate([a_im[dr], -a_im[dr]], -1)]
    rows += [jnp.concatenate([zeros, zeros], -1)] * 2
    acoef = jnp.stack(rows, axis=1)
    dvec = jnp.tile(d_skip.reshape(g_, h_), (1, tc)).reshape(g_, 1, tc * h_)
    return ea, ba, pq, po, ca, acoef, dvec


def _split_bf16(x):
    hi = x.astype(BF16)
    return hi, (x - hi.astype(F32)).astype(BF16)


def _s5v2_body(vc_ref, vl_ref, ea_ref, ba_ref, pq_ref, po_ref, ca_ref, a_ref, d_ref, o_ref, y_scr, s_scr, h_scr,
               *, n_ctx, n_lat, nb):
    gb = ea_ref.shape[0]
    pw = 2 * S5_STATE
    gl = S5_GROUP
    kd = S5_CHUNK * gl
    lanes = 128
    n_tot = n_ctx + n_lat
    nt = (((1,), (1,)), ((), ()))

    def outer(x_ref, y_ref, q, dr):
        z = (x_ref[q, dr, 0][:, None, :] * y_ref[q, dr, 0][None, :, :]
             + x_ref[q, dr, 1][:, None, :] * y_ref[q, dr, 1][None, :, :])
        return z.reshape(kd, pw)
    v = jnp.concatenate([p for b in range(nb) for p in (vc_ref[0, b], vl_ref[0, b])], axis=0)
    width = v.shape[1]

    r_in = lax.broadcasted_iota(I32, (width, kd), 0)
    c_in = lax.broadcasted_iota(I32, (width, kd), 1)
    tgt_in = (r_in // lanes) * gl + (r_in % gl)
    grp_in = (r_in % lanes) // gl
    r_out = lax.broadcasted_iota(I32, (kd, width), 0)
    c_out = lax.broadcasted_iota(I32, (kd, width), 1)
    tgt_out = (c_out // lanes) * gl + (c_out % gl)
    grp_out = (c_out % lanes) // gl

    rs = lax.broadcasted_iota(I32, (kd, kd), 0)
    ct = lax.broadcasted_iota(I32, (kd, kd), 1)
    fmask = ct // gl >= rs // gl
    bmask = ct // gl <= rs // gl
    diag = rs == ct

    for q in range(gb):
        sel = jnp.where(jnp.logical_and(c_in == tgt_in, grp_in == q), 1.0, 0.0).astype(BF16)
        x2 = _dot(v, sel).astype(BF16)
        mm, ws = [], []
        for dr in range(2):
            ws.append(outer(ea_ref, ba_ref, q, dr))
            a_hi, a_lo = _split_bf16(ws[dr])
            b_hi, b_lo = _split_bf16(outer(pq_ref, ca_ref, q, dr))
            mm.append(lax.dot_general(a_hi, b_hi, nt, preferred_element_type=F32)
                      + lax.dot_general(a_hi, b_lo, nt, preferred_element_type=F32)
                      + lax.dot_general(a_lo, b_hi, nt, preferred_element_type=F32))
        m = jnp.where(fmask, mm[0], 0.0) + jnp.where(bmask, mm[1], 0.0) + jnp.where(diag, d_ref[q], 0.0)
        wsf, wsb = ws
        wcat = jnp.concatenate(
            [m, wsf, pltpu.roll(wsf, S5_STATE, 1), wsb, pltpu.roll(wsb, S5_STATE, 1)], axis=1).astype(BF16)
        r = _dot(x2, wcat)
        for b in range(nb):
            rb = r[b * n_tot:(b + 1) * n_tot, :]
            for j in range(kd // lanes):
                y_scr[q, j, pl.ds(b, n_tot, stride=nb), :] = rb[:, j * lanes:(j + 1) * lanes]
            for j in range(4):
                s_scr[q, j, pl.ds(b, n_tot, stride=nb), :] = rb[:, kd + j * pw:kd + (j + 1) * pw]

    a1f, a2f, a2sf = a_ref[:, 0:1, :], a_ref[:, 1:2, :], a_ref[:, 2:3, :]
    a1b, a2b, a2sb = a_ref[:, 3:4, :], a_ref[:, 4:5, :], a_ref[:, 5:6, :]
    first = lax.broadcasted_iota(I32, (gb, 2 * nb, pw), 1) < nb

    def both(x):
        rr = pltpu.roll(x, nb, 1)
        return jnp.where(first, x, rr), jnp.where(first, rr, x)

    def step(pf, pb, carry):
        hf, hfs, hb, hbs = carry
        rf = pl.ds(pl.multiple_of(pf * 2 * nb, 2 * nb), 2 * nb)
        rb = pl.ds(pl.multiple_of(pb * 2 * nb, 2 * nb), 2 * nb)
        sfa, sfb = both(s_scr[:, 0, rf, :])
        sfsa, sfsb = both(s_scr[:, 1, rf, :])
        sba, sbb = both(s_scr[:, 2, rb, :])
        sbsa, sbsb = both(s_scr[:, 3, rb, :])
        hf1 = a1f * hf + a2f * hfs + sfa
        hfs1 = a1f * hfs + a2sf * hf + sfsa
        h_scr[:, 0, rf, :] = jnp.where(first, hf, hf1)
        hf2 = a1f * hf1 + a2f * hfs1 + sfb
        hfs2 = a1f * hfs1 + a2sf * hf1 + sfsb
        hb1 = a1b * hb + a2b * hbs + sbb
        hbs1 = a1b * hbs + a2sb * hb + sbsb
        h_scr[:, 1, rb, :] = jnp.where(first, hb1, hb)
        hb2 = a1b * hb1 + a2b * hbs1 + sba
        hbs2 = a1b * hbs1 + a2sb * hb1 + sbsa
        return hf2, hfs2, hb2, hbs2

    z = jnp.zeros((gb, 2 * nb, pw), F32)
    carry = (z, z, z, z)
    pc, pl_ = n_ctx // 2, n_lat // 2
    carry = lax.fori_loop(0, pc, lambda i, c: step(i, pc - 1 - i, c), carry)
    carry = lax.fori_loop(0, pl_, lambda i, c: step(pc + i, pc + pl_ - 1 - i, c), carry)

    lo = n_ctx * nb
    for q in range(gb):
        yq = (lax.dot_general(h_scr[q, 0].astype(BF16), outer(po_ref, ca_ref, q, 0).astype(BF16), nt,
                              preferred_element_type=F32)
              + lax.dot_general(h_scr[q, 1].astype(BF16), outer(po_ref, ca_ref, q, 1).astype(BF16), nt,
                                preferred_element_type=F32))
        for j in range(kd // lanes):
            y_scr[q, j] = y_scr[q, j] + yq[:, j * lanes:(j + 1) * lanes]
        y = jnp.concatenate(
            [jnp.concatenate([y_scr[q, j, pl.ds(lo + b, n_lat, stride=nb), :] for j in range(kd // lanes)], axis=-1)
             for b in range(nb)], axis=0)
        selt = jnp.where(jnp.logical_and(r_out == tgt_out, grp_out == q), 1.0, 0.0).astype(BF16)
        placed = _dot(y.astype(BF16), selt)
        if q == 0:
            o_ref[0] = placed
        else:
            o_ref[0] = o_ref[0] + placed


def _s5v2(v_ctx, v_lat, ea, ba, pq, po, ca, acoef, dvec):
    nblk, nb, n_ctx, width = v_ctx.shape
    n_lat = v_lat.shape[2]
    rows = (n_ctx + n_lat) * nb
    g_ = ea.shape[0]
    gb = g_ // nblk
    pw = 2 * S5_STATE
    kd = S5_CHUNK * S5_GROUP
    out_rows = n_lat * nb
    fac = lambda: pl.BlockSpec((gb, 2, 2, S5_CHUNK, pw), lambda i: (i, 0, 0, 0, 0))
    return pl.pallas_call(
        functools.partial(_s5v2_body, n_ctx=n_ctx, n_lat=n_lat, nb=nb),
        grid=(nblk,),
        in_specs=[
            pl.BlockSpec((1, nb, n_ctx, width), lambda i: (i, 0, 0, 0)),
            pl.BlockSpec((1, nb, n_lat, width), lambda i: (i, 0, 0, 0)),
            fac(), fac(), fac(), fac(), fac(),
            pl.BlockSpec((gb, 8, pw), lambda i: (i, 0, 0)),
            pl.BlockSpec((gb, 1, kd), lambda i: (i, 0, 0)),
        ],
        out_specs=pl.BlockSpec((1, out_rows, width), lambda i: (i, 0, 0)),
        out_shape=jax.ShapeDtypeStruct((nblk, out_rows, width), F32),
        scratch_shapes=[
            pltpu.VMEM((gb, kd // 128, rows, 128), F32),
            pltpu.VMEM((gb, 4, rows, pw), F32),
            pltpu.VMEM((gb, 2, rows, pw), F32),
        ],
        compiler_params=_cparams(("parallel",)),
        name="s5",
    )(v_ctx, v_lat, ea, ba, pq, po, ca, acoef, dvec)


def _log_sigmoid(z):
    return jnp.minimum(z, 0.0) - jnp.log1p(jnp.exp(-jnp.abs(z)))


def _chunk_cumsum(g, pos, reverse):
    n = g.shape[0]
    x = g
    sh = 1
    while sh < GLA_CHUNK:
        if reverse:
            x = x + jnp.where(pos < GLA_CHUNK - sh, pltpu.roll(x, n - sh, 0), 0.0)
        else:
            x = x + jnp.where(pos >= sh, pltpu.roll(x, sh, 0), 0.0)
        sh *= 2
    return x


def _gla_body(q_ref, k_ref, v_ref, kc_ref, vc_ref, gk_ref, gkc_ref, up_ref, bias_ref, o_ref,
              bc_scr, kb_scr, qb_scr, st_ref, *, scale):
    c = GLA_CHUNK
    l_ = q_ref.shape[1]
    lc = kc_ref.shape[1]
    n_lat, n_ctx = l_ // c, lc // c
    row = lax.broadcasted_iota(I32, (c, c), 0)
    col = lax.broadcasted_iota(I32, (c, c), 1)
    nt = (((1,), (1,)), ((), ()))
    tn = (((0,), (0,)), ((), ()))

    for dr in range(2):
        smask = (col <= row) if dr == 0 else (col > row)
        up = up_ref[dr]
        bias = bias_ref[dr]
        edge_tile, edge_row = (c - 8, 7) if dr == 0 else (0, 0)

        for gk_r, k_r, base, n, latent in ((gkc_ref, kc_ref, 0, lc, False), (gk_ref, k_ref, lc, l_, True)):
            z = _dot(gk_r[0].astype(BF16), up) + bias
            g = _log_sigmoid(z) * (1.0 / GLA_GATE_NORM)
            pos = lax.broadcasted_iota(I32, g.shape, 0) & (c - 1)
            bc = _chunk_cumsum(g, pos, reverse=(dr == 1))
            bc_scr[base:base + n, :] = bc
            kb_scr[base:base + n, :] = k_r[0].astype(F32) * jnp.exp(-bc)
            if latent:
                qb_scr[...] = (q_ref[0].astype(F32) * (scale * jnp.exp(bc))).astype(BF16)

        def chunk_rows(n, base):
            r0 = pl.multiple_of(n * c, c)
            tot = bc_scr[pl.ds(base + r0 + edge_tile, 8), :][edge_row:edge_row + 1, :]
            return r0, pl.ds(base + r0, c), jnp.exp(tot)

        def update_state(kb, vv, decay):
            kd = (kb * decay).astype(BF16)
            ds_t = lax.dot_general(vv, kd, tn, preferred_element_type=F32)
            st_ref[...] = decay * st_ref[...] + ds_t

        def ctx_step(i, _):
            n = i if dr == 0 else n_ctx - 1 - i
            r0, rs, decay = chunk_rows(n, 0)
            update_state(kb_scr[rs, :], vc_ref[0, pl.ds(r0, c), :], decay)
            return 0

        def lat_step(i, _):
            n = i if dr == 0 else n_lat - 1 - i
            r0, rs, decay = chunk_rows(n, lc)
            ro = pl.ds(r0, c)
            kb = kb_scr[rs, :]
            vv = v_ref[0, ro, :]
            qb = qb_scr[ro, :]
            scores = lax.dot_general(qb, kb.astype(BF16), nt, preferred_element_type=F32)
            scores = jnp.where(smask, scores, 0.0).astype(BF16)
            o = _dot(scores, vv) + lax.dot_general(qb, st_ref[...].astype(BF16), nt,
                                                   preferred_element_type=F32)
            if dr == 0:
                o_ref[0, ro, :] = o
            else:
                o_ref[0, ro, :] = o_ref[0, ro, :] + o
            update_state(kb, vv, decay)
            return 0

        st_ref[...] = jnp.zeros_like(st_ref)
        lax.fori_loop(0, n_ctx, ctx_step, 0, unroll=2)
        lax.fori_loop(0, n_lat, lat_step, 0, unroll=4)


def _gla(p_lat, p_ctx, gk_lat, gk_ctx, up_pad, bias, dk, dv):
    b_, l_, _ = p_lat.shape
    lc = p_ctx.shape[1]
    hh = GLA_HEADS
    s5w = hh * dv
    q0, k0, v0 = s5w // dk, (s5w + hh * dk) // dk, (s5w + 2 * hh * dk) // dv
    kc0, vc0 = s5w // dk, (s5w + hh * dk) // dv
    return pl.pallas_call(
        functools.partial(_gla_body, scale=dk ** -0.5),
        grid=(b_, hh),
        in_specs=[
            pl.BlockSpec((1, l_, dk), lambda b, h: (b, 0, q0 + h)),
            pl.BlockSpec((1, l_, dk), lambda b, h: (b, 0, k0 + h)),
            pl.BlockSpec((1, l_, dv), lambda b, h: (b, 0, v0 + h)),
            pl.BlockSpec((1, lc, dk), lambda b, h: (b, 0, kc0 + h)),
            pl.BlockSpec((1, lc, dv), lambda b, h: (b, 0, vc0 + h)),
            pl.BlockSpec((1, l_, 128), lambda b, h: (b, 0, 0)),
            pl.BlockSpec((1, lc, 128), lambda b, h: (b, 0, 0)),
            pl.BlockSpec((2, 128, dk), lambda b, h: (0, 0, h)),
            pl.BlockSpec((2, 1, dk), lambda b, h: (0, 0, h)),
        ],
        out_specs=pl.BlockSpec((1, l_, dv), lambda b, h: (b, 0, h)),
        out_shape=jax.ShapeDtypeStruct((b_, l_, hh * dv), F32),
        scratch_shapes=[
            pltpu.VMEM((lc + l_, dk), F32),
            pltpu.VMEM((lc + l_, dk), F32),
            pltpu.VMEM((l_, dk), BF16),
            pltpu.VMEM((dv, dk), F32),
        ],
        compiler_params=_cparams(("parallel", "parallel")),
        name="gla",
    )(p_lat, p_lat, p_lat, p_ctx, p_ctx, gk_lat, gk_ctx, up_pad, bias)


def _gelu_tanh(x):
    return 0.5 * x * (1.0 + jnp.tanh(math.sqrt(2.0 / math.pi) * (x + 0.044715 * (x * x * x))))


def _merge_body(ys_ref, yg_ref, go_ref, gs_ref, gg_ref, x_ref, g1_ref, sh2_ref, sc2_ref,
                glu_ref, ws5_ref, wgla_ref, wout_ref, gnw_ref, n2w_ref, rw_ref,
                x1_ref, h2_ref, lg_ref, ys_scr, *, dv):
    nch = ys_ref.shape[1]
    for c in range(ys_ref.shape[0]):
        for t in range(S5_CHUNK):
            ys_scr[c, pl.ds(t, nch, stride=S5_CHUNK), :] = ys_ref[c, :, t * 128:(t + 1) * 128]
    a = _gelu_tanh(jnp.concatenate([ys_scr[c] for c in range(ys_ref.shape[0])], axis=-1))
    a = a * _sigmoid(_dot(a.astype(BF16), glu_ref[...]))
    pa = _dot(a.astype(BF16), ws5_ref[...])

    yg = yg_ref[0]
    parts = []
    for h in range(GLA_HEADS):
        oh = yg[:, h * dv:(h + 1) * dv]
        ms = jnp.mean(oh * oh, axis=-1, keepdims=True)
        parts.append(oh * lax.rsqrt(ms + EPS) * gnw_ref[...])
    gl = jnp.concatenate(parts, axis=-1) * _silu(go_ref[0].astype(F32))
    pb = _dot(gl.astype(BF16), wgla_ref[...])

    m = _sigmoid(gs_ref[0].astype(F32)) * pa + _sigmoid(gg_ref[0].astype(F32)) * pb
    y = _dot(m.astype(BF16), wout_ref[...])
    x1 = x_ref[0] + g1_ref[0, 0] * y
    x1_ref[0] = x1

    h2 = _modulated_norm(x1, n2w_ref[...], sh2_ref[0, 0], sc2_ref[0, 0])
    h_hi = h2.astype(BF16)
    h_lo = (h2 - h_hi.astype(F32)).astype(BF16)
    half = h2.shape[-1] // 2
    _store_row_tiles(h2_ref, (0,), _pack_bf16_pair(h2[:, :half], h2[:, half:]))
    lg_ref[0] = _dot(h_hi, rw_ref[0]) + _dot(h_lo, rw_ref[0]) + _dot(h_hi, rw_ref[1])


def _resident(shape):
    nd = len(shape)
    return pl.BlockSpec(shape, lambda b, i: (0,) * nd, pipeline_mode=pl.Buffered(1))


def _merge(ys, yg, p_lat, x, mod4, glu_w, ws5, wgla, wout, gnw, n2w, rw2, dv, tm):
    b_, l_, d = x.shape
    nblk, _, vw = ys.shape
    w5 = nblk * 128
    nch = tm // S5_CHUNK
    go0 = (p_lat.shape[-1] - 2 * d - w5) // w5
    gs0 = (p_lat.shape[-1] - 2 * d) // d
    modspec = lambda r: pl.BlockSpec((1, 1, 1, d), lambda b, i: (b, r, 0, 0))
    return pl.pallas_call(
        functools.partial(_merge_body, dv=dv),
        grid=(b_, l_ // tm),
        in_specs=[
            pl.BlockSpec((nblk, nch, vw), lambda b, i: (0, b * (l_ // tm) + i, 0)),
            pl.BlockSpec((1, tm, w5), lambda b, i: (b, i, 0)),
            pl.BlockSpec((1, tm, w5), lambda b, i: (b, i, go0)),
            pl.BlockSpec((1, tm, d), lambda b, i: (b, i, gs0)),
            pl.BlockSpec((1, tm, d), lambda b, i: (b, i, gs0 + 1)),
            pl.BlockSpec((1, tm, d), lambda b, i: (b, i, 0)),
            modspec(2), modspec(3), modspec(4),
            _resident(glu_w.shape), _resident(ws5.shape), _resident(wgla.shape), _resident(wout.shape),
            _resident(gnw.shape), _resident(n2w.shape), _resident(rw2.shape),
        ],
        out_specs=[
            pl.BlockSpec((1, tm, d), lambda b, i: (b, i, 0)),
            pl.BlockSpec((1, tm * (d // 256), 128), lambda b, i: (b, i, 0)),
            pl.BlockSpec((1, tm, 128), lambda b, i: (b, i, 0)),
        ],
        out_shape=[
            jax.ShapeDtypeStruct((b_, l_, d), F32),
            jax.ShapeDtypeStruct((b_, l_ * (d // 256), 128), U32),
            jax.ShapeDtypeStruct((b_, l_, 128), F32),
        ],
        scratch_shapes=[pltpu.VMEM((nblk, tm, 128), F32)],
        compiler_params=_cparams(("parallel", "parallel")),
        name="merge",
    )(ys, yg, p_lat, p_lat, p_lat, x, mod4, mod4, mod4, glu_w, ws5, wgla, wout, gnw, n2w, rw2)


def _route_body(lg_ref, bias_ref, eidx_ref, w_ref, rank_ref, cnt_ref, carry_ref):
    ne, tt = lg_ref.shape
    gsz = ne // N_GROUPS
    neg = -jnp.inf

    @pl.when(pl.program_id(0) == 0)
    def _():
        carry_ref[...] = jnp.zeros_like(carry_ref)

    sc = _sigmoid(lg_ref[...])
    ch = sc + bias_ref[...]

    def first_max(v, idx, big):
        m = jnp.max(v, axis=0, keepdims=True)
        i = jnp.min(jnp.where(v == m, idx, big), axis=0, keepdims=True)
        return m, i

    midx = lax.broadcasted_iota(I32, (gsz, tt), 0).astype(F32)
    gs_rows = []
    for g in range(N_GROUPS):
        v = ch[g * gsz:(g + 1) * gsz, :]
        m1, i1 = first_max(v, midx, float(gsz))
        m2 = jnp.max(jnp.where(midx == i1, neg, v), axis=0, keepdims=True)
        gs_rows.append(m1 + m2)
    gscore = jnp.concatenate(gs_rows, axis=0)

    gidx = lax.broadcasted_iota(I32, (N_GROUPS, tt), 0).astype(F32)
    gsel = jnp.zeros((N_GROUPS, tt), F32)
    cur = gscore
    for _ in range(TOPK_GROUPS):
        _, i = first_max(cur, gidx, float(N_GROUPS))
        hit = gidx == i
        gsel = jnp.where(hit, 1.0, gsel)
        cur = jnp.where(hit, neg, cur)

    masked = jnp.concatenate(
        [jnp.where(gsel[g:g + 1, :] > 0.0, ch[g * gsz:(g + 1) * gsz, :], neg) for g in range(N_GROUPS)], axis=0)

    eiota = lax.broadcasted_iota(I32, (ne, tt), 0).astype(F32)
    sel = jnp.zeros((ne, tt), F32)
    cur = masked
    idx_rows, s_rows = [], []
    for _ in range(TOP_K):
        _, i = first_max(cur, eiota, float(ne))
        hit = eiota == i
        idx_rows.append(i)
        s_rows.append(jnp.sum(jnp.where(hit, sc, 0.0), axis=0, keepdims=True))
        sel = jnp.where(hit, 1.0, sel)
        cur = jnp.where(hit, neg, cur)
    idx = jnp.concatenate(idx_rows, axis=0)
    s = jnp.concatenate(s_rows, axis=0)
    w_ref[...] = s / jnp.sum(s, axis=0, keepdims=True) * ROUTED_SCALE
    eidx_ref[...] = idx.astype(I32)

    cw = 256
    a_i = lax.broadcasted_iota(I32, (cw, cw), 0)
    b_i = lax.broadcasted_iota(I32, (cw, cw), 1)
    upper = jnp.where(a_i < b_i, 1.0, 0.0).astype(BF16)
    carry = carry_ref[:, 0:1]
    pieces = []
    for c0 in range(0, tt, cw):
        sc_c = sel[:, c0:c0 + cw]
        pieces.append(_dot(sc_c.astype(BF16), upper) + carry)
        carry = carry + jnp.sum(sc_c, axis=1, keepdims=True)
    rank_full = jnp.concatenate(pieces, axis=1)
    carry_ref[...] = jnp.broadcast_to(carry, carry_ref.shape)
    cnt_ref[...] = jnp.broadcast_to(carry, cnt_ref.shape)
    rk = [jnp.sum(jnp.where(eiota == idx_rows[k], rank_full, 0.0), axis=0, keepdims=True) for k in range(TOP_K)]
    rank_ref[...] = jnp.concatenate(rk, axis=0).astype(I32)


def _route(logits_t, bias):
    ne, t = logits_t.shape
    tt = 1024
    return pl.pallas_call(
        _route_body,
        grid=(t // tt,),
        in_specs=[
            pl.BlockSpec((ne, tt), lambda i: (0, i)),
            pl.BlockSpec((ne, 1), lambda i: (0, 0)),
        ],
        out_specs=[
            pl.BlockSpec((TOP_K, tt), lambda i: (0, i)),
            pl.BlockSpec((TOP_K, tt), lambda i: (0, i)),
            pl.BlockSpec((TOP_K, tt), lambda i: (0, i)),
            pl.BlockSpec((ne, 128), lambda i: (0, 0)),
        ],
        out_shape=[
            jax.ShapeDtypeStruct((TOP_K, t), I32),
            jax.ShapeDtypeStruct((TOP_K, t), F32),
            jax.ShapeDtypeStruct((TOP_K, t), I32),
            jax.ShapeDtypeStruct((ne, 128), F32),
        ],
        scratch_shapes=[pltpu.VMEM((ne, 128), F32)],
        compiler_params=_cparams(("arbitrary",)),
        name="route",
    )(logits_t, bias.reshape(ne, 1))


def _slots_body(cnt_ref, eidx_ref, rank_ref, dest_ref, rows_ref, be_ref, first_ref, nxt_ref, nu_ref,
                off_smem, dchunk, sem):
    ne = cnt_ref.shape[0]
    nb = be_ref.shape[0]
    tl = eidx_ref.shape[1]
    r_ = EXPERT_ROWS
    shift = r_.bit_length() - 1
    c = pl.program_id(0)

    @pl.when(c == 0)
    def _():
        def per_expert(e, off):
            cnt = cnt_ref[e]
            n = lax.shift_right_logical(cnt + (r_ - 1), shift)
            off_smem[e] = off

            def fill(j, _):
                be_ref[off + j] = e
                first_ref[off + j] = jnp.where(j == 0, 1, 0)
                return 0
            lax.fori_loop(0, n, fill, 0)

            def pad(j, _):
                rows_ref[off * r_ + j] = 0
                return 0
            lax.fori_loop(cnt, n * r_, pad, 0)
            return off + n

        n_used = lax.fori_loop(0, ne, per_expert, 0)
        nu_ref[0] = n_used
        last_e = be_ref[n_used - 1]

        def tail(j, _):
            be_ref[j] = last_e
            first_ref[j] = 0

            def zero_row(q, _):
                rows_ref[j * r_ + q] = 0
                return 0
            lax.fori_loop(0, r_, zero_row, 0)
            return 0
        lax.fori_loop(n_used, nb, tail, 0)

        def back(i, nx):
            j = nb - 1 - i
            nxt_ref[j] = nx
            return jnp.where(first_ref[j] == 1, be_ref[j], nx)
        lax.fori_loop(0, nb, back, -1)

    ei = eidx_ref[...]
    acc = rank_ref[...]
    for e in range(ne):
        acc = acc + jnp.where(ei == e, off_smem[e] * r_, 0)
    dest_ref[...] = acc
    cp = pltpu.make_async_copy(dest_ref, dchunk, sem)
    cp.start()
    cp.wait()
    base = c * tl

    def scatter(j, _):
        for k in range(TOP_K):
            rows_ref[dchunk[k, j]] = base + j
        return 0
    lax.fori_loop(0, tl, scatter, 0)


def _slots(counts, eidx, rank, nb):
    k_, t = eidx.shape
    tl = 1024
    smem = lambda: pl.BlockSpec(memory_space=pltpu.SMEM)
    chunk = lambda: pl.BlockSpec((k_, tl), lambda i: (0, i))
    return pl.pallas_call(
        _slots_body,
        grid=(t // tl,),
        in_specs=[smem(), chunk(), chunk()],
        out_specs=[chunk(), smem(), smem(), smem(), smem(), smem()],
        out_shape=[
            jax.ShapeDtypeStruct((k_, t), I32),
            jax.ShapeDtypeStruct((nb * EXPERT_ROWS,), I32),
            jax.ShapeDtypeStruct((nb,), I32),
            jax.ShapeDtypeStruct((nb,), I32),
            jax.ShapeDtypeStruct((nb,), I32),
            jax.ShapeDtypeStruct((1,), I32),
        ],
        scratch_shapes=[
            pltpu.SMEM((counts.shape[0],), I32),
            pltpu.SMEM((k_, tl), I32),
            pltpu.SemaphoreType.DMA,
        ],
        compiler_params=_cparams(("arbitrary",)),
        name="slots",
    )(counts, eidx, rank)


EXPERT_ISSUE_GROUPS = 8
EXPERT_LOOKAHEAD = 2


def _expert_body(be_ref, first_ref, nxt_ref, nu_ref, rows_hbm, x_hbm, wg_hbm, wu_hbm, wd_hbm, y_ref,
                 idx_smem, xbuf, wg_f, wu_f, wd_f, wg_s, wu_s, wd_s, isem, gsem, wsem):
    i = pl.program_id(0)
    n_used = nu_ref[0]
    r_ = EXPERT_ROWS
    tiles = x_hbm.shape[1]
    ns = EXPERT_LOOKAHEAD + 1
    slot = lax.rem(i, ns)
    aslot = lax.rem(i + EXPERT_LOOKAHEAD, ns)

    def idx_copy(blk, s):
        return pltpu.make_async_copy(rows_hbm.at[pl.ds(blk, 1)], idx_smem.at[s], isem.at[s])

    def row_copy(s, r):
        return pltpu.make_async_copy(x_hbm.at[idx_smem[s, 0, r]], xbuf.at[s, pl.ds(r * tiles, tiles)], gsem.at[s])

    def issue_rows(s, g):
        step = r_ // EXPERT_ISSUE_GROUPS
        for r in range(g * step, (g + 1) * step):
            row_copy(s, r).start()

    def wait_rows(s):
        pltpu.make_async_copy(xbuf.at[s], xbuf.at[s], gsem.at[s]).wait()

    def weight_copies(e):
        return (pltpu.make_async_copy(wg_hbm.at[e], wg_f, wsem.at[0]),
                pltpu.make_async_copy(wu_hbm.at[e], wu_f, wsem.at[1]),
                pltpu.make_async_copy(wd_hbm.at[e], wd_f, wsem.at[2]))

    @pl.when(i == 0)
    def _():
        for cp in weight_copies(be_ref[0]):
            cp.start(priority=1)
        for b in range(EXPERT_LOOKAHEAD):
            idx_copy(b, b).start()
        for b in range(EXPERT_LOOKAHEAD):
            idx_copy(b, b).wait()

            def body(r, _, b=b):
                row_copy(b, r).start()
                return 0
            lax.fori_loop(0, r_, body, 0)
        idx_copy(EXPERT_LOOKAHEAD, EXPERT_LOOKAHEAD).start()

    @pl.when(first_ref[i] == 1)
    def _():
        for cp in weight_copies(be_ref[i]):
            cp.wait()
        wg_s[...] = wg_f[...].astype(BF16)
        wu_s[...] = wu_f[...].astype(BF16)
        wd_s[...] = wd_f[...].astype(BF16)

        @pl.when(nxt_ref[i] >= 0)
        def _():
            for cp in weight_copies(nxt_ref[i]):
                cp.start(priority=1)

    @pl.when(i < n_used)
    def _():
        idx_copy(i + EXPERT_LOOKAHEAD, aslot).wait()
        wait_rows(slot)
        lo, hi = _unpack_bf16_pair(_load_row_tiles(xbuf, (slot,), 0, r_, tiles))
        lo = lo.astype(BF16)
        hi = hi.astype(BF16)
        half = lo.shape[-1]
        issue_rows(aslot, 0)
        a = _dot(lo, wg_s[0:half, :])
        issue_rows(aslot, 1)
        a = a + _dot(hi, wg_s[half:, :])
        issue_rows(aslot, 2)
        u = _dot(lo, wu_s[0:half, :])
        issue_rows(aslot, 3)
        u = u + _dot(hi, wu_s[half:, :])
        issue_rows(aslot, 4)
        hid = (_silu(a) * u).astype(BF16)
        y_lo = _dot(hid, wd_s[:, 0:half])
        issue_rows(aslot, 5)
        y_hi = _dot(hid, wd_s[:, half:])
        issue_rows(aslot, 6)
        _store_row_tiles(y_ref, (), _pack_bf16_pair(y_lo, y_hi))
        issue_rows(aslot, 7)

        @pl.when(i + 1 < n_used)
        def _():
            idx_copy(i + EXPERT_LOOKAHEAD + 1, slot).start()

    @pl.when(jnp.logical_and(i >= n_used, i < n_used + EXPERT_LOOKAHEAD))
    def _():
        wait_rows(slot)

    @pl.when(i >= n_used)
    def _():
        y_ref[...] = jnp.zeros_like(y_ref)


def _experts(block_e, first, nxt, n_used, rows, h2p, wg, wu, wd):
    nb = block_e.shape[0]
    t, tiles, _ = h2p.shape
    ne, d, f = wg.shape
    r_ = EXPERT_ROWS
    hbm = lambda: pl.BlockSpec(memory_space=pl.ANY)
    grid_spec = pltpu.PrefetchScalarGridSpec(
        num_scalar_prefetch=4,
        grid=(nb,),
        in_specs=[hbm(), hbm(), hbm(), hbm(), hbm()],
        out_specs=pl.BlockSpec((r_ * tiles, 128), lambda i, *_: (i, 0)),
        scratch_shapes=[
            pltpu.SMEM((EXPERT_LOOKAHEAD + 1, 1, r_), I32),
            pltpu.VMEM((EXPERT_LOOKAHEAD + 1, r_ * tiles, 128), U32),
            pltpu.VMEM((d, f), F32),
            pltpu.VMEM((d, f), F32),
            pltpu.VMEM((f, d), F32),
            pltpu.VMEM((d, f), BF16),
            pltpu.VMEM((d, f), BF16),
            pltpu.VMEM((f, d), BF16),
            pltpu.SemaphoreType.DMA((EXPERT_LOOKAHEAD + 1,)),
            pltpu.SemaphoreType.DMA((EXPERT_LOOKAHEAD + 1,)),
            pltpu.SemaphoreType.DMA((3,)),
        ],
    )
    return pl.pallas_call(
        _expert_body,
        grid_spec=grid_spec,
        out_shape=jax.ShapeDtypeStruct((nb * r_ * tiles, 128), U32),
        compiler_params=_cparams(("arbitrary",)),
        name="experts",
    )(block_e, first, nxt, n_used, rows, h2p, wg, wu, wd)


def _combine_body(dest_hbm, y_hbm, w_ref, x1_ref, h2_ref, g2_ref, sg_ref, su_ref, sd_ref, fnw_ref, o_ref,
                  idx_smem, ybuf, isem, gsem):
    i = pl.program_id(0)
    n = pl.num_programs(0)
    tm = x1_ref.shape[0]
    tiles = y_hbm.shape[1]
    slot = i % 2
    nslot = 1 - slot

    def idx_copy(blk, s):
        cols = pl.ds(pl.multiple_of(blk * tm, tm), tm)
        return pltpu.make_async_copy(dest_hbm.at[:, cols], idx_smem.at[s], isem.at[s])

    def issue_rows(s):
        def body(j, _):
            for k in range(TOP_K):
                dst = pl.ds(pl.multiple_of((k * tm + j) * tiles, tiles), tiles)
                pltpu.make_async_copy(y_hbm.at[idx_smem[s, k, j]], ybuf.at[s, dst],
                                      gsem.at[s]).start(priority=k % 2)
            return 0
        lax.fori_loop(0, tm, body, 0)

    def wait_rows(s):
        pltpu.make_async_copy(ybuf.at[s], ybuf.at[s], gsem.at[s]).wait()

    @pl.when(i == 0)
    def _():
        idx_copy(0, 0).start()
        idx_copy(0, 0).wait()
        issue_rows(0)

        @pl.when(n > 1)
        def _():
            idx_copy(1, 1).start()

    @pl.when(i + 1 < n)
    def _():
        idx_copy(i + 1, nslot).wait()
        issue_rows(nslot)

    wait_rows(slot)

    @pl.when(i + 2 < n)
    def _():
        idx_copy(i + 2, slot).start()

    w = w_ref[...]
    half = tiles * 128
    r_lo = [jnp.zeros((tm, 128), F32) for _ in range(tiles)]
    r_hi = [jnp.zeros((tm, 128), F32) for _ in range(tiles)]
    for k in range(TOP_K):
        wk = w[:, k:k + 1]
        for kc in range(tiles):
            lo, hi = _unpack_bf16_pair(ybuf[slot, pl.ds(k * tm * tiles + kc, tm, stride=tiles), :])
            r_lo[kc] = r_lo[kc] + wk * lo
            r_hi[kc] = r_hi[kc] + wk * hi
    routed = jnp.concatenate(r_lo + r_hi, axis=-1)

    x_lo, x_hi = _unpack_bf16_pair(_load_row_tiles(h2_ref, (), 0, tm, tiles))
    x_lo = x_lo.astype(BF16)
    x_hi = x_hi.astype(BF16)
    a = _dot(x_lo, sg_ref[0:half, :]) + _dot(x_hi, sg_ref[half:, :])
    u = _dot(x_lo, su_ref[0:half, :]) + _dot(x_hi, su_ref[half:, :])
    shared = _dot((_silu(a) * u).astype(BF16), sd_ref[...])

    x2 = x1_ref[...] + g2_ref[0, 0] * (routed + shared)
    ms = jnp.mean(x2 * x2, axis=-1, keepdims=True)
    o_ref[...] = x2 * lax.rsqrt(ms + EPS) * fnw_ref[...]


def _combine(dest, y_sorted, w_tok, x1, h2p, mod4, sg, su, sd, fnw, tiles_per_batch, tm):
    t, d = x1.shape
    half = d // 2
    nrow = TOP_K * tm
    res = lambda shape: pl.BlockSpec(shape, lambda i: (0,) * len(shape), pipeline_mode=pl.Buffered(1))
    return pl.pallas_call(
        _combine_body,
        grid=(t // tm,),
        in_specs=[
            pl.BlockSpec(memory_space=pl.ANY),
            pl.BlockSpec(memory_space=pl.ANY),
            pl.BlockSpec((tm, TOP_K), lambda i: (i, 0)),
            pl.BlockSpec((tm, d), lambda i: (i, 0)),
            pl.BlockSpec((tm * (half // 128), 128), lambda i: (i, 0)),
            pl.BlockSpec((1, 1, 1, d), lambda i: (i // tiles_per_batch, 5, 0, 0)),
            res(sg.shape), res(su.shape), res(sd.shape), res(fnw.shape),
        ],
        out_specs=pl.BlockSpec((tm, d), lambda i: (i, 0)),
        out_shape=jax.ShapeDtypeStruct((t, d), F32),
        scratch_shapes=[
            pltpu.SMEM((2, TOP_K, tm), I32),
            pltpu.VMEM((2, nrow * (half // 128), 128), U32),
            pltpu.SemaphoreType.DMA((2,)),
            pltpu.SemaphoreType.DMA((2,)),
        ],
        compiler_params=_cparams(("arbitrary",)),
        name="combine",
    )(dest, y_sorted, w_tok, x1, h2p, mod4, sg, su, sd, fnw)


def kernel(x, c, ctx, c_ctx, ada_w, ada_b, norm1_w, norm2_w, w_in, s5_lam_re, s5_lam_im, s5_log_dt,
           s5_b_re, s5_b_im, s5_c_re, s5_c_im, s5_d, s5_glu_w, gla_gk_up, gla_gk_b, gla_norm_w,
           w_s5_proj, w_gla_proj, w_out, router_w, router_bias, exp_w_gate, exp_w_up, exp_w_down,
           sh_w_gate, sh_w_up, sh_w_down, final_norm_w):
    depth = ada_w.shape[0]
    assert depth == 1, "single-layer block: context outputs are never consumed"
    b_, l_, d = x.shape
    lc = ctx.shape[1]
    s5w = s5_d.shape[1]
    kw = gla_gk_up.shape[-1]
    vw = w_gla_proj.shape[1]
    dk, dv = kw // GLA_HEADS, vw // GLA_HEADS
    rank2 = 2 * GLA_GATE_RANK
    li = 0

    c8 = jnp.zeros((8, d), F32).at[:b_].set(c).at[b_].set(c_ctx)
    mod4 = _ada(c8, ada_w[li], ada_b[li]).reshape(8, N_MOD, 1, d)

    cuts = [0, s5w, s5w + kw, s5w + 2 * kw, s5w + 2 * kw + vw, s5w + 2 * kw + 2 * vw]
    c_gkd = cuts[5]
    w = w_in[li]
    w_main = jnp.concatenate([w[:, :c_gkd].astype(BF16), w[:, c_gkd + rank2:].astype(BF16)], axis=1)
    w_ctx = jnp.concatenate([w[:, cuts[0]:cuts[1]].astype(BF16), w[:, cuts[2]:cuts[4]].astype(BF16)], axis=1)
    w_gkd = jnp.zeros((d, 128), F32).at[:, :rank2].set(w[:, c_gkd:c_gkd + rank2]).astype(BF16)
    nw1 = norm1_w[li].reshape(1, d)
    p_lat, gk_lat, v_lat = _inproj(x, mod4, lambda b: b, nw1, w_main, w_gkd, tm=1024, tn=1024, s5w=s5w)
    p_ctx, gk_ctx, v_ctx = _inproj(ctx, mod4, lambda b: b_, nw1, w_ctx, w_gkd, tm=lc, tn=w_ctx.shape[1] // 2,
                                   s5w=s5w)

    factors = _s5_factors(s5_lam_re[li], s5_lam_im[li], s5_log_dt[li], s5_b_re[li], s5_b_im[li],
                          s5_c_re[li], s5_c_im[li], s5_d[li])
    ys = _s5v2(v_ctx, v_lat, *factors)

    up_pad = jnp.zeros((2, 128, kw), F32)
    up_pad = up_pad.at[0, :GLA_GATE_RANK].set(gla_gk_up[li, 0]).at[1, GLA_GATE_RANK:rank2].set(gla_gk_up[li, 1])
    yg = _gla(p_lat, p_ctx, gk_lat, gk_ctx, up_pad.astype(BF16), gla_gk_b[li].reshape(2, 1, kw), dk, dv)

    rw = jnp.zeros((d, 128), F32).at[:, :N_EXPERTS].set(router_w[li])
    rw_hi = rw.astype(BF16)
    rw2 = jnp.stack([rw_hi, (rw - rw_hi.astype(F32)).astype(BF16)])
    x1, h2p, logits = _merge(
        ys, yg, p_lat, x, mod4, s5_glu_w[li].astype(BF16), w_s5_proj[li].astype(BF16),
        w_gla_proj[li].astype(BF16), w_out[li].astype(BF16), gla_norm_w[li].reshape(1, dv),
        norm2_w[li].reshape(1, d), rw2, dv, tm=256)

    t = b_ * l_
    logits_t = logits.reshape(t, 128)[:, :N_EXPERTS].T
    eidx, wts, rank, cnt = _route(logits_t, router_bias[li])
    nb = (t * TOP_K + N_EXPERTS * (EXPERT_ROWS - 1) + EXPERT_ROWS - 1) // EXPERT_ROWS + EXPERT_LOOKAHEAD
    dest, rows, block_e, first, nxt, n_used = _slots(cnt[:, 0].astype(I32), eidx, rank, nb)

    tiles = d // 256
    y_sorted = _experts(block_e, first, nxt, n_used, rows.reshape(nb, EXPERT_ROWS), h2p.reshape(t, tiles, 128),
                        exp_w_gate[li], exp_w_up[li], exp_w_down[li])

    tm_c = 256
    out = _combine(dest, y_sorted.reshape(-1, tiles, 128), wts.T, x1.reshape(t, d), h2p.reshape(t * tiles, 128), mod4,
                   sh_w_gate[li].astype(BF16), sh_w_up[li].astype(BF16), sh_w_down[li].astype(BF16),
                   final_norm_w.reshape(1, d), l_ // tm_c, tm_c)
    return out.reshape(b_, l_, d)
```

```python
import functools
import math

import jax
import jax.numpy as jnp
from jax import lax
from jax.experimental import pallas as pl
from jax.experimental.pallas import tpu as pltpu

F32 = jnp.float32
BF16 = jnp.bfloat16
U32 = jnp.uint32
I32 = jnp.int32

EPS = 1e-6
N_MOD = 6

S5_GROUP = 16
S5_STATE = 64
S5_CHUNK = 16
S5_GROUPS_PER_STEP = 8

GLA_HEADS = 4
GLA_GATE_RANK = 16
GLA_GATE_NORM = 16.0
GLA_CHUNK = 64

N_EXPERTS = 64
TOP_K = 8
N_GROUPS = 8
TOPK_GROUPS = 4
ROUTED_SCALE = 2.5
EXPERT_ROWS = 256

VMEM_LIMIT = 56 * 1024 * 1024


def _cparams(sem, vmem=VMEM_LIMIT):
    return pltpu.CompilerParams(dimension_semantics=sem, vmem_limit_bytes=vmem)


def _dot(a, b):
    return jnp.dot(a, b, preferred_element_type=F32)


def _sigmoid(x):
    return 1.0 / (1.0 + jnp.exp(-x))


def _silu(x):
    return x * _sigmoid(x)


def _pack_bf16_pair(lo, hi):
    lo_bits = lax.bitcast_convert_type(lo.astype(BF16).astype(F32), U32)
    hi_bits = lax.bitcast_convert_type(hi.astype(BF16).astype(F32), U32)
    return (lo_bits >> 16) | (hi_bits & jnp.uint32(0xFFFF0000))


def _unpack_bf16_pair(w):
    lo = lax.bitcast_convert_type(w << 16, F32)
    hi = lax.bitcast_convert_type(w & jnp.uint32(0xFFFF0000), F32)
    return lo, hi


def _store_row_tiles(ref, lead, val):
    rows, rt = val.shape[0], val.shape[1] // 128
    for kc in range(rt):
        ref[(*lead, pl.ds(kc, rows, stride=rt), slice(None))] = val[:, kc * 128:(kc + 1) * 128]


def _load_row_tiles(ref, lead, row0, rows, rt):
    return jnp.concatenate(
        [ref[(*lead, pl.ds(row0 * rt + kc, rows, stride=rt), slice(None))] for kc in range(rt)], axis=-1)


def _ada_body(c_ref, w_ref, b_ref, o_ref):
    c = c_ref[...]
    s = _silu(c).astype(BF16)
    o_ref[...] = _dot(s, w_ref[...].astype(BF16)) + b_ref[...]


def _ada(c8, ada_w, ada_b):
    d, n = ada_w.shape
    tn = 512
    return pl.pallas_call(
        _ada_body,
        grid=(n // tn,),
        in_specs=[
            pl.BlockSpec((8, d), lambda j: (0, 0)),
            pl.BlockSpec((d, tn), lambda j: (0, j)),
            pl.BlockSpec((1, tn), lambda j: (0, j)),
        ],
        out_specs=pl.BlockSpec((8, tn), lambda j: (0, j)),
        out_shape=jax.ShapeDtypeStruct((8, n), F32),
        compiler_params=_cparams(("parallel",)),
        name="ada",
    )(c8, ada_w, ada_b.reshape(1, n))


def _modulated_norm(x, nw, sh, sc):
    ms = jnp.mean(x * x, axis=-1, keepdims=True)
    y = x * lax.rsqrt(ms + EPS) * nw
    return y * (1.0 + sc) + sh


def _inproj_body(x_ref, sh_ref, sc_ref, nw_ref, w_ref, wg_ref, p_ref, g_ref, v_ref, h_scr, u_scr):
    @pl.when(pl.program_id(2) == 0)
    def _():
        h = _modulated_norm(x_ref[0], nw_ref[...], sh_ref[0, 0], sc_ref[0, 0]).astype(BF16)
        h_scr[...] = h
        g_ref[0] = _dot(h, wg_ref[...])

    r = _dot(h_scr[...], w_ref[...])
    p_ref[0] = r.astype(BF16)

    @pl.when(pl.program_id(2) == 0)
    def _():
        nch = u_scr.shape[1] // S5_CHUNK
        for c in range(u_scr.shape[0]):
            u_scr[c] = r[:, c * 128:(c + 1) * 128]
            for t in range(S5_CHUNK):
                piece = u_scr[c, pl.ds(t, nch, stride=S5_CHUNK), :]
                v_ref[c, 0, :, t * 128:(t + 1) * 128] = piece.astype(BF16)


def _inproj(x, mod4, mod_row, nw, w, wg, tm, tn, s5w):
    b_, l_, d = x.shape
    n = w.shape[1]
    assert tn >= s5w and tm % (16 * S5_CHUNK) == 0
    nblk, vw = s5w // 128, S5_CHUNK * 128
    return pl.pallas_call(
        _inproj_body,
        grid=(b_, l_ // tm, n // tn),
        in_specs=[
            pl.BlockSpec((1, tm, d), lambda b, i, j: (b, i, 0)),
            pl.BlockSpec((1, 1, 1, d), lambda b, i, j: (mod_row(b), 0, 0, 0)),
            pl.BlockSpec((1, 1, 1, d), lambda b, i, j: (mod_row(b), 1, 0, 0)),
            pl.BlockSpec((1, d), lambda b, i, j: (0, 0)),
            pl.BlockSpec((d, tn), lambda b, i, j: (0, j)),
            pl.BlockSpec((d, 128), lambda b, i, j: (0, 0)),
        ],
        out_specs=[
            pl.BlockSpec((1, tm, tn), lambda b, i, j: (b, i, j)),
            pl.BlockSpec((1, tm, 128), lambda b, i, j: (b, i, 0)),
            pl.BlockSpec((nblk, 1, tm // S5_CHUNK, vw), lambda b, i, j: (0, b, i, 0)),
        ],
        out_shape=[
            jax.ShapeDtypeStruct((b_, l_, n), BF16),
            jax.ShapeDtypeStruct((b_, l_, 128), F32),
            jax.ShapeDtypeStruct((nblk, b_, l_ // S5_CHUNK, vw), BF16),
        ],
        scratch_shapes=[pltpu.VMEM((tm, d), BF16), pltpu.VMEM((nblk, tm, 128), F32)],
        compiler_params=_cparams(("parallel", "parallel", "arbitrary")),
        name="inproj",
    )(x, mod4, mod4, nw, w, wg)


def _s5_weights(lam_re, lam_im, log_dt, b_re, b_im, c_re, c_im, d_skip):
    hp = lax.Precision.HIGHEST
    tc = S5_CHUNK
    g_, p_ = lam_re.shape[1:]
    h_ = b_re.shape[-1]
    dt = jnp.exp(log_dt)[..., None]
    ar, ai = lam_re * dt, lam_im * dt
    k = jnp.arange(tc + 1, dtype=F32)
    mag = jnp.exp(ar[..., None] * k)
    pw_re = mag * jnp.cos(ai[..., None] * k)
    pw_im = mag * jnp.sin(ai[..., None] * k)
    num_re = jnp.expm1(ar) * jnp.cos(ai) - 2.0 * jnp.sin(0.5 * ai) ** 2
    num_im = jnp.exp(ar) * jnp.sin(ai)
    den = lam_re * lam_re + lam_im * lam_im
    f_re = (num_re * lam_re + num_im * lam_im) / den
    f_im = (num_im * lam_re - num_re * lam_im) / den
    bb_re = f_re[..., None] * b_re - f_im[..., None] * b_im
    bb_im = f_re[..., None] * b_im + f_im[..., None] * b_re

    cp_re = c_re[..., None] * pw_re[:, :, None] - c_im[..., None] * pw_im[:, :, None]
    cp_im = c_re[..., None] * pw_im[:, :, None] + c_im[..., None] * pw_re[:, :, None]
    kk = (jnp.einsum('dgipt,dgpj->dgtij', cp_re, bb_re, precision=hp)
          - jnp.einsum('dgipt,dgpj->dgtij', cp_im, bb_im, precision=hp))

    s_idx = jnp.arange(tc)[:, None]
    t_idx = jnp.arange(tc)[None, :]
    lag_f = t_idx - s_idx
    lag_b = s_idx - t_idx
    kf = jnp.where((lag_f >= 0)[None, :, :, None, None], kk[0][:, jnp.clip(lag_f, 0, tc)], 0.0)
    kb = jnp.where((lag_b >= 0)[None, :, :, None, None], kk[1][:, jnp.clip(lag_b, 0, tc)], 0.0)
    dsk = d_skip.reshape(g_, h_)
    eye_t = jnp.eye(tc, dtype=F32)[None, :, :, None, None]
    eye_h = jnp.eye(h_, dtype=F32)[None, None, None]
    m_all = kf + kb + eye_t * eye_h * dsk[:, None, None, :, None]
    m_all = m_all.transpose(0, 1, 4, 2, 3).reshape(g_, tc * h_, tc * h_)

    def state_w(dr, expo):
        e_re = pw_re[dr][:, :, expo]
        e_im = pw_im[dr][:, :, expo]
        w_re = jnp.einsum('gps,gpj->gsjp', e_re, bb_re[dr], precision=hp) - jnp.einsum(
            'gps,gpj->gsjp', e_im, bb_im[dr], precision=hp)
        w_im = jnp.einsum('gps,gpj->gsjp', e_re, bb_im[dr], precision=hp) + jnp.einsum(
            'gps,gpj->gsjp', e_im, bb_re[dr], precision=hp)
        w_re = w_re.reshape(g_, tc * h_, p_)
        w_im = w_im.reshape(g_, tc * h_, p_)
        return jnp.concatenate([w_re, w_im, w_im, w_re], axis=-1)

    ws_f = state_w(0, tc - 1 - jnp.arange(tc))
    ws_b = state_w(1, jnp.arange(tc))
    wcat = jnp.concatenate([m_all, ws_f, ws_b], axis=-1)

    def out_w(dr, expo):
        q_re = cp_re[dr][..., expo]
        q_im = cp_im[dr][..., expo]
        top = q_re.transpose(0, 2, 3, 1).reshape(g_, p_, tc * h_)
        bot = (-q_im).transpose(0, 2, 3, 1).reshape(g_, p_, tc * h_)
        return jnp.concatenate([top, bot], axis=1)

    wout = jnp.concatenate([out_w(0, 1 + jnp.arange(tc)), out_w(1, tc - jnp.arange(tc))], axis=1)

    a_re, a_im = pw_re[..., tc], pw_im[..., tc]
    zeros = jnp.zeros_like(a_re[0])
    rows = []
    for dr in range(2):
        rows += [jnp.concatenate([a_re[dr], a_re[dr]], -1),
                 jnp.concatenate([-a_im[dr], a_im[dr]], -1),
                 jnp.concatenate([a_im[dr], -a_im[dr]], -1)]
    rows += [jnp.concatenate([zeros, zeros], -1)] * 2
    acoef = jnp.stack(rows, axis=1)
    return wcat.astype(BF16), wout.astype(BF16), acoef


def _s5_body(x_ref, wcat_ref, wout_ref, a_ref, y_ref, y_scr, s_scr, h_scr, *, n_ctx, n_lat, nb):
    gb = x_ref.shape[0]
    pw = 2 * S5_STATE
    for g in range(gb):
        r = _dot(x_ref[g], wcat_ref[g])
        y_scr[g] = r[:, : y_scr.shape[-1]]
        s_scr[g] = r[:, y_scr.shape[-1]:]

    a1f, a2f, a2sf = a_ref[:, 0:1, :], a_ref[:, 1:2, :], a_ref[:, 2:3, :]
    a1b, a2b, a2sb = a_ref[:, 3:4, :], a_ref[:, 4:5, :], a_ref[:, 5:6, :]

    first = lax.broadcasted_iota(I32, (gb, 2 * nb, pw), 1) < nb

    def both(v):
        r = pltpu.roll(v, nb, 1)
        return jnp.where(first, v, r), jnp.where(first, r, v)

    def step(pf, pb, carry):
        hf, hfs, hb, hbs = carry
        rf = pl.ds(pl.multiple_of(pf * 2 * nb, 2 * nb), 2 * nb)
        rb = pl.ds(pl.multiple_of(pb * 2 * nb, 2 * nb), 2 * nb)
        sfa, sfb = both(s_scr[:, rf, 0:pw])
        sfsa, sfsb = both(s_scr[:, rf, pw:2 * pw])
        sba, sbb = both(s_scr[:, rb, 2 * pw:3 * pw])
        sbsa, sbsb = both(s_scr[:, rb, 3 * pw:4 * pw])
        hf1 = a1f * hf + a2f * hfs + sfa
        hfs1 = a1f * hfs + a2sf * hf + sfsa
        h_scr[:, rf, 0:pw] = jnp.where(first, hf, hf1)
        hf2 = a1f * hf1 + a2f * hfs1 + sfb
        hfs2 = a1f * hfs1 + a2sf * hf1 + sfsb
        hb1 = a1b * hb + a2b * hbs + sbb
        hbs1 = a1b * hbs + a2sb * hb + sbsb
        h_scr[:, rb, pw:2 * pw] = jnp.where(first, hb1, hb)
        hb2 = a1b * hb1 + a2b * hbs1 + sba
        hbs2 = a1b * hbs1 + a2sb * hb1 + sbsa
        return hf2, hfs2, hb2, hbs2

    z = jnp.zeros((gb, 2 * nb, pw), F32)
    carry = (z, z, z, z)
    pc, pl_ = n_ctx // 2, n_lat // 2
    carry = lax.fori_loop(0, pc, lambda i, c: step(i, pc - 1 - i, c), carry)
    carry = lax.fori_loop(0, pl_, lambda i, c: step(pc + i, pc + pl_ - 1 - i, c), carry)

    lo = n_ctx * nb
    for g in range(gb):
        y = y_scr[g] + _dot(h_scr[g].astype(BF16), wout_ref[g])
        y_ref[g] = y[lo:, :]


def _s5(x2, wcat, wout, acoef, n_ctx, n_lat, nb):
    g_, rows, kdim = x2.shape
    gb = S5_GROUPS_PER_STEP
    pw = 2 * S5_STATE
    out_rows = n_lat * nb
    return pl.pallas_call(
        functools.partial(_s5_body, n_ctx=n_ctx, n_lat=n_lat, nb=nb),
        grid=(g_ // gb,),
        in_specs=[
            pl.BlockSpec((gb, rows, kdim), lambda i: (i, 0, 0)),
            pl.BlockSpec((gb, kdim, wcat.shape[-1]), lambda i: (i, 0, 0)),
            pl.BlockSpec((gb, 2 * pw, kdim), lambda i: (i, 0, 0)),
            pl.BlockSpec((gb, 8, pw), lambda i: (i, 0, 0)),
        ],
        out_specs=pl.BlockSpec((gb, out_rows, kdim), lambda i: (i, 0, 0)),
        out_shape=jax.ShapeDtypeStruct((g_, out_rows, kdim), F32),
        scratch_shapes=[
            pltpu.VMEM((gb, rows, kdim), F32),
            pltpu.VMEM((gb, rows, 4 * pw), F32),
            pltpu.VMEM((gb, rows, 2 * pw), F32),
        ],
        compiler_params=_cparams(("parallel",)),
        name="s5",
    )(x2, wcat, wout, acoef)


def _s5_factors(lam_re, lam_im, log_dt, b_re, b_im, c_re, c_im, d_skip):
    tc = S5_CHUNK
    g_, p_ = lam_re.shape[1:]
    h_ = b_re.shape[-1]
    dt = jnp.exp(log_dt)[..., None]
    ar, ai = lam_re * dt, lam_im * dt

    def powers(k):
        k = jnp.asarray(k, F32)
        mag = jnp.exp(ar[..., None] * k)
        return mag * jnp.cos(ai[..., None] * k), mag * jnp.sin(ai[..., None] * k)

    num_re = jnp.expm1(ar) * jnp.cos(ai) - 2.0 * jnp.sin(0.5 * ai) ** 2
    num_im = jnp.exp(ar) * jnp.sin(ai)
    den = lam_re * lam_re + lam_im * lam_im
    f_re = (num_re * lam_re + num_im * lam_im) / den
    f_im = (num_im * lam_re - num_re * lam_im) / den
    bb_re = f_re[..., None] * b_re - f_im[..., None] * b_im
    bb_im = f_re[..., None] * b_im + f_im[..., None] * b_re
    t_idx = jnp.arange(tc)

    def halves(a, b):
        return jnp.concatenate([a, b], axis=-1).transpose(1, 0, 2, 3)

    def power_rows(expo_f, expo_b):
        re_f, im_f = powers(expo_f)
        re_b, im_b = powers(expo_b)
        re = jnp.stack([re_f[0], re_b[1]]).transpose(0, 1, 3, 2)
        im = jnp.stack([im_f[0], im_b[1]]).transpose(0, 1, 3, 2)
        return re, im

    e_re, e_im = power_rows(tc - 1 - t_idx, t_idx)
    bt_re, bt_im = bb_re.transpose(0, 1, 3, 2), bb_im.transpose(0, 1, 3, 2)
    ea = jnp.stack([halves(e_re, e_re), halves(-e_im, e_im)], axis=2)
    ba = jnp.stack([halves(bt_re, bt_im), halves(bt_im, bt_re)], axis=2)

    def readout_rows(expo_f, expo_b):
        p_re, p_im = power_rows(expo_f, expo_b)
        return jnp.stack([halves(p_re, -p_im), halves(-p_im, -p_re)], axis=2)

    pq = readout_rows(t_idx - (tc - 1), -t_idx)
    po = readout_rows(t_idx + 1, tc - t_idx)
    ca = jnp.stack([halves(c_re, c_re), halves(c_im, c_im)], axis=2)

    a_re, a_im = powers(jnp.array([tc]))
    a_re, a_im = a_re[..., 0], a_im[..., 0]
    zeros = jnp.zeros_like(a_re[0])
    rows = []
    for dr in range(2):
        rows += [jnp.concatenate([a_re[dr], a_re[dr]], -1),
                 jnp.concatenate([-a_im[dr], a_im[dr]], -1),
                 jnp.concatenate([a_im[dr], -a_im[dr]], -1)]
    rows += [jnp.concatenate([zeros, zeros], -1)] * 2
    acoef = jnp.stack(rows, axis=1)
    dvec = jnp.tile(d_skip.reshape(g_, h_), (1, tc)).reshape(g_, 1, tc * h_)
    return ea, ba, pq, po, ca, acoef, dvec


def _split_bf16(x):
    hi = x.astype(BF16)
    return hi, (x - hi.astype(F32)).astype(BF16)


def _s5v2_body(vc_ref, vl_ref, ea_ref, ba_ref, pq_ref, po_ref, ca_ref, a_ref, d_ref, o_ref, y_scr, s_scr, h_scr,
               *, n_ctx, n_lat, nb):
    gb = ea_ref.shape[0]
    pw = 2 * S5_STATE
    gl = S5_GROUP
    kd = S5_CHUNK * gl
    lanes = 128
    n_tot = n_ctx + n_lat
    nt = (((1,), (1,)), ((), ()))

    def outer(x_ref, y_ref, q, dr):
        z = (x_ref[q, dr, 0][:, None, :] * y_ref[q, dr, 0][None, :, :]
             + x_ref[q, dr, 1][:, None, :] * y_ref[q, dr, 1][None, :, :])
        return z.reshape(kd, pw)
    v = jnp.concatenate([p for b in range(nb) for p in (vc_ref[0, b], vl_ref[0, b])], axis=0)
    width = v.shape[1]

    r_in = lax.broadcasted_iota(I32, (width, kd), 0)
    c_in = lax.broadcasted_iota(I32, (width, kd), 1)
    tgt_in = (r_in // lanes) * gl + (r_in % gl)
    grp_in = (r_in % lanes) // gl
    r_out = lax.broadcasted_iota(I32, (kd, width), 0)
    c_out = lax.broadcasted_iota(I32, (kd, width), 1)
    tgt_out = (c_out // lanes) * gl + (c_out % gl)
    grp_out = (c_out % lanes) // gl

    rs = lax.broadcasted_iota(I32, (kd, kd), 0)
    ct = lax.broadcasted_iota(I32, (kd, kd), 1)
    fmask = ct // gl >= rs // gl
    bmask = ct // gl <= rs // gl
    diag = rs == ct

    for q in range(gb):
        sel = jnp.where(jnp.logical_and(c_in == tgt_in, grp_in == q), 1.0, 0.0).astype(BF16)
        x2 = _dot(v, sel).astype(BF16)
        mm, ws = [], []
        for dr in range(2):
            ws.append(outer(ea_ref, ba_ref, q, dr))
            a_hi, a_lo = _split_bf16(ws[dr])
            b_hi, b_lo = _split_bf16(outer(pq_ref, ca_ref, q, dr))
            mm.append(lax.dot_general(a_hi, b_hi, nt, preferred_element_type=F32)
                      + lax.dot_general(a_hi, b_lo, nt, preferred_element_type=F32)
                      + lax.dot_general(a_lo, b_hi, nt, preferred_element_type=F32))
        m = jnp.where(fmask, mm[0], 0.0) + jnp.where(bmask, mm[1], 0.0) + jnp.where(diag, d_ref[q], 0.0)
        wsf, wsb = ws
        wcat = jnp.concatenate(
            [m, wsf, pltpu.roll(wsf, S5_STATE, 1), wsb, pltpu.roll(wsb, S5_STATE, 1)], axis=1).astype(BF16)
        r = _dot(x2, wcat)
        for b in range(nb):
            rb = r[b * n_tot:(b + 1) * n_tot, :]
            for j in range(kd // lanes):
                y_scr[q, j, pl.ds(b, n_tot, stride=nb), :] = rb[:, j * lanes:(j + 1) * lanes]
            for j in range(4):
                s_scr[q, j, pl.ds(b, n_tot, stride=nb), :] = rb[:, kd + j * pw:kd + (j + 1) * pw]

    a1f, a2f, a2sf = a_ref[:, 0:1, :], a_ref[:, 1:2, :], a_ref[:, 2:3, :]
    a1b, a2b, a2sb = a_ref[:, 3:4, :], a_ref[:, 4:5, :], a_ref[:, 5:6, :]
    first = lax.broadcasted_iota(I32, (gb, 2 * nb, pw), 1) < nb

    def both(x):
        rr = pltpu.roll(x, nb, 1)
        return jnp.where(first, x, rr), jnp.where(first, rr, x)

    def step(pf, pb, carry):
        hf, hfs, hb, hbs = carry
        rf = pl.ds(pl.multiple_of(pf * 2 * nb, 2 * nb), 2 * nb)
        rb = pl.ds(pl.multiple_of(pb * 2 * nb, 2 * nb), 2 * nb)
        sfa, sfb = both(s_scr[:, 0, rf, :])
        sfsa, sfsb = both(s_scr[:, 1, rf, :])
        sba, sbb = both(s_scr[:, 2, rb, :])
        sbsa, sbsb = both(s_scr[:, 3, rb, :])
        hf1 = a1f * hf + a2f * hfs + sfa
        hfs1 = a1f * hfs + a2sf * hf + sfsa
        h_scr[:, 0, rf, :] = jnp.where(first, hf, hf1)
        hf2 = a1f * hf1 + a2f * hfs1 + sfb
        hfs2 = a1f * hfs1 + a2sf * hf1 + sfsb
        hb1 = a1b * hb + a2b * hbs + sbb
        hbs1 = a1b * hbs + a2sb * hb + sbsb
        h_scr[:, 1, rb, :] = jnp.where(first, hb1, hb)
        hb2 = a1b * hb1 + a2b * hbs1 + sba
        hbs2 = a1b * hbs1 + a2sb * hb1 + sbsa
        return hf2, hfs2, hb2, hbs2

    z = jnp.zeros((gb, 2 * nb, pw), F32)
    carry = (z, z, z, z)
    pc, pl_ = n_ctx // 2, n_lat // 2
    carry = lax.fori_loop(0, pc, lambda i, c: step(i, pc - 1 - i, c), carry)
    carry = lax.fori_loop(0, pl_, lambda i, c: step(pc + i, pc + pl_ - 1 - i, c), carry)

    lo = n_ctx * nb
    for q in range(gb):
        yq = (lax.dot_general(h_scr[q, 0].astype(BF16), outer(po_ref, ca_ref, q, 0).astype(BF16), nt,
                              preferred_element_type=F32)
              + lax.dot_general(h_scr[q, 1].astype(BF16), outer(po_ref, ca_ref, q, 1).astype(BF16), nt,
                                preferred_element_type=F32))
        for j in range(kd // lanes):
            y_scr[q, j] = y_scr[q, j] + yq[:, j * lanes:(j + 1) * lanes]
        y = jnp.concatenate(
            [jnp.concatenate([y_scr[q, j, pl.ds(lo + b, n_lat, stride=nb), :] for j in range(kd // lanes)], axis=-1)
             for b in range(nb)], axis=0)
        selt = jnp.where(jnp.logical_and(r_out == tgt_out, grp_out == q), 1.0, 0.0).astype(BF16)
        placed = _dot(y.astype(BF16), selt)
        if q == 0:
            o_ref[0] = placed
        else:
            o_ref[0] = o_ref[0] + placed


def _s5v2(v_ctx, v_lat, ea, ba, pq, po, ca, acoef, dvec):
    nblk, nb, n_ctx, width = v_ctx.shape
    n_lat = v_lat.shape[2]
    rows = (n_ctx + n_lat) * nb
    g_ = ea.shape[0]
    gb = g_ // nblk
    pw = 2 * S5_STATE
    kd = S5_CHUNK * S5_GROUP
    out_rows = n_lat * nb
    fac = lambda: pl.BlockSpec((gb, 2, 2, S5_CHUNK, pw), lambda i: (i, 0, 0, 0, 0))
    return pl.pallas_call(
        functools.partial(_s5v2_body, n_ctx=n_ctx, n_lat=n_lat, nb=nb),
        grid=(nblk,),
        in_specs=[
            pl.BlockSpec((1, nb, n_ctx, width), lambda i: (i, 0, 0, 0)),
            pl.BlockSpec((1, nb, n_lat, width), lambda i: (i, 0, 0, 0)),
            fac(), fac(), fac(), fac(), fac(),
            pl.BlockSpec((gb, 8, pw), lambda i: (i, 0, 0)),
            pl.BlockSpec((gb, 1, kd), lambda i: (i, 0, 0)),
        ],
        out_specs=pl.BlockSpec((1, out_rows, width), lambda i: (i, 0, 0)),
        out_shape=jax.ShapeDtypeStruct((nblk, out_rows, width), F32),
        scratch_shapes=[
            pltpu.VMEM((gb, kd // 128, rows, 128), F32),
            pltpu.VMEM((gb, 4, rows, pw), F32),
            pltpu.VMEM((gb, 2, rows, pw), F32),
        ],
        compiler_params=_cparams(("parallel",)),
        name="s5",
    )(v_ctx, v_lat, ea, ba, pq, po, ca, acoef, dvec)


def _log_sigmoid(z):
    return jnp.minimum(z, 0.0) - jnp.log1p(jnp.exp(-jnp.abs(z)))


def _chunk_cumsum(g, pos, reverse):
    n = g.shape[0]
    x = g
    sh = 1
    while sh < GLA_CHUNK:
        if reverse:
            x = x + jnp.where(pos < GLA_CHUNK - sh, pltpu.roll(x, n - sh, 0), 0.0)
        else:
            x = x + jnp.where(pos >= sh, pltpu.roll(x, sh, 0), 0.0)
        sh *= 2
    return x


def _gla_body(q_ref, k_ref, v_ref, kc_ref, vc_ref, gk_ref, gkc_ref, up_ref, bias_ref, o_ref,
              bc_scr, kb_scr, qb_scr, st_ref, *, scale):
    c = GLA_CHUNK
    l_ = q_ref.shape[1]
    lc = kc_ref.shape[1]
    n_lat, n_ctx = l_ // c, lc // c
    row = lax.broadcasted_iota(I32, (c, c), 0)
    col = lax.broadcasted_iota(I32, (c, c), 1)
    nt = (((1,), (1,)), ((), ()))
    tn = (((0,), (0,)), ((), ()))

    for dr in range(2):
        smask = (col <= row) if dr == 0 else (col > row)
        up = up_ref[dr]
        bias = bias_ref[dr]
        edge_tile, edge_row = (c - 8, 7) if dr == 0 else (0, 0)

        for gk_r, k_r, base, n, latent in ((gkc_ref, kc_ref, 0, lc, False), (gk_ref, k_ref, lc, l_, True)):
            z = _dot(gk_r[0].astype(BF16), up) + bias
            g = _log_sigmoid(z) * (1.0 / GLA_GATE_NORM)
            pos = lax.broadcasted_iota(I32, g.shape, 0) & (c - 1)
            bc = _chunk_cumsum(g, pos, reverse=(dr == 1))
            bc_scr[base:base + n, :] = bc
            kb_scr[base:base + n, :] = k_r[0].astype(F32) * jnp.exp(-bc)
            if latent:
                qb_scr[...] = (q_ref[0].astype(F32) * (scale * jnp.exp(bc))).astype(BF16)

        def chunk_rows(n, base):
            r0 = pl.multiple_of(n * c, c)
            tot = bc_scr[pl.ds(base + r0 + edge_tile, 8), :][edge_row:edge_row + 1, :]
            return r0, pl.ds(base + r0, c), jnp.exp(tot)

        def update_state(kb, vv, decay):
            kd = (kb * decay).astype(BF16)
            ds_t = lax.dot_general(vv, kd, tn, preferred_element_type=F32)
            st_ref[...] = decay * st_ref[...] + ds_t

        def ctx_step(i, _):
            n = i if dr == 0 else n_ctx - 1 - i
            r0, rs, decay = chunk_rows(n, 0)
            update_state(kb_scr[rs, :], vc_ref[0, pl.ds(r0, c), :], decay)
            return 0

        def lat_step(i, _):
            n = i if dr == 0 else n_lat - 1 - i
            r0, rs, decay = chunk_rows(n, lc)
            ro = pl.ds(r0, c)
            kb = kb_scr[rs, :]
            vv = v_ref[0, ro, :]
            qb = qb_scr[ro, :]
            scores = lax.dot_general(qb, kb.astype(BF16), nt, preferred_element_type=F32)
            scores = jnp.where(smask, scores, 0.0).astype(BF16)
            o = _dot(scores, vv) + lax.dot_general(qb, st_ref[...].astype(BF16), nt,
                                                   preferred_element_type=F32)
            if dr == 0:
                o_ref[0, ro, :] = o
            else:
                o_ref[0, ro, :] = o_ref[0, ro, :] + o
            update_state(kb, vv, decay)
            return 0

        st_ref[...] = jnp.zeros_like(st_ref)
        lax.fori_loop(0, n_ctx, ctx_step, 0, unroll=2)
        lax.fori_loop(0, n_lat, lat_step, 0, unroll=4)


def _gla(p_lat, p_ctx, gk_lat, gk_ctx, up_pad, bias, dk, dv):
    b_, l_, _ = p_lat.shape
    lc = p_ctx.shape[1]
    hh = GLA_HEADS
    s5w = hh * dv
    q0, k0, v0 = s5w // dk, (s5w + hh * dk) // dk, (s5w + 2 * hh * dk) // dv
    kc0, vc0 = s5w // dk, (s5w + hh * dk) // dv
    return pl.pallas_call(
        functools.partial(_gla_body, scale=dk ** -0.5),
        grid=(b_, hh),
        in_specs=[
            pl.BlockSpec((1, l_, dk), lambda b, h: (b, 0, q0 + h)),
            pl.BlockSpec((1, l_, dk), lambda b, h: (b, 0, k0 + h)),
            pl.BlockSpec((1, l_, dv), lambda b, h: (b, 0, v0 + h)),
            pl.BlockSpec((1, lc, dk), lambda b, h: (b, 0, kc0 + h)),
            pl.BlockSpec((1, lc, dv), lambda b, h: (b, 0, vc0 + h)),
            pl.BlockSpec((1, l_, 128), lambda b, h: (b, 0, 0)),
            pl.BlockSpec((1, lc, 128), lambda b, h: (b, 0, 0)),
            pl.BlockSpec((2, 128, dk), lambda b, h: (0, 0, h)),
            pl.BlockSpec((2, 1, dk), lambda b, h: (0, 0, h)),
        ],
        out_specs=pl.BlockSpec((1, l_, dv), lambda b, h: (b, 0, h)),
        out_shape=jax.ShapeDtypeStruct((b_, l_, hh * dv), F32),
        scratch_shapes=[
            pltpu.VMEM((lc + l_, dk), F32),
            pltpu.VMEM((lc + l_, dk), F32),
            pltpu.VMEM((l_, dk), BF16),
            pltpu.VMEM((dv, dk), F32),
        ],
        compiler_params=_cparams(("parallel", "parallel")),
        name="gla",
    )(p_lat, p_lat, p_lat, p_ctx, p_ctx, gk_lat, gk_ctx, up_pad, bias)


def _gelu_tanh(x):
    return 0.5 * x * (1.0 + jnp.tanh(math.sqrt(2.0 / math.pi) * (x + 0.044715 * (x * x * x))))


def _merge_body(ys_ref, yg_ref, go_ref, gs_ref, gg_ref, x_ref, g1_ref, sh2_ref, sc2_ref,
                glu_ref, ws5_ref, wgla_ref, wout_ref, gnw_ref, n2w_ref, rw_ref,
                x1_ref, h2_ref, lg_ref, ys_scr, *, dv):
    nch = ys_ref.shape[1]
    for c in range(ys_ref.shape[0]):
        for t in range(S5_CHUNK):
            ys_scr[c, pl.ds(t, nch, stride=S5_CHUNK), :] = ys_ref[c, :, t * 128:(t + 1) * 128]
    a = _gelu_tanh(jnp.concatenate([ys_scr[c] for c in range(ys_ref.shape[0])], axis=-1))
    a = a * _sigmoid(_dot(a.astype(BF16), glu_ref[...]))
    pa = _dot(a.astype(BF16), ws5_ref[...])

    yg = yg_ref[0]
    parts = []
    for h in range(GLA_HEADS):
        oh = yg[:, h * dv:(h + 1) * dv]
        ms = jnp.mean(oh * oh, axis=-1, keepdims=True)
        parts.append(oh * lax.rsqrt(ms + EPS) * gnw_ref[...])
    gl = jnp.concatenate(parts, axis=-1) * _silu(go_ref[0].astype(F32))
    pb = _dot(gl.astype(BF16), wgla_ref[...])

    m = _sigmoid(gs_ref[0].astype(F32)) * pa + _sigmoid(gg_ref[0].astype(F32)) * pb
    y = _dot(m.astype(BF16), wout_ref[...])
    x1 = x_ref[0] + g1_ref[0, 0] * y
    x1_ref[0] = x1

    h2 = _modulated_norm(x1, n2w_ref[...], sh2_ref[0, 0], sc2_ref[0, 0])
    h_hi = h2.astype(BF16)
    h_lo = (h2 - h_hi.astype(F32)).astype(BF16)
    half = h2.shape[-1] // 2
    _store_row_tiles(h2_ref, (0,), _pack_bf16_pair(h2[:, :half], h2[:, half:]))
    lg_ref[0] = _dot(h_hi, rw_ref[0]) + _dot(h_lo, rw_ref[0]) + _dot(h_hi, rw_ref[1])


def _resident(shape):
    nd = len(shape)
    return pl.BlockSpec(shape, lambda b, i: (0,) * nd, pipeline_mode=pl.Buffered(1))


def _merge(ys, yg, p_lat, x, mod4, glu_w, ws5, wgla, wout, gnw, n2w, rw2, dv, tm):
    b_, l_, d = x.shape
    nblk, _, vw = ys.shape
    w5 = nblk * 128
    nch = tm // S5_CHUNK
    go0 = (p_lat.shape[-1] - 2 * d - w5) // w5
    gs0 = (p_lat.shape[-1] - 2 * d) // d
    modspec = lambda r: pl.BlockSpec((1, 1, 1, d), lambda b, i: (b, r, 0, 0))
    return pl.pallas_call(
        functools.partial(_merge_body, dv=dv),
        grid=(b_, l_ // tm),
        in_specs=[
            pl.BlockSpec((nblk, nch, vw), lambda b, i: (0, b * (l_ // tm) + i, 0)),
            pl.BlockSpec((1, tm, w5), lambda b, i: (b, i, 0)),
            pl.BlockSpec((1, tm, w5), lambda b, i: (b, i, go0)),
            pl.BlockSpec((1, tm, d), lambda b, i: (b, i, gs0)),
            pl.BlockSpec((1, tm, d), lambda b, i: (b, i, gs0 + 1)),
            pl.BlockSpec((1, tm, d), lambda b, i: (b, i, 0)),
            modspec(2), modspec(3), modspec(4),
            _resident(glu_w.shape), _resident(ws5.shape), _resident(wgla.shape), _resident(wout.shape),
            _resident(gnw.shape), _resident(n2w.shape), _resident(rw2.shape),
        ],
        out_specs=[
            pl.BlockSpec((1, tm, d), lambda b, i: (b, i, 0)),
            pl.BlockSpec((1, tm * (d // 256), 128), lambda b, i: (b, i, 0)),
            pl.BlockSpec((1, tm, 128), lambda b, i: (b, i, 0)),
        ],
        out_shape=[
            jax.ShapeDtypeStruct((b_, l_, d), F32),
            jax.ShapeDtypeStruct((b_, l_ * (d // 256), 128), U32),
            jax.ShapeDtypeStruct((b_, l_, 128), F32),
        ],
        scratch_shapes=[pltpu.VMEM((nblk, tm, 128), F32)],
        compiler_params=_cparams(("parallel", "parallel")),
        name="merge",
    )(ys, yg, p_lat, p_lat, p_lat, x, mod4, mod4, mod4, glu_w, ws5, wgla, wout, gnw, n2w, rw2)


def _route_body(lg_ref, bias_ref, eidx_ref, w_ref, rank_ref, cnt_ref, carry_ref):
    ne, tt = lg_ref.shape
    gsz = ne // N_GROUPS
    neg = -jnp.inf

    @pl.when(pl.program_id(0) == 0)
    def _():
        carry_ref[...] = jnp.zeros_like(carry_ref)

    sc = _sigmoid(lg_ref[...])
    ch = sc + bias_ref[...]

    def first_max(v, idx, big):
        m = jnp.max(v, axis=0, keepdims=True)
        i = jnp.min(jnp.where(v == m, idx, big), axis=0, keepdims=True)
        return m, i

    midx = lax.broadcasted_iota(I32, (gsz, tt), 0).astype(F32)
    gs_rows = []
    for g in range(N_GROUPS):
        v = ch[g * gsz:(g + 1) * gsz, :]
        m1, i1 = first_max(v, midx, float(gsz))
        m2 = jnp.max(jnp.where(midx == i1, neg, v), axis=0, keepdims=True)
        gs_rows.append(m1 + m2)
    gscore = jnp.concatenate(gs_rows, axis=0)

    gidx = lax.broadcasted_iota(I32, (N_GROUPS, tt), 0).astype(F32)
    gsel = jnp.zeros((N_GROUPS, tt), F32)
    cur = gscore
    for _ in range(TOPK_GROUPS):
        _, i = first_max(cur, gidx, float(N_GROUPS))
        hit = gidx == i
        gsel = jnp.where(hit, 1.0, gsel)
        cur = jnp.where(hit, neg, cur)

    masked = jnp.concatenate(
        [jnp.where(gsel[g:g + 1, :] > 0.0, ch[g * gsz:(g + 1) * gsz, :], neg) for g in range(N_GROUPS)], axis=0)

    eiota = lax.broadcasted_iota(I32, (ne, tt), 0).astype(F32)
    sel = jnp.zeros((ne, tt), F32)
    cur = masked
    idx_rows, s_rows = [], []
    for _ in range(TOP_K):
        _, i = first_max(cur, eiota, float(ne))
        hit = eiota == i
        idx_rows.append(i)
        s_rows.append(jnp.sum(jnp.where(hit, sc, 0.0), axis=0, keepdims=True))
        sel = jnp.where(hit, 1.0, sel)
        cur = jnp.where(hit, neg, cur)
    idx = jnp.concatenate(idx_rows, axis=0)
    s = jnp.concatenate(s_rows, axis=0)
    w_ref[...] = s / jnp.sum(s, axis=0, keepdims=True) * ROUTED_SCALE
    eidx_ref[...] = idx.astype(I32)

    cw = 256
    a_i = lax.broadcasted_iota(I32, (cw, cw), 0)
    b_i = lax.broadcasted_iota(I32, (cw, cw), 1)
    upper = jnp.where(a_i < b_i, 1.0, 0.0).astype(BF16)
    carry = carry_ref[:, 0:1]
    pieces = []
    for c0 in range(0, tt, cw):
        sc_c = sel[:, c0:c0 + cw]
        pieces.append(_dot(sc_c.astype(BF16), upper) + carry)
        carry = carry + jnp.sum(sc_c, axis=1, keepdims=True)
    rank_full = jnp.concatenate(pieces, axis=1)
    carry_ref[...] = jnp.broadcast_to(carry, carry_ref.shape)
    cnt_ref[...] = jnp.broadcast_to(carry, cnt_ref.shape)
    rk = [jnp.sum(jnp.where(eiota == idx_rows[k], rank_full, 0.0), axis=0, keepdims=True) for k in range(TOP_K)]
    rank_ref[...] = jnp.concatenate(rk, axis=0).astype(I32)


def _route(logits_t, bias):
    ne, t = logits_t.shape
    tt = 1024
    return pl.pallas_call(
        _route_body,
        grid=(t // tt,),
        in_specs=[
            pl.BlockSpec((ne, tt), lambda i: (0, i)),
            pl.BlockSpec((ne, 1), lambda i: (0, 0)),
        ],
        out_specs=[
            pl.BlockSpec((TOP_K, tt), lambda i: (0, i)),
            pl.BlockSpec((TOP_K, tt), lambda i: (0, i)),
            pl.BlockSpec((TOP_K, tt), lambda i: (0, i)),
            pl.BlockSpec((ne, 128), lambda i: (0, 0)),
        ],
        out_shape=[
            jax.ShapeDtypeStruct((TOP_K, t), I32),
            jax.ShapeDtypeStruct((TOP_K, t), F32),
            jax.ShapeDtypeStruct((TOP_K, t), I32),
            jax.ShapeDtypeStruct((ne, 128), F32),
        ],
        scratch_shapes=[pltpu.VMEM((ne, 128), F32)],
        compiler_params=_cparams(("arbitrary",)),
        name="route",
    )(logits_t, bias.reshape(ne, 1))


def _slots_body(cnt_ref, eidx_ref, rank_ref, dest_ref, rows_ref, be_ref, first_ref, nxt_ref, nu_ref,
                off_smem, dchunk, sem):
    ne = cnt_ref.shape[0]
    nb = be_ref.shape[0]
    tl = eidx_ref.shape[1]
    r_ = EXPERT_ROWS
    shift = r_.bit_length() - 1
    c = pl.program_id(0)

    @pl.when(c == 0)
    def _():
        def per_expert(e, off):
            cnt = cnt_ref[e]
            n = lax.shift_right_logical(cnt + (r_ - 1), shift)
            off_smem[e] = off

            def fill(j, _):
                be_ref[off + j] = e
                first_ref[off + j] = jnp.where(j == 0, 1, 0)
                return 0
            lax.fori_loop(0, n, fill, 0)

            def pad(j, _):
                rows_ref[off * r_ + j] = 0
                return 0
            lax.fori_loop(cnt, n * r_, pad, 0)
            return off + n

        n_used = lax.fori_loop(0, ne, per_expert, 0)
        nu_ref[0] = n_used
        last_e = be_ref[n_used - 1]

        def tail(j, _):
            be_ref[j] = last_e
            first_ref[j] = 0

            def zero_row(q, _):
                rows_ref[j * r_ + q] = 0
                return 0
            lax.fori_loop(0, r_, zero_row, 0)
            return 0
        lax.fori_loop(n_used, nb, tail, 0)

        def back(i, nx):
            j = nb - 1 - i
            nxt_ref[j] = nx
            return jnp.where(first_ref[j] == 1, be_ref[j], nx)
        lax.fori_loop(0, nb, back, -1)

    ei = eidx_ref[...]
    acc = rank_ref[...]
    for e in range(ne):
        acc = acc + jnp.where(ei == e, off_smem[e] * r_, 0)
    dest_ref[...] = acc
    cp = pltpu.make_async_copy(dest_ref, dchunk, sem)
    cp.start()
    cp.wait()
    base = c * tl

    def scatter(j, _):
        for k in range(TOP_K):
            rows_ref[dchunk[k, j]] = base + j
        return 0
    lax.fori_loop(0, tl, scatter, 0)


def _slots(counts, eidx, rank, nb):
    k_, t = eidx.shape
    tl = 1024
    smem = lambda: pl.BlockSpec(memory_space=pltpu.SMEM)
    chunk = lambda: pl.BlockSpec((k_, tl), lambda i: (0, i))
    return pl.pallas_call(
        _slots_body,
        grid=(t // tl,),
        in_specs=[smem(), chunk(), chunk()],
        out_specs=[chunk(), smem(), smem(), smem(), smem(), smem()],
        out_shape=[
            jax.ShapeDtypeStruct((k_, t), I32),
            jax.ShapeDtypeStruct((nb * EXPERT_ROWS,), I32),
            jax.ShapeDtypeStruct((nb,), I32),
            jax.ShapeDtypeStruct((nb,), I32),
            jax.ShapeDtypeStruct((nb,), I32),
            jax.ShapeDtypeStruct((1,), I32),
        ],
        scratch_shapes=[
            pltpu.SMEM((counts.shape[0],), I32),
            pltpu.SMEM((k_, tl), I32),
            pltpu.SemaphoreType.DMA,
        ],
        compiler_params=_cparams(("arbitrary",)),
        name="slots",
    )(counts, eidx, rank)


EXPERT_ISSUE_GROUPS = 8
EXPERT_LOOKAHEAD = 2


def _expert_body(be_ref, first_ref, nxt_ref, nu_ref, rows_hbm, x_hbm, wg_hbm, wu_hbm, wd_hbm, y_ref,
                 idx_smem, xbuf, wg_f, wu_f, wd_f, wg_s, wu_s, wd_s, isem, gsem, wsem):
    i = pl.program_id(0)
    n_used = nu_ref[0]
    r_ = EXPERT_ROWS
    tiles = x_hbm.shape[1]
    ns = EXPERT_LOOKAHEAD + 1
    slot = lax.rem(i, ns)
    aslot = lax.rem(i + EXPERT_LOOKAHEAD, ns)

    def idx_copy(blk, s):
        return pltpu.make_async_copy(rows_hbm.at[pl.ds(blk, 1)], idx_smem.at[s], isem.at[s])

    def row_copy(s, r):
        return pltpu.make_async_copy(x_hbm.at[idx_smem[s, 0, r]], xbuf.at[s, pl.ds(r * tiles, tiles)], gsem.at[s])

    def issue_rows(s, g):
        step = r_ // EXPERT_ISSUE_GROUPS
        for r in range(g * step, (g + 1) * step):
            row_copy(s, r).start()

    def wait_rows(s):
        pltpu.make_async_copy(xbuf.at[s], xbuf.at[s], gsem.at[s]).wait()

    def weight_copies(e):
        return (pltpu.make_async_copy(wg_hbm.at[e], wg_f, wsem.at[0]),
                pltpu.make_async_copy(wu_hbm.at[e], wu_f, wsem.at[1]),
                pltpu.make_async_copy(wd_hbm.at[e], wd_f, wsem.at[2]))

    @pl.when(i == 0)
    def _():
        for cp in weight_copies(be_ref[0]):
            cp.start(priority=1)
        for b in range(EXPERT_LOOKAHEAD):
            idx_copy(b, b).start()
        for b in range(EXPERT_LOOKAHEAD):
            idx_copy(b, b).wait()

            def body(r, _, b=b):
                row_copy(b, r).start()
                return 0
            lax.fori_loop(0, r_, body, 0)
        idx_copy(EXPERT_LOOKAHEAD, EXPERT_LOOKAHEAD).start()

    @pl.when(first_ref[i] == 1)
    def _():
        for cp in weight_copies(be_ref[i]):
            cp.wait()
        wg_s[...] = wg_f[...].astype(BF16)
        wu_s[...] = wu_f[...].astype(BF16)
        wd_s[...] = wd_f[...].astype(BF16)

        @pl.when(nxt_ref[i] >= 0)
        def _():
            for cp in weight_copies(nxt_ref[i]):
                cp.start(priority=1)

    @pl.when(i < n_used)
    def _():
        idx_copy(i + EXPERT_LOOKAHEAD, aslot).wait()
        for g in range(EXPERT_ISSUE_GROUPS):
            issue_rows(aslot, g)
        wait_rows(slot)
        lo, hi = _unpack_bf16_pair(_load_row_tiles(xbuf, (slot,), 0, r_, tiles))
        lo = lo.astype(BF16)
        hi = hi.astype(BF16)
        half = lo.shape[-1]
        a = _dot(lo, wg_s[0:half, :]) + _dot(hi, wg_s[half:, :])
        u = _dot(lo, wu_s[0:half, :]) + _dot(hi, wu_s[half:, :])
        hid = (_silu(a) * u).astype(BF16)
        y_lo = _dot(hid, wd_s[:, 0:half])
        y_hi = _dot(hid, wd_s[:, half:])
        _store_row_tiles(y_ref, (), _pack_bf16_pair(y_lo, y_hi))

        @pl.when(i + 1 < n_used)
        def _():
            idx_copy(i + EXPERT_LOOKAHEAD + 1, slot).start()

    @pl.when(jnp.logical_and(i >= n_used, i < n_used + EXPERT_LOOKAHEAD))
    def _():
        wait_rows(slot)

    @pl.when(i >= n_used)
    def _():
        y_ref[...] = jnp.zeros_like(y_ref)


def _experts(block_e, first, nxt, n_used, rows, h2p, wg, wu, wd):
    nb = block_e.shape[0]
    t, tiles, _ = h2p.shape
    ne, d, f = wg.shape
    r_ = EXPERT_ROWS
    hbm = lambda: pl.BlockSpec(memory_space=pl.ANY)
    grid_spec = pltpu.PrefetchScalarGridSpec(
        num_scalar_prefetch=4,
        grid=(nb,),
        in_specs=[hbm(), hbm(), hbm(), hbm(), hbm()],
        out_specs=pl.BlockSpec((r_ * tiles, 128), lambda i, *_: (i, 0)),
        scratch_shapes=[
            pltpu.SMEM((EXPERT_LOOKAHEAD + 1, 1, r_), I32),
            pltpu.VMEM((EXPERT_LOOKAHEAD + 1, r_ * tiles, 128), U32),
            pltpu.VMEM((d, f), F32),
            pltpu.VMEM((d, f), F32),
            pltpu.VMEM((f, d), F32),
            pltpu.VMEM((d, f), BF16),
            pltpu.VMEM((d, f), BF16),
            pltpu.VMEM((f, d), BF16),
            pltpu.SemaphoreType.DMA((EXPERT_LOOKAHEAD + 1,)),
            pltpu.SemaphoreType.DMA((EXPERT_LOOKAHEAD + 1,)),
            pltpu.SemaphoreType.DMA((3,)),
        ],
    )
    return pl.pallas_call(
        _expert_body,
        grid_spec=grid_spec,
        out_shape=jax.ShapeDtypeStruct((nb * r_ * tiles, 128), U32),
        compiler_params=_cparams(("arbitrary",)),
        name="experts",
    )(block_e, first, nxt, n_used, rows, h2p, wg, wu, wd)


def _combine_body(dest_hbm, y_hbm, w_ref, x1_ref, h2_ref, g2_ref, sg_ref, su_ref, sd_ref, fnw_ref, o_ref,
                  idx_smem, ybuf, isem, gsem):
    i = pl.program_id(0)
    n = pl.num_programs(0)
    tm = x1_ref.shape[0]
    tiles = y_hbm.shape[1]
    slot = i % 2
    nslot = 1 - slot

    def idx_copy(blk, s):
        cols = pl.ds(pl.multiple_of(blk * tm, tm), tm)
        return pltpu.make_async_copy(dest_hbm.at[:, cols], idx_smem.at[s], isem.at[s])

    def row_copy(s, k, j):
        return pltpu.make_async_copy(y_hbm.at[idx_smem[s, k, j]], ybuf.at[s, pl.ds((k * tm + j) * tiles, tiles)],
                                     gsem.at[s])

    def wait_rows(s):
        pltpu.make_async_copy(ybuf.at[s], ybuf.at[s], gsem.at[s]).wait()

    def ahead(step):
        return jnp.minimum(step + 1, n - 1)

    @pl.when(i == 0)
    def _():
        idx_copy(0, 0).start()
        idx_copy(0, 0).wait()

        def body(j, _):
            for k in range(TOP_K):
                dst = pl.ds(pl.multiple_of((k * tm + j) * tiles, tiles), tiles)
                pltpu.make_async_copy(y_hbm.at[idx_smem[0, k, j]], ybuf.at[0, dst], gsem.at[0]).start()
            return 0
        lax.fori_loop(0, tm, body, 0)
        idx_copy(ahead(0), 1).start()

    idx_copy(ahead(i), nslot).wait()
    wait_rows(slot)

    w = w_ref[...]
    half = tiles * 128
    r_lo = [jnp.zeros((tm, 128), F32) for _ in range(tiles)]
    r_hi = [jnp.zeros((tm, 128), F32) for _ in range(tiles)]
    for k in range(TOP_K):
        for j in range(tm):
            row_copy(nslot, k, j).start(priority=j % 2)
        wk = w[:, k:k + 1]
        for kc in range(tiles):
            lo, hi = _unpack_bf16_pair(ybuf[slot, pl.ds(k * tm * tiles + kc, tm, stride=tiles), :])
            r_lo[kc] = r_lo[kc] + wk * lo
            r_hi[kc] = r_hi[kc] + wk * hi
    routed = jnp.concatenate(r_lo + r_hi, axis=-1)

    @pl.when(i + 1 < n)
    def _():
        idx_copy(ahead(i + 1), slot).start()

    @pl.when(i + 1 == n)
    def _():
        wait_rows(nslot)

    x_lo, x_hi = _unpack_bf16_pair(_load_row_tiles(h2_ref, (), 0, tm, tiles))
    x_lo = x_lo.astype(BF16)
    x_hi = x_hi.astype(BF16)
    a = _dot(x_lo, sg_ref[0:half, :]) + _dot(x_hi, sg_ref[half:, :])
    u = _dot(x_lo, su_ref[0:half, :]) + _dot(x_hi, su_ref[half:, :])
    shared = _dot((_silu(a) * u).astype(BF16), sd_ref[...])

    x2 = x1_ref[...] + g2_ref[0, 0] * (routed + shared)
    ms = jnp.mean(x2 * x2, axis=-1, keepdims=True)
    o_ref[...] = x2 * lax.rsqrt(ms + EPS) * fnw_ref[...]


def _combine(dest, y_sorted, w_tok, x1, h2p, mod4, sg, su, sd, fnw, tiles_per_batch, tm):
    t, d = x1.shape
    half = d // 2
    nrow = TOP_K * tm
    res = lambda shape: pl.BlockSpec(shape, lambda i: (0,) * len(shape), pipeline_mode=pl.Buffered(1))
    return pl.pallas_call(
        _combine_body,
        grid=(t // tm,),
        in_specs=[
            pl.BlockSpec(memory_space=pl.ANY),
            pl.BlockSpec(memory_space=pl.ANY),
            pl.BlockSpec((tm, TOP_K), lambda i: (i, 0)),
            pl.BlockSpec((tm, d), lambda i: (i, 0)),
            pl.BlockSpec((tm * (half // 128), 128), lambda i: (i, 0)),
            pl.BlockSpec((1, 1, 1, d), lambda i: (i // tiles_per_batch, 5, 0, 0)),
            res(sg.shape), res(su.shape), res(sd.shape), res(fnw.shape),
        ],
        out_specs=pl.BlockSpec((tm, d), lambda i: (i, 0)),
        out_shape=jax.ShapeDtypeStruct((t, d), F32),
        scratch_shapes=[
            pltpu.SMEM((2, TOP_K, tm), I32),
            pltpu.VMEM((2, nrow * (half // 128), 128), U32),
            pltpu.SemaphoreType.DMA((2,)),
            pltpu.SemaphoreType.DMA((2,)),
        ],
        compiler_params=_cparams(("arbitrary",)),
        name="combine",
    )(dest, y_sorted, w_tok, x1, h2p, mod4, sg, su, sd, fnw)


def kernel(x, c, ctx, c_ctx, ada_w, ada_b, norm1_w, norm2_w, w_in, s5_lam_re, s5_lam_im, s5_log_dt,
           s5_b_re, s5_b_im, s5_c_re, s5_c_im, s5_d, s5_glu_w, gla_gk_up, gla_gk_b, gla_norm_w,
           w_s5_proj, w_gla_proj, w_out, router_w, router_bias, exp_w_gate, exp_w_up, exp_w_down,
           sh_w_gate, sh_w_up, sh_w_down, final_norm_w):
    depth = ada_w.shape[0]
    assert depth == 1, "single-layer block: context outputs are never consumed"
    b_, l_, d = x.shape
    lc = ctx.shape[1]
    s5w = s5_d.shape[1]
    kw = gla_gk_up.shape[-1]
    vw = w_gla_proj.shape[1]
    dk, dv = kw // GLA_HEADS, vw // GLA_HEADS
    rank2 = 2 * GLA_GATE_RANK
    li = 0

    c8 = jnp.zeros((8, d), F32).at[:b_].set(c).at[b_].set(c_ctx)
    mod4 = _ada(c8, ada_w[li], ada_b[li]).reshape(8, N_MOD, 1, d)

    cuts = [0, s5w, s5w + kw, s5w + 2 * kw, s5w + 2 * kw + vw, s5w + 2 * kw + 2 * vw]
    c_gkd = cuts[5]
    w = w_in[li]
    w_main = jnp.concatenate([w[:, :c_gkd].astype(BF16), w[:, c_gkd + rank2:].astype(BF16)], axis=1)
    w_ctx = jnp.concatenate([w[:, cuts[0]:cuts[1]].astype(BF16), w[:, cuts[2]:cuts[4]].astype(BF16)], axis=1)
    w_gkd = jnp.zeros((d, 128), F32).at[:, :rank2].set(w[:, c_gkd:c_gkd + rank2]).astype(BF16)
    nw1 = norm1_w[li].reshape(1, d)
    p_lat, gk_lat, v_lat = _inproj(x, mod4, lambda b: b, nw1, w_main, w_gkd, tm=1024, tn=1024, s5w=s5w)
    p_ctx, gk_ctx, v_ctx = _inproj(ctx, mod4, lambda b: b_, nw1, w_ctx, w_gkd, tm=lc, tn=w_ctx.shape[1] // 2,
                                   s5w=s5w)

    factors = _s5_factors(s5_lam_re[li], s5_lam_im[li], s5_log_dt[li], s5_b_re[li], s5_b_im[li],
                          s5_c_re[li], s5_c_im[li], s5_d[li])
    ys = _s5v2(v_ctx, v_lat, *factors)

    up_pad = jnp.zeros((2, 128, kw), F32)
    up_pad = up_pad.at[0, :GLA_GATE_RANK].set(gla_gk_up[li, 0]).at[1, GLA_GATE_RANK:rank2].set(gla_gk_up[li, 1])
    yg = _gla(p_lat, p_ctx, gk_lat, gk_ctx, up_pad.astype(BF16), gla_gk_b[li].reshape(2, 1, kw), dk, dv)

    rw = jnp.zeros((d, 128), F32).at[:, :N_EXPERTS].set(router_w[li])
    rw_hi = rw.astype(BF16)
    rw2 = jnp.stack([rw_hi, (rw - rw_hi.astype(F32)).astype(BF16)])
    x1, h2p, logits = _merge(
        ys, yg, p_lat, x, mod4, s5_glu_w[li].astype(BF16), w_s5_proj[li].astype(BF16),
        w_gla_proj[li].astype(BF16), w_out[li].astype(BF16), gla_norm_w[li].reshape(1, dv),
        norm2_w[li].reshape(1, d), rw2, dv, tm=256)

    t = b_ * l_
    logits_t = logits.reshape(t, 128)[:, :N_EXPERTS].T
    eidx, wts, rank, cnt = _route(logits_t, router_bias[li])
    nb = (t * TOP_K + N_EXPERTS * (EXPERT_ROWS - 1) + EXPERT_ROWS - 1) // EXPERT_ROWS + EXPERT_LOOKAHEAD
    dest, rows, block_e, first, nxt, n_used = _slots(cnt[:, 0].astype(I32), eidx, rank, nb)

    tiles = d // 256
    y_sorted = _experts(block_e, first, nxt, n_used, rows.reshape(nb, EXPERT_ROWS), h2p.reshape(t, tiles, 128),
                        exp_w_gate[li], exp_w_up[li], exp_w_down[li])

    tm_c = 128
    out = _combine(dest, y_sorted.reshape(-1, tiles, 128), wts.T, x1.reshape(t, d), h2p.reshape(t * tiles, 128), mod4,
                   sh_w_gate[li].astype(BF16), sh_w_up[li].astype(BF16), sh_w_down[li].astype(BF16),
                   final_norm_w.reshape(1, d), l_ // tm_c, tm_c)
    return out.reshape(b_, l_, d)
```

```python
import functools
import math

import jax
import jax.numpy as jnp
from jax import lax
from jax.experimental import pallas as pl
from jax.experimental.pallas import tpu as pltpu

F32 = jnp.float32
BF16 = jnp.bfloat16
U32 = jnp.uint32
I32 = jnp.int32

EPS = 1e-6
N_MOD = 6

S5_GROUP = 16
S5_STATE = 64
S5_CHUNK = 16
S5_GROUPS_PER_STEP = 8

GLA_HEADS = 4
GLA_GATE_RANK = 16
GLA_GATE_NORM = 16.0
GLA_CHUNK = 64

N_EXPERTS = 64
TOP_K = 8
N_GROUPS = 8
TOPK_GROUPS = 4
ROUTED_SCALE = 2.5
EXPERT_ROWS = 256

VMEM_LIMIT = 56 * 1024 * 1024


def _cparams(sem, vmem=VMEM_LIMIT):
    return pltpu.CompilerParams(dimension_semantics=sem, vmem_limit_bytes=vmem)


def _dot(a, b):
    return jnp.dot(a, b, preferred_element_type=F32)


def _sigmoid(x):
    return 1.0 / (1.0 + jnp.exp(-x))


def _silu(x):
    return x * _sigmoid(x)


def _pack_bf16_pair(lo, hi):
    lo_bits = lax.bitcast_convert_type(lo.astype(BF16).astype(F32), U32)
    hi_bits = lax.bitcast_convert_type(hi.astype(BF16).astype(F32), U32)
    return (lo_bits >> 16) | (hi_bits & jnp.uint32(0xFFFF0000))


def _unpack_bf16_pair(w):
    lo = lax.bitcast_convert_type(w << 16, F32)
    hi = lax.bitcast_convert_type(w & jnp.uint32(0xFFFF0000), F32)
    return lo, hi


def _store_row_tiles(ref, lead, val):
    rows, rt = val.shape[0], val.shape[1] // 128
    for kc in range(rt):
        ref[(*lead, pl.ds(kc, rows, stride=rt), slice(None))] = val[:, kc * 128:(kc + 1) * 128]


def _load_row_tiles(ref, lead, row0, rows, rt):
    return jnp.concatenate(
        [ref[(*lead, pl.ds(row0 * rt + kc, rows, stride=rt), slice(None))] for kc in range(rt)], axis=-1)


def _ada_body(c_ref, w_ref, b_ref, o_ref):
    c = c_ref[...]
    s = _silu(c).astype(BF16)
    o_ref[...] = _dot(s, w_ref[...].astype(BF16)) + b_ref[...]


def _ada(c8, ada_w, ada_b):
    d, n = ada_w.shape
    tn = 512
    return pl.pallas_call(
        _ada_body,
        grid=(n // tn,),
        in_specs=[
            pl.BlockSpec((8, d), lambda j: (0, 0)),
            pl.BlockSpec((d, tn), lambda j: (0, j)),
            pl.BlockSpec((1, tn), lambda j: (0, j)),
        ],
        out_specs=pl.BlockSpec((8, tn), lambda j: (0, j)),
        out_shape=jax.ShapeDtypeStruct((8, n), F32),
        compiler_params=_cparams(("parallel",)),
        name="ada",
    )(c8, ada_w, ada_b.reshape(1, n))


def _modulated_norm(x, nw, sh, sc):
    ms = jnp.mean(x * x, axis=-1, keepdims=True)
    y = x * lax.rsqrt(ms + EPS) * nw
    return y * (1.0 + sc) + sh


def _inproj_body(x_ref, sh_ref, sc_ref, nw_ref, w_ref, wg_ref, p_ref, g_ref, v_ref, h_scr, u_scr):
    @pl.when(pl.program_id(2) == 0)
    def _():
        h = _modulated_norm(x_ref[0], nw_ref[...], sh_ref[0, 0], sc_ref[0, 0]).astype(BF16)
        h_scr[...] = h
        g_ref[0] = _dot(h, wg_ref[...])

    r = _dot(h_scr[...], w_ref[...])
    p_ref[0] = r.astype(BF16)

    @pl.when(pl.program_id(2) == 0)
    def _():
        nch = u_scr.shape[1] // S5_CHUNK
        for c in range(u_scr.shape[0]):
            u_scr[c] = r[:, c * 128:(c + 1) * 128]
            for t in range(S5_CHUNK):
                piece = u_scr[c, pl.ds(t, nch, stride=S5_CHUNK), :]
                v_ref[c, 0, :, t * 128:(t + 1) * 128] = piece.astype(BF16)


def _inproj(x, mod4, mod_row, nw, w, wg, tm, tn, s5w):
    b_, l_, d = x.shape
    n = w.shape[1]
    assert tn >= s5w and tm % (16 * S5_CHUNK) == 0
    nblk, vw = s5w // 128, S5_CHUNK * 128
    return pl.pallas_call(
        _inproj_body,
        grid=(b_, l_ // tm, n // tn),
        in_specs=[
            pl.BlockSpec((1, tm, d), lambda b, i, j: (b, i, 0)),
            pl.BlockSpec((1, 1, 1, d), lambda b, i, j: (mod_row(b), 0, 0, 0)),
            pl.BlockSpec((1, 1, 1, d), lambda b, i, j: (mod_row(b), 1, 0, 0)),
            pl.BlockSpec((1, d), lambda b, i, j: (0, 0)),
            pl.BlockSpec((d, tn), lambda b, i, j: (0, j)),
            pl.BlockSpec((d, 128), lambda b, i, j: (0, 0)),
        ],
        out_specs=[
            pl.BlockSpec((1, tm, tn), lambda b, i, j: (b, i, j)),
            pl.BlockSpec((1, tm, 128), lambda b, i, j: (b, i, 0)),
            pl.BlockSpec((nblk, 1, tm // S5_CHUNK, vw), lambda b, i, j: (0, b, i, 0)),
        ],
        out_shape=[
            jax.ShapeDtypeStruct((b_, l_, n), BF16),
            jax.ShapeDtypeStruct((b_, l_, 128), F32),
            jax.ShapeDtypeStruct((nblk, b_, l_ // S5_CHUNK, vw), BF16),
        ],
        scratch_shapes=[pltpu.VMEM((tm, d), BF16), pltpu.VMEM((nblk, tm, 128), F32)],
        compiler_params=_cparams(("parallel", "parallel", "arbitrary")),
        name="inproj",
    )(x, mod4, mod4, nw, w, wg)


def _s5_weights(lam_re, lam_im, log_dt, b_re, b_im, c_re, c_im, d_skip):
    hp = lax.Precision.HIGHEST
    tc = S5_CHUNK
    g_, p_ = lam_re.shape[1:]
    h_ = b_re.shape[-1]
    dt = jnp.exp(log_dt)[..., None]
    ar, ai = lam_re * dt, lam_im * dt
    k = jnp.arange(tc + 1, dtype=F32)
    mag = jnp.exp(ar[..., None] * k)
    pw_re = mag * jnp.cos(ai[..., None] * k)
    pw_im = mag * jnp.sin(ai[..., None] * k)
    num_re = jnp.expm1(ar) * jnp.cos(ai) - 2.0 * jnp.sin(0.5 * ai) ** 2
    num_im = jnp.exp(ar) * jnp.sin(ai)
    den = lam_re * lam_re + lam_im * lam_im
    f_re = (num_re * lam_re + num_im * lam_im) / den
    f_im = (num_im * lam_re - num_re * lam_im) / den
    bb_re = f_re[..., None] * b_re - f_im[..., None] * b_im
    bb_im = f_re[..., None] * b_im + f_im[..., None] * b_re

    cp_re = c_re[..., None] * pw_re[:, :, None] - c_im[..., None] * pw_im[:, :, None]
    cp_im = c_re[..., None] * pw_im[:, :, None] + c_im[..., None] * pw_re[:, :, None]
    kk = (jnp.einsum('dgipt,dgpj->dgtij', cp_re, bb_re, precision=hp)
          - jnp.einsum('dgipt,dgpj->dgtij', cp_im, bb_im, precision=hp))

    s_idx = jnp.arange(tc)[:, None]
    t_idx = jnp.arange(tc)[None, :]
    lag_f = t_idx - s_idx
    lag_b = s_idx - t_idx
    kf = jnp.where((lag_f >= 0)[None, :, :, None, None], kk[0][:, jnp.clip(lag_f, 0, tc)], 0.0)
    kb = jnp.where((lag_b >= 0)[None, :, :, None, None], kk[1][:, jnp.clip(lag_b, 0, tc)], 0.0)
    dsk = d_skip.reshape(g_, h_)
    eye_t = jnp.eye(tc, dtype=F32)[None, :, :, None, None]
    eye_h = jnp.eye(h_, dtype=F32)[None, None, None]
    m_all = kf + kb + eye_t * eye_h * dsk[:, None, None, :, None]
    m_all = m_all.transpose(0, 1, 4, 2, 3).reshape(g_, tc * h_, tc * h_)

    def state_w(dr, expo):
        e_re = pw_re[dr][:, :, expo]
        e_im = pw_im[dr][:, :, expo]
        w_re = jnp.einsum('gps,gpj->gsjp', e_re, bb_re[dr], precision=hp) - jnp.einsum(
            'gps,gpj->gsjp', e_im, bb_im[dr], precision=hp)
        w_im = jnp.einsum('gps,gpj->gsjp', e_re, bb_im[dr], precision=hp) + jnp.einsum(
            'gps,gpj->gsjp', e_im, bb_re[dr], precision=hp)
        w_re = w_re.reshape(g_, tc * h_, p_)
        w_im = w_im.reshape(g_, tc * h_, p_)
        return jnp.concatenate([w_re, w_im, w_im, w_re], axis=-1)

    ws_f = state_w(0, tc - 1 - jnp.arange(tc))
    ws_b = state_w(1, jnp.arange(tc))
    wcat = jnp.concatenate([m_all, ws_f, ws_b], axis=-1)

    def out_w(dr, expo):
        q_re = cp_re[dr][..., expo]
        q_im = cp_im[dr][..., expo]
        top = q_re.transpose(0, 2, 3, 1).reshape(g_, p_, tc * h_)
        bot = (-q_im).transpose(0, 2, 3, 1).reshape(g_, p_, tc * h_)
        return jnp.concatenate([top, bot], axis=1)

    wout = jnp.concatenate([out_w(0, 1 + jnp.arange(tc)), out_w(1, tc - jnp.arange(tc))], axis=1)

    a_re, a_im = pw_re[..., tc], pw_im[..., tc]
    zeros = jnp.zeros_like(a_re[0])
    rows = []
    for dr in range(2):
        rows += [jnp.concatenate([a_re[dr], a_re[dr]], -1),
                 jnp.concatenate([-a_im[dr], a_im[dr]], -1),
                 jnp.concatenate([a_im[dr], -a_im[dr]], -1)]
    rows += [jnp.concatenate([zeros, zeros], -1)] * 2
    acoef = jnp.stack(rows, axis=1)
    return wcat.astype(BF16), wout.astype(BF16), acoef


def _s5_body(x_ref, wcat_ref, wout_ref, a_ref, y_ref, y_scr, s_scr, h_scr, *, n_ctx, n_lat, nb):
    gb = x_ref.shape[0]
    pw = 2 * S5_STATE
    for g in range(gb):
        r = _dot(x_ref[g], wcat_ref[g])
        y_scr[g] = r[:, : y_scr.shape[-1]]
        s_scr[g] = r[:, y_scr.shape[-1]:]

    a1f, a2f, a2sf = a_ref[:, 0:1, :], a_ref[:, 1:2, :], a_ref[:, 2:3, :]
    a1b, a2b, a2sb = a_ref[:, 3:4, :], a_ref[:, 4:5, :], a_ref[:, 5:6, :]

    first = lax.broadcasted_iota(I32, (gb, 2 * nb, pw), 1) < nb

    def both(v):
        r = pltpu.roll(v, nb, 1)
        return jnp.where(first, v, r), jnp.where(first, r, v)

    def step(pf, pb, carry):
        hf, hfs, hb, hbs = carry
        rf = pl.ds(pl.multiple_of(pf * 2 * nb, 2 * nb), 2 * nb)
        rb = pl.ds(pl.multiple_of(pb * 2 * nb, 2 * nb), 2 * nb)
        sfa, sfb = both(s_scr[:, rf, 0:pw])
        sfsa, sfsb = both(s_scr[:, rf, pw:2 * pw])
        sba, sbb = both(s_scr[:, rb, 2 * pw:3 * pw])
        sbsa, sbsb = both(s_scr[:, rb, 3 * pw:4 * pw])
        hf1 = a1f * hf + a2f * hfs + sfa
        hfs1 = a1f * hfs + a2sf * hf + sfsa
        h_scr[:, rf, 0:pw] = jnp.where(first, hf, hf1)
        hf2 = a1f * hf1 + a2f * hfs1 + sfb
        hfs2 = a1f * hfs1 + a2sf * hf1 + sfsb
        hb1 = a1b * hb + a2b * hbs + sbb
        hbs1 = a1b * hbs + a2sb * hb + sbsb
        h_scr[:, rb, pw:2 * pw] = jnp.where(first, hb1, hb)
        hb2 = a1b * hb1 + a2b * hbs1 + sba
        hbs2 = a1b * hbs1 + a2sb * hb1 + sbsa
        return hf2, hfs2, hb2, hbs2

    z = jnp.zeros((gb, 2 * nb, pw), F32)
    carry = (z, z, z, z)
    pc, pl_ = n_ctx // 2, n_lat // 2
    carry = lax.fori_loop(0, pc, lambda i, c: step(i, pc - 1 - i, c), carry)
    carry = lax.fori_loop(0, pl_, lambda i, c: step(pc + i, pc + pl_ - 1 - i, c), carry)

    lo = n_ctx * nb
    for g in range(gb):
        y = y_scr[g] + _dot(h_scr[g].astype(BF16), wout_ref[g])
        y_ref[g] = y[lo:, :]


def _s5(x2, wcat, wout, acoef, n_ctx, n_lat, nb):
    g_, rows, kdim = x2.shape
    gb = S5_GROUPS_PER_STEP
    pw = 2 * S5_STATE
    out_rows = n_lat * nb
    return pl.pallas_call(
        functools.partial(_s5_body, n_ctx=n_ctx, n_lat=n_lat, nb=nb),
        grid=(g_ // gb,),
        in_specs=[
            pl.BlockSpec((gb, rows, kdim), lambda i: (i, 0, 0)),
            pl.BlockSpec((gb, kdim, wcat.shape[-1]), lambda i: (i, 0, 0)),
            pl.BlockSpec((gb, 2 * pw, kdim), lambda i: (i, 0, 0)),
            pl.BlockSpec((gb, 8, pw), lambda i: (i, 0, 0)),
        ],
        out_specs=pl.BlockSpec((gb, out_rows, kdim), lambda i: (i, 0, 0)),
        out_shape=jax.ShapeDtypeStruct((g_, out_rows, kdim), F32),
        scratch_shapes=[
            pltpu.VMEM((gb, rows, kdim), F32),
            pltpu.VMEM((gb, rows, 4 * pw), F32),
            pltpu.VMEM((gb, rows, 2 * pw), F32),
        ],
        compiler_params=_cparams(("parallel",)),
        name="s5",
    )(x2, wcat, wout, acoef)


def _s5_factors(lam_re, lam_im, log_dt, b_re, b_im, c_re, c_im, d_skip):
    tc = S5_CHUNK
    g_, p_ = lam_re.shape[1:]
    h_ = b_re.shape[-1]
    dt = jnp.exp(log_dt)[..., None]
    ar, ai = lam_re * dt, lam_im * dt

    def powers(k):
        k = jnp.asarray(k, F32)
        mag = jnp.exp(ar[..., None] * k)
        return mag * jnp.cos(ai[..., None] * k), mag * jnp.sin(ai[..., None] * k)

    num_re = jnp.expm1(ar) * jnp.cos(ai) - 2.0 * jnp.sin(0.5 * ai) ** 2
    num_im = jnp.exp(ar) * jnp.sin(ai)
    den = lam_re * lam_re + lam_im * lam_im
    f_re = (num_re * lam_re + num_im * lam_im) / den
    f_im = (num_im * lam_re - num_re * lam_im) / den
    bb_re = f_re[..., None] * b_re - f_im[..., None] * b_im
    bb_im = f_re[..., None] * b_im + f_im[..., None] * b_re
    t_idx = jnp.arange(tc)

    def halves(a, b):
        return jnp.concatenate([a, b], axis=-1).transpose(1, 0, 2, 3)

    def power_rows(expo_f, expo_b):
        re_f, im_f = powers(expo_f)
        re_b, im_b = powers(expo_b)
        re = jnp.stack([re_f[0], re_b[1]]).transpose(0, 1, 3, 2)
        im = jnp.stack([im_f[0], im_b[1]]).transpose(0, 1, 3, 2)
        return re, im

    e_re, e_im = power_rows(tc - 1 - t_idx, t_idx)
    bt_re, bt_im = bb_re.transpose(0, 1, 3, 2), bb_im.transpose(0, 1, 3, 2)
    ea = jnp.stack([halves(e_re, e_re), halves(-e_im, e_im)], axis=2)
    ba = jnp.stack([halves(bt_re, bt_im), halves(bt_im, bt_re)], axis=2)

    def readout_rows(expo_f, expo_b):
        p_re, p_im = power_rows(expo_f, expo_b)
        return jnp.stack([halves(p_re, -p_im), halves(-p_im, -p_re)], axis=2)

    pq = readout_rows(t_idx - (tc - 1), -t_idx)
    po = readout_rows(t_idx + 1, tc - t_idx)
    ca = jnp.stack([halves(c_re, c_re), halves(c_im, c_im)], axis=2)

    a_re, a_im = powers(jnp.array([tc]))
    a_re, a_im = a_re[..., 0], a_im[..., 0]
    zeros = jnp.zeros_like(a_re[0])
    rows = []
    for dr in range(2):
        rows += [jnp.concatenate([a_re[dr], a_re[dr]], -1),
                 jnp.concatenate([-a_im[dr], a_im[dr]], -1),
                 jnp.concatenate([a_im[dr], -a_im[dr]], -1)]
    rows += [jnp.concatenate([zeros, zeros], -1)] * 2
    acoef = jnp.stack(rows, axis=1)
    dvec = jnp.tile(d_skip.reshape(g_, h_), (1, tc)).reshape(g_, 1, tc * h_)
    return ea, ba, pq, po, ca, acoef, dvec


def _split_bf16(x):
    hi = x.astype(BF16)
    return hi, (x - hi.astype(F32)).astype(BF16)


def _s5v2_body(vc_ref, vl_ref, ea_ref, ba_ref, pq_ref, po_ref, ca_ref, a_ref, d_ref, o_ref, y_scr, s_scr, h_scr,
               *, n_ctx, n_lat, nb):
    gb = ea_ref.shape[0]
    pw = 2 * S5_STATE
    gl = S5_GROUP
    kd = S5_CHUNK * gl
    lanes = 128
    n_tot = n_ctx + n_lat
    nt = (((1,), (1,)), ((), ()))

    def outer(x_ref, y_ref, q, dr):
        z = (x_ref[q, dr, 0][:, None, :] * y_ref[q, dr, 0][None, :, :]
             + x_ref[q, dr, 1][:, None, :] * y_ref[q, dr, 1][None, :, :])
        return z.reshape(kd, pw)
    v = jnp.concatenate([p for b in range(nb) for p in (vc_ref[0, b], vl_ref[0, b])], axis=0)
    width = v.shape[1]

    r_in = lax.broadcasted_iota(I32, (width, kd), 0)
    c_in = lax.broadcasted_iota(I32, (width, kd), 1)
    tgt_in = (r_in // lanes) * gl + (r_in % gl)
    grp_in = (r_in % lanes) // gl
    r_out = lax.broadcasted_iota(I32, (kd, width), 0)
    c_out = lax.broadcasted_iota(I32, (kd, width), 1)
    tgt_out = (c_out // lanes) * gl + (c_out % gl)
    grp_out = (c_out % lanes) // gl

    rs = lax.broadcasted_iota(I32, (kd, kd), 0)
    ct = lax.broadcasted_iota(I32, (kd, kd), 1)
    fmask = ct // gl >= rs // gl
    bmask = ct // gl <= rs // gl
    diag = rs == ct

    for q in range(gb):
        sel = jnp.where(jnp.logical_and(c_in == tgt_in, grp_in == q), 1.0, 0.0).astype(BF16)
        x2 = _dot(v, sel).astype(BF16)
        mm, ws = [], []
        for dr in range(2):
            ws.append(outer(ea_ref, ba_ref, q, dr))
            a_hi, a_lo = _split_bf16(ws[dr])
            b_hi, b_lo = _split_bf16(outer(pq_ref, ca_ref, q, dr))
            mm.append(lax.dot_general(a_hi, b_hi, nt, preferred_element_type=F32)
                      + lax.dot_general(a_hi, b_lo, nt, preferred_element_type=F32)
                      + lax.dot_general(a_lo, b_hi, nt, preferred_element_type=F32))
        m = jnp.where(fmask, mm[0], 0.0) + jnp.where(bmask, mm[1], 0.0) + jnp.where(diag, d_ref[q], 0.0)
        wsf, wsb = ws
        wcat = jnp.concatenate(
            [m, wsf, pltpu.roll(wsf, S5_STATE, 1), wsb, pltpu.roll(wsb, S5_STATE, 1)], axis=1).astype(BF16)
        r = _dot(x2, wcat)
        for b in range(nb):
            rb = r[b * n_tot:(b + 1) * n_tot, :]
            for j in range(kd // lanes):
                y_scr[q, j, pl.ds(b, n_tot, stride=nb), :] = rb[:, j * lanes:(j + 1) * lanes]
            for j in range(4):
                s_scr[q, j, pl.ds(b, n_tot, stride=nb), :] = rb[:, kd + j * pw:kd + (j + 1) * pw]

    a1f, a2f, a2sf = a_ref[:, 0:1, :], a_ref[:, 1:2, :], a_ref[:, 2:3, :]
    a1b, a2b, a2sb = a_ref[:, 3:4, :], a_ref[:, 4:5, :], a_ref[:, 5:6, :]
    first = lax.broadcasted_iota(I32, (gb, 2 * nb, pw), 1) < nb

    def both(x):
        rr = pltpu.roll(x, nb, 1)
        return jnp.where(first, x, rr), jnp.where(first, rr, x)

    def step(pf, pb, carry):
        hf, hfs, hb, hbs = carry
        rf = pl.ds(pl.multiple_of(pf * 2 * nb, 2 * nb), 2 * nb)
        rb = pl.ds(pl.multiple_of(pb * 2 * nb, 2 * nb), 2 * nb)
        sfa, sfb = both(s_scr[:, 0, rf, :])
        sfsa, sfsb = both(s_scr[:, 1, rf, :])
        sba, sbb = both(s_scr[:, 2, rb, :])
        sbsa, sbsb = both(s_scr[:, 3, rb, :])
        hf1 = a1f * hf + a2f * hfs + sfa
        hfs1 = a1f * hfs + a2sf * hf + sfsa
        h_scr[:, 0, rf, :] = jnp.where(first, hf, hf1)
        hf2 = a1f * hf1 + a2f * hfs1 + sfb
        hfs2 = a1f * hfs1 + a2sf * hf1 + sfsb
        hb1 = a1b * hb + a2b * hbs + sbb
        hbs1 = a1b * hbs + a2sb * hb + sbsb
        h_scr[:, 1, rb, :] = jnp.where(first, hb1, hb)
        hb2 = a1b * hb1 + a2b * hbs1 + sba
        hbs2 = a1b * hbs1 + a2sb * hb1 + sbsa
        return hf2, hfs2, hb2, hbs2

    z = jnp.zeros((gb, 2 * nb, pw), F32)
    carry = (z, z, z, z)
    pc, pl_ = n_ctx // 2, n_lat // 2
    carry = lax.fori_loop(0, pc, lambda i, c: step(i, pc - 1 - i, c), carry)
    carry = lax.fori_loop(0, pl_, lambda i, c: step(pc + i, pc + pl_ - 1 - i, c), carry)

    lo = n_ctx * nb
    for q in range(gb):
        yq = (lax.dot_general(h_scr[q, 0].astype(BF16), outer(po_ref, ca_ref, q, 0).astype(BF16), nt,
                              preferred_element_type=F32)
              + lax.dot_general(h_scr[q, 1].astype(BF16), outer(po_ref, ca_ref, q, 1).astype(BF16), nt,
                                preferred_element_type=F32))
        for j in range(kd // lanes):
            y_scr[q, j] = y_scr[q, j] + yq[:, j * lanes:(j + 1) * lanes]
        y = jnp.concatenate(
            [jnp.concatenate([y_scr[q, j, pl.ds(lo + b, n_lat, stride=nb), :] for j in range(kd // lanes)], axis=-1)
             for b in range(nb)], axis=0)
        selt = jnp.where(jnp.logical_and(r_out == tgt_out, grp_out == q), 1.0, 0.0).astype(BF16)
        placed = _dot(y.astype(BF16), selt)
        if q == 0:
            o_ref[0] = placed
        else:
            o_ref[0] = o_ref[0] + placed


def _s5v2(v_ctx, v_lat, ea, ba, pq, po, ca, acoef, dvec):
    nblk, nb, n_ctx, width = v_ctx.shape
    n_lat = v_lat.shape[2]
    rows = (n_ctx + n_lat) * nb
    g_ = ea.shape[0]
    gb = g_ // nblk
    pw = 2 * S5_STATE
    kd = S5_CHUNK * S5_GROUP
    out_rows = n_lat * nb
    fac = lambda: pl.BlockSpec((gb, 2, 2, S5_CHUNK, pw), lambda i: (i, 0, 0, 0, 0))
    return pl.pallas_call(
        functools.partial(_s5v2_body, n_ctx=n_ctx, n_lat=n_lat, nb=nb),
        grid=(nblk,),
        in_specs=[
            pl.BlockSpec((1, nb, n_ctx, width), lambda i: (i, 0, 0, 0)),
            pl.BlockSpec((1, nb, n_lat, width), lambda i: (i, 0, 0, 0)),
            fac(), fac(), fac(), fac(), fac(),
            pl.BlockSpec((gb, 8, pw), lambda i: (i, 0, 0)),
            pl.BlockSpec((gb, 1, kd), lambda i: (i, 0, 0)),
        ],
        out_specs=pl.BlockSpec((1, out_rows, width), lambda i: (i, 0, 0)),
        out_shape=jax.ShapeDtypeStruct((nblk, out_rows, width), F32),
        scratch_shapes=[
            pltpu.VMEM((gb, kd // 128, rows, 128), F32),
            pltpu.VMEM((gb, 4, rows, pw), F32),
            pltpu.VMEM((gb, 2, rows, pw), F32),
        ],
        compiler_params=_cparams(("parallel",)),
        name="s5",
    )(v_ctx, v_lat, ea, ba, pq, po, ca, acoef, dvec)


def _log_sigmoid(z):
    return jnp.minimum(z, 0.0) - jnp.log1p(jnp.exp(-jnp.abs(z)))


def _chunk_cumsum(g, pos, reverse):
    n = g.shape[0]
    x = g
    sh = 1
    while sh < GLA_CHUNK:
        if reverse:
            x = x + jnp.where(pos < GLA_CHUNK - sh, pltpu.roll(x, n - sh, 0), 0.0)
        else:
            x = x + jnp.where(pos >= sh, pltpu.roll(x, sh, 0), 0.0)
        sh *= 2
    return x


def _gla_body(q_ref, k_ref, v_ref, kc_ref, vc_ref, gk_ref, gkc_ref, up_ref, bias_ref, o_ref,
              bc_scr, kb_scr, qb_scr, st_ref, *, scale):
    c = GLA_CHUNK
    l_ = q_ref.shape[1]
    lc = kc_ref.shape[1]
    n_lat, n_ctx = l_ // c, lc // c
    row = lax.broadcasted_iota(I32, (c, c), 0)
    col = lax.broadcasted_iota(I32, (c, c), 1)
    nt = (((1,), (1,)), ((), ()))
    tn = (((0,), (0,)), ((), ()))

    for dr in range(2):
        smask = (col <= row) if dr == 0 else (col > row)
        up = up_ref[dr]
        bias = bias_ref[dr]
        edge_tile, edge_row = (c - 8, 7) if dr == 0 else (0, 0)

        for gk_r, k_r, base, n, latent in ((gkc_ref, kc_ref, 0, lc, False), (gk_ref, k_ref, lc, l_, True)):
            z = _dot(gk_r[0].astype(BF16), up) + bias
            g = _log_sigmoid(z) * (1.0 / GLA_GATE_NORM)
            pos = lax.broadcasted_iota(I32, g.shape, 0) & (c - 1)
            bc = _chunk_cumsum(g, pos, reverse=(dr == 1))
            bc_scr[base:base + n, :] = bc
            kb_scr[base:base + n, :] = k_r[0].astype(F32) * jnp.exp(-bc)
            if latent:
                qb_scr[...] = (q_ref[0].astype(F32) * (scale * jnp.exp(bc))).astype(BF16)

        def chunk_rows(n, base):
            r0 = pl.multiple_of(n * c, c)
            tot = bc_scr[pl.ds(base + r0 + edge_tile, 8), :][edge_row:edge_row + 1, :]
            return r0, pl.ds(base + r0, c), jnp.exp(tot)

        def update_state(kb, vv, decay):
            kd = (kb * decay).astype(BF16)
            ds_t = lax.dot_general(vv, kd, tn, preferred_element_type=F32)
            st_ref[...] = decay * st_ref[...] + ds_t

        def ctx_step(i, _):
            n = i if dr == 0 else n_ctx - 1 - i
            r0, rs, decay = chunk_rows(n, 0)
            update_state(kb_scr[rs, :], vc_ref[0, pl.ds(r0, c), :], decay)
            return 0

        def lat_step(i, _):
            n = i if dr == 0 else n_lat - 1 - i
            r0, rs, decay = chunk_rows(n, lc)
            ro = pl.ds(r0, c)
            kb = kb_scr[rs, :]
            vv = v_ref[0, ro, :]
            qb = qb_scr[ro, :]
            scores = lax.dot_general(qb, kb.astype(BF16), nt, preferred_element_type=F32)
            scores = jnp.where(smask, scores, 0.0).astype(BF16)
            o = _dot(scores, vv) + lax.dot_general(qb, st_ref[...].astype(BF16), nt,
                                                   preferred_element_type=F32)
            if dr == 0:
                o_ref[0, ro, :] = o
            else:
                o_ref[0, ro, :] = o_ref[0, ro, :] + o
            update_state(kb, vv, decay)
            return 0

        st_ref[...] = jnp.zeros_like(st_ref)
        lax.fori_loop(0, n_ctx, ctx_step, 0, unroll=2)
        lax.fori_loop(0, n_lat, lat_step, 0, unroll=4)


def _gla(p_lat, p_ctx, gk_lat, gk_ctx, up_pad, bias, dk, dv):
    b_, l_, _ = p_lat.shape
    lc = p_ctx.shape[1]
    hh = GLA_HEADS
    s5w = hh * dv
    q0, k0, v0 = s5w // dk, (s5w + hh * dk) // dk, (s5w + 2 * hh * dk) // dv
    kc0, vc0 = s5w // dk, (s5w + hh * dk) // dv
    return pl.pallas_call(
        functools.partial(_gla_body, scale=dk ** -0.5),
        grid=(b_, hh),
        in_specs=[
            pl.BlockSpec((1, l_, dk), lambda b, h: (b, 0, q0 + h)),
            pl.BlockSpec((1, l_, dk), lambda b, h: (b, 0, k0 + h)),
            pl.BlockSpec((1, l_, dv), lambda b, h: (b, 0, v0 + h)),
            pl.BlockSpec((1, lc, dk), lambda b, h: (b, 0, kc0 + h)),
            pl.BlockSpec((1, lc, dv), lambda b, h: (b, 0, vc0 + h)),
            pl.BlockSpec((1, l_, 128), lambda b, h: (b, 0, 0)),
            pl.BlockSpec((1, lc, 128), lambda b, h: (b, 0, 0)),
            pl.BlockSpec((2, 128, dk), lambda b, h: (0, 0, h)),
            pl.BlockSpec((2, 1, dk), lambda b, h: (0, 0, h)),
        ],
        out_specs=pl.BlockSpec((1, l_, dv), lambda b, h: (b, 0, h)),
        out_shape=jax.ShapeDtypeStruct((b_, l_, hh * dv), F32),
        scratch_shapes=[
            pltpu.VMEM((lc + l_, dk), F32),
            pltpu.VMEM((lc + l_, dk), F32),
            pltpu.VMEM((l_, dk), BF16),
            pltpu.VMEM((dv, dk), F32),
        ],
        compiler_params=_cparams(("parallel", "parallel")),
        name="gla",
    )(p_lat, p_lat, p_lat, p_ctx, p_ctx, gk_lat, gk_ctx, up_pad, bias)


def _gelu_tanh(x):
    return 0.5 * x * (1.0 + jnp.tanh(math.sqrt(2.0 / math.pi) * (x + 0.044715 * (x * x * x))))


def _merge_body(ys_ref, yg_ref, go_ref, gs_ref, gg_ref, x_ref, g1_ref, sh2_ref, sc2_ref,
                glu_ref, ws5_ref, wgla_ref, wout_ref, gnw_ref, n2w_ref, rw_ref,
                x1_ref, h2_ref, lg_ref, ys_scr, *, dv):
    nch = ys_ref.shape[1]
    for c in range(ys_ref.shape[0]):
        for t in range(S5_CHUNK):
            ys_scr[c, pl.ds(t, nch, stride=S5_CHUNK), :] = ys_ref[c, :, t * 128:(t + 1) * 128]
    a = _gelu_tanh(jnp.concatenate([ys_scr[c] for c in range(ys_ref.shape[0])], axis=-1))
    a = a * _sigmoid(_dot(a.astype(BF16), glu_ref[...]))
    pa = _dot(a.astype(BF16), ws5_ref[...])

    yg = yg_ref[0]
    parts = []
    for h in range(GLA_HEADS):
        oh = yg[:, h * dv:(h + 1) * dv]
        ms = jnp.mean(oh * oh, axis=-1, keepdims=True)
        parts.append(oh * lax.rsqrt(ms + EPS) * gnw_ref[...])
    gl = jnp.concatenate(parts, axis=-1) * _silu(go_ref[0].astype(F32))
    pb = _dot(gl.astype(BF16), wgla_ref[...])

    m = _sigmoid(gs_ref[0].astype(F32)) * pa + _sigmoid(gg_ref[0].astype(F32)) * pb
    y = _dot(m.astype(BF16), wout_ref[...])
    x1 = x_ref[0] + g1_ref[0, 0] * y
    x1_ref[0] = x1

    h2 = _modulated_norm(x1, n2w_ref[...], sh2_ref[0, 0], sc2_ref[0, 0])
    h_hi = h2.astype(BF16)
    h_lo = (h2 - h_hi.astype(F32)).astype(BF16)
    half = h2.shape[-1] // 2
    _store_row_tiles(h2_ref, (0,), _pack_bf16_pair(h2[:, :half], h2[:, half:]))
    lg_ref[0] = _dot(h_hi, rw_ref[0]) + _dot(h_lo, rw_ref[0]) + _dot(h_hi, rw_ref[1])


def _resident(shape):
    nd = len(shape)
    return pl.BlockSpec(shape, lambda b, i: (0,) * nd, pipeline_mode=pl.Buffered(1))


def _merge(ys, yg, p_lat, x, mod4, glu_w, ws5, wgla, wout, gnw, n2w, rw2, dv, tm):
    b_, l_, d = x.shape
    nblk, _, vw = ys.shape
    w5 = nblk * 128
    nch = tm // S5_CHUNK
    go0 = (p_lat.shape[-1] - 2 * d - w5) // w5
    gs0 = (p_lat.shape[-1] - 2 * d) // d
    modspec = lambda r: pl.BlockSpec((1, 1, 1, d), lambda b, i: (b, r, 0, 0))
    return pl.pallas_call(
        functools.partial(_merge_body, dv=dv),
        grid=(b_, l_ // tm),
        in_specs=[
            pl.BlockSpec((nblk, nch, vw), lambda b, i: (0, b * (l_ // tm) + i, 0)),
            pl.BlockSpec((1, tm, w5), lambda b, i: (b, i, 0)),
            pl.BlockSpec((1, tm, w5), lambda b, i: (b, i, go0)),
            pl.BlockSpec((1, tm, d), lambda b, i: (b, i, gs0)),
            pl.BlockSpec((1, tm, d), lambda b, i: (b, i, gs0 + 1)),
            pl.BlockSpec((1, tm, d), lambda b, i: (b, i, 0)),
            modspec(2), modspec(3), modspec(4),
            _resident(glu_w.shape), _resident(ws5.shape), _resident(wgla.shape), _resident(wout.shape),
            _resident(gnw.shape), _resident(n2w.shape), _resident(rw2.shape),
        ],
        out_specs=[
            pl.BlockSpec((1, tm, d), lambda b, i: (b, i, 0)),
            pl.BlockSpec((1, tm * (d // 256), 128), lambda b, i: (b, i, 0)),
            pl.BlockSpec((1, tm, 128), lambda b, i: (b, i, 0)),
        ],
        out_shape=[
            jax.ShapeDtypeStruct((b_, l_, d), F32),
            jax.ShapeDtypeStruct((b_, l_ * (d // 256), 128), U32),
            jax.ShapeDtypeStruct((b_, l_, 128), F32),
        ],
        scratch_shapes=[pltpu.VMEM((nblk, tm, 128), F32)],
        compiler_params=_cparams(("parallel", "parallel")),
        name="merge",
    )(ys, yg, p_lat, p_lat, p_lat, x, mod4, mod4, mod4, glu_w, ws5, wgla, wout, gnw, n2w, rw2)


def _route_body(lg_ref, bias_ref, eidx_ref, w_ref, rank_ref, cnt_ref, carry_ref):
    ne, tt = lg_ref.shape
    gsz = ne // N_GROUPS
    neg = -jnp.inf

    @pl.when(pl.program_id(0) == 0)
    def _():
        carry_ref[...] = jnp.zeros_like(carry_ref)

    sc = _sigmoid(lg_ref[...])
    ch = sc + bias_ref[...]

    def first_max(v, idx, big):
        m = jnp.max(v, axis=0, keepdims=True)
        i = jnp.min(jnp.where(v == m, idx, big), axis=0, keepdims=True)
        return m, i

    midx = lax.broadcasted_iota(I32, (gsz, tt), 0).astype(F32)
    gs_rows = []
    for g in range(N_GROUPS):
        v = ch[g * gsz:(g + 1) * gsz, :]
        m1, i1 = first_max(v, midx, float(gsz))
        m2 = jnp.max(jnp.where(midx == i1, neg, v), axis=0, keepdims=True)
        gs_rows.append(m1 + m2)
    gscore = jnp.concatenate(gs_rows, axis=0)

    gidx = lax.broadcasted_iota(I32, (N_GROUPS, tt), 0).astype(F32)
    gsel = jnp.zeros((N_GROUPS, tt), F32)
    cur = gscore
    for _ in range(TOPK_GROUPS):
        _, i = first_max(cur, gidx, float(N_GROUPS))
        hit = gidx == i
        gsel = jnp.where(hit, 1.0, gsel)
        cur = jnp.where(hit, neg, cur)

    masked = jnp.concatenate(
        [jnp.where(gsel[g:g + 1, :] > 0.0, ch[g * gsz:(g + 1) * gsz, :], neg) for g in range(N_GROUPS)], axis=0)

    eiota = lax.broadcasted_iota(I32, (ne, tt), 0).astype(F32)
    sel = jnp.zeros((ne, tt), F32)
    cur = masked
    idx_rows, s_rows = [], []
    for _ in range(TOP_K):
        _, i = first_max(cur, eiota, float(ne))
        hit = eiota == i
        idx_rows.append(i)
        s_rows.append(jnp.sum(jnp.where(hit, sc, 0.0), axis=0, keepdims=True))
        sel = jnp.where(hit, 1.0, sel)
        cur = jnp.where(hit, neg, cur)
    idx = jnp.concatenate(idx_rows, axis=0)
    s = jnp.concatenate(s_rows, axis=0)
    w_ref[...] = s / jnp.sum(s, axis=0, keepdims=True) * ROUTED_SCALE
    eidx_ref[...] = idx.astype(I32)

    cw = 256
    a_i = lax.broadcasted_iota(I32, (cw, cw), 0)
    b_i = lax.broadcasted_iota(I32, (cw, cw), 1)
    upper = jnp.where(a_i < b_i, 1.0, 0.0).astype(BF16)
    carry = carry_ref[:, 0:1]
    pieces = []
    for c0 in range(0, tt, cw):
        sc_c = sel[:, c0:c0 + cw]
        pieces.append(_dot(sc_c.astype(BF16), upper) + carry)
        carry = carry + jnp.sum(sc_c, axis=1, keepdims=True)
    rank_full = jnp.concatenate(pieces, axis=1)
    carry_ref[...] = jnp.broadcast_to(carry, carry_ref.shape)
    cnt_ref[...] = jnp.broadcast_to(carry, cnt_ref.shape)
    rk = [jnp.sum(jnp.where(eiota == idx_rows[k], rank_full, 0.0), axis=0, keepdims=True) for k in range(TOP_K)]
    rank_ref[...] = jnp.concatenate(rk, axis=0).astype(I32)


def _route(logits_t, bias):
    ne, t = logits_t.shape
    tt = 1024
    return pl.pallas_call(
        _route_body,
        grid=(t // tt,),
        in_specs=[
            pl.BlockSpec((ne, tt), lambda i: (0, i)),
            pl.BlockSpec((ne, 1), lambda i: (0, 0)),
        ],
        out_specs=[
            pl.BlockSpec((TOP_K, tt), lambda i: (0, i)),
            pl.BlockSpec((TOP_K, tt), lambda i: (0, i)),
            pl.BlockSpec((TOP_K, tt), lambda i: (0, i)),
            pl.BlockSpec((ne, 128), lambda i: (0, 0)),
        ],
        out_shape=[
            jax.ShapeDtypeStruct((TOP_K, t), I32),
            jax.ShapeDtypeStruct((TOP_K, t), F32),
            jax.ShapeDtypeStruct((TOP_K, t), I32),
            jax.ShapeDtypeStruct((ne, 128), F32),
        ],
        scratch_shapes=[pltpu.VMEM((ne, 128), F32)],
        compiler_params=_cparams(("arbitrary",)),
        name="route",
    )(logits_t, bias.reshape(ne, 1))


def _slots_body(cnt_ref, eidx_ref, rank_ref, dest_ref, rows_ref, be_ref, first_ref, nxt_ref, nu_ref,
                off_smem, dchunk, sem):
    ne = cnt_ref.shape[0]
    nb = be_ref.shape[0]
    tl = eidx_ref.shape[1]
    r_ = EXPERT_ROWS
    shift = r_.bit_length() - 1
    c = pl.program_id(0)

    @pl.when(c == 0)
    def _():
        def per_expert(e, off):
            cnt = cnt_ref[e]
            n = lax.shift_right_logical(cnt + (r_ - 1), shift)
            off_smem[e] = off

            def fill(j, _):
                be_ref[off + j] = e
                first_ref[off + j] = jnp.where(j == 0, 1, 0)
                return 0
            lax.fori_loop(0, n, fill, 0)

            def pad(j, _):
                rows_ref[off * r_ + j] = 0
                return 0
            lax.fori_loop(cnt, n * r_, pad, 0)
            return off + n

        n_used = lax.fori_loop(0, ne, per_expert, 0)
        nu_ref[0] = n_used
        last_e = be_ref[n_used - 1]

        def tail(j, _):
            be_ref[j] = last_e
            first_ref[j] = 0

            def zero_row(q, _):
                rows_ref[j * r_ + q] = 0
                return 0
            lax.fori_loop(0, r_, zero_row, 0)
            return 0
        lax.fori_loop(n_used, nb, tail, 0)

        def back(i, nx):
            j = nb - 1 - i
            nxt_ref[j] = nx
            return jnp.where(first_ref[j] == 1, be_ref[j], nx)
        lax.fori_loop(0, nb, back, -1)

    ei = eidx_ref[...]
    acc = rank_ref[...]
    for e in range(ne):
        acc = acc + jnp.where(ei == e, off_smem[e] * r_, 0)
    dest_ref[...] = acc
    cp = pltpu.make_async_copy(dest_ref, dchunk, sem)
    cp.start()
    cp.wait()
    base = c * tl

    def scatter(j, _):
        for k in range(TOP_K):
            rows_ref[dchunk[k, j]] = base + j
        return 0
    lax.fori_loop(0, tl, scatter, 0)


def _slots(counts, eidx, rank, nb):
    k_, t = eidx.shape
    tl = 1024
    smem = lambda: pl.BlockSpec(memory_space=pltpu.SMEM)
    chunk = lambda: pl.BlockSpec((k_, tl), lambda i: (0, i))
    return pl.pallas_call(
        _slots_body,
        grid=(t // tl,),
        in_specs=[smem(), chunk(), chunk()],
        out_specs=[chunk(), smem(), smem(), smem(), smem(), smem()],
        out_shape=[
            jax.ShapeDtypeStruct((k_, t), I32),
            jax.ShapeDtypeStruct((nb * EXPERT_ROWS,), I32),
            jax.ShapeDtypeStruct((nb,), I32),
            jax.ShapeDtypeStruct((nb,), I32),
            jax.ShapeDtypeStruct((nb,), I32),
            jax.ShapeDtypeStruct((1,), I32),
        ],
        scratch_shapes=[
            pltpu.SMEM((counts.shape[0],), I32),
            pltpu.SMEM((k_, tl), I32),
            pltpu.SemaphoreType.DMA,
        ],
        compiler_params=_cparams(("arbitrary",)),
        name="slots",
    )(counts, eidx, rank)


def _slots2_body(cnt_ref, eidx_ref, rank_ref, dest_ref, be_ref, first_ref, nxt_ref, nu_ref, pad_ref, off_smem):
    ne = cnt_ref.shape[0]
    nb = be_ref.shape[0]
    r_ = EXPERT_ROWS
    shift = r_.bit_length() - 1

    @pl.when(pl.program_id(0) == 0)
    def _():
        def per_expert(e, off):
            cnt = cnt_ref[e]
            n = lax.shift_right_logical(cnt + (r_ - 1), shift)
            off_smem[e] = off
            pad_ref[0, e] = off * r_ + cnt
            pad_ref[1, e] = n * r_ - cnt

            def fill(j, _):
                be_ref[off + j] = e
                first_ref[off + j] = jnp.where(j == 0, 1, 0)
                return 0
            lax.fori_loop(0, n, fill, 0)
            return off + n

        n_used = lax.fori_loop(0, ne, per_expert, 0)
        nu_ref[0] = n_used
        last_e = be_ref[n_used - 1]

        def tail(j, _):
            be_ref[j] = last_e
            first_ref[j] = 0
            return 0
        lax.fori_loop(n_used, nb, tail, 0)

        def back(i, nx):
            j = nb - 1 - i
            nxt_ref[j] = nx
            return jnp.where(first_ref[j] == 1, be_ref[j], nx)
        lax.fori_loop(0, nb, back, -1)

    ei = eidx_ref[...]
    acc = rank_ref[...]
    for e in range(ne):
        acc = acc + jnp.where(ei == e, off_smem[e] * r_, 0)
    dest_ref[...] = acc


def _slots2(counts, eidx, rank, nb):
    k_, t = eidx.shape
    ne = counts.shape[0]
    tl = 1024
    smem = lambda: pl.BlockSpec(memory_space=pltpu.SMEM)
    chunk = lambda: pl.BlockSpec((k_, tl), lambda i: (0, i))
    return pl.pallas_call(
        _slots2_body,
        grid=(t // tl,),
        in_specs=[smem(), chunk(), chunk()],
        out_specs=[chunk(), smem(), smem(), smem(), smem(), smem()],
        out_shape=[
            jax.ShapeDtypeStruct((k_, t), I32),
            jax.ShapeDtypeStruct((nb,), I32),
            jax.ShapeDtypeStruct((nb,), I32),
            jax.ShapeDtypeStruct((nb,), I32),
            jax.ShapeDtypeStruct((1,), I32),
            jax.ShapeDtypeStruct((2, ne), I32),
        ],
        scratch_shapes=[pltpu.SMEM((ne,), I32)],
        compiler_params=_cparams(("arbitrary",)),
        name="slots",
    )(counts, eidx, rank)


def _dispatch_body(pad_ref, nu_ref, dest_hbm, h2_hbm, x_hbm, idx_smem, buf, zero_buf, zero_blk, isem, insem, ssem,
                   zsem):
    i = pl.program_id(0)
    n = pl.num_programs(0)
    tiles = x_hbm.shape[1]
    r_ = zero_blk.shape[0]
    n_blocks = x_hbm.shape[0] // r_
    tm = buf.shape[1] // tiles
    ne = pad_ref.shape[1]
    slot = i % 2
    nslot = 1 - slot

    def idx_copy(blk, s):
        cols = pl.ds(pl.multiple_of(blk * tm, tm), tm)
        return pltpu.make_async_copy(dest_hbm.at[:, cols], idx_smem.at[s], isem.at[s])

    def in_copy(blk, s):
        rows = pl.ds(pl.multiple_of(blk * tm * tiles, tm * tiles), tm * tiles)
        return pltpu.make_async_copy(h2_hbm.at[rows], buf.at[s], insem.at[s])

    def wait_scatter(s):
        span = x_hbm.at[pl.ds(0, TOP_K * tm)]
        pltpu.make_async_copy(span, span, ssem.at[s]).wait()

    def zero_copy(row):
        return pltpu.make_async_copy(zero_buf, x_hbm.at[row], zsem)

    def zero_block_copy(blk):
        return pltpu.make_async_copy(zero_blk, x_hbm.at[pl.ds(pl.multiple_of(blk * r_, r_), r_)], zsem)

    @pl.when(i == 0)
    def _():
        zero_buf[...] = jnp.zeros_like(zero_buf)
        zero_blk[...] = jnp.zeros_like(zero_blk)
        idx_copy(0, 0).start()
        in_copy(0, 0).start()

        def blk_body(b, _):
            zero_block_copy(b).start(priority=1)
            return 0
        lax.fori_loop(nu_ref[0], n_blocks, blk_body, 0)
        for e in range(ne):
            start = pad_ref[0, e]

            def body(r, _, start=start):
                zero_copy(start + r).start(priority=1)
                return 0
            lax.fori_loop(0, pad_ref[1, e], body, 0)

    @pl.when(i >= 1)
    def _():
        wait_scatter(nslot)

    @pl.when(i + 1 < n)
    def _():
        in_copy(i + 1, nslot).start()
        idx_copy(i + 1, nslot).start()

    in_copy(i, slot).wait()
    idx_copy(i, slot).wait()
    for j in range(tm):
        for k in range(TOP_K):
            pltpu.make_async_copy(buf.at[slot, pl.ds(j * tiles, tiles)], x_hbm.at[idx_smem[slot, k, j]],
                                  ssem.at[slot]).start()

    @pl.when(i + 1 == n)
    def _():
        wait_scatter(slot)
        for e in range(ne):
            def body(r, _):
                zero_copy(0).wait()
                return 0
            lax.fori_loop(0, pad_ref[1, e], body, 0)

        def blk_wait(b, _):
            zero_block_copy(0).wait()
            return 0
        lax.fori_loop(nu_ref[0], n_blocks, blk_wait, 0)


def _dispatch(pad, n_used, dest, h2p, n_slots, tm):
    k_, t = dest.shape
    tiles = h2p.shape[0] // t
    grid_spec = pltpu.PrefetchScalarGridSpec(
        num_scalar_prefetch=2,
        grid=(t // tm,),
        in_specs=[pl.BlockSpec(memory_space=pl.ANY), pl.BlockSpec(memory_space=pl.ANY)],
        out_specs=pl.BlockSpec(memory_space=pl.ANY),
        scratch_shapes=[
            pltpu.SMEM((2, k_, tm), I32),
            pltpu.VMEM((2, tm * tiles, 128), U32),
            pltpu.VMEM((tiles, 128), U32),
            pltpu.VMEM((EXPERT_ROWS, tiles, 128), U32),
            pltpu.SemaphoreType.DMA((2,)),
            pltpu.SemaphoreType.DMA((2,)),
            pltpu.SemaphoreType.DMA((2,)),
            pltpu.SemaphoreType.DMA,
        ],
    )
    return pl.pallas_call(
        _dispatch_body,
        grid_spec=grid_spec,
        out_shape=jax.ShapeDtypeStruct((n_slots, tiles, 128), U32),
        compiler_params=_cparams(("arbitrary",)),
        name="dispatch",
    )(pad, n_used, dest, h2p)


def _expert2_body(be_ref, first_ref, nxt_ref, nu_ref, x_hbm, wg_hbm, wu_hbm, wd_hbm, y_ref,
                  xbuf, wg_f, wu_f, wd_f, wg_s, wu_s, wd_s, xsem, wsem):
    i = pl.program_id(0)
    n_used = nu_ref[0]
    r_ = EXPERT_ROWS
    tiles = xbuf.shape[1] // r_
    slot = i % 2
    nslot = 1 - slot

    def x_copy(blk, s):
        rows = pl.ds(pl.multiple_of(blk * r_ * tiles, r_ * tiles), r_ * tiles)
        return pltpu.make_async_copy(x_hbm.at[rows], xbuf.at[s], xsem.at[s])

    def weight_copies(e):
        return (pltpu.make_async_copy(wg_hbm.at[e], wg_f, wsem.at[0]),
                pltpu.make_async_copy(wu_hbm.at[e], wu_f, wsem.at[1]),
                pltpu.make_async_copy(wd_hbm.at[e], wd_f, wsem.at[2]))

    @pl.when(i == 0)
    def _():
        for cp in weight_copies(be_ref[0]):
            cp.start(priority=1)
        x_copy(0, 0).start()

    @pl.when(first_ref[i] == 1)
    def _():
        for cp in weight_copies(be_ref[i]):
            cp.wait()
        wg_s[...] = wg_f[...].astype(BF16)
        wu_s[...] = wu_f[...].astype(BF16)
        wd_s[...] = wd_f[...].astype(BF16)

        @pl.when(nxt_ref[i] >= 0)
        def _():
            for cp in weight_copies(nxt_ref[i]):
                cp.start(priority=1)

    @pl.when(i < n_used)
    def _():
        @pl.when(i + 1 < n_used)
        def _():
            x_copy(i + 1, nslot).start()

        x_copy(i, slot).wait()
        lo, hi = _unpack_bf16_pair(_load_row_tiles(xbuf, (slot,), 0, r_, tiles))
        lo = lo.astype(BF16)
        hi = hi.astype(BF16)
        half = lo.shape[-1]
        a = _dot(lo, wg_s[0:half, :]) + _dot(hi, wg_s[half:, :])
        u = _dot(lo, wu_s[0:half, :]) + _dot(hi, wu_s[half:, :])
        hid = (_silu(a) * u).astype(BF16)
        y_lo = _dot(hid, wd_s[:, 0:half])
        y_hi = _dot(hid, wd_s[:, half:])
        _store_row_tiles(y_ref, (), _pack_bf16_pair(y_lo, y_hi))

    @pl.when(i >= n_used)
    def _():
        y_ref[...] = jnp.zeros_like(y_ref)


def _experts2(block_e, first, nxt, n_used, x_sorted, wg, wu, wd):
    nb = block_e.shape[0]
    ne, d, f = wg.shape
    r_ = EXPERT_ROWS
    tiles = d // 256
    hbm = lambda: pl.BlockSpec(memory_space=pl.ANY)
    grid_spec = pltpu.PrefetchScalarGridSpec(
        num_scalar_prefetch=4,
        grid=(nb,),
        in_specs=[hbm(), hbm(), hbm(), hbm()],
        out_specs=pl.BlockSpec((r_ * tiles, 128), lambda i, *_: (i, 0)),
        scratch_shapes=[
            pltpu.VMEM((2, r_ * tiles, 128), U32),
            pltpu.VMEM((d, f), F32),
            pltpu.VMEM((d, f), F32),
            pltpu.VMEM((f, d), F32),
            pltpu.VMEM((d, f), BF16),
            pltpu.VMEM((d, f), BF16),
            pltpu.VMEM((f, d), BF16),
            pltpu.SemaphoreType.DMA((2,)),
            pltpu.SemaphoreType.DMA((3,)),
        ],
    )
    return pl.pallas_call(
        _expert2_body,
        grid_spec=grid_spec,
        out_shape=jax.ShapeDtypeStruct((nb * r_ * tiles, 128), U32),
        compiler_params=_cparams(("arbitrary",)),
        name="experts",
    )(block_e, first, nxt, n_used, x_sorted, wg, wu, wd)


EXPERT_ISSUE_GROUPS = 8
EXPERT_LOOKAHEAD = 2


def _expert_body(be_ref, first_ref, nxt_ref, nu_ref, rows_hbm, x_hbm, wg_hbm, wu_hbm, wd_hbm, y_ref,
                 idx_smem, xbuf, wg_f, wu_f, wd_f, wg_s, wu_s, wd_s, isem, gsem, wsem):
    i = pl.program_id(0)
    n_used = nu_ref[0]
    r_ = EXPERT_ROWS
    tiles = x_hbm.shape[1]
    ns = EXPERT_LOOKAHEAD + 1
    slot = lax.rem(i, ns)
    aslot = lax.rem(i + EXPERT_LOOKAHEAD, ns)

    def idx_copy(blk, s):
        return pltpu.make_async_copy(rows_hbm.at[pl.ds(blk, 1)], idx_smem.at[s], isem.at[s])

    def row_copy(s, r):
        return pltpu.make_async_copy(x_hbm.at[idx_smem[s, 0, r]], xbuf.at[s, pl.ds(r * tiles, tiles)], gsem.at[s])

    def issue_rows(s, g):
        step = r_ // EXPERT_ISSUE_GROUPS
        for r in range(g * step, (g + 1) * step):
            row_copy(s, r).start()

    def wait_rows(s):
        pltpu.make_async_copy(xbuf.at[s], xbuf.at[s], gsem.at[s]).wait()

    def weight_copies(e):
        return (pltpu.make_async_copy(wg_hbm.at[e], wg_f, wsem.at[0]),
                pltpu.make_async_copy(wu_hbm.at[e], wu_f, wsem.at[1]),
                pltpu.make_async_copy(wd_hbm.at[e], wd_f, wsem.at[2]))

    @pl.when(i == 0)
    def _():
        for cp in weight_copies(be_ref[0]):
            cp.start(priority=1)
        for b in range(EXPERT_LOOKAHEAD):
            idx_copy(b, b).start()
        for b in range(EXPERT_LOOKAHEAD):
            idx_copy(b, b).wait()

            def body(r, _, b=b):
                row_copy(b, r).start()
                return 0
            lax.fori_loop(0, r_, body, 0)
        idx_copy(EXPERT_LOOKAHEAD, EXPERT_LOOKAHEAD).start()

    @pl.when(first_ref[i] == 1)
    def _():
        for cp in weight_copies(be_ref[i]):
            cp.wait()
        wg_s[...] = wg_f[...].astype(BF16)
        wu_s[...] = wu_f[...].astype(BF16)
        wd_s[...] = wd_f[...].astype(BF16)

        @pl.when(nxt_ref[i] >= 0)
        def _():
            for cp in weight_copies(nxt_ref[i]):
                cp.start(priority=1)

    @pl.when(i < n_used)
    def _():
        idx_copy(i + EXPERT_LOOKAHEAD, aslot).wait()
        for g in range(EXPERT_ISSUE_GROUPS):
            issue_rows(aslot, g)
        wait_rows(slot)
        lo, hi = _unpack_bf16_pair(_load_row_tiles(xbuf, (slot,), 0, r_, tiles))
        lo = lo.astype(BF16)
        hi = hi.astype(BF16)
        half = lo.shape[-1]
        a = _dot(lo, wg_s[0:half, :]) + _dot(hi, wg_s[half:, :])
        u = _dot(lo, wu_s[0:half, :]) + _dot(hi, wu_s[half:, :])
        hid = (_silu(a) * u).astype(BF16)
        y_lo = _dot(hid, wd_s[:, 0:half])
        y_hi = _dot(hid, wd_s[:, half:])
        _store_row_tiles(y_ref, (), _pack_bf16_pair(y_lo, y_hi))

        @pl.when(i + 1 < n_used)
        def _():
            idx_copy(i + EXPERT_LOOKAHEAD + 1, slot).start()

    @pl.when(jnp.logical_and(i >= n_used, i < n_used + EXPERT_LOOKAHEAD))
    def _():
        wait_rows(slot)

    @pl.when(i >= n_used)
    def _():
        y_ref[...] = jnp.zeros_like(y_ref)


def _experts(block_e, first, nxt, n_used, rows, h2p, wg, wu, wd):
    nb = block_e.shape[0]
    t, tiles, _ = h2p.shape
    ne, d, f = wg.shape
    r_ = EXPERT_ROWS
    hbm = lambda: pl.BlockSpec(memory_space=pl.ANY)
    grid_spec = pltpu.PrefetchScalarGridSpec(
        num_scalar_prefetch=4,
        grid=(nb,),
        in_specs=[hbm(), hbm(), hbm(), hbm(), hbm()],
        out_specs=pl.BlockSpec((r_ * tiles, 128), lambda i, *_: (i, 0)),
        scratch_shapes=[
            pltpu.SMEM((EXPERT_LOOKAHEAD + 1, 1, r_), I32),
            pltpu.VMEM((EXPERT_LOOKAHEAD + 1, r_ * tiles, 128), U32),
            pltpu.VMEM((d, f), F32),
            pltpu.VMEM((d, f), F32),
            pltpu.VMEM((f, d), F32),
            pltpu.VMEM((d, f), BF16),
            pltpu.VMEM((d, f), BF16),
            pltpu.VMEM((f, d), BF16),
            pltpu.SemaphoreType.DMA((EXPERT_LOOKAHEAD + 1,)),
            pltpu.SemaphoreType.DMA((EXPERT_LOOKAHEAD + 1,)),
            pltpu.SemaphoreType.DMA((3,)),
        ],
    )
    return pl.pallas_call(
        _expert_body,
        grid_spec=grid_spec,
        out_shape=jax.ShapeDtypeStruct((nb * r_ * tiles, 128), U32),
        compiler_params=_cparams(("arbitrary",)),
        name="experts",
    )(block_e, first, nxt, n_used, rows, h2p, wg, wu, wd)


def _combine_body(dest_hbm, y_hbm, w_ref, x1_ref, h2_ref, g2_ref, sg_ref, su_ref, sd_ref, fnw_ref, o_ref,
                  idx_smem, ybuf, isem, gsem):
    i = pl.program_id(0)
    n = pl.num_programs(0)
    tm = x1_ref.shape[0]
    tiles = y_hbm.shape[1]
    slot = i % 2
    nslot = 1 - slot

    def idx_copy(blk, s):
        cols = pl.ds(pl.multiple_of(blk * tm, tm), tm)
        return pltpu.make_async_copy(dest_hbm.at[:, cols], idx_smem.at[s], isem.at[s])

    def row_copy(s, k, j):
        return pltpu.make_async_copy(y_hbm.at[idx_smem[s, k, j]], ybuf.at[s, pl.ds((k * tm + j) * tiles, tiles)],
                                     gsem.at[s])

    def wait_rows(s):
        pltpu.make_async_copy(ybuf.at[s], ybuf.at[s], gsem.at[s]).wait()

    def ahead(step):
        return jnp.minimum(step + 1, n - 1)

    @pl.when(i == 0)
    def _():
        idx_copy(0, 0).start()
        idx_copy(0, 0).wait()

        def body(j, _):
            for k in range(TOP_K):
                dst = pl.ds(pl.multiple_of((k * tm + j) * tiles, tiles), tiles)
                pltpu.make_async_copy(y_hbm.at[idx_smem[0, k, j]], ybuf.at[0, dst], gsem.at[0]).start()
            return 0
        lax.fori_loop(0, tm, body, 0)
        idx_copy(ahead(0), 1).start()

    idx_copy(ahead(i), nslot).wait()
    wait_rows(slot)

    w = w_ref[...]
    half = tiles * 128
    r_lo = [jnp.zeros((tm, 128), F32) for _ in range(tiles)]
    r_hi = [jnp.zeros((tm, 128), F32) for _ in range(tiles)]
    for k in range(TOP_K):
        for j in range(tm):
            row_copy(nslot, k, j).start(priority=j % 2)
        wk = w[:, k:k + 1]
        for kc in range(tiles):
            lo, hi = _unpack_bf16_pair(ybuf[slot, pl.ds(k * tm * tiles + kc, tm, stride=tiles), :])
            r_lo[kc] = r_lo[kc] + wk * lo
            r_hi[kc] = r_hi[kc] + wk * hi
    routed = jnp.concatenate(r_lo + r_hi, axis=-1)

    @pl.when(i + 1 < n)
    def _():
        idx_copy(ahead(i + 1), slot).start()

    @pl.when(i + 1 == n)
    def _():
        wait_rows(nslot)

    x_lo, x_hi = _unpack_bf16_pair(_load_row_tiles(h2_ref, (), 0, tm, tiles))
    x_lo = x_lo.astype(BF16)
    x_hi = x_hi.astype(BF16)
    a = _dot(x_lo, sg_ref[0:half, :]) + _dot(x_hi, sg_ref[half:, :])
    u = _dot(x_lo, su_ref[0:half, :]) + _dot(x_hi, su_ref[half:, :])
    shared = _dot((_silu(a) * u).astype(BF16), sd_ref[...])

    x2 = x1_ref[...] + g2_ref[0, 0] * (routed + shared)
    ms = jnp.mean(x2 * x2, axis=-1, keepdims=True)
    o_ref[...] = x2 * lax.rsqrt(ms + EPS) * fnw_ref[...]


def _combine(dest, y_sorted, w_tok, x1, h2p, mod4, sg, su, sd, fnw, tiles_per_batch, tm):
    t, d = x1.shape
    half = d // 2
    nrow = TOP_K * tm
    res = lambda shape: pl.BlockSpec(shape, lambda i: (0,) * len(shape), pipeline_mode=pl.Buffered(1))
    return pl.pallas_call(
        _combine_body,
        grid=(t // tm,),
        in_specs=[
            pl.BlockSpec(memory_space=pl.ANY),
            pl.BlockSpec(memory_space=pl.ANY),
            pl.BlockSpec((tm, TOP_K), lambda i: (i, 0)),
            pl.BlockSpec((tm, d), lambda i: (i, 0)),
            pl.BlockSpec((tm * (half // 128), 128), lambda i: (i, 0)),
            pl.BlockSpec((1, 1, 1, d), lambda i: (i // tiles_per_batch, 5, 0, 0)),
            res(sg.shape), res(su.shape), res(sd.shape), res(fnw.shape),
        ],
        out_specs=pl.BlockSpec((tm, d), lambda i: (i, 0)),
        out_shape=jax.ShapeDtypeStruct((t, d), F32),
        scratch_shapes=[
            pltpu.SMEM((2, TOP_K, tm), I32),
            pltpu.VMEM((2, nrow * (half // 128), 128), U32),
            pltpu.SemaphoreType.DMA((2,)),
            pltpu.SemaphoreType.DMA((2,)),
        ],
        compiler_params=_cparams(("arbitrary",)),
        name="combine",
    )(dest, y_sorted, w_tok, x1, h2p, mod4, sg, su, sd, fnw)


def kernel(x, c, ctx, c_ctx, ada_w, ada_b, norm1_w, norm2_w, w_in, s5_lam_re, s5_lam_im, s5_log_dt,
           s5_b_re, s5_b_im, s5_c_re, s5_c_im, s5_d, s5_glu_w, gla_gk_up, gla_gk_b, gla_norm_w,
           w_s5_proj, w_gla_proj, w_out, router_w, router_bias, exp_w_gate, exp_w_up, exp_w_down,
           sh_w_gate, sh_w_up, sh_w_down, final_norm_w):
    depth = ada_w.shape[0]
    assert depth == 1, "single-layer block: context outputs are never consumed"
    b_, l_, d = x.shape
    lc = ctx.shape[1]
    s5w = s5_d.shape[1]
    kw = gla_gk_up.shape[-1]
    vw = w_gla_proj.shape[1]
    dk, dv = kw // GLA_HEADS, vw // GLA_HEADS
    rank2 = 2 * GLA_GATE_RANK
    li = 0

    c8 = jnp.zeros((8, d), F32).at[:b_].set(c).at[b_].set(c_ctx)
    mod4 = _ada(c8, ada_w[li], ada_b[li]).reshape(8, N_MOD, 1, d)

    cuts = [0, s5w, s5w + kw, s5w + 2 * kw, s5w + 2 * kw + vw, s5w + 2 * kw + 2 * vw]
    c_gkd = cuts[5]
    w = w_in[li]
    w_main = jnp.concatenate([w[:, :c_gkd].astype(BF16), w[:, c_gkd + rank2:].astype(BF16)], axis=1)
    w_ctx = jnp.concatenate([w[:, cuts[0]:cuts[1]].astype(BF16), w[:, cuts[2]:cuts[4]].astype(BF16)], axis=1)
    w_gkd = jnp.zeros((d, 128), F32).at[:, :rank2].set(w[:, c_gkd:c_gkd + rank2]).astype(BF16)
    nw1 = norm1_w[li].reshape(1, d)
    p_lat, gk_lat, v_lat = _inproj(x, mod4, lambda b: b, nw1, w_main, w_gkd, tm=1024, tn=1024, s5w=s5w)
    p_ctx, gk_ctx, v_ctx = _inproj(ctx, mod4, lambda b: b_, nw1, w_ctx, w_gkd, tm=lc, tn=w_ctx.shape[1] // 2,
                                   s5w=s5w)

    factors = _s5_factors(s5_lam_re[li], s5_lam_im[li], s5_log_dt[li], s5_b_re[li], s5_b_im[li],
                          s5_c_re[li], s5_c_im[li], s5_d[li])
    ys = _s5v2(v_ctx, v_lat, *factors)

    up_pad = jnp.zeros((2, 128, kw), F32)
    up_pad = up_pad.at[0, :GLA_GATE_RANK].set(gla_gk_up[li, 0]).at[1, GLA_GATE_RANK:rank2].set(gla_gk_up[li, 1])
    yg = _gla(p_lat, p_ctx, gk_lat, gk_ctx, up_pad.astype(BF16), gla_gk_b[li].reshape(2, 1, kw), dk, dv)

    rw = jnp.zeros((d, 128), F32).at[:, :N_EXPERTS].set(router_w[li])
    rw_hi = rw.astype(BF16)
    rw2 = jnp.stack([rw_hi, (rw - rw_hi.astype(F32)).astype(BF16)])
    x1, h2p, logits = _merge(
        ys, yg, p_lat, x, mod4, s5_glu_w[li].astype(BF16), w_s5_proj[li].astype(BF16),
        w_gla_proj[li].astype(BF16), w_out[li].astype(BF16), gla_norm_w[li].reshape(1, dv),
        norm2_w[li].reshape(1, d), rw2, dv, tm=256)

    t = b_ * l_
    logits_t = logits.reshape(t, 128)[:, :N_EXPERTS].T
    eidx, wts, rank, cnt = _route(logits_t, router_bias[li])
    nb = (t * TOP_K + N_EXPERTS * (EXPERT_ROWS - 1) + EXPERT_ROWS - 1) // EXPERT_ROWS
    dest, block_e, first, nxt, n_used, pad = _slots2(cnt[:, 0].astype(I32), eidx, rank, nb)

    tiles = d // 256
    x_sorted = _dispatch(pad, n_used, dest, h2p.reshape(t * tiles, 128), nb * EXPERT_ROWS, tm=128)
    y_sorted = _experts2(block_e, first, nxt, n_used, x_sorted.reshape(nb * EXPERT_ROWS * tiles, 128),
                         exp_w_gate[li], exp_w_up[li], exp_w_down[li])

    tm_c = 128
    out = _combine(dest, y_sorted.reshape(-1, tiles, 128), wts.T, x1.reshape(t, d), h2p.reshape(t * tiles, 128), mod4,
                   sh_w_gate[li].astype(BF16), sh_w_up[li].astype(BF16), sh_w_down[li].astype(BF16),
                   final_norm_w.reshape(1, d), l_ // tm_c, tm_c)
    return out.reshape(b_, l_, d)
```

```python
import functools
import math

import jax
import jax.numpy as jnp
from jax import lax
from jax.experimental import pallas as pl
from jax.experimental.pallas import tpu as pltpu

F32 = jnp.float32
BF16 = jnp.bfloat16
U32 = jnp.uint32
I32 = jnp.int32

EPS = 1e-6
N_MOD = 6

S5_GROUP = 16
S5_STATE = 64
S5_CHUNK = 16

GLA_HEADS = 4
GLA_GATE_RANK = 16
GLA_GATE_NORM = 16.0
GLA_CHUNK = 64

N_EXPERTS = 64
TOP_K = 8
N_GROUPS = 8
TOPK_GROUPS = 4
ROUTED_SCALE = 2.5
EXPERT_ROWS = 256

VMEM_LIMIT = 56 * 1024 * 1024


def _cparams(sem, vmem=VMEM_LIMIT):
    return pltpu.CompilerParams(dimension_semantics=sem, vmem_limit_bytes=vmem)


def _dot(a, b):
    return jnp.dot(a, b, preferred_element_type=F32)


def _sigmoid(x):
    return 1.0 / (1.0 + jnp.exp(-x))


def _silu(x):
    return x * _sigmoid(x)


def _pack_bf16_pair(lo, hi):
    lo_bits = lax.bitcast_convert_type(lo.astype(BF16).astype(F32), U32)
    hi_bits = lax.bitcast_convert_type(hi.astype(BF16).astype(F32), U32)
    return (lo_bits >> 16) | (hi_bits & jnp.uint32(0xFFFF0000))


def _unpack_bf16_pair(w):
    lo = lax.bitcast_convert_type(w << 16, F32)
    hi = lax.bitcast_convert_type(w & jnp.uint32(0xFFFF0000), F32)
    return lo, hi


def _store_row_tiles(ref, lead, val):
    rows, rt = val.shape[0], val.shape[1] // 128
    for kc in range(rt):
        ref[(*lead, pl.ds(kc, rows, stride=rt), slice(None))] = val[:, kc * 128:(kc + 1) * 128]


def _load_row_tiles(ref, lead, row0, rows, rt):
    return jnp.concatenate(
        [ref[(*lead, pl.ds(row0 * rt + kc, rows, stride=rt), slice(None))] for kc in range(rt)], axis=-1)


def _ada_body(c_ref, w_ref, b_ref, o_ref):
    c = c_ref[...]
    s = _silu(c).astype(BF16)
    o_ref[...] = _dot(s, w_ref[...].astype(BF16)) + b_ref[...]


def _ada(c8, ada_w, ada_b):
    d, n = ada_w.shape
    tn = 512
    return pl.pallas_call(
        _ada_body,
        grid=(n // tn,),
        in_specs=[
            pl.BlockSpec((8, d), lambda j: (0, 0)),
            pl.BlockSpec((d, tn), lambda j: (0, j)),
            pl.BlockSpec((1, tn), lambda j: (0, j)),
        ],
        out_specs=pl.BlockSpec((8, tn), lambda j: (0, j)),
        out_shape=jax.ShapeDtypeStruct((8, n), F32),
        compiler_params=_cparams(("parallel",)),
        name="ada",
    )(c8, ada_w, ada_b.reshape(1, n))


def _modulated_norm(x, nw, sh, sc):
    ms = jnp.mean(x * x, axis=-1, keepdims=True)
    y = x * lax.rsqrt(ms + EPS) * nw
    return y * (1.0 + sc) + sh


def _inproj_body(x_ref, sh_ref, sc_ref, nw_ref, w_ref, wg_ref, p_ref, g_ref, v_ref, h_scr, u_scr):
    @pl.when(pl.program_id(2) == 0)
    def _():
        h = _modulated_norm(x_ref[0], nw_ref[...], sh_ref[0, 0], sc_ref[0, 0]).astype(BF16)
        h_scr[...] = h
        g_ref[0] = _dot(h, wg_ref[...])

    r = _dot(h_scr[...], w_ref[...])
    p_ref[0] = r.astype(BF16)

    @pl.when(pl.program_id(2) == 0)
    def _():
        nch = u_scr.shape[1] // S5_CHUNK
        for c in range(u_scr.shape[0]):
            u_scr[c] = r[:, c * 128:(c + 1) * 128]
            for t in range(S5_CHUNK):
                piece = u_scr[c, pl.ds(t, nch, stride=S5_CHUNK), :]
                v_ref[c, 0, :, t * 128:(t + 1) * 128] = piece.astype(BF16)


def _inproj(x, mod4, mod_row, nw, w, wg, tm, tn, s5w):
    b_, l_, d = x.shape
    n = w.shape[1]
    assert tn >= s5w and tm % (16 * S5_CHUNK) == 0
    nblk, vw = s5w // 128, S5_CHUNK * 128
    return pl.pallas_call(
        _inproj_body,
        grid=(b_, l_ // tm, n // tn),
        in_specs=[
            pl.BlockSpec((1, tm, d), lambda b, i, j: (b, i, 0)),
            pl.BlockSpec((1, 1, 1, d), lambda b, i, j: (mod_row(b), 0, 0, 0)),
            pl.BlockSpec((1, 1, 1, d), lambda b, i, j: (mod_row(b), 1, 0, 0)),
            pl.BlockSpec((1, d), lambda b, i, j: (0, 0)),
            pl.BlockSpec((d, tn), lambda b, i, j: (0, j)),
            pl.BlockSpec((d, 128), lambda b, i, j: (0, 0)),
        ],
        out_specs=[
            pl.BlockSpec((1, tm, tn), lambda b, i, j: (b, i, j)),
            pl.BlockSpec((1, tm, 128), lambda b, i, j: (b, i, 0)),
            pl.BlockSpec((nblk, 1, tm // S5_CHUNK, vw), lambda b, i, j: (0, b, i, 0)),
        ],
        out_shape=[
            jax.ShapeDtypeStruct((b_, l_, n), BF16),
            jax.ShapeDtypeStruct((b_, l_, 128), F32),
            jax.ShapeDtypeStruct((nblk, b_, l_ // S5_CHUNK, vw), BF16),
        ],
        scratch_shapes=[pltpu.VMEM((tm, d), BF16), pltpu.VMEM((nblk, tm, 128), F32)],
        compiler_params=_cparams(("parallel", "parallel", "arbitrary")),
        name="inproj",
    )(x, mod4, mod4, nw, w, wg)


def _s5_factors(lam_re, lam_im, log_dt, b_re, b_im, c_re, c_im, d_skip):
    tc = S5_CHUNK
    g_, p_ = lam_re.shape[1:]
    h_ = b_re.shape[-1]
    dt = jnp.exp(log_dt)[..., None]
    ar, ai = lam_re * dt, lam_im * dt

    def powers(k):
        k = jnp.asarray(k, F32)
        mag = jnp.exp(ar[..., None] * k)
        return mag * jnp.cos(ai[..., None] * k), mag * jnp.sin(ai[..., None] * k)

    num_re = jnp.expm1(ar) * jnp.cos(ai) - 2.0 * jnp.sin(0.5 * ai) ** 2
    num_im = jnp.exp(ar) * jnp.sin(ai)
    den = lam_re * lam_re + lam_im * lam_im
    f_re = (num_re * lam_re + num_im * lam_im) / den
    f_im = (num_im * lam_re - num_re * lam_im) / den
    bb_re = f_re[..., None] * b_re - f_im[..., None] * b_im
    bb_im = f_re[..., None] * b_im + f_im[..., None] * b_re
    t_idx = jnp.arange(tc)

    def halves(a, b):
        return jnp.concatenate([a, b], axis=-1).transpose(1, 0, 2, 3)

    def power_rows(expo_f, expo_b):
        re_f, im_f = powers(expo_f)
        re_b, im_b = powers(expo_b)
        re = jnp.stack([re_f[0], re_b[1]]).transpose(0, 1, 3, 2)
        im = jnp.stack([im_f[0], im_b[1]]).transpose(0, 1, 3, 2)
        return re, im

    e_re, e_im = power_rows(tc - 1 - t_idx, t_idx)
    bt_re, bt_im = bb_re.transpose(0, 1, 3, 2), bb_im.transpose(0, 1, 3, 2)
    ea = jnp.stack([halves(e_re, e_re), halves(-e_im, e_im)], axis=2)
    ba = jnp.stack([halves(bt_re, bt_im), halves(bt_im, bt_re)], axis=2)

    def readout_rows(expo_f, expo_b):
        p_re, p_im = power_rows(expo_f, expo_b)
        return jnp.stack([halves(p_re, -p_im), halves(-p_im, -p_re)], axis=2)

    pq = readout_rows(t_idx - (tc - 1), -t_idx)
    po = readout_rows(t_idx + 1, tc - t_idx)
    ca = jnp.stack([halves(c_re, c_re), halves(c_im, c_im)], axis=2)

    a_re, a_im = powers(jnp.array([tc]))
    a_re, a_im = a_re[..., 0], a_im[..., 0]
    zeros = jnp.zeros_like(a_re[0])
    rows = []
    for dr in range(2):
        rows += [jnp.concatenate([a_re[dr], a_re[dr]], -1),
                 jnp.concatenate([-a_im[dr], a_im[dr]], -1),
                 jnp.concatenate([a_im[dr], -a_im[dr]], -1)]
    rows += [jnp.concatenate([zeros, zeros], -1)] * 2
    acoef = jnp.stack(rows, axis=1)
    dvec = jnp.tile(d_skip.reshape(g_, h_), (1, tc)).reshape(g_, 1, tc * h_)
    return ea, ba, pq, po, ca, acoef, dvec


def _split_bf16(x):
    hi = x.astype(BF16)
    return hi, (x - hi.astype(F32)).astype(BF16)


def _s5_body(vc_ref, vl_ref, ea_ref, ba_ref, pq_ref, po_ref, ca_ref, a_ref, d_ref, o_ref, y_scr, s_scr, h_scr,
               *, n_ctx, n_lat, nb):
    gb = ea_ref.shape[0]
    pw = 2 * S5_STATE
    gl = S5_GROUP
    kd = S5_CHUNK * gl
    lanes = 128
    n_tot = n_ctx + n_lat
    nt = (((1,), (1,)), ((), ()))

    def outer(x_ref, y_ref, q, dr):
        z = (x_ref[q, dr, 0][:, None, :] * y_ref[q, dr, 0][None, :, :]
             + x_ref[q, dr, 1][:, None, :] * y_ref[q, dr, 1][None, :, :])
        return z.reshape(kd, pw)
    v = jnp.concatenate([p for b in range(nb) for p in (vc_ref[0, b], vl_ref[0, b])], axis=0)
    width = v.shape[1]

    r_in = lax.broadcasted_iota(I32, (width, kd), 0)
    c_in = lax.broadcasted_iota(I32, (width, kd), 1)
    tgt_in = (r_in // lanes) * gl + (r_in % gl)
    grp_in = (r_in % lanes) // gl
    r_out = lax.broadcasted_iota(I32, (kd, width), 0)
    c_out = lax.broadcasted_iota(I32, (kd, width), 1)
    tgt_out = (c_out // lanes) * gl + (c_out % gl)
    grp_out = (c_out % lanes) // gl

    rs = lax.broadcasted_iota(I32, (kd, kd), 0)
    ct = lax.broadcasted_iota(I32, (kd, kd), 1)
    fmask = ct // gl >= rs // gl
    bmask = ct // gl <= rs // gl
    diag = rs == ct

    for q in range(gb):
        sel = jnp.where(jnp.logical_and(c_in == tgt_in, grp_in == q), 1.0, 0.0).astype(BF16)
        x2 = _dot(v, sel).astype(BF16)
        mm, ws = [], []
        for dr in range(2):
            ws.append(outer(ea_ref, ba_ref, q, dr))
            a_hi, a_lo = _split_bf16(ws[dr])
            b_hi, b_lo = _split_bf16(outer(pq_ref, ca_ref, q, dr))
            mm.append(lax.dot_general(a_hi, b_hi, nt, preferred_element_type=F32)
                      + lax.dot_general(a_hi, b_lo, nt, preferred_element_type=F32)
                      + lax.dot_general(a_lo, b_hi, nt, preferred_element_type=F32))
        m = jnp.where(fmask, mm[0], 0.0) + jnp.where(bmask, mm[1], 0.0) + jnp.where(diag, d_ref[q], 0.0)
        wsf, wsb = ws
        wcat = jnp.concatenate(
            [m, wsf, pltpu.roll(wsf, S5_STATE, 1), wsb, pltpu.roll(wsb, S5_STATE, 1)], axis=1).astype(BF16)
        r = _dot(x2, wcat)
        for b in range(nb):
            rb = r[b * n_tot:(b + 1) * n_tot, :]
            for j in range(kd // lanes):
                y_scr[q, j, pl.ds(b, n_tot, stride=nb), :] = rb[:, j * lanes:(j + 1) * lanes]
            for j in range(4):
                s_scr[q, j, pl.ds(b, n_tot, stride=nb), :] = rb[:, kd + j * pw:kd + (j + 1) * pw]

    a1f, a2f, a2sf = a_ref[:, 0:1, :], a_ref[:, 1:2, :], a_ref[:, 2:3, :]
    a1b, a2b, a2sb = a_ref[:, 3:4, :], a_ref[:, 4:5, :], a_ref[:, 5:6, :]
    first = lax.broadcasted_iota(I32, (gb, 2 * nb, pw), 1) < nb

    def both(x):
        rr = pltpu.roll(x, nb, 1)
        return jnp.where(first, x, rr), jnp.where(first, rr, x)

    def step(pf, pb, carry):
        hf, hfs, hb, hbs = carry
        rf = pl.ds(pl.multiple_of(pf * 2 * nb, 2 * nb), 2 * nb)
        rb = pl.ds(pl.multiple_of(pb * 2 * nb, 2 * nb), 2 * nb)
        sfa, sfb = both(s_scr[:, 0, rf, :])
        sfsa, sfsb = both(s_scr[:, 1, rf, :])
        sba, sbb = both(s_scr[:, 2, rb, :])
        sbsa, sbsb = both(s_scr[:, 3, rb, :])
        hf1 = a1f * hf + a2f * hfs + sfa
        hfs1 = a1f * hfs + a2sf * hf + sfsa
        h_scr[:, 0, rf, :] = jnp.where(first, hf, hf1)
        hf2 = a1f * hf1 + a2f * hfs1 + sfb
        hfs2 = a1f * hfs1 + a2sf * hf1 + sfsb
        hb1 = a1b * hb + a2b * hbs + sbb
        hbs1 = a1b * hbs + a2sb * hb + sbsb
        h_scr[:, 1, rb, :] = jnp.where(first, hb1, hb)
        hb2 = a1b * hb1 + a2b * hbs1 + sba
        hbs2 = a1b * hbs1 + a2sb * hb1 + sbsa
        return hf2, hfs2, hb2, hbs2

    z = jnp.zeros((gb, 2 * nb, pw), F32)
    carry = (z, z, z, z)
    pc, pl_ = n_ctx // 2, n_lat // 2
    carry = lax.fori_loop(0, pc, lambda i, c: step(i, pc - 1 - i, c), carry)
    carry = lax.fori_loop(0, pl_, lambda i, c: step(pc + i, pc + pl_ - 1 - i, c), carry)

    lo = n_ctx * nb
    for q in range(gb):
        yq = (lax.dot_general(h_scr[q, 0].astype(BF16), outer(po_ref, ca_ref, q, 0).astype(BF16), nt,
                              preferred_element_type=F32)
              + lax.dot_general(h_scr[q, 1].astype(BF16), outer(po_ref, ca_ref, q, 1).astype(BF16), nt,
                                preferred_element_type=F32))
        for j in range(kd // lanes):
            y_scr[q, j] = y_scr[q, j] + yq[:, j * lanes:(j + 1) * lanes]
        y = jnp.concatenate(
            [jnp.concatenate([y_scr[q, j, pl.ds(lo + b, n_lat, stride=nb), :] for j in range(kd // lanes)], axis=-1)
             for b in range(nb)], axis=0)
        selt = jnp.where(jnp.logical_and(r_out == tgt_out, grp_out == q), 1.0, 0.0).astype(BF16)
        placed = _dot(y.astype(BF16), selt)
        if q == 0:
            o_ref[0] = placed
        else:
            o_ref[0] = o_ref[0] + placed


def _s5(v_ctx, v_lat, ea, ba, pq, po, ca, acoef, dvec):
    nblk, nb, n_ctx, width = v_ctx.shape
    n_lat = v_lat.shape[2]
    rows = (n_ctx + n_lat) * nb
    g_ = ea.shape[0]
    gb = g_ // nblk
    pw = 2 * S5_STATE
    kd = S5_CHUNK * S5_GROUP
    out_rows = n_lat * nb
    fac = lambda: pl.BlockSpec((gb, 2, 2, S5_CHUNK, pw), lambda i: (i, 0, 0, 0, 0))
    return pl.pallas_call(
        functools.partial(_s5_body, n_ctx=n_ctx, n_lat=n_lat, nb=nb),
        grid=(nblk,),
        in_specs=[
            pl.BlockSpec((1, nb, n_ctx, width), lambda i: (i, 0, 0, 0)),
            pl.BlockSpec((1, nb, n_lat, width), lambda i: (i, 0, 0, 0)),
            fac(), fac(), fac(), fac(), fac(),
            pl.BlockSpec((gb, 8, pw), lambda i: (i, 0, 0)),
            pl.BlockSpec((gb, 1, kd), lambda i: (i, 0, 0)),
        ],
        out_specs=pl.BlockSpec((1, out_rows, width), lambda i: (i, 0, 0)),
        out_shape=jax.ShapeDtypeStruct((nblk, out_rows, width), F32),
        scratch_shapes=[
            pltpu.VMEM((gb, kd // 128, rows, 128), F32),
            pltpu.VMEM((gb, 4, rows, pw), F32),
            pltpu.VMEM((gb, 2, rows, pw), F32),
        ],
        compiler_params=_cparams(("parallel",)),
        name="s5",
    )(v_ctx, v_lat, ea, ba, pq, po, ca, acoef, dvec)


def _log_sigmoid(z):
    return jnp.minimum(z, 0.0) - jnp.log1p(jnp.exp(-jnp.abs(z)))


def _chunk_cumsum(g, pos, reverse):
    n = g.shape[0]
    x = g
    sh = 1
    while sh < GLA_CHUNK:
        if reverse:
            x = x + jnp.where(pos < GLA_CHUNK - sh, pltpu.roll(x, n - sh, 0), 0.0)
        else:
            x = x + jnp.where(pos >= sh, pltpu.roll(x, sh, 0), 0.0)
        sh *= 2
    return x


def _gla_body(q_ref, k_ref, v_ref, kc_ref, vc_ref, gk_ref, gkc_ref, up_ref, bias_ref, o_ref,
              bc_scr, kb_scr, qb_scr, st_ref, *, scale):
    c = GLA_CHUNK
    l_ = q_ref.shape[1]
    lc = kc_ref.shape[1]
    n_lat, n_ctx = l_ // c, lc // c
    row = lax.broadcasted_iota(I32, (c, c), 0)
    col = lax.broadcasted_iota(I32, (c, c), 1)
    nt = (((1,), (1,)), ((), ()))
    tn = (((0,), (0,)), ((), ()))

    for dr in range(2):
        smask = (col <= row) if dr == 0 else (col > row)
        up = up_ref[dr]
        bias = bias_ref[dr]
        edge_tile, edge_row = (c - 8, 7) if dr == 0 else (0, 0)

        for gk_r, k_r, base, n, latent in ((gkc_ref, kc_ref, 0, lc, False), (gk_ref, k_ref, lc, l_, True)):
            z = _dot(gk_r[0].astype(BF16), up) + bias
            g = _log_sigmoid(z) * (1.0 / GLA_GATE_NORM)
            pos = lax.broadcasted_iota(I32, g.shape, 0) & (c - 1)
            bc = _chunk_cumsum(g, pos, reverse=(dr == 1))
            bc_scr[base:base + n, :] = bc
            kb_scr[base:base + n, :] = k_r[0].astype(F32) * jnp.exp(-bc)
            if latent:
                qb_scr[...] = (q_ref[0].astype(F32) * (scale * jnp.exp(bc))).astype(BF16)

        def chunk_rows(n, base):
            r0 = pl.multiple_of(n * c, c)
            tot = bc_scr[pl.ds(base + r0 + edge_tile, 8), :][edge_row:edge_row + 1, :]
            return r0, pl.ds(base + r0, c), jnp.exp(tot)

        def update_state(kb, vv, decay):
            kd = (kb * decay).astype(BF16)
            ds_t = lax.dot_general(vv, kd, tn, preferred_element_type=F32)
            st_ref[...] = decay * st_ref[...] + ds_t

        def ctx_step(i, _):
            n = i if dr == 0 else n_ctx - 1 - i
            r0, rs, decay = chunk_rows(n, 0)
            update_state(kb_scr[rs, :], vc_ref[0, pl.ds(r0, c), :], decay)
            return 0

        def lat_step(i, _):
            n = i if dr == 0 else n_lat - 1 - i
            r0, rs, decay = chunk_rows(n, lc)
            ro = pl.ds(r0, c)
            kb = kb_scr[rs, :]
            vv = v_ref[0, ro, :]
            qb = qb_scr[ro, :]
            scores = lax.dot_general(qb, kb.astype(BF16), nt, preferred_element_type=F32)
            scores = jnp.where(smask, scores, 0.0).astype(BF16)
            o = _dot(scores, vv) + lax.dot_general(qb, st_ref[...].astype(BF16), nt,
                                                   preferred_element_type=F32)
            if dr == 0:
                o_ref[0, ro, :] = o
            else:
                o_ref[0, ro, :] = o_ref[0, ro, :] + o
            update_state(kb, vv, decay)
            return 0

        st_ref[...] = jnp.zeros_like(st_ref)
        lax.fori_loop(0, n_ctx, ctx_step, 0, unroll=2)
        lax.fori_loop(0, n_lat, lat_step, 0, unroll=4)


def _gla(p_lat, p_ctx, gk_lat, gk_ctx, up_pad, bias, dk, dv):
    b_, l_, _ = p_lat.shape
    lc = p_ctx.shape[1]
    hh = GLA_HEADS
    s5w = hh * dv
    q0, k0, v0 = s5w // dk, (s5w + hh * dk) // dk, (s5w + 2 * hh * dk) // dv
    kc0, vc0 = s5w // dk, (s5w + hh * dk) // dv
    return pl.pallas_call(
        functools.partial(_gla_body, scale=dk ** -0.5),
        grid=(b_, hh),
        in_specs=[
            pl.BlockSpec((1, l_, dk), lambda b, h: (b, 0, q0 + h)),
            pl.BlockSpec((1, l_, dk), lambda b, h: (b, 0, k0 + h)),
            pl.BlockSpec((1, l_, dv), lambda b, h: (b, 0, v0 + h)),
            pl.BlockSpec((1, lc, dk), lambda b, h: (b, 0, kc0 + h)),
            pl.BlockSpec((1, lc, dv), lambda b, h: (b, 0, vc0 + h)),
            pl.BlockSpec((1, l_, 128), lambda b, h: (b, 0, 0)),
            pl.BlockSpec((1, lc, 128), lambda b, h: (b, 0, 0)),
            pl.BlockSpec((2, 128, dk), lambda b, h: (0, 0, h)),
            pl.BlockSpec((2, 1, dk), lambda b, h: (0, 0, h)),
        ],
        out_specs=pl.BlockSpec((1, l_, dv), lambda b, h: (b, 0, h)),
        out_shape=jax.ShapeDtypeStruct((b_, l_, hh * dv), F32),
        scratch_shapes=[
            pltpu.VMEM((lc + l_, dk), F32),
            pltpu.VMEM((lc + l_, dk), F32),
            pltpu.VMEM((l_, dk), BF16),
            pltpu.VMEM((dv, dk), F32),
        ],
        compiler_params=_cparams(("parallel", "parallel")),
        name="gla",
    )(p_lat, p_lat, p_lat, p_ctx, p_ctx, gk_lat, gk_ctx, up_pad, bias)


def _gelu_tanh(x):
    return 0.5 * x * (1.0 + jnp.tanh(math.sqrt(2.0 / math.pi) * (x + 0.044715 * (x * x * x))))


def _merge_body(ys_ref, yg_ref, go_ref, gs_ref, gg_ref, x_ref, g1_ref, sh2_ref, sc2_ref,
                glu_ref, ws5_ref, wgla_ref, wout_ref, gnw_ref, n2w_ref, rw_ref,
                x1_ref, h2_ref, lg_ref, ys_scr, *, dv):
    nch = ys_ref.shape[1]
    for c in range(ys_ref.shape[0]):
        for t in range(S5_CHUNK):
            ys_scr[c, pl.ds(t, nch, stride=S5_CHUNK), :] = ys_ref[c, :, t * 128:(t + 1) * 128]
    a = _gelu_tanh(jnp.concatenate([ys_scr[c] for c in range(ys_ref.shape[0])], axis=-1))
    a = a * _sigmoid(_dot(a.astype(BF16), glu_ref[...]))
    pa = _dot(a.astype(BF16), ws5_ref[...])

    yg = yg_ref[0]
    parts = []
    for h in range(GLA_HEADS):
        oh = yg[:, h * dv:(h + 1) * dv]
        ms = jnp.mean(oh * oh, axis=-1, keepdims=True)
        parts.append(oh * lax.rsqrt(ms + EPS) * gnw_ref[...])
    gl = jnp.concatenate(parts, axis=-1) * _silu(go_ref[0].astype(F32))
    pb = _dot(gl.astype(BF16), wgla_ref[...])

    m = _sigmoid(gs_ref[0].astype(F32)) * pa + _sigmoid(gg_ref[0].astype(F32)) * pb
    y = _dot(m.astype(BF16), wout_ref[...])
    x1 = x_ref[0] + g1_ref[0, 0] * y
    x1_ref[0] = x1

    h2 = _modulated_norm(x1, n2w_ref[...], sh2_ref[0, 0], sc2_ref[0, 0])
    h_hi = h2.astype(BF16)
    h_lo = (h2 - h_hi.astype(F32)).astype(BF16)
    half = h2.shape[-1] // 2
    _store_row_tiles(h2_ref, (0,), _pack_bf16_pair(h2[:, :half], h2[:, half:]))
    lg_ref[0] = _dot(h_hi, rw_ref[0]) + _dot(h_lo, rw_ref[0]) + _dot(h_hi, rw_ref[1])


def _resident(shape):
    nd = len(shape)
    return pl.BlockSpec(shape, lambda b, i: (0,) * nd, pipeline_mode=pl.Buffered(1))


def _merge(ys, yg, p_lat, x, mod4, glu_w, ws5, wgla, wout, gnw, n2w, rw2, dv, tm):
    b_, l_, d = x.shape
    nblk, _, vw = ys.shape
    w5 = nblk * 128
    nch = tm // S5_CHUNK
    go0 = (p_lat.shape[-1] - 2 * d - w5) // w5
    gs0 = (p_lat.shape[-1] - 2 * d) // d
    modspec = lambda r: pl.BlockSpec((1, 1, 1, d), lambda b, i: (b, r, 0, 0))
    return pl.pallas_call(
        functools.partial(_merge_body, dv=dv),
        grid=(b_, l_ // tm),
        in_specs=[
            pl.BlockSpec((nblk, nch, vw), lambda b, i: (0, b * (l_ // tm) + i, 0)),
            pl.BlockSpec((1, tm, w5), lambda b, i: (b, i, 0)),
            pl.BlockSpec((1, tm, w5), lambda b, i: (b, i, go0)),
            pl.BlockSpec((1, tm, d), lambda b, i: (b, i, gs0)),
            pl.BlockSpec((1, tm, d), lambda b, i: (b, i, gs0 + 1)),
            pl.BlockSpec((1, tm, d), lambda b, i: (b, i, 0)),
            modspec(2), modspec(3), modspec(4),
            _resident(glu_w.shape), _resident(ws5.shape), _resident(wgla.shape), _resident(wout.shape),
            _resident(gnw.shape), _resident(n2w.shape), _resident(rw2.shape),
        ],
        out_specs=[
            pl.BlockSpec((1, tm, d), lambda b, i: (b, i, 0)),
            pl.BlockSpec((1, tm * (d // 256), 128), lambda b, i: (b, i, 0)),
            pl.BlockSpec((1, tm, 128), lambda b, i: (b, i, 0)),
        ],
        out_shape=[
            jax.ShapeDtypeStruct((b_, l_, d), F32),
            jax.ShapeDtypeStruct((b_, l_ * (d // 256), 128), U32),
            jax.ShapeDtypeStruct((b_, l_, 128), F32),
        ],
        scratch_shapes=[pltpu.VMEM((nblk, tm, 128), F32)],
        compiler_params=_cparams(("parallel", "parallel")),
        name="merge",
    )(ys, yg, p_lat, p_lat, p_lat, x, mod4, mod4, mod4, glu_w, ws5, wgla, wout, gnw, n2w, rw2)


def _route_body(lg_ref, bias_ref, eidx_ref, w_ref, rank_ref, cnt_ref, carry_ref):
    ne, tt = lg_ref.shape
    gsz = ne // N_GROUPS
    neg = -jnp.inf

    @pl.when(pl.program_id(0) == 0)
    def _():
        carry_ref[...] = jnp.zeros_like(carry_ref)

    sc = _sigmoid(lg_ref[...])
    ch = sc + bias_ref[...]

    def first_max(v, idx, big):
        m = jnp.max(v, axis=0, keepdims=True)
        i = jnp.min(jnp.where(v == m, idx, big), axis=0, keepdims=True)
        return m, i

    midx = lax.broadcasted_iota(I32, (gsz, tt), 0).astype(F32)
    gs_rows = []
    for g in range(N_GROUPS):
        v = ch[g * gsz:(g + 1) * gsz, :]
        m1, i1 = first_max(v, midx, float(gsz))
        m2 = jnp.max(jnp.where(midx == i1, neg, v), axis=0, keepdims=True)
        gs_rows.append(m1 + m2)
    gscore = jnp.concatenate(gs_rows, axis=0)

    gidx = lax.broadcasted_iota(I32, (N_GROUPS, tt), 0).astype(F32)
    gsel = jnp.zeros((N_GROUPS, tt), F32)
    cur = gscore
    for _ in range(TOPK_GROUPS):
        _, i = first_max(cur, gidx, float(N_GROUPS))
        hit = gidx == i
        gsel = jnp.where(hit, 1.0, gsel)
        cur = jnp.where(hit, neg, cur)

    masked = jnp.concatenate(
        [jnp.where(gsel[g:g + 1, :] > 0.0, ch[g * gsz:(g + 1) * gsz, :], neg) for g in range(N_GROUPS)], axis=0)

    eiota = lax.broadcasted_iota(I32, (ne, tt), 0).astype(F32)
    sel = jnp.zeros((ne, tt), F32)
    cur = masked
    idx_rows, s_rows = [], []
    for _ in range(TOP_K):
        _, i = first_max(cur, eiota, float(ne))
        hit = eiota == i
        idx_rows.append(i)
        s_rows.append(jnp.sum(jnp.where(hit, sc, 0.0), axis=0, keepdims=True))
        sel = jnp.where(hit, 1.0, sel)
        cur = jnp.where(hit, neg, cur)
    idx = jnp.concatenate(idx_rows, axis=0)
    s = jnp.concatenate(s_rows, axis=0)
    w_ref[...] = s / jnp.sum(s, axis=0, keepdims=True) * ROUTED_SCALE
    eidx_ref[...] = idx.astype(I32)

    cw = 256
    a_i = lax.broadcasted_iota(I32, (cw, cw), 0)
    b_i = lax.broadcasted_iota(I32, (cw, cw), 1)
    upper = jnp.where(a_i < b_i, 1.0, 0.0).astype(BF16)
    carry = carry_ref[:, 0:1]
    pieces = []
    for c0 in range(0, tt, cw):
        sc_c = sel[:, c0:c0 + cw]
        pieces.append(_dot(sc_c.astype(BF16), upper) + carry)
        carry = carry + jnp.sum(sc_c, axis=1, keepdims=True)
    rank_full = jnp.concatenate(pieces, axis=1)
    carry_ref[...] = jnp.broadcast_to(carry, carry_ref.shape)
    cnt_ref[...] = jnp.broadcast_to(carry, cnt_ref.shape)
    rk = [jnp.sum(jnp.where(eiota == idx_rows[k], rank_full, 0.0), axis=0, keepdims=True) for k in range(TOP_K)]
    rank_ref[...] = jnp.concatenate(rk, axis=0).astype(I32)


def _route(logits_t, bias):
    ne, t = logits_t.shape
    tt = 1024
    return pl.pallas_call(
        _route_body,
        grid=(t // tt,),
        in_specs=[
            pl.BlockSpec((ne, tt), lambda i: (0, i)),
            pl.BlockSpec((ne, 1), lambda i: (0, 0)),
        ],
        out_specs=[
            pl.BlockSpec((TOP_K, tt), lambda i: (0, i)),
            pl.BlockSpec((TOP_K, tt), lambda i: (0, i)),
            pl.BlockSpec((TOP_K, tt), lambda i: (0, i)),
            pl.BlockSpec((ne, 128), lambda i: (0, 0)),
        ],
        out_shape=[
            jax.ShapeDtypeStruct((TOP_K, t), I32),
            jax.ShapeDtypeStruct((TOP_K, t), F32),
            jax.ShapeDtypeStruct((TOP_K, t), I32),
            jax.ShapeDtypeStruct((ne, 128), F32),
        ],
        scratch_shapes=[pltpu.VMEM((ne, 128), F32)],
        compiler_params=_cparams(("arbitrary",)),
        name="route",
    )(logits_t, bias.reshape(ne, 1))


def _slots_body(cnt_ref, eidx_ref, rank_ref, dest_ref, be_ref, first_ref, nxt_ref, nu_ref, pad_ref, off_smem):
    ne = cnt_ref.shape[0]
    nb = be_ref.shape[0]
    r_ = EXPERT_ROWS
    shift = r_.bit_length() - 1

    @pl.when(pl.program_id(0) == 0)
    def _():
        def per_expert(e, off):
            cnt = cnt_ref[e]
            n = lax.shift_right_logical(cnt + (r_ - 1), shift)
            off_smem[e] = off
            pad_ref[0, e] = off * r_ + cnt
            pad_ref[1, e] = n * r_ - cnt

            def fill(j, _):
                be_ref[off + j] = e
                first_ref[off + j] = jnp.where(j == 0, 1, 0)
                return 0
            lax.fori_loop(0, n, fill, 0)
            return off + n

        n_used = lax.fori_loop(0, ne, per_expert, 0)
        nu_ref[0] = n_used
        last_e = be_ref[n_used - 1]

        def tail(j, _):
            be_ref[j] = last_e
            first_ref[j] = 0
            return 0
        lax.fori_loop(n_used, nb, tail, 0)

        def back(i, nx):
            j = nb - 1 - i
            nxt_ref[j] = nx
            return jnp.where(first_ref[j] == 1, be_ref[j], nx)
        lax.fori_loop(0, nb, back, -1)

    ei = eidx_ref[...]
    acc = rank_ref[...]
    for e in range(ne):
        acc = acc + jnp.where(ei == e, off_smem[e] * r_, 0)
    dest_ref[...] = acc


def _slots(counts, eidx, rank, nb):
    k_, t = eidx.shape
    ne = counts.shape[0]
    tl = 1024
    smem = lambda: pl.BlockSpec(memory_space=pltpu.SMEM)
    chunk = lambda: pl.BlockSpec((k_, tl), lambda i: (0, i))
    return pl.pallas_call(
        _slots_body,
        grid=(t // tl,),
        in_specs=[smem(), chunk(), chunk()],
        out_specs=[chunk(), smem(), smem(), smem(), smem(), smem()],
        out_shape=[
            jax.ShapeDtypeStruct((k_, t), I32),
            jax.ShapeDtypeStruct((nb,), I32),
            jax.ShapeDtypeStruct((nb,), I32),
            jax.ShapeDtypeStruct((nb,), I32),
            jax.ShapeDtypeStruct((1,), I32),
            jax.ShapeDtypeStruct((2, ne), I32),
        ],
        scratch_shapes=[pltpu.SMEM((ne,), I32)],
        compiler_params=_cparams(("arbitrary",)),
        name="slots",
    )(counts, eidx, rank)


def _dispatch_body(pad_ref, nu_ref, dest_hbm, h2_hbm, x_hbm, idx_smem, buf, zero_blk, isem, insem, ssem, zsem):
    i = pl.program_id(0)
    n = pl.num_programs(0)
    tiles = x_hbm.shape[1]
    r_ = zero_blk.shape[0]
    n_blocks = x_hbm.shape[0] // r_
    tm = buf.shape[1] // tiles
    ne = pad_ref.shape[1]
    slot = i % 2
    nslot = 1 - slot

    def idx_copy(blk, s):
        cols = pl.ds(pl.multiple_of(blk * tm, tm), tm)
        return pltpu.make_async_copy(dest_hbm.at[:, cols], idx_smem.at[s], isem.at[s])

    def in_copy(blk, s):
        rows = pl.ds(pl.multiple_of(blk * tm * tiles, tm * tiles), tm * tiles)
        return pltpu.make_async_copy(h2_hbm.at[rows], buf.at[s], insem.at[s])

    def wait_scatter(s):
        span = x_hbm.at[pl.ds(0, TOP_K * tm)]
        pltpu.make_async_copy(span, span, ssem.at[s]).wait()

    def pad_fill(e, wait):
        start, length = pad_ref[0, e], pad_ref[1, e]
        size = r_ // 2
        while size >= 1:
            take = (length & size) != 0

            @pl.when(take)
            def _(start=start, size=size):
                cp = pltpu.make_async_copy(zero_blk.at[pl.ds(0, size)], x_hbm.at[pl.ds(start, size)], zsem)
                if wait:
                    cp.wait()
                else:
                    cp.start(priority=1)
            start = start + jnp.where(take, size, 0)
            size //= 2
        return 0

    def zero_block_copy(blk):
        return pltpu.make_async_copy(zero_blk, x_hbm.at[pl.ds(pl.multiple_of(blk * r_, r_), r_)], zsem)

    @pl.when(i == 0)
    def _():
        zero_blk[...] = jnp.zeros_like(zero_blk)
        idx_copy(0, 0).start()
        in_copy(0, 0).start()

        def blk_body(b, _):
            zero_block_copy(b).start(priority=1)
            return 0
        lax.fori_loop(nu_ref[0], n_blocks, blk_body, 0)
        lax.fori_loop(0, ne, lambda e, _: pad_fill(e, False), 0)

    @pl.when(i >= 1)
    def _():
        wait_scatter(nslot)

    @pl.when(i + 1 < n)
    def _():
        in_copy(i + 1, nslot).start()
        idx_copy(i + 1, nslot).start()

    in_copy(i, slot).wait()
    idx_copy(i, slot).wait()
    for j in range(tm):
        for k in range(TOP_K):
            pltpu.make_async_copy(buf.at[slot, pl.ds(j * tiles, tiles)], x_hbm.at[idx_smem[slot, k, j]],
                                  ssem.at[slot]).start()

    @pl.when(i + 1 == n)
    def _():
        wait_scatter(slot)
        lax.fori_loop(0, ne, lambda e, _: pad_fill(e, True), 0)

        def blk_wait(b, _):
            zero_block_copy(0).wait()
            return 0
        lax.fori_loop(nu_ref[0], n_blocks, blk_wait, 0)


def _dispatch(pad, n_used, dest, h2p, n_slots, tm):
    k_, t = dest.shape
    tiles = h2p.shape[0] // t
    grid_spec = pltpu.PrefetchScalarGridSpec(
        num_scalar_prefetch=2,
        grid=(t // tm,),
        in_specs=[pl.BlockSpec(memory_space=pl.ANY), pl.BlockSpec(memory_space=pl.ANY)],
        out_specs=pl.BlockSpec(memory_space=pl.ANY),
        scratch_shapes=[
            pltpu.SMEM((2, k_, tm), I32),
            pltpu.VMEM((2, tm * tiles, 128), U32),
            pltpu.VMEM((EXPERT_ROWS, tiles, 128), U32),
            pltpu.SemaphoreType.DMA((2,)),
            pltpu.SemaphoreType.DMA((2,)),
            pltpu.SemaphoreType.DMA((2,)),
            pltpu.SemaphoreType.DMA,
        ],
    )
    return pl.pallas_call(
        _dispatch_body,
        grid_spec=grid_spec,
        out_shape=jax.ShapeDtypeStruct((n_slots, tiles, 128), U32),
        compiler_params=_cparams(("arbitrary",)),
        name="dispatch",
    )(pad, n_used, dest, h2p)


def _expert_body(be_ref, first_ref, nxt_ref, nu_ref, x_hbm, wg_hbm, wu_hbm, wd_hbm, y_ref,
                  xbuf, wg_f, wu_f, wd_f, wg_s, wu_s, wd_s, xsem, wsem):
    i = pl.program_id(0)
    n_used = nu_ref[0]
    r_ = EXPERT_ROWS
    tiles = xbuf.shape[1] // r_
    slot = i % 2
    nslot = 1 - slot

    def x_copy(blk, s):
        rows = pl.ds(pl.multiple_of(blk * r_ * tiles, r_ * tiles), r_ * tiles)
        return pltpu.make_async_copy(x_hbm.at[rows], xbuf.at[s], xsem.at[s])

    def weight_copies(e):
        return (pltpu.make_async_copy(wg_hbm.at[e], wg_f, wsem.at[0]),
                pltpu.make_async_copy(wu_hbm.at[e], wu_f, wsem.at[1]),
                pltpu.make_async_copy(wd_hbm.at[e], wd_f, wsem.at[2]))

    @pl.when(i == 0)
    def _():
        for cp in weight_copies(be_ref[0]):
            cp.start(priority=1)
        x_copy(0, 0).start()

    @pl.when(first_ref[i] == 1)
    def _():
        for cp in weight_copies(be_ref[i]):
            cp.wait()
        wg_s[...] = wg_f[...].astype(BF16)
        wu_s[...] = wu_f[...].astype(BF16)
        wd_s[...] = wd_f[...].astype(BF16)

        @pl.when(nxt_ref[i] >= 0)
        def _():
            for cp in weight_copies(nxt_ref[i]):
                cp.start(priority=1)

    @pl.when(i < n_used)
    def _():
        @pl.when(i + 1 < n_used)
        def _():
            x_copy(i + 1, nslot).start()

        x_copy(i, slot).wait()
        lo, hi = _unpack_bf16_pair(_load_row_tiles(xbuf, (slot,), 0, r_, tiles))
        lo = lo.astype(BF16)
        hi = hi.astype(BF16)
        half = lo.shape[-1]
        a = _dot(lo, wg_s[0:half, :]) + _dot(hi, wg_s[half:, :])
        u = _dot(lo, wu_s[0:half, :]) + _dot(hi, wu_s[half:, :])
        hid = (_silu(a) * u).astype(BF16)
        y_lo = _dot(hid, wd_s[:, 0:half])
        y_hi = _dot(hid, wd_s[:, half:])
        _store_row_tiles(y_ref, (), _pack_bf16_pair(y_lo, y_hi))

    @pl.when(i >= n_used)
    def _():
        y_ref[...] = jnp.zeros_like(y_ref)


def _experts(block_e, first, nxt, n_used, x_sorted, wg, wu, wd):
    nb = block_e.shape[0]
    ne, d, f = wg.shape
    r_ = EXPERT_ROWS
    tiles = d // 256
    hbm = lambda: pl.BlockSpec(memory_space=pl.ANY)
    grid_spec = pltpu.PrefetchScalarGridSpec(
        num_scalar_prefetch=4,
        grid=(nb,),
        in_specs=[hbm(), hbm(), hbm(), hbm()],
        out_specs=pl.BlockSpec((r_ * tiles, 128), lambda i, *_: (i, 0)),
        scratch_shapes=[
            pltpu.VMEM((2, r_ * tiles, 128), U32),
            pltpu.VMEM((d, f), F32),
            pltpu.VMEM((d, f), F32),
            pltpu.VMEM((f, d), F32),
            pltpu.VMEM((d, f), BF16),
            pltpu.VMEM((d, f), BF16),
            pltpu.VMEM((f, d), BF16),
            pltpu.SemaphoreType.DMA((2,)),
            pltpu.SemaphoreType.DMA((3,)),
        ],
    )
    return pl.pallas_call(
        _expert_body,
        grid_spec=grid_spec,
        out_shape=jax.ShapeDtypeStruct((nb * r_ * tiles, 128), U32),
        compiler_params=_cparams(("arbitrary",)),
        name="experts",
    )(block_e, first, nxt, n_used, x_sorted, wg, wu, wd)


def _combine_body(dest_hbm, y_hbm, w_ref, x1_ref, h2_ref, g2_ref, sg_ref, su_ref, sd_ref, fnw_ref, o_ref,
                  idx_smem, ybuf, isem, gsem):
    i = pl.program_id(0)
    n = pl.num_programs(0)
    tm = x1_ref.shape[0]
    tiles = y_hbm.shape[1]
    slot = i % 2
    nslot = 1 - slot

    def idx_copy(blk, s):
        cols = pl.ds(pl.multiple_of(blk * tm, tm), tm)
        return pltpu.make_async_copy(dest_hbm.at[:, cols], idx_smem.at[s], isem.at[s])

    def row_copy(s, k, j):
        return pltpu.make_async_copy(y_hbm.at[idx_smem[s, k, j]], ybuf.at[s, pl.ds((k * tm + j) * tiles, tiles)],
                                     gsem.at[s])

    def wait_rows(s):
        pltpu.make_async_copy(ybuf.at[s], ybuf.at[s], gsem.at[s]).wait()

    def ahead(step):
        return jnp.minimum(step + 1, n - 1)

    @pl.when(i == 0)
    def _():
        idx_copy(0, 0).start()
        idx_copy(0, 0).wait()

        def body(j, _):
            for k in range(TOP_K):
                dst = pl.ds(pl.multiple_of((k * tm + j) * tiles, tiles), tiles)
                pltpu.make_async_copy(y_hbm.at[idx_smem[0, k, j]], ybuf.at[0, dst], gsem.at[0]).start()
            return 0
        lax.fori_loop(0, tm, body, 0)
        idx_copy(ahead(0), 1).start()

    idx_copy(ahead(i), nslot).wait()
    wait_rows(slot)

    w = w_ref[...]
    half = tiles * 128
    r_lo = [jnp.zeros((tm, 128), F32) for _ in range(tiles)]
    r_hi = [jnp.zeros((tm, 128), F32) for _ in range(tiles)]
    for k in range(TOP_K):
        for j in range(tm):
            row_copy(nslot, k, j).start(priority=j % 2)
        wk = w[:, k:k + 1]
        for kc in range(tiles):
            lo, hi = _unpack_bf16_pair(ybuf[slot, pl.ds(k * tm * tiles + kc, tm, stride=tiles), :])
            r_lo[kc] = r_lo[kc] + wk * lo
            r_hi[kc] = r_hi[kc] + wk * hi
    routed = jnp.concatenate(r_lo + r_hi, axis=-1)

    @pl.when(i + 1 < n)
    def _():
        idx_copy(ahead(i + 1), slot).start()

    @pl.when(i + 1 == n)
    def _():
        wait_rows(nslot)

    x_lo, x_hi = _unpack_bf16_pair(_load_row_tiles(h2_ref, (), 0, tm, tiles))
    x_lo = x_lo.astype(BF16)
    x_hi = x_hi.astype(BF16)
    a = _dot(x_lo, sg_ref[0:half, :]) + _dot(x_hi, sg_ref[half:, :])
    u = _dot(x_lo, su_ref[0:half, :]) + _dot(x_hi, su_ref[half:, :])
    shared = _dot((_silu(a) * u).astype(BF16), sd_ref[...])

    x2 = x1_ref[...] + g2_ref[0, 0] * (routed + shared)
    ms = jnp.mean(x2 * x2, axis=-1, keepdims=True)
    o_ref[...] = x2 * lax.rsqrt(ms + EPS) * fnw_ref[...]


def _combine(dest, y_sorted, w_tok, x1, h2p, mod4, sg, su, sd, fnw, tiles_per_batch, tm):
    t, d = x1.shape
    half = d // 2
    nrow = TOP_K * tm
    res = lambda shape: pl.BlockSpec(shape, lambda i: (0,) * len(shape), pipeline_mode=pl.Buffered(1))
    return pl.pallas_call(
        _combine_body,
        grid=(t // tm,),
        in_specs=[
            pl.BlockSpec(memory_space=pl.ANY),
            pl.BlockSpec(memory_space=pl.ANY),
            pl.BlockSpec((tm, TOP_K), lambda i: (i, 0)),
            pl.BlockSpec((tm, d), lambda i: (i, 0)),
            pl.BlockSpec((tm * (half // 128), 128), lambda i: (i, 0)),
            pl.BlockSpec((1, 1, 1, d), lambda i: (i // tiles_per_batch, 5, 0, 0)),
            res(sg.shape), res(su.shape), res(sd.shape), res(fnw.shape),
        ],
        out_specs=pl.BlockSpec((tm, d), lambda i: (i, 0)),
        out_shape=jax.ShapeDtypeStruct((t, d), F32),
        scratch_shapes=[
            pltpu.SMEM((2, TOP_K, tm), I32),
            pltpu.VMEM((2, nrow * (half // 128), 128), U32),
            pltpu.SemaphoreType.DMA((2,)),
            pltpu.SemaphoreType.DMA((2,)),
        ],
        compiler_params=_cparams(("arbitrary",)),
        name="combine",
    )(dest, y_sorted, w_tok, x1, h2p, mod4, sg, su, sd, fnw)


def kernel(x, c, ctx, c_ctx, ada_w, ada_b, norm1_w, norm2_w, w_in, s5_lam_re, s5_lam_im, s5_log_dt,
           s5_b_re, s5_b_im, s5_c_re, s5_c_im, s5_d, s5_glu_w, gla_gk_up, gla_gk_b, gla_norm_w,
           w_s5_proj, w_gla_proj, w_out, router_w, router_bias, exp_w_gate, exp_w_up, exp_w_down,
           sh_w_gate, sh_w_up, sh_w_down, final_norm_w):
    depth = ada_w.shape[0]
    assert depth == 1, "single-layer block: context outputs are never consumed"
    b_, l_, d = x.shape
    lc = ctx.shape[1]
    s5w = s5_d.shape[1]
    kw = gla_gk_up.shape[-1]
    vw = w_gla_proj.shape[1]
    dk, dv = kw // GLA_HEADS, vw // GLA_HEADS
    rank2 = 2 * GLA_GATE_RANK
    li = 0

    c8 = jnp.zeros((8, d), F32).at[:b_].set(c).at[b_].set(c_ctx)
    mod4 = _ada(c8, ada_w[li], ada_b[li]).reshape(8, N_MOD, 1, d)

    cuts = [0, s5w, s5w + kw, s5w + 2 * kw, s5w + 2 * kw + vw, s5w + 2 * kw + 2 * vw]
    c_gkd = cuts[5]
    w = w_in[li]
    w_main = jnp.concatenate([w[:, :c_gkd].astype(BF16), w[:, c_gkd + rank2:].astype(BF16)], axis=1)
    w_ctx = jnp.concatenate([w[:, cuts[0]:cuts[1]].astype(BF16), w[:, cuts[2]:cuts[4]].astype(BF16)], axis=1)
    w_gkd = jnp.zeros((d, 128), F32).at[:, :rank2].set(w[:, c_gkd:c_gkd + rank2]).astype(BF16)
    nw1 = norm1_w[li].reshape(1, d)
    p_lat, gk_lat, v_lat = _inproj(x, mod4, lambda b: b, nw1, w_main, w_gkd, tm=1024, tn=1024, s5w=s5w)
    p_ctx, gk_ctx, v_ctx = _inproj(ctx, mod4, lambda b: b_, nw1, w_ctx, w_gkd, tm=lc, tn=w_ctx.shape[1] // 2,
                                   s5w=s5w)

    factors = _s5_factors(s5_lam_re[li], s5_lam_im[li], s5_log_dt[li], s5_b_re[li], s5_b_im[li],
                          s5_c_re[li], s5_c_im[li], s5_d[li])
    ys = _s5(v_ctx, v_lat, *factors)

    up_pad = jnp.zeros((2, 128, kw), F32)
    up_pad = up_pad.at[0, :GLA_GATE_RANK].set(gla_gk_up[li, 0]).at[1, GLA_GATE_RANK:rank2].set(gla_gk_up[li, 1])
    yg = _gla(p_lat, p_ctx, gk_lat, gk_ctx, up_pad.astype(BF16), gla_gk_b[li].reshape(2, 1, kw), dk, dv)

    rw = jnp.zeros((d, 128), F32).at[:, :N_EXPERTS].set(router_w[li])
    rw_hi = rw.astype(BF16)
    rw2 = jnp.stack([rw_hi, (rw - rw_hi.astype(F32)).astype(BF16)])
    x1, h2p, logits = _merge(
        ys, yg, p_lat, x, mod4, s5_glu_w[li].astype(BF16), w_s5_proj[li].astype(BF16),
        w_gla_proj[li].astype(BF16), w_out[li].astype(BF16), gla_norm_w[li].reshape(1, dv),
        norm2_w[li].reshape(1, d), rw2, dv, tm=256)

    t = b_ * l_
    logits_t = logits.reshape(t, 128)[:, :N_EXPERTS].T
    eidx, wts, rank, cnt = _route(logits_t, router_bias[li])
    nb = (t * TOP_K + N_EXPERTS * (EXPERT_ROWS - 1) + EXPERT_ROWS - 1) // EXPERT_ROWS
    dest, block_e, first, nxt, n_used, pad = _slots(cnt[:, 0].astype(I32), eidx, rank, nb)

    tiles = d // 256
    x_sorted = _dispatch(pad, n_used, dest, h2p.reshape(t * tiles, 128), nb * EXPERT_ROWS, tm=128)
    y_sorted = _experts(block_e, first, nxt, n_used, x_sorted.reshape(nb * EXPERT_ROWS * tiles, 128),
                         exp_w_gate[li], exp_w_up[li], exp_w_down[li])

    tm_c = 128
    out = _combine(dest, y_sorted.reshape(-1, tiles, 128), wts.T, x1.reshape(t, d), h2p.reshape(t * tiles, 128), mod4,
                   sh_w_gate[li].astype(BF16), sh_w_up[li].astype(BF16), sh_w_down[li].astype(BF16),
                   final_norm_w.reshape(1, d), l_ // tm_c, tm_c)
    return out.reshape(b_, l_, d)
```

```python
import functools
import math

import jax
import jax.numpy as jnp
from jax import lax
from jax.experimental import pallas as pl
from jax.experimental.pallas import tpu as pltpu

F32 = jnp.float32
BF16 = jnp.bfloat16
U32 = jnp.uint32
I32 = jnp.int32

EPS = 1e-6
N_MOD = 6

S5_GROUP = 16
S5_STATE = 64
S5_CHUNK = 16

GLA_HEADS = 4
GLA_GATE_RANK = 16
GLA_GATE_NORM = 16.0
GLA_CHUNK = 64

N_EXPERTS = 64
TOP_K = 8
N_GROUPS = 8
TOPK_GROUPS = 4
ROUTED_SCALE = 2.5
EXPERT_ROWS = 256
DISPATCH_RING = 3

VMEM_LIMIT = 56 * 1024 * 1024


def _cparams(sem, vmem=VMEM_LIMIT):
    return pltpu.CompilerParams(dimension_semantics=sem, vmem_limit_bytes=vmem)


def _dot(a, b):
    return jnp.dot(a, b, preferred_element_type=F32)


def _sigmoid(x):
    return 1.0 / (1.0 + jnp.exp(-x))


def _silu(x):
    return x * _sigmoid(x)


def _pack_bf16_pair(lo, hi):
    lo_bits = lax.bitcast_convert_type(lo.astype(BF16).astype(F32), U32)
    hi_bits = lax.bitcast_convert_type(hi.astype(BF16).astype(F32), U32)
    return (lo_bits >> 16) | (hi_bits & jnp.uint32(0xFFFF0000))


def _unpack_bf16_pair(w):
    lo = lax.bitcast_convert_type(w << 16, F32)
    hi = lax.bitcast_convert_type(w & jnp.uint32(0xFFFF0000), F32)
    return lo, hi


def _store_row_tiles(ref, lead, val):
    rows, rt = val.shape[0], val.shape[1] // 128
    for kc in range(rt):
        ref[(*lead, pl.ds(kc, rows, stride=rt), slice(None))] = val[:, kc * 128:(kc + 1) * 128]


def _load_row_tiles(ref, lead, row0, rows, rt):
    return jnp.concatenate(
        [ref[(*lead, pl.ds(row0 * rt + kc, rows, stride=rt), slice(None))] for kc in range(rt)], axis=-1)


def _ada_body(c_ref, w_ref, b_ref, o_ref):
    c = c_ref[...]
    s = _silu(c).astype(BF16)
    o_ref[...] = _dot(s, w_ref[...].astype(BF16)) + b_ref[...]


def _ada(c8, ada_w, ada_b):
    d, n = ada_w.shape
    tn = 512
    return pl.pallas_call(
        _ada_body,
        grid=(n // tn,),
        in_specs=[
            pl.BlockSpec((8, d), lambda j: (0, 0)),
            pl.BlockSpec((d, tn), lambda j: (0, j)),
            pl.BlockSpec((1, tn), lambda j: (0, j)),
        ],
        out_specs=pl.BlockSpec((8, tn), lambda j: (0, j)),
        out_shape=jax.ShapeDtypeStruct((8, n), F32),
        compiler_params=_cparams(("parallel",)),
        name="ada",
    )(c8, ada_w, ada_b.reshape(1, n))


def _modulated_norm(x, nw, sh, sc):
    ms = jnp.mean(x * x, axis=-1, keepdims=True)
    y = x * lax.rsqrt(ms + EPS) * nw
    return y * (1.0 + sc) + sh


def _inproj_body(x_ref, sh_ref, sc_ref, nw_ref, w_ref, wg_ref, p_ref, g_ref, v_ref, h_scr, u_scr):
    @pl.when(pl.program_id(2) == 0)
    def _():
        h = _modulated_norm(x_ref[0], nw_ref[...], sh_ref[0, 0], sc_ref[0, 0]).astype(BF16)
        h_scr[...] = h
        g_ref[0] = _dot(h, wg_ref[...])

    r = _dot(h_scr[...], w_ref[...])
    p_ref[0] = r.astype(BF16)

    @pl.when(pl.program_id(2) == 0)
    def _():
        nch = u_scr.shape[1] // S5_CHUNK
        for c in range(u_scr.shape[0]):
            u_scr[c] = r[:, c * 128:(c + 1) * 128]
            for t in range(S5_CHUNK):
                piece = u_scr[c, pl.ds(t, nch, stride=S5_CHUNK), :]
                v_ref[c, 0, :, t * 128:(t + 1) * 128] = piece.astype(BF16)


def _inproj(x, mod4, mod_row, nw, w, wg, tm, tn, s5w):
    b_, l_, d = x.shape
    n = w.shape[1]
    assert tn >= s5w and tm % (16 * S5_CHUNK) == 0
    nblk, vw = s5w // 128, S5_CHUNK * 128
    return pl.pallas_call(
        _inproj_body,
        grid=(b_, l_ // tm, n // tn),
        in_specs=[
            pl.BlockSpec((1, tm, d), lambda b, i, j: (b, i, 0)),
            pl.BlockSpec((1, 1, 1, d), lambda b, i, j: (mod_row(b), 0, 0, 0)),
            pl.BlockSpec((1, 1, 1, d), lambda b, i, j: (mod_row(b), 1, 0, 0)),
            pl.BlockSpec((1, d), lambda b, i, j: (0, 0)),
            pl.BlockSpec((d, tn), lambda b, i, j: (0, j)),
            pl.BlockSpec((d, 128), lambda b, i, j: (0, 0)),
        ],
        out_specs=[
            pl.BlockSpec((1, tm, tn), lambda b, i, j: (b, i, j)),
            pl.BlockSpec((1, tm, 128), lambda b, i, j: (b, i, 0)),
            pl.BlockSpec((nblk, 1, tm // S5_CHUNK, vw), lambda b, i, j: (0, b, i, 0)),
        ],
        out_shape=[
            jax.ShapeDtypeStruct((b_, l_, n), BF16),
            jax.ShapeDtypeStruct((b_, l_, 128), F32),
            jax.ShapeDtypeStruct((nblk, b_, l_ // S5_CHUNK, vw), BF16),
        ],
        scratch_shapes=[pltpu.VMEM((tm, d), BF16), pltpu.VMEM((nblk, tm, 128), F32)],
        compiler_params=_cparams(("parallel", "parallel", "arbitrary")),
        name="inproj",
    )(x, mod4, mod4, nw, w, wg)


def _s5_factors(lam_re, lam_im, log_dt, b_re, b_im, c_re, c_im, d_skip):
    tc = S5_CHUNK
    g_, p_ = lam_re.shape[1:]
    h_ = b_re.shape[-1]
    dt = jnp.exp(log_dt)[..., None]
    ar, ai = lam_re * dt, lam_im * dt

    def powers(k):
        k = jnp.asarray(k, F32)
        mag = jnp.exp(ar[..., None] * k)
        return mag * jnp.cos(ai[..., None] * k), mag * jnp.sin(ai[..., None] * k)

    num_re = jnp.expm1(ar) * jnp.cos(ai) - 2.0 * jnp.sin(0.5 * ai) ** 2
    num_im = jnp.exp(ar) * jnp.sin(ai)
    den = lam_re * lam_re + lam_im * lam_im
    f_re = (num_re * lam_re + num_im * lam_im) / den
    f_im = (num_im * lam_re - num_re * lam_im) / den
    bb_re = f_re[..., None] * b_re - f_im[..., None] * b_im
    bb_im = f_re[..., None] * b_im + f_im[..., None] * b_re
    t_idx = jnp.arange(tc)

    def halves(a, b):
        return jnp.concatenate([a, b], axis=-1).transpose(1, 0, 2, 3)

    def power_rows(expo_f, expo_b):
        re_f, im_f = powers(expo_f)
        re_b, im_b = powers(expo_b)
        re = jnp.stack([re_f[0], re_b[1]]).transpose(0, 1, 3, 2)
        im = jnp.stack([im_f[0], im_b[1]]).transpose(0, 1, 3, 2)
        return re, im

    e_re, e_im = power_rows(tc - 1 - t_idx, t_idx)
    bt_re, bt_im = bb_re.transpose(0, 1, 3, 2), bb_im.transpose(0, 1, 3, 2)
    ea = jnp.stack([halves(e_re, e_re), halves(-e_im, e_im)], axis=2)
    ba = jnp.stack([halves(bt_re, bt_im), halves(bt_im, bt_re)], axis=2)

    def readout_rows(expo_f, expo_b):
        p_re, p_im = power_rows(expo_f, expo_b)
        return jnp.stack([halves(p_re, -p_im), halves(-p_im, -p_re)], axis=2)

    pq = readout_rows(t_idx - (tc - 1), -t_idx)
    po = readout_rows(t_idx + 1, tc - t_idx)
    ca = jnp.stack([halves(c_re, c_re), halves(c_im, c_im)], axis=2)

    a_re, a_im = powers(jnp.array([tc]))
    a_re, a_im = a_re[..., 0], a_im[..., 0]
    zeros = jnp.zeros_like(a_re[0])
    rows = []
    for dr in range(2):
        rows += [jnp.concatenate([a_re[dr], a_re[dr]], -1),
                 jnp.concatenate([-a_im[dr], a_im[dr]], -1),
                 jnp.concatenate([a_im[dr], -a_im[dr]], -1)]
    rows += [jnp.concatenate([zeros, zeros], -1)] * 2
    acoef = jnp.stack(rows, axis=1)
    dvec = jnp.tile(d_skip.reshape(g_, h_), (1, tc)).reshape(g_, 1, tc * h_)
    return ea, ba, pq, po, ca, acoef, dvec


def _split_bf16(x):
    hi = x.astype(BF16)
    return hi, (x - hi.astype(F32)).astype(BF16)


def _s5_body(vc_ref, vl_ref, ea_ref, ba_ref, pq_ref, po_ref, ca_ref, a_ref, d_ref, o_ref, y_scr, s_scr, h_scr,
               *, n_ctx, n_lat, nb):
    gb = ea_ref.shape[0]
    pw = 2 * S5_STATE
    gl = S5_GROUP
    kd = S5_CHUNK * gl
    lanes = 128
    n_tot = n_ctx + n_lat
    nt = (((1,), (1,)), ((), ()))

    def outer(x_ref, y_ref, q, dr):
        z = (x_ref[q, dr, 0][:, None, :] * y_ref[q, dr, 0][None, :, :]
             + x_ref[q, dr, 1][:, None, :] * y_ref[q, dr, 1][None, :, :])
        return z.reshape(kd, pw)
    v = jnp.concatenate([p for b in range(nb) for p in (vc_ref[0, b], vl_ref[0, b])], axis=0)
    width = v.shape[1]

    r_in = lax.broadcasted_iota(I32, (width, kd), 0)
    c_in = lax.broadcasted_iota(I32, (width, kd), 1)
    tgt_in = (r_in // lanes) * gl + (r_in % gl)
    grp_in = (r_in % lanes) // gl
    r_out = lax.broadcasted_iota(I32, (kd, width), 0)
    c_out = lax.broadcasted_iota(I32, (kd, width), 1)
    tgt_out = (c_out // lanes) * gl + (c_out % gl)
    grp_out = (c_out % lanes) // gl

    rs = lax.broadcasted_iota(I32, (kd, kd), 0)
    ct = lax.broadcasted_iota(I32, (kd, kd), 1)
    fmask = ct // gl >= rs // gl
    bmask = ct // gl <= rs // gl
    diag = rs == ct

    for q in range(gb):
        sel = jnp.where(jnp.logical_and(c_in == tgt_in, grp_in == q), 1.0, 0.0).astype(BF16)
        x2 = _dot(v, sel).astype(BF16)
        mm, ws = [], []
        for dr in range(2):
            ws.append(outer(ea_ref, ba_ref, q, dr))
            a_hi, a_lo = _split_bf16(ws[dr])
            b_hi, b_lo = _split_bf16(outer(pq_ref, ca_ref, q, dr))
            mm.append(lax.dot_general(a_hi, b_hi, nt, preferred_element_type=F32)
                      + lax.dot_general(a_hi, b_lo, nt, preferred_element_type=F32)
                      + lax.dot_general(a_lo, b_hi, nt, preferred_element_type=F32))
        m = jnp.where(fmask, mm[0], 0.0) + jnp.where(bmask, mm[1], 0.0) + jnp.where(diag, d_ref[q], 0.0)
        wsf, wsb = ws
        wcat = jnp.concatenate(
            [m, wsf, pltpu.roll(wsf, S5_STATE, 1), wsb, pltpu.roll(wsb, S5_STATE, 1)], axis=1).astype(BF16)
        r = _dot(x2, wcat)
        for b in range(nb):
            rb = r[b * n_tot:(b + 1) * n_tot, :]
            for j in range(kd // lanes):
                y_scr[q, j, pl.ds(b, n_tot, stride=nb), :] = rb[:, j * lanes:(j + 1) * lanes]
            for j in range(4):
                s_scr[q, j, pl.ds(b, n_tot, stride=nb), :] = rb[:, kd + j * pw:kd + (j + 1) * pw]

    a1f, a2f, a2sf = a_ref[:, 0:1, :], a_ref[:, 1:2, :], a_ref[:, 2:3, :]
    a1b, a2b, a2sb = a_ref[:, 3:4, :], a_ref[:, 4:5, :], a_ref[:, 5:6, :]
    first = lax.broadcasted_iota(I32, (gb, 2 * nb, pw), 1) < nb

    def both(x):
        rr = pltpu.roll(x, nb, 1)
        return jnp.where(first, x, rr), jnp.where(first, rr, x)

    def step(pf, pb, carry):
        hf, hfs, hb, hbs = carry
        rf = pl.ds(pl.multiple_of(pf * 2 * nb, 2 * nb), 2 * nb)
        rb = pl.ds(pl.multiple_of(pb * 2 * nb, 2 * nb), 2 * nb)
        sfa, sfb = both(s_scr[:, 0, rf, :])
        sfsa, sfsb = both(s_scr[:, 1, rf, :])
        sba, sbb = both(s_scr[:, 2, rb, :])
        sbsa, sbsb = both(s_scr[:, 3, rb, :])
        hf1 = a1f * hf + a2f * hfs + sfa
        hfs1 = a1f * hfs + a2sf * hf + sfsa
        h_scr[:, 0, rf, :] = jnp.where(first, hf, hf1)
        hf2 = a1f * hf1 + a2f * hfs1 + sfb
        hfs2 = a1f * hfs1 + a2sf * hf1 + sfsb
        hb1 = a1b * hb + a2b * hbs + sbb
        hbs1 = a1b * hbs + a2sb * hb + sbsb
        h_scr[:, 1, rb, :] = jnp.where(first, hb1, hb)
        hb2 = a1b * hb1 + a2b * hbs1 + sba
        hbs2 = a1b * hbs1 + a2sb * hb1 + sbsa
        return hf2, hfs2, hb2, hbs2

    z = jnp.zeros((gb, 2 * nb, pw), F32)
    carry = (z, z, z, z)
    pc, pl_ = n_ctx // 2, n_lat // 2
    carry = lax.fori_loop(0, pc, lambda i, c: step(i, pc - 1 - i, c), carry)
    carry = lax.fori_loop(0, pl_, lambda i, c: step(pc + i, pc + pl_ - 1 - i, c), carry)

    lo = n_ctx * nb
    for q in range(gb):
        yq = (lax.dot_general(h_scr[q, 0].astype(BF16), outer(po_ref, ca_ref, q, 0).astype(BF16), nt,
                              preferred_element_type=F32)
              + lax.dot_general(h_scr[q, 1].astype(BF16), outer(po_ref, ca_ref, q, 1).astype(BF16), nt,
                                preferred_element_type=F32))
        for j in range(kd // lanes):
            y_scr[q, j] = y_scr[q, j] + yq[:, j * lanes:(j + 1) * lanes]
        y = jnp.concatenate(
            [jnp.concatenate([y_scr[q, j, pl.ds(lo + b, n_lat, stride=nb), :] for j in range(kd // lanes)], axis=-1)
             for b in range(nb)], axis=0)
        selt = jnp.where(jnp.logical_and(r_out == tgt_out, grp_out == q), 1.0, 0.0).astype(BF16)
        placed = _dot(y.astype(BF16), selt)
        if q == 0:
            o_ref[0] = placed
        else:
            o_ref[0] = o_ref[0] + placed


def _s5(v_ctx, v_lat, ea, ba, pq, po, ca, acoef, dvec):
    nblk, nb, n_ctx, width = v_ctx.shape
    n_lat = v_lat.shape[2]
    rows = (n_ctx + n_lat) * nb
    g_ = ea.shape[0]
    gb = g_ // nblk
    pw = 2 * S5_STATE
    kd = S5_CHUNK * S5_GROUP
    out_rows = n_lat * nb
    fac = lambda: pl.BlockSpec((gb, 2, 2, S5_CHUNK, pw), lambda i: (i, 0, 0, 0, 0))
    return pl.pallas_call(
        functools.partial(_s5_body, n_ctx=n_ctx, n_lat=n_lat, nb=nb),
        grid=(nblk,),
        in_specs=[
            pl.BlockSpec((1, nb, n_ctx, width), lambda i: (i, 0, 0, 0)),
            pl.BlockSpec((1, nb, n_lat, width), lambda i: (i, 0, 0, 0)),
            fac(), fac(), fac(), fac(), fac(),
            pl.BlockSpec((gb, 8, pw), lambda i: (i, 0, 0)),
            pl.BlockSpec((gb, 1, kd), lambda i: (i, 0, 0)),
        ],
        out_specs=pl.BlockSpec((1, out_rows, width), lambda i: (i, 0, 0)),
        out_shape=jax.ShapeDtypeStruct((nblk, out_rows, width), F32),
        scratch_shapes=[
            pltpu.VMEM((gb, kd // 128, rows, 128), F32),
            pltpu.VMEM((gb, 4, rows, pw), F32),
            pltpu.VMEM((gb, 2, rows, pw), F32),
        ],
        compiler_params=_cparams(("parallel",)),
        name="s5",
    )(v_ctx, v_lat, ea, ba, pq, po, ca, acoef, dvec)


def _log_sigmoid(z):
    return jnp.minimum(z, 0.0) - jnp.log1p(jnp.exp(-jnp.abs(z)))


def _chunk_cumsum(g, pos, reverse):
    n = g.shape[0]
    x = g
    sh = 1
    while sh < GLA_CHUNK:
        if reverse:
            x = x + jnp.where(pos < GLA_CHUNK - sh, pltpu.roll(x, n - sh, 0), 0.0)
        else:
            x = x + jnp.where(pos >= sh, pltpu.roll(x, sh, 0), 0.0)
        sh *= 2
    return x


def _gla_body(q_ref, k_ref, v_ref, kc_ref, vc_ref, gk_ref, gkc_ref, up_ref, bias_ref, o_ref,
              bc_scr, kb_scr, qb_scr, st_ref, *, scale):
    c = GLA_CHUNK
    l_ = q_ref.shape[1]
    lc = kc_ref.shape[1]
    n_lat, n_ctx = l_ // c, lc // c
    row = lax.broadcasted_iota(I32, (c, c), 0)
    col = lax.broadcasted_iota(I32, (c, c), 1)
    nt = (((1,), (1,)), ((), ()))
    tn = (((0,), (0,)), ((), ()))

    for dr in range(2):
        smask = (col <= row) if dr == 0 else (col > row)
        up = up_ref[dr]
        bias = bias_ref[dr]
        edge_tile, edge_row = (c - 8, 7) if dr == 0 else (0, 0)

        for gk_r, k_r, base, n, latent in ((gkc_ref, kc_ref, 0, lc, False), (gk_ref, k_ref, lc, l_, True)):
            z = _dot(gk_r[0].astype(BF16), up) + bias
            g = _log_sigmoid(z) * (1.0 / GLA_GATE_NORM)
            pos = lax.broadcasted_iota(I32, g.shape, 0) & (c - 1)
            bc = _chunk_cumsum(g, pos, reverse=(dr == 1))
            bc_scr[base:base + n, :] = bc
            kb_scr[base:base + n, :] = k_r[0].astype(F32) * jnp.exp(-bc)
            if latent:
                qb_scr[...] = (q_ref[0].astype(F32) * (scale * jnp.exp(bc))).astype(BF16)

        def chunk_rows(n, base):
            r0 = pl.multiple_of(n * c, c)
            tot = bc_scr[pl.ds(base + r0 + edge_tile, 8), :][edge_row:edge_row + 1, :]
            return r0, pl.ds(base + r0, c), jnp.exp(tot)

        def update_state(kb, vv, decay):
            kd = (kb * decay).astype(BF16)
            ds_t = lax.dot_general(vv, kd, tn, preferred_element_type=F32)
            st_ref[...] = decay * st_ref[...] + ds_t

        def ctx_step(i, _):
            n = i if dr == 0 else n_ctx - 1 - i
            r0, rs, decay = chunk_rows(n, 0)
            update_state(kb_scr[rs, :], vc_ref[0, pl.ds(r0, c), :], decay)
            return 0

        def lat_step(i, _):
            n = i if dr == 0 else n_lat - 1 - i
            r0, rs, decay = chunk_rows(n, lc)
            ro = pl.ds(r0, c)
            kb = kb_scr[rs, :]
            vv = v_ref[0, ro, :]
            qb = qb_scr[ro, :]
            scores = lax.dot_general(qb, kb.astype(BF16), nt, preferred_element_type=F32)
            scores = jnp.where(smask, scores, 0.0).astype(BF16)
            o = _dot(scores, vv) + lax.dot_general(qb, st_ref[...].astype(BF16), nt,
                                                   preferred_element_type=F32)
            if dr == 0:
                o_ref[0, ro, :] = o
            else:
                o_ref[0, ro, :] = o_ref[0, ro, :] + o
            update_state(kb, vv, decay)
            return 0

        st_ref[...] = jnp.zeros_like(st_ref)
        lax.fori_loop(0, n_ctx, ctx_step, 0, unroll=2)
        lax.fori_loop(0, n_lat, lat_step, 0, unroll=4)


def _gla(p_lat, p_ctx, gk_lat, gk_ctx, up_pad, bias, dk, dv):
    b_, l_, _ = p_lat.shape
    lc = p_ctx.shape[1]
    hh = GLA_HEADS
    s5w = hh * dv
    q0, k0, v0 = s5w // dk, (s5w + hh * dk) // dk, (s5w + 2 * hh * dk) // dv
    kc0, vc0 = s5w // dk, (s5w + hh * dk) // dv
    return pl.pallas_call(
        functools.partial(_gla_body, scale=dk ** -0.5),
        grid=(b_, hh),
        in_specs=[
            pl.BlockSpec((1, l_, dk), lambda b, h: (b, 0, q0 + h)),
            pl.BlockSpec((1, l_, dk), lambda b, h: (b, 0, k0 + h)),
            pl.BlockSpec((1, l_, dv), lambda b, h: (b, 0, v0 + h)),
            pl.BlockSpec((1, lc, dk), lambda b, h: (b, 0, kc0 + h)),
            pl.BlockSpec((1, lc, dv), lambda b, h: (b, 0, vc0 + h)),
            pl.BlockSpec((1, l_, 128), lambda b, h: (b, 0, 0)),
            pl.BlockSpec((1, lc, 128), lambda b, h: (b, 0, 0)),
            pl.BlockSpec((2, 128, dk), lambda b, h: (0, 0, h)),
            pl.BlockSpec((2, 1, dk), lambda b, h: (0, 0, h)),
        ],
        out_specs=pl.BlockSpec((1, l_, dv), lambda b, h: (b, 0, h)),
        out_shape=jax.ShapeDtypeStruct((b_, l_, hh * dv), F32),
        scratch_shapes=[
            pltpu.VMEM((lc + l_, dk), F32),
            pltpu.VMEM((lc + l_, dk), F32),
            pltpu.VMEM((l_, dk), BF16),
            pltpu.VMEM((dv, dk), F32),
        ],
        compiler_params=_cparams(("parallel", "parallel")),
        name="gla",
    )(p_lat, p_lat, p_lat, p_ctx, p_ctx, gk_lat, gk_ctx, up_pad, bias)


def _gelu_tanh(x):
    return 0.5 * x * (1.0 + jnp.tanh(math.sqrt(2.0 / math.pi) * (x + 0.044715 * (x * x * x))))


def _merge_body(ys_ref, yg_ref, go_ref, gs_ref, gg_ref, x_ref, g1_ref, sh2_ref, sc2_ref,
                glu_ref, ws5_ref, wgla_ref, wout_ref, gnw_ref, n2w_ref, rw_ref,
                x1_ref, h2_ref, lg_ref, ys_scr, *, dv):
    nch = ys_ref.shape[1]
    for c in range(ys_ref.shape[0]):
        for t in range(S5_CHUNK):
            ys_scr[c, pl.ds(t, nch, stride=S5_CHUNK), :] = ys_ref[c, :, t * 128:(t + 1) * 128]
    a = _gelu_tanh(jnp.concatenate([ys_scr[c] for c in range(ys_ref.shape[0])], axis=-1))
    a = a * _sigmoid(_dot(a.astype(BF16), glu_ref[...]))
    pa = _dot(a.astype(BF16), ws5_ref[...])

    yg = yg_ref[0]
    parts = []
    for h in range(GLA_HEADS):
        oh = yg[:, h * dv:(h + 1) * dv]
        ms = jnp.mean(oh * oh, axis=-1, keepdims=True)
        parts.append(oh * lax.rsqrt(ms + EPS) * gnw_ref[...])
    gl = jnp.concatenate(parts, axis=-1) * _silu(go_ref[0].astype(F32))
    pb = _dot(gl.astype(BF16), wgla_ref[...])

    m = _sigmoid(gs_ref[0].astype(F32)) * pa + _sigmoid(gg_ref[0].astype(F32)) * pb
    y = _dot(m.astype(BF16), wout_ref[...])
    x1 = x_ref[0] + g1_ref[0, 0] * y
    x1_ref[0] = x1

    h2 = _modulated_norm(x1, n2w_ref[...], sh2_ref[0, 0], sc2_ref[0, 0])
    h_hi = h2.astype(BF16)
    h_lo = (h2 - h_hi.astype(F32)).astype(BF16)
    half = h2.shape[-1] // 2
    _store_row_tiles(h2_ref, (0,), _pack_bf16_pair(h2[:, :half], h2[:, half:]))
    lg_ref[0] = _dot(h_hi, rw_ref[0]) + _dot(h_lo, rw_ref[0]) + _dot(h_hi, rw_ref[1])


def _resident(shape):
    nd = len(shape)
    return pl.BlockSpec(shape, lambda b, i: (0,) * nd, pipeline_mode=pl.Buffered(1))


def _merge(ys, yg, p_lat, x, mod4, glu_w, ws5, wgla, wout, gnw, n2w, rw2, dv, tm):
    b_, l_, d = x.shape
    nblk, _, vw = ys.shape
    w5 = nblk * 128
    nch = tm // S5_CHUNK
    go0 = (p_lat.shape[-1] - 2 * d - w5) // w5
    gs0 = (p_lat.shape[-1] - 2 * d) // d
    modspec = lambda r: pl.BlockSpec((1, 1, 1, d), lambda b, i: (b, r, 0, 0))
    return pl.pallas_call(
        functools.partial(_merge_body, dv=dv),
        grid=(b_, l_ // tm),
        in_specs=[
            pl.BlockSpec((nblk, nch, vw), lambda b, i: (0, b * (l_ // tm) + i, 0)),
            pl.BlockSpec((1, tm, w5), lambda b, i: (b, i, 0)),
            pl.BlockSpec((1, tm, w5), lambda b, i: (b, i, go0)),
            pl.BlockSpec((1, tm, d), lambda b, i: (b, i, gs0)),
            pl.BlockSpec((1, tm, d), lambda b, i: (b, i, gs0 + 1)),
            pl.BlockSpec((1, tm, d), lambda b, i: (b, i, 0)),
            modspec(2), modspec(3), modspec(4),
            _resident(glu_w.shape), _resident(ws5.shape), _resident(wgla.shape), _resident(wout.shape),
            _resident(gnw.shape), _resident(n2w.shape), _resident(rw2.shape),
        ],
        out_specs=[
            pl.BlockSpec((1, tm, d), lambda b, i: (b, i, 0)),
            pl.BlockSpec((1, tm * (d // 256), 128), lambda b, i: (b, i, 0)),
            pl.BlockSpec((1, tm, 128), lambda b, i: (b, i, 0)),
        ],
        out_shape=[
            jax.ShapeDtypeStruct((b_, l_, d), F32),
            jax.ShapeDtypeStruct((b_, l_ * (d // 256), 128), U32),
            jax.ShapeDtypeStruct((b_, l_, 128), F32),
        ],
        scratch_shapes=[pltpu.VMEM((nblk, tm, 128), F32)],
        compiler_params=_cparams(("parallel", "parallel")),
        name="merge",
    )(ys, yg, p_lat, p_lat, p_lat, x, mod4, mod4, mod4, glu_w, ws5, wgla, wout, gnw, n2w, rw2)


def _route_body(lg_ref, bias_ref, eidx_ref, w_ref, rank_ref, cnt_ref, carry_ref):
    ne, tt = lg_ref.shape
    gsz = ne // N_GROUPS
    neg = -jnp.inf

    @pl.when(pl.program_id(0) == 0)
    def _():
        carry_ref[...] = jnp.zeros_like(carry_ref)

    sc = _sigmoid(lg_ref[...])
    ch = sc + bias_ref[...]

    def first_max(v, idx, big):
        m = jnp.max(v, axis=0, keepdims=True)
        i = jnp.min(jnp.where(v == m, idx, big), axis=0, keepdims=True)
        return m, i

    midx = lax.broadcasted_iota(I32, (gsz, tt), 0).astype(F32)
    gs_rows = []
    for g in range(N_GROUPS):
        v = ch[g * gsz:(g + 1) * gsz, :]
        m1, i1 = first_max(v, midx, float(gsz))
        m2 = jnp.max(jnp.where(midx == i1, neg, v), axis=0, keepdims=True)
        gs_rows.append(m1 + m2)
    gscore = jnp.concatenate(gs_rows, axis=0)

    gidx = lax.broadcasted_iota(I32, (N_GROUPS, tt), 0).astype(F32)
    gsel = jnp.zeros((N_GROUPS, tt), F32)
    cur = gscore
    for _ in range(TOPK_GROUPS):
        _, i = first_max(cur, gidx, float(N_GROUPS))
        hit = gidx == i
        gsel = jnp.where(hit, 1.0, gsel)
        cur = jnp.where(hit, neg, cur)

    masked = jnp.concatenate(
        [jnp.where(gsel[g:g + 1, :] > 0.0, ch[g * gsz:(g + 1) * gsz, :], neg) for g in range(N_GROUPS)], axis=0)

    eiota = lax.broadcasted_iota(I32, (ne, tt), 0).astype(F32)
    sel = jnp.zeros((ne, tt), F32)
    cur = masked
    idx_rows, s_rows = [], []
    for _ in range(TOP_K):
        _, i = first_max(cur, eiota, float(ne))
        hit = eiota == i
        idx_rows.append(i)
        s_rows.append(jnp.sum(jnp.where(hit, sc, 0.0), axis=0, keepdims=True))
        sel = jnp.where(hit, 1.0, sel)
        cur = jnp.where(hit, neg, cur)
    idx = jnp.concatenate(idx_rows, axis=0)
    s = jnp.concatenate(s_rows, axis=0)
    w_ref[...] = s / jnp.sum(s, axis=0, keepdims=True) * ROUTED_SCALE
    eidx_ref[...] = idx.astype(I32)

    cw = 256
    a_i = lax.broadcasted_iota(I32, (cw, cw), 0)
    b_i = lax.broadcasted_iota(I32, (cw, cw), 1)
    upper = jnp.where(a_i < b_i, 1.0, 0.0).astype(BF16)
    carry = carry_ref[:, 0:1]
    pieces = []
    for c0 in range(0, tt, cw):
        sc_c = sel[:, c0:c0 + cw]
        pieces.append(_dot(sc_c.astype(BF16), upper) + carry)
        carry = carry + jnp.sum(sc_c, axis=1, keepdims=True)
    rank_full = jnp.concatenate(pieces, axis=1)
    carry_ref[...] = jnp.broadcast_to(carry, carry_ref.shape)
    cnt_ref[...] = jnp.broadcast_to(carry, cnt_ref.shape)
    rk = [jnp.sum(jnp.where(eiota == idx_rows[k], rank_full, 0.0), axis=0, keepdims=True) for k in range(TOP_K)]
    rank_ref[...] = jnp.concatenate(rk, axis=0).astype(I32)


def _route(logits_t, bias):
    ne, t = logits_t.shape
    tt = 1024
    return pl.pallas_call(
        _route_body,
        grid=(t // tt,),
        in_specs=[
            pl.BlockSpec((ne, tt), lambda i: (0, i)),
            pl.BlockSpec((ne, 1), lambda i: (0, 0)),
        ],
        out_specs=[
            pl.BlockSpec((TOP_K, tt), lambda i: (0, i)),
            pl.BlockSpec((TOP_K, tt), lambda i: (0, i)),
            pl.BlockSpec((TOP_K, tt), lambda i: (0, i)),
            pl.BlockSpec((ne, 128), lambda i: (0, 0)),
        ],
        out_shape=[
            jax.ShapeDtypeStruct((TOP_K, t), I32),
            jax.ShapeDtypeStruct((TOP_K, t), F32),
            jax.ShapeDtypeStruct((TOP_K, t), I32),
            jax.ShapeDtypeStruct((ne, 128), F32),
        ],
        scratch_shapes=[pltpu.VMEM((ne, 128), F32)],
        compiler_params=_cparams(("arbitrary",)),
        name="route",
    )(logits_t, bias.reshape(ne, 1))


def _slots_body(cnt_ref, eidx_ref, rank_ref, dest_ref, be_ref, first_ref, nxt_ref, nu_ref, pad_ref, off_smem):
    ne = cnt_ref.shape[0]
    nb = be_ref.shape[0]
    r_ = EXPERT_ROWS
    shift = r_.bit_length() - 1

    @pl.when(pl.program_id(0) == 0)
    def _():
        def per_expert(e, off):
            cnt = cnt_ref[e]
            n = lax.shift_right_logical(cnt + (r_ - 1), shift)
            off_smem[e] = off
            pad_ref[0, e] = off * r_ + cnt
            pad_ref[1, e] = n * r_ - cnt

            def fill(j, _):
                be_ref[off + j] = e
                first_ref[off + j] = jnp.where(j == 0, 1, 0)
                return 0
            lax.fori_loop(0, n, fill, 0)
            return off + n

        n_used = lax.fori_loop(0, ne, per_expert, 0)
        nu_ref[0] = n_used
        last_e = be_ref[n_used - 1]

        def tail(j, _):
            be_ref[j] = last_e
            first_ref[j] = 0
            return 0
        lax.fori_loop(n_used, nb, tail, 0)

        def back(i, nx):
            j = nb - 1 - i
            nxt_ref[j] = nx
            return jnp.where(first_ref[j] == 1, be_ref[j], nx)
        lax.fori_loop(0, nb, back, -1)

    ei = eidx_ref[...]
    acc = rank_ref[...]
    for e in range(ne):
        acc = acc + jnp.where(ei == e, off_smem[e] * r_, 0)
    dest_ref[...] = acc


def _slots(counts, eidx, rank, nb):
    k_, t = eidx.shape
    ne = counts.shape[0]
    tl = 1024
    smem = lambda: pl.BlockSpec(memory_space=pltpu.SMEM)
    chunk = lambda: pl.BlockSpec((k_, tl), lambda i: (0, i))
    return pl.pallas_call(
        _slots_body,
        grid=(t // tl,),
        in_specs=[smem(), chunk(), chunk()],
        out_specs=[chunk(), smem(), smem(), smem(), smem(), smem()],
        out_shape=[
            jax.ShapeDtypeStruct((k_, t), I32),
            jax.ShapeDtypeStruct((nb,), I32),
            jax.ShapeDtypeStruct((nb,), I32),
            jax.ShapeDtypeStruct((nb,), I32),
            jax.ShapeDtypeStruct((1,), I32),
            jax.ShapeDtypeStruct((2, ne), I32),
        ],
        scratch_shapes=[pltpu.SMEM((ne,), I32)],
        compiler_params=_cparams(("arbitrary",)),
        name="slots",
    )(counts, eidx, rank)


def _dispatch_body(pad_ref, nu_ref, dest_hbm, h2_hbm, x_hbm, idx_smem, buf, zero_blk, isem, insem, ssem, zsem):
    i = pl.program_id(0)
    n = pl.num_programs(0)
    tiles = x_hbm.shape[1]
    r_ = zero_blk.shape[0]
    n_blocks = x_hbm.shape[0] // r_
    tm = buf.shape[1] // tiles
    ne = pad_ref.shape[1]
    ring = buf.shape[0]
    slot = lax.rem(i, ring)
    nslot = lax.rem(i + 1, ring)

    def idx_copy(blk, s):
        cols = pl.ds(pl.multiple_of(blk * tm, tm), tm)
        return pltpu.make_async_copy(dest_hbm.at[:, cols], idx_smem.at[s], isem.at[s])

    def in_copy(blk, s):
        rows = pl.ds(pl.multiple_of(blk * tm * tiles, tm * tiles), tm * tiles)
        return pltpu.make_async_copy(h2_hbm.at[rows], buf.at[s], insem.at[s])

    def wait_scatter(s):
        span = x_hbm.at[pl.ds(0, TOP_K * tm)]
        pltpu.make_async_copy(span, span, ssem.at[s]).wait()

    def pad_fill(e, wait):
        start, length = pad_ref[0, e], pad_ref[1, e]
        size = r_ // 2
        while size >= 1:
            take = (length & size) != 0

            @pl.when(take)
            def _(start=start, size=size):
                cp = pltpu.make_async_copy(zero_blk.at[pl.ds(0, size)], x_hbm.at[pl.ds(start, size)], zsem)
                if wait:
                    cp.wait()
                else:
                    cp.start(priority=1)
            start = start + jnp.where(take, size, 0)
            size //= 2
        return 0

    def zero_block_copy(blk):
        return pltpu.make_async_copy(zero_blk, x_hbm.at[pl.ds(pl.multiple_of(blk * r_, r_), r_)], zsem)

    @pl.when(i == 0)
    def _():
        zero_blk[...] = jnp.zeros_like(zero_blk)
        idx_copy(0, 0).start()
        in_copy(0, 0).start()

        def blk_body(b, _):
            zero_block_copy(b).start(priority=1)
            return 0
        lax.fori_loop(nu_ref[0], n_blocks, blk_body, 0)
        lax.fori_loop(0, ne, lambda e, _: pad_fill(e, False), 0)

    @pl.when(i >= ring - 1)
    def _():
        wait_scatter(nslot)

    @pl.when(i + 1 < n)
    def _():
        in_copy(i + 1, nslot).start()
        idx_copy(i + 1, nslot).start()

    in_copy(i, slot).wait()
    idx_copy(i, slot).wait()
    for j in range(tm):
        for k in range(TOP_K):
            pltpu.make_async_copy(buf.at[slot, pl.ds(j * tiles, tiles)], x_hbm.at[idx_smem[slot, k, j]],
                                  ssem.at[slot]).start(priority=k % 2)

    @pl.when(i + 1 == n)
    def _():
        for back in range(ring - 1):
            @pl.when(i >= back)
            def _(back=back):
                wait_scatter(lax.rem(i + ring - back, ring))
        lax.fori_loop(0, ne, lambda e, _: pad_fill(e, True), 0)

        def blk_wait(b, _):
            zero_block_copy(0).wait()
            return 0
        lax.fori_loop(nu_ref[0], n_blocks, blk_wait, 0)


def _dispatch(pad, n_used, dest, h2p, n_slots, tm):
    k_, t = dest.shape
    tiles = h2p.shape[0] // t
    grid_spec = pltpu.PrefetchScalarGridSpec(
        num_scalar_prefetch=2,
        grid=(t // tm,),
        in_specs=[pl.BlockSpec(memory_space=pl.ANY), pl.BlockSpec(memory_space=pl.ANY)],
        out_specs=pl.BlockSpec(memory_space=pl.ANY),
        scratch_shapes=[
            pltpu.SMEM((DISPATCH_RING, k_, tm), I32),
            pltpu.VMEM((DISPATCH_RING, tm * tiles, 128), U32),
            pltpu.VMEM((EXPERT_ROWS, tiles, 128), U32),
            pltpu.SemaphoreType.DMA((DISPATCH_RING,)),
            pltpu.SemaphoreType.DMA((DISPATCH_RING,)),
            pltpu.SemaphoreType.DMA((DISPATCH_RING,)),
            pltpu.SemaphoreType.DMA,
        ],
    )
    return pl.pallas_call(
        _dispatch_body,
        grid_spec=grid_spec,
        out_shape=jax.ShapeDtypeStruct((n_slots, tiles, 128), U32),
        compiler_params=_cparams(("arbitrary",)),
        name="dispatch",
    )(pad, n_used, dest, h2p)


def _expert_body(be_ref, first_ref, nxt_ref, nu_ref, x_hbm, wg_hbm, wu_hbm, wd_hbm, y_ref,
                  xbuf, wg_f, wu_f, wd_f, wg_s, wu_s, wd_s, xsem, wsem):
    i = pl.program_id(0)
    n_used = nu_ref[0]
    r_ = EXPERT_ROWS
    tiles = xbuf.shape[1] // r_
    slot = i % 2
    nslot = 1 - slot

    def x_copy(blk, s):
        rows = pl.ds(pl.multiple_of(blk * r_ * tiles, r_ * tiles), r_ * tiles)
        return pltpu.make_async_copy(x_hbm.at[rows], xbuf.at[s], xsem.at[s])

    def weight_copies(e):
        return (pltpu.make_async_copy(wg_hbm.at[e], wg_f, wsem.at[0]),
                pltpu.make_async_copy(wu_hbm.at[e], wu_f, wsem.at[1]),
                pltpu.make_async_copy(wd_hbm.at[e], wd_f, wsem.at[2]))

    @pl.when(i == 0)
    def _():
        for cp in weight_copies(be_ref[0]):
            cp.start(priority=1)
        x_copy(0, 0).start()

    @pl.when(first_ref[i] == 1)
    def _():
        for cp in weight_copies(be_ref[i]):
            cp.wait()
        wg_s[...] = wg_f[...].astype(BF16)
        wu_s[...] = wu_f[...].astype(BF16)
        wd_s[...] = wd_f[...].astype(BF16)

        @pl.when(nxt_ref[i] >= 0)
        def _():
            for cp in weight_copies(nxt_ref[i]):
                cp.start(priority=1)

    @pl.when(i < n_used)
    def _():
        @pl.when(i + 1 < n_used)
        def _():
            x_copy(i + 1, nslot).start()

        x_copy(i, slot).wait()
        lo, hi = _unpack_bf16_pair(_load_row_tiles(xbuf, (slot,), 0, r_, tiles))
        lo = lo.astype(BF16)
        hi = hi.astype(BF16)
        half = lo.shape[-1]
        a = _dot(lo, wg_s[0:half, :]) + _dot(hi, wg_s[half:, :])
        u = _dot(lo, wu_s[0:half, :]) + _dot(hi, wu_s[half:, :])
        hid = (_silu(a) * u).astype(BF16)
        y_lo = _dot(hid, wd_s[:, 0:half])
        y_hi = _dot(hid, wd_s[:, half:])
        _store_row_tiles(y_ref, (), _pack_bf16_pair(y_lo, y_hi))

    @pl.when(i >= n_used)
    def _():
        y_ref[...] = jnp.zeros_like(y_ref)


def _experts(block_e, first, nxt, n_used, x_sorted, wg, wu, wd):
    nb = block_e.shape[0]
    ne, d, f = wg.shape
    r_ = EXPERT_ROWS
    tiles = d // 256
    hbm = lambda: pl.BlockSpec(memory_space=pl.ANY)
    grid_spec = pltpu.PrefetchScalarGridSpec(
        num_scalar_prefetch=4,
        grid=(nb,),
        in_specs=[hbm(), hbm(), hbm(), hbm()],
        out_specs=pl.BlockSpec((r_ * tiles, 128), lambda i, *_: (i, 0)),
        scratch_shapes=[
            pltpu.VMEM((2, r_ * tiles, 128), U32),
            pltpu.VMEM((d, f), F32),
            pltpu.VMEM((d, f), F32),
            pltpu.VMEM((f, d), F32),
            pltpu.VMEM((d, f), BF16),
            pltpu.VMEM((d, f), BF16),
            pltpu.VMEM((f, d), BF16),
            pltpu.SemaphoreType.DMA((2,)),
            pltpu.SemaphoreType.DMA((3,)),
        ],
    )
    return pl.pallas_call(
        _expert_body,
        grid_spec=grid_spec,
        out_shape=jax.ShapeDtypeStruct((nb * r_ * tiles, 128), U32),
        compiler_params=_cparams(("arbitrary",)),
        name="experts",
    )(block_e, first, nxt, n_used, x_sorted, wg, wu, wd)


def _combine_body(dest_hbm, y_hbm, w_ref, x1_ref, h2_ref, g2_ref, sg_ref, su_ref, sd_ref, fnw_ref, o_ref,
                  idx_smem, ybuf, isem, gsem):
    i = pl.program_id(0)
    n = pl.num_programs(0)
    tm = x1_ref.shape[0]
    tiles = y_hbm.shape[1]
    slot = i % 2
    nslot = 1 - slot

    def idx_copy(blk, s):
        cols = pl.ds(pl.multiple_of(blk * tm, tm), tm)
        return pltpu.make_async_copy(dest_hbm.at[:, cols], idx_smem.at[s], isem.at[s])

    def row_copy(s, k, j):
        return pltpu.make_async_copy(y_hbm.at[idx_smem[s, k, j]], ybuf.at[s, pl.ds((k * tm + j) * tiles, tiles)],
                                     gsem.at[s])

    def wait_rows(s):
        pltpu.make_async_copy(ybuf.at[s], ybuf.at[s], gsem.at[s]).wait()

    def ahead(step):
        return jnp.minimum(step + 1, n - 1)

    @pl.when(i == 0)
    def _():
        idx_copy(0, 0).start()
        idx_copy(0, 0).wait()

        def body(j, _):
            for k in range(TOP_K):
                dst = pl.ds(pl.multiple_of((k * tm + j) * tiles, tiles), tiles)
                pltpu.make_async_copy(y_hbm.at[idx_smem[0, k, j]], ybuf.at[0, dst], gsem.at[0]).start()
            return 0
        lax.fori_loop(0, tm, body, 0)
        idx_copy(ahead(0), 1).start()

    idx_copy(ahead(i), nslot).wait()
    wait_rows(slot)

    w = w_ref[...]
    half = tiles * 128
    r_lo = [jnp.zeros((tm, 128), F32) for _ in range(tiles)]
    r_hi = [jnp.zeros((tm, 128), F32) for _ in range(tiles)]
    for k in range(TOP_K):
        for j in range(tm):
            row_copy(nslot, k, j).start(priority=j % 2)
        wk = w[:, k:k + 1]
        for kc in range(tiles):
            lo, hi = _unpack_bf16_pair(ybuf[slot, pl.ds(k * tm * tiles + kc, tm, stride=tiles), :])
            r_lo[kc] = r_lo[kc] + wk * lo
            r_hi[kc] = r_hi[kc] + wk * hi
    routed = jnp.concatenate(r_lo + r_hi, axis=-1)

    @pl.when(i + 1 < n)
    def _():
        idx_copy(ahead(i + 1), slot).start()

    @pl.when(i + 1 == n)
    def _():
        wait_rows(nslot)

    x_lo, x_hi = _unpack_bf16_pair(_load_row_tiles(h2_ref, (), 0, tm, tiles))
    x_lo = x_lo.astype(BF16)
    x_hi = x_hi.astype(BF16)
    a = _dot(x_lo, sg_ref[0:half, :]) + _dot(x_hi, sg_ref[half:, :])
    u = _dot(x_lo, su_ref[0:half, :]) + _dot(x_hi, su_ref[half:, :])
    shared = _dot((_silu(a) * u).astype(BF16), sd_ref[...])

    x2 = x1_ref[...] + g2_ref[0, 0] * (routed + shared)
    ms = jnp.mean(x2 * x2, axis=-1, keepdims=True)
    o_ref[...] = x2 * lax.rsqrt(ms + EPS) * fnw_ref[...]


def _combine(dest, y_sorted, w_tok, x1, h2p, mod4, sg, su, sd, fnw, tiles_per_batch, tm):
    t, d = x1.shape
    half = d // 2
    nrow = TOP_K * tm
    res = lambda shape: pl.BlockSpec(shape, lambda i: (0,) * len(shape), pipeline_mode=pl.Buffered(1))
    return pl.pallas_call(
        _combine_body,
        grid=(t // tm,),
        in_specs=[
            pl.BlockSpec(memory_space=pl.ANY),
            pl.BlockSpec(memory_space=pl.ANY),
            pl.BlockSpec((tm, TOP_K), lambda i: (i, 0)),
            pl.BlockSpec((tm, d), lambda i: (i, 0)),
            pl.BlockSpec((tm * (half // 128), 128), lambda i: (i, 0)),
            pl.BlockSpec((1, 1, 1, d), lambda i: (i // tiles_per_batch, 5, 0, 0)),
            res(sg.shape), res(su.shape), res(sd.shape), res(fnw.shape),
        ],
        out_specs=pl.BlockSpec((tm, d), lambda i: (i, 0)),
        out_shape=jax.ShapeDtypeStruct((t, d), F32),
        scratch_shapes=[
            pltpu.SMEM((2, TOP_K, tm), I32),
            pltpu.VMEM((2, nrow * (half // 128), 128), U32),
            pltpu.SemaphoreType.DMA((2,)),
            pltpu.SemaphoreType.DMA((2,)),
        ],
        compiler_params=_cparams(("arbitrary",)),
        name="combine",
    )(dest, y_sorted, w_tok, x1, h2p, mod4, sg, su, sd, fnw)


def kernel(x, c, ctx, c_ctx, ada_w, ada_b, norm1_w, norm2_w, w_in, s5_lam_re, s5_lam_im, s5_log_dt,
           s5_b_re, s5_b_im, s5_c_re, s5_c_im, s5_d, s5_glu_w, gla_gk_up, gla_gk_b, gla_norm_w,
           w_s5_proj, w_gla_proj, w_out, router_w, router_bias, exp_w_gate, exp_w_up, exp_w_down,
           sh_w_gate, sh_w_up, sh_w_down, final_norm_w):
    depth = ada_w.shape[0]
    assert depth == 1, "single-layer block: context outputs are never consumed"
    b_, l_, d = x.shape
    lc = ctx.shape[1]
    s5w = s5_d.shape[1]
    kw = gla_gk_up.shape[-1]
    vw = w_gla_proj.shape[1]
    dk, dv = kw // GLA_HEADS, vw // GLA_HEADS
    rank2 = 2 * GLA_GATE_RANK
    li = 0

    c8 = jnp.zeros((8, d), F32).at[:b_].set(c).at[b_].set(c_ctx)
    mod4 = _ada(c8, ada_w[li], ada_b[li]).reshape(8, N_MOD, 1, d)

    cuts = [0, s5w, s5w + kw, s5w + 2 * kw, s5w + 2 * kw + vw, s5w + 2 * kw + 2 * vw]
    c_gkd = cuts[5]
    w = w_in[li]
    w_main = jnp.concatenate([w[:, :c_gkd].astype(BF16), w[:, c_gkd + rank2:].astype(BF16)], axis=1)
    w_ctx = jnp.concatenate([w[:, cuts[0]:cuts[1]].astype(BF16), w[:, cuts[2]:cuts[4]].astype(BF16)], axis=1)
    w_gkd = jnp.zeros((d, 128), F32).at[:, :rank2].set(w[:, c_gkd:c_gkd + rank2]).astype(BF16)
    nw1 = norm1_w[li].reshape(1, d)
    p_lat, gk_lat, v_lat = _inproj(x, mod4, lambda b: b, nw1, w_main, w_gkd, tm=1024, tn=1024, s5w=s5w)
    p_ctx, gk_ctx, v_ctx = _inproj(ctx, mod4, lambda b: b_, nw1, w_ctx, w_gkd, tm=lc, tn=w_ctx.shape[1] // 2,
                                   s5w=s5w)

    factors = _s5_factors(s5_lam_re[li], s5_lam_im[li], s5_log_dt[li], s5_b_re[li], s5_b_im[li],
                          s5_c_re[li], s5_c_im[li], s5_d[li])
    ys = _s5(v_ctx, v_lat, *factors)

    up_pad = jnp.zeros((2, 128, kw), F32)
    up_pad = up_pad.at[0, :GLA_GATE_RANK].set(gla_gk_up[li, 0]).at[1, GLA_GATE_RANK:rank2].set(gla_gk_up[li, 1])
    yg = _gla(p_lat, p_ctx, gk_lat, gk_ctx, up_pad.astype(BF16), gla_gk_b[li].reshape(2, 1, kw), dk, dv)

    rw = jnp.zeros((d, 128), F32).at[:, :N_EXPERTS].set(router_w[li])
    rw_hi = rw.astype(BF16)
    rw2 = jnp.stack([rw_hi, (rw - rw_hi.astype(F32)).astype(BF16)])
    x1, h2p, logits = _merge(
        ys, yg, p_lat, x, mod4, s5_glu_w[li].astype(BF16), w_s5_proj[li].astype(BF16),
        w_gla_proj[li].astype(BF16), w_out[li].astype(BF16), gla_norm_w[li].reshape(1, dv),
        norm2_w[li].reshape(1, d), rw2, dv, tm=256)

    t = b_ * l_
    logits_t = logits.reshape(t, 128)[:, :N_EXPERTS].T
    eidx, wts, rank, cnt = _route(logits_t, router_bias[li])
    nb = (t * TOP_K + N_EXPERTS * (EXPERT_ROWS - 1) + EXPERT_ROWS - 1) // EXPERT_ROWS
    dest, block_e, first, nxt, n_used, pad = _slots(cnt[:, 0].astype(I32), eidx, rank, nb)

    tiles = d // 256
    x_sorted = _dispatch(pad, n_used, dest, h2p.reshape(t * tiles, 128), nb * EXPERT_ROWS, tm=128)
    y_sorted = _experts(block_e, first, nxt, n_used, x_sorted.reshape(nb * EXPERT_ROWS * tiles, 128),
                         exp_w_gate[li], exp_w_up[li], exp_w_down[li])

    tm_c = 128
    out = _combine(dest, y_sorted.reshape(-1, tiles, 128), wts.T, x1.reshape(t, d), h2p.reshape(t * tiles, 128), mod4,
                   sh_w_gate[li].astype(BF16), sh_w_up[li].astype(BF16), sh_w_down[li].astype(BF16),
                   final_norm_w.reshape(1, d), l_ // tm_c, tm_c)
    return out.reshape(b_, l_, d)
```

```python
import functools
import math

import jax
import jax.numpy as jnp
from jax import lax
from jax.experimental import pallas as pl
from jax.experimental.pallas import tpu as pltpu

F32 = jnp.float32
BF16 = jnp.bfloat16
U32 = jnp.uint32
I32 = jnp.int32

EPS = 1e-6
N_MOD = 6

S5_GROUP = 16
S5_STATE = 64
S5_CHUNK = 16

GLA_HEADS = 4
GLA_GATE_RANK = 16
GLA_GATE_NORM = 16.0
GLA_CHUNK = 64

N_EXPERTS = 64
TOP_K = 8
N_GROUPS = 8
TOPK_GROUPS = 4
ROUTED_SCALE = 2.5
EXPERT_ROWS = 256
DISPATCH_RING = 3
COMBINE_RING = 3

VMEM_LIMIT = 56 * 1024 * 1024


def _cparams(sem, vmem=VMEM_LIMIT):
    return pltpu.CompilerParams(dimension_semantics=sem, vmem_limit_bytes=vmem)


def _dot(a, b):
    return jnp.dot(a, b, preferred_element_type=F32)


def _sigmoid(x):
    return 1.0 / (1.0 + jnp.exp(-x))


def _silu(x):
    return x * _sigmoid(x)


def _pack_bf16_pair(lo, hi):
    lo_bits = lax.bitcast_convert_type(lo.astype(BF16).astype(F32), U32)
    hi_bits = lax.bitcast_convert_type(hi.astype(BF16).astype(F32), U32)
    return (lo_bits >> 16) | (hi_bits & jnp.uint32(0xFFFF0000))


def _unpack_bf16_pair(w):
    lo = lax.bitcast_convert_type(w << 16, F32)
    hi = lax.bitcast_convert_type(w & jnp.uint32(0xFFFF0000), F32)
    return lo, hi


def _store_row_tiles(ref, lead, val):
    rows, rt = val.shape[0], val.shape[1] // 128
    for kc in range(rt):
        ref[(*lead, pl.ds(kc, rows, stride=rt), slice(None))] = val[:, kc * 128:(kc + 1) * 128]


def _load_row_tiles(ref, lead, row0, rows, rt):
    return jnp.concatenate(
        [ref[(*lead, pl.ds(row0 * rt + kc, rows, stride=rt), slice(None))] for kc in range(rt)], axis=-1)


def _ada_body(c_ref, w_ref, b_ref, o_ref):
    c = c_ref[...]
    s = _silu(c).astype(BF16)
    o_ref[...] = _dot(s, w_ref[...].astype(BF16)) + b_ref[...]


def _ada(c8, ada_w, ada_b):
    d, n = ada_w.shape
    tn = 512
    return pl.pallas_call(
        _ada_body,
        grid=(n // tn,),
        in_specs=[
            pl.BlockSpec((8, d), lambda j: (0, 0)),
            pl.BlockSpec((d, tn), lambda j: (0, j)),
            pl.BlockSpec((1, tn), lambda j: (0, j)),
        ],
        out_specs=pl.BlockSpec((8, tn), lambda j: (0, j)),
        out_shape=jax.ShapeDtypeStruct((8, n), F32),
        compiler_params=_cparams(("parallel",)),
        name="ada",
    )(c8, ada_w, ada_b.reshape(1, n))


def _modulated_norm(x, nw, sh, sc):
    ms = jnp.mean(x * x, axis=-1, keepdims=True)
    y = x * lax.rsqrt(ms + EPS) * nw
    return y * (1.0 + sc) + sh


def _inproj_body(x_ref, sh_ref, sc_ref, nw_ref, w_ref, wg_ref, p_ref, g_ref, v_ref, h_scr, u_scr):
    @pl.when(pl.program_id(2) == 0)
    def _():
        h = _modulated_norm(x_ref[0], nw_ref[...], sh_ref[0, 0], sc_ref[0, 0]).astype(BF16)
        h_scr[...] = h
        g_ref[0] = _dot(h, wg_ref[...])

    r = _dot(h_scr[...], w_ref[...])
    p_ref[0] = r.astype(BF16)

    @pl.when(pl.program_id(2) == 0)
    def _():
        nch = u_scr.shape[1] // S5_CHUNK
        for c in range(u_scr.shape[0]):
            u_scr[c] = r[:, c * 128:(c + 1) * 128]
            for t in range(S5_CHUNK):
                piece = u_scr[c, pl.ds(t, nch, stride=S5_CHUNK), :]
                v_ref[c, 0, :, t * 128:(t + 1) * 128] = piece.astype(BF16)


def _inproj(x, mod4, mod_row, nw, w, wg, tm, tn, s5w):
    b_, l_, d = x.shape
    n = w.shape[1]
    assert tn >= s5w and tm % (16 * S5_CHUNK) == 0
    nblk, vw = s5w // 128, S5_CHUNK * 128
    return pl.pallas_call(
        _inproj_body,
        grid=(b_, l_ // tm, n // tn),
        in_specs=[
            pl.BlockSpec((1, tm, d), lambda b, i, j: (b, i, 0)),
            pl.BlockSpec((1, 1, 1, d), lambda b, i, j: (mod_row(b), 0, 0, 0)),
            pl.BlockSpec((1, 1, 1, d), lambda b, i, j: (mod_row(b), 1, 0, 0)),
            pl.BlockSpec((1, d), lambda b, i, j: (0, 0)),
            pl.BlockSpec((d, tn), lambda b, i, j: (0, j)),
            pl.BlockSpec((d, 128), lambda b, i, j: (0, 0)),
        ],
        out_specs=[
            pl.BlockSpec((1, tm, tn), lambda b, i, j: (b, i, j)),
            pl.BlockSpec((1, tm, 128), lambda b, i, j: (b, i, 0)),
            pl.BlockSpec((nblk, 1, tm // S5_CHUNK, vw), lambda b, i, j: (0, b, i, 0)),
        ],
        out_shape=[
            jax.ShapeDtypeStruct((b_, l_, n), BF16),
            jax.ShapeDtypeStruct((b_, l_, 128), F32),
            jax.ShapeDtypeStruct((nblk, b_, l_ // S5_CHUNK, vw), BF16),
        ],
        scratch_shapes=[pltpu.VMEM((tm, d), BF16), pltpu.VMEM((nblk, tm, 128), F32)],
        compiler_params=_cparams(("parallel", "parallel", "arbitrary")),
        name="inproj",
    )(x, mod4, mod4, nw, w, wg)


def _s5_factors(lam_re, lam_im, log_dt, b_re, b_im, c_re, c_im, d_skip):
    tc = S5_CHUNK
    g_, p_ = lam_re.shape[1:]
    h_ = b_re.shape[-1]
    dt = jnp.exp(log_dt)[..., None]
    ar, ai = lam_re * dt, lam_im * dt

    def powers(k):
        k = jnp.asarray(k, F32)
        mag = jnp.exp(ar[..., None] * k)
        return mag * jnp.cos(ai[..., None] * k), mag * jnp.sin(ai[..., None] * k)

    num_re = jnp.expm1(ar) * jnp.cos(ai) - 2.0 * jnp.sin(0.5 * ai) ** 2
    num_im = jnp.exp(ar) * jnp.sin(ai)
    den = lam_re * lam_re + lam_im * lam_im
    f_re = (num_re * lam_re + num_im * lam_im) / den
    f_im = (num_im * lam_re - num_re * lam_im) / den
    bb_re = f_re[..., None] * b_re - f_im[..., None] * b_im
    bb_im = f_re[..., None] * b_im + f_im[..., None] * b_re
    t_idx = jnp.arange(tc)

    def halves(a, b):
        return jnp.concatenate([a, b], axis=-1).transpose(1, 0, 2, 3)

    def power_rows(expo_f, expo_b):
        re_f, im_f = powers(expo_f)
        re_b, im_b = powers(expo_b)
        re = jnp.stack([re_f[0], re_b[1]]).transpose(0, 1, 3, 2)
        im = jnp.stack([im_f[0], im_b[1]]).transpose(0, 1, 3, 2)
        return re, im

    e_re, e_im = power_rows(tc - 1 - t_idx, t_idx)
    bt_re, bt_im = bb_re.transpose(0, 1, 3, 2), bb_im.transpose(0, 1, 3, 2)
    ea = jnp.stack([halves(e_re, e_re), halves(-e_im, e_im)], axis=2)
    ba = jnp.stack([halves(bt_re, bt_im), halves(bt_im, bt_re)], axis=2)

    def readout_rows(expo_f, expo_b):
        p_re, p_im = power_rows(expo_f, expo_b)
        return jnp.stack([halves(p_re, -p_im), halves(-p_im, -p_re)], axis=2)

    pq = readout_rows(t_idx - (tc - 1), -t_idx)
    po = readout_rows(t_idx + 1, tc - t_idx)
    ca = jnp.stack([halves(c_re, c_re), halves(c_im, c_im)], axis=2)

    a_re, a_im = powers(jnp.array([tc]))
    a_re, a_im = a_re[..., 0], a_im[..., 0]
    zeros = jnp.zeros_like(a_re[0])
    rows = []
    for dr in range(2):
        rows += [jnp.concatenate([a_re[dr], a_re[dr]], -1),
                 jnp.concatenate([-a_im[dr], a_im[dr]], -1),
                 jnp.concatenate([a_im[dr], -a_im[dr]], -1)]
    rows += [jnp.concatenate([zeros, zeros], -1)] * 2
    acoef = jnp.stack(rows, axis=1)
    dvec = jnp.tile(d_skip.reshape(g_, h_), (1, tc)).reshape(g_, 1, tc * h_)
    return ea, ba, pq, po, ca, acoef, dvec


def _split_bf16(x):
    hi = x.astype(BF16)
    return hi, (x - hi.astype(F32)).astype(BF16)


def _s5_body(vc_ref, vl_ref, ea_ref, ba_ref, pq_ref, po_ref, ca_ref, a_ref, d_ref, o_ref, y_scr, s_scr, h_scr,
               *, n_ctx, n_lat, nb):
    gb = ea_ref.shape[0]
    pw = 2 * S5_STATE
    gl = S5_GROUP
    kd = S5_CHUNK * gl
    lanes = 128
    n_tot = n_ctx + n_lat
    nt = (((1,), (1,)), ((), ()))

    def outer(x_ref, y_ref, q, dr):
        z = (x_ref[q, dr, 0][:, None, :] * y_ref[q, dr, 0][None, :, :]
             + x_ref[q, dr, 1][:, None, :] * y_ref[q, dr, 1][None, :, :])
        return z.reshape(kd, pw)
    v = jnp.concatenate([p for b in range(nb) for p in (vc_ref[0, b], vl_ref[0, b])], axis=0)
    width = v.shape[1]

    r_in = lax.broadcasted_iota(I32, (width, kd), 0)
    c_in = lax.broadcasted_iota(I32, (width, kd), 1)
    tgt_in = (r_in // lanes) * gl + (r_in % gl)
    grp_in = (r_in % lanes) // gl
    r_out = lax.broadcasted_iota(I32, (kd, width), 0)
    c_out = lax.broadcasted_iota(I32, (kd, width), 1)
    tgt_out = (c_out // lanes) * gl + (c_out % gl)
    grp_out = (c_out % lanes) // gl

    rs = lax.broadcasted_iota(I32, (kd, kd), 0)
    ct = lax.broadcasted_iota(I32, (kd, kd), 1)
    fmask = ct // gl >= rs // gl
    bmask = ct // gl <= rs // gl
    diag = rs == ct

    for q in range(gb):
        sel = jnp.where(jnp.logical_and(c_in == tgt_in, grp_in == q), 1.0, 0.0).astype(BF16)
        x2 = _dot(v, sel).astype(BF16)
        mm, ws = [], []
        for dr in range(2):
            ws.append(outer(ea_ref, ba_ref, q, dr))
            a_hi, a_lo = _split_bf16(ws[dr])
            b_hi, b_lo = _split_bf16(outer(pq_ref, ca_ref, q, dr))
            mm.append(lax.dot_general(a_hi, b_hi, nt, preferred_element_type=F32)
                      + lax.dot_general(a_hi, b_lo, nt, preferred_element_type=F32)
                      + lax.dot_general(a_lo, b_hi, nt, preferred_element_type=F32))
        m = jnp.where(fmask, mm[0], 0.0) + jnp.where(bmask, mm[1], 0.0) + jnp.where(diag, d_ref[q], 0.0)
        wsf, wsb = ws
        wcat = jnp.concatenate(
            [m, wsf, pltpu.roll(wsf, S5_STATE, 1), wsb, pltpu.roll(wsb, S5_STATE, 1)], axis=1).astype(BF16)
        r = _dot(x2, wcat)
        for b in range(nb):
            rb = r[b * n_tot:(b + 1) * n_tot, :]
            for j in range(kd // lanes):
                y_scr[q, j, pl.ds(b, n_tot, stride=nb), :] = rb[:, j * lanes:(j + 1) * lanes]
            for j in range(4):
                s_scr[q, j, pl.ds(b, n_tot, stride=nb), :] = rb[:, kd + j * pw:kd + (j + 1) * pw]

    a1f, a2f, a2sf = a_ref[:, 0:1, :], a_ref[:, 1:2, :], a_ref[:, 2:3, :]
    a1b, a2b, a2sb = a_ref[:, 3:4, :], a_ref[:, 4:5, :], a_ref[:, 5:6, :]
    first = lax.broadcasted_iota(I32, (gb, 2 * nb, pw), 1) < nb

    def both(x):
        rr = pltpu.roll(x, nb, 1)
        return jnp.where(first, x, rr), jnp.where(first, rr, x)

    def step(pf, pb, carry):
        hf, hfs, hb, hbs = carry
        rf = pl.ds(pl.multiple_of(pf * 2 * nb, 2 * nb), 2 * nb)
        rb = pl.ds(pl.multiple_of(pb * 2 * nb, 2 * nb), 2 * nb)
        sfa, sfb = both(s_scr[:, 0, rf, :])
        sfsa, sfsb = both(s_scr[:, 1, rf, :])
        sba, sbb = both(s_scr[:, 2, rb, :])
        sbsa, sbsb = both(s_scr[:, 3, rb, :])
        hf1 = a1f * hf + a2f * hfs + sfa
        hfs1 = a1f * hfs + a2sf * hf + sfsa
        h_scr[:, 0, rf, :] = jnp.where(first, hf, hf1)
        hf2 = a1f * hf1 + a2f * hfs1 + sfb
        hfs2 = a1f * hfs1 + a2sf * hf1 + sfsb
        hb1 = a1b * hb + a2b * hbs + sbb
        hbs1 = a1b * hbs + a2sb * hb + sbsb
        h_scr[:, 1, rb, :] = jnp.where(first, hb1, hb)
        hb2 = a1b * hb1 + a2b * hbs1 + sba
        hbs2 = a1b * hbs1 + a2sb * hb1 + sbsa
        return hf2, hfs2, hb2, hbs2

    z = jnp.zeros((gb, 2 * nb, pw), F32)
    carry = (z, z, z, z)
    pc, pl_ = n_ctx // 2, n_lat // 2
    carry = lax.fori_loop(0, pc, lambda i, c: step(i, pc - 1 - i, c), carry)
    carry = lax.fori_loop(0, pl_, lambda i, c: step(pc + i, pc + pl_ - 1 - i, c), carry)

    lo = n_ctx * nb
    for q in range(gb):
        yq = (lax.dot_general(h_scr[q, 0].astype(BF16), outer(po_ref, ca_ref, q, 0).astype(BF16), nt,
                              preferred_element_type=F32)
              + lax.dot_general(h_scr[q, 1].astype(BF16), outer(po_ref, ca_ref, q, 1).astype(BF16), nt,
                                preferred_element_type=F32))
        for j in range(kd // lanes):
            y_scr[q, j] = y_scr[q, j] + yq[:, j * lanes:(j + 1) * lanes]
        y = jnp.concatenate(
            [jnp.concatenate([y_scr[q, j, pl.ds(lo + b, n_lat, stride=nb), :] for j in range(kd // lanes)], axis=-1)
             for b in range(nb)], axis=0)
        selt = jnp.where(jnp.logical_and(r_out == tgt_out, grp_out == q), 1.0, 0.0).astype(BF16)
        placed = _dot(y.astype(BF16), selt)
        if q == 0:
            o_ref[0] = placed
        else:
            o_ref[0] = o_ref[0] + placed


def _s5(v_ctx, v_lat, ea, ba, pq, po, ca, acoef, dvec):
    nblk, nb, n_ctx, width = v_ctx.shape
    n_lat = v_lat.shape[2]
    rows = (n_ctx + n_lat) * nb
    g_ = ea.shape[0]
    gb = g_ // nblk
    pw = 2 * S5_STATE
    kd = S5_CHUNK * S5_GROUP
    out_rows = n_lat * nb
    fac = lambda: pl.BlockSpec((gb, 2, 2, S5_CHUNK, pw), lambda i: (i, 0, 0, 0, 0))
    return pl.pallas_call(
        functools.partial(_s5_body, n_ctx=n_ctx, n_lat=n_lat, nb=nb),
        grid=(nblk,),
        in_specs=[
            pl.BlockSpec((1, nb, n_ctx, width), lambda i: (i, 0, 0, 0)),
            pl.BlockSpec((1, nb, n_lat, width), lambda i: (i, 0, 0, 0)),
            fac(), fac(), fac(), fac(), fac(),
            pl.BlockSpec((gb, 8, pw), lambda i: (i, 0, 0)),
            pl.BlockSpec((gb, 1, kd), lambda i: (i, 0, 0)),
        ],
        out_specs=pl.BlockSpec((1, out_rows, width), lambda i: (i, 0, 0)),
        out_shape=jax.ShapeDtypeStruct((nblk, out_rows, width), F32),
        scratch_shapes=[
            pltpu.VMEM((gb, kd // 128, rows, 128), F32),
            pltpu.VMEM((gb, 4, rows, pw), F32),
            pltpu.VMEM((gb, 2, rows, pw), F32),
        ],
        compiler_params=_cparams(("parallel",)),
        name="s5",
    )(v_ctx, v_lat, ea, ba, pq, po, ca, acoef, dvec)


def _log_sigmoid(z):
    return jnp.minimum(z, 0.0) - jnp.log1p(jnp.exp(-jnp.abs(z)))


def _chunk_cumsum(g, pos, reverse):
    n = g.shape[0]
    x = g
    sh = 1
    while sh < GLA_CHUNK:
        if reverse:
            x = x + jnp.where(pos < GLA_CHUNK - sh, pltpu.roll(x, n - sh, 0), 0.0)
        else:
            x = x + jnp.where(pos >= sh, pltpu.roll(x, sh, 0), 0.0)
        sh *= 2
    return x


def _gla_body(q_ref, k_ref, v_ref, kc_ref, vc_ref, gk_ref, gkc_ref, up_ref, bias_ref, o_ref,
              bc_scr, kb_scr, qb_scr, st_ref, *, scale):
    c = GLA_CHUNK
    l_ = q_ref.shape[1]
    lc = kc_ref.shape[1]
    n_lat, n_ctx = l_ // c, lc // c
    row = lax.broadcasted_iota(I32, (c, c), 0)
    col = lax.broadcasted_iota(I32, (c, c), 1)
    nt = (((1,), (1,)), ((), ()))
    tn = (((0,), (0,)), ((), ()))

    for dr in range(2):
        smask = (col <= row) if dr == 0 else (col > row)
        up = up_ref[dr]
        bias = bias_ref[dr]
        edge_tile, edge_row = (c - 8, 7) if dr == 0 else (0, 0)

        for gk_r, k_r, base, n, latent in ((gkc_ref, kc_ref, 0, lc, False), (gk_ref, k_ref, lc, l_, True)):
            z = _dot(gk_r[0].astype(BF16), up) + bias
            g = _log_sigmoid(z) * (1.0 / GLA_GATE_NORM)
            pos = lax.broadcasted_iota(I32, g.shape, 0) & (c - 1)
            bc = _chunk_cumsum(g, pos, reverse=(dr == 1))
            bc_scr[base:base + n, :] = bc
            kb_scr[base:base + n, :] = k_r[0].astype(F32) * jnp.exp(-bc)
            if latent:
                qb_scr[...] = (q_ref[0].astype(F32) * (scale * jnp.exp(bc))).astype(BF16)

        def chunk_rows(n, base):
            r0 = pl.multiple_of(n * c, c)
            tot = bc_scr[pl.ds(base + r0 + edge_tile, 8), :][edge_row:edge_row + 1, :]
            return r0, pl.ds(base + r0, c), jnp.exp(tot)

        def update_state(kb, vv, decay):
            kd = (kb * decay).astype(BF16)
            ds_t = lax.dot_general(vv, kd, tn, preferred_element_type=F32)
            st_ref[...] = decay * st_ref[...] + ds_t

        def ctx_step(i, _):
            n = i if dr == 0 else n_ctx - 1 - i
            r0, rs, decay = chunk_rows(n, 0)
            update_state(kb_scr[rs, :], vc_ref[0, pl.ds(r0, c), :], decay)
            return 0

        def lat_step(i, _):
            n = i if dr == 0 else n_lat - 1 - i
            r0, rs, decay = chunk_rows(n, lc)
            ro = pl.ds(r0, c)
            kb = kb_scr[rs, :]
            vv = v_ref[0, ro, :]
            qb = qb_scr[ro, :]
            scores = lax.dot_general(qb, kb.astype(BF16), nt, preferred_element_type=F32)
            scores = jnp.where(smask, scores, 0.0).astype(BF16)
            o = _dot(scores, vv) + lax.dot_general(qb, st_ref[...].astype(BF16), nt,
                                                   preferred_element_type=F32)
            if dr == 0:
                o_ref[0, ro, :] = o
            else:
                o_ref[0, ro, :] = o_ref[0, ro, :] + o
            update_state(kb, vv, decay)
            return 0

        st_ref[...] = jnp.zeros_like(st_ref)
        lax.fori_loop(0, n_ctx, ctx_step, 0, unroll=2)
        lax.fori_loop(0, n_lat, lat_step, 0, unroll=4)


def _gla(p_lat, p_ctx, gk_lat, gk_ctx, up_pad, bias, dk, dv):
    b_, l_, _ = p_lat.shape
    lc = p_ctx.shape[1]
    hh = GLA_HEADS
    s5w = hh * dv
    q0, k0, v0 = s5w // dk, (s5w + hh * dk) // dk, (s5w + 2 * hh * dk) // dv
    kc0, vc0 = s5w // dk, (s5w + hh * dk) // dv
    return pl.pallas_call(
        functools.partial(_gla_body, scale=dk ** -0.5),
        grid=(b_, hh),
        in_specs=[
            pl.BlockSpec((1, l_, dk), lambda b, h: (b, 0, q0 + h)),
            pl.BlockSpec((1, l_, dk), lambda b, h: (b, 0, k0 + h)),
            pl.BlockSpec((1, l_, dv), lambda b, h: (b, 0, v0 + h)),
            pl.BlockSpec((1, lc, dk), lambda b, h: (b, 0, kc0 + h)),
            pl.BlockSpec((1, lc, dv), lambda b, h: (b, 0, vc0 + h)),
            pl.BlockSpec((1, l_, 128), lambda b, h: (b, 0, 0)),
            pl.BlockSpec((1, lc, 128), lambda b, h: (b, 0, 0)),
            pl.BlockSpec((2, 128, dk), lambda b, h: (0, 0, h)),
            pl.BlockSpec((2, 1, dk), lambda b, h: (0, 0, h)),
        ],
        out_specs=pl.BlockSpec((1, l_, dv), lambda b, h: (b, 0, h)),
        out_shape=jax.ShapeDtypeStruct((b_, l_, hh * dv), F32),
        scratch_shapes=[
            pltpu.VMEM((lc + l_, dk), F32),
            pltpu.VMEM((lc + l_, dk), F32),
            pltpu.VMEM((l_, dk), BF16),
            pltpu.VMEM((dv, dk), F32),
        ],
        compiler_params=_cparams(("parallel", "parallel")),
        name="gla",
    )(p_lat, p_lat, p_lat, p_ctx, p_ctx, gk_lat, gk_ctx, up_pad, bias)


def _gelu_tanh(x):
    return 0.5 * x * (1.0 + jnp.tanh(math.sqrt(2.0 / math.pi) * (x + 0.044715 * (x * x * x))))


def _merge_body(ys_ref, yg_ref, go_ref, gs_ref, gg_ref, x_ref, g1_ref, sh2_ref, sc2_ref,
                glu_ref, ws5_ref, wgla_ref, wout_ref, gnw_ref, n2w_ref, rw_ref,
                x1_ref, h2_ref, lg_ref, ys_scr, *, dv):
    nch = ys_ref.shape[1]
    for c in range(ys_ref.shape[0]):
        for t in range(S5_CHUNK):
            ys_scr[c, pl.ds(t, nch, stride=S5_CHUNK), :] = ys_ref[c, :, t * 128:(t + 1) * 128]
    a = _gelu_tanh(jnp.concatenate([ys_scr[c] for c in range(ys_ref.shape[0])], axis=-1))
    a = a * _sigmoid(_dot(a.astype(BF16), glu_ref[...]))
    pa = _dot(a.astype(BF16), ws5_ref[...])

    yg = yg_ref[0]
    parts = []
    for h in range(GLA_HEADS):
        oh = yg[:, h * dv:(h + 1) * dv]
        ms = jnp.mean(oh * oh, axis=-1, keepdims=True)
        parts.append(oh * lax.rsqrt(ms + EPS) * gnw_ref[...])
    gl = jnp.concatenate(parts, axis=-1) * _silu(go_ref[0].astype(F32))
    pb = _dot(gl.astype(BF16), wgla_ref[...])

    m = _sigmoid(gs_ref[0].astype(F32)) * pa + _sigmoid(gg_ref[0].astype(F32)) * pb
    y = _dot(m.astype(BF16), wout_ref[...])
    x1 = x_ref[0] + g1_ref[0, 0] * y
    x1_ref[0] = x1

    h2 = _modulated_norm(x1, n2w_ref[...], sh2_ref[0, 0], sc2_ref[0, 0])
    h_hi = h2.astype(BF16)
    h_lo = (h2 - h_hi.astype(F32)).astype(BF16)
    half = h2.shape[-1] // 2
    _store_row_tiles(h2_ref, (0,), _pack_bf16_pair(h2[:, :half], h2[:, half:]))
    lg_ref[0] = _dot(h_hi, rw_ref[0]) + _dot(h_lo, rw_ref[0]) + _dot(h_hi, rw_ref[1])


def _resident(shape):
    nd = len(shape)
    return pl.BlockSpec(shape, lambda b, i: (0,) * nd, pipeline_mode=pl.Buffered(1))


def _merge(ys, yg, p_lat, x, mod4, glu_w, ws5, wgla, wout, gnw, n2w, rw2, dv, tm):
    b_, l_, d = x.shape
    nblk, _, vw = ys.shape
    w5 = nblk * 128
    nch = tm // S5_CHUNK
    go0 = (p_lat.shape[-1] - 2 * d - w5) // w5
    gs0 = (p_lat.shape[-1] - 2 * d) // d
    modspec = lambda r: pl.BlockSpec((1, 1, 1, d), lambda b, i: (b, r, 0, 0))
    return pl.pallas_call(
        functools.partial(_merge_body, dv=dv),
        grid=(b_, l_ // tm),
        in_specs=[
            pl.BlockSpec((nblk, nch, vw), lambda b, i: (0, b * (l_ // tm) + i, 0)),
            pl.BlockSpec((1, tm, w5), lambda b, i: (b, i, 0)),
            pl.BlockSpec((1, tm, w5), lambda b, i: (b, i, go0)),
            pl.BlockSpec((1, tm, d), lambda b, i: (b, i, gs0)),
            pl.BlockSpec((1, tm, d), lambda b, i: (b, i, gs0 + 1)),
            pl.BlockSpec((1, tm, d), lambda b, i: (b, i, 0)),
            modspec(2), modspec(3), modspec(4),
            _resident(glu_w.shape), _resident(ws5.shape), _resident(wgla.shape), _resident(wout.shape),
            _resident(gnw.shape), _resident(n2w.shape), _resident(rw2.shape),
        ],
        out_specs=[
            pl.BlockSpec((1, tm, d), lambda b, i: (b, i, 0)),
            pl.BlockSpec((1, tm * (d // 256), 128), lambda b, i: (b, i, 0)),
            pl.BlockSpec((1, tm, 128), lambda b, i: (b, i, 0)),
        ],
        out_shape=[
            jax.ShapeDtypeStruct((b_, l_, d), F32),
            jax.ShapeDtypeStruct((b_, l_ * (d // 256), 128), U32),
            jax.ShapeDtypeStruct((b_, l_, 128), F32),
        ],
        scratch_shapes=[pltpu.VMEM((nblk, tm, 128), F32)],
        compiler_params=_cparams(("parallel", "parallel")),
        name="merge",
    )(ys, yg, p_lat, p_lat, p_lat, x, mod4, mod4, mod4, glu_w, ws5, wgla, wout, gnw, n2w, rw2)


def _route_body(lg_ref, bias_ref, eidx_ref, w_ref, rank_ref, cnt_ref, carry_ref):
    ne, tt = lg_ref.shape
    gsz = ne // N_GROUPS
    neg = -jnp.inf

    @pl.when(pl.program_id(0) == 0)
    def _():
        carry_ref[...] = jnp.zeros_like(carry_ref)

    sc = _sigmoid(lg_ref[...])
    ch = sc + bias_ref[...]

    def first_max(v, idx, big):
        m = jnp.max(v, axis=0, keepdims=True)
        i = jnp.min(jnp.where(v == m, idx, big), axis=0, keepdims=True)
        return m, i

    midx = lax.broadcasted_iota(I32, (gsz, tt), 0).astype(F32)
    gs_rows = []
    for g in range(N_GROUPS):
        v = ch[g * gsz:(g + 1) * gsz, :]
        m1, i1 = first_max(v, midx, float(gsz))
        m2 = jnp.max(jnp.where(midx == i1, neg, v), axis=0, keepdims=True)
        gs_rows.append(m1 + m2)
    gscore = jnp.concatenate(gs_rows, axis=0)

    gidx = lax.broadcasted_iota(I32, (N_GROUPS, tt), 0).astype(F32)
    gsel = jnp.zeros((N_GROUPS, tt), F32)
    cur = gscore
    for _ in range(TOPK_GROUPS):
        _, i = first_max(cur, gidx, float(N_GROUPS))
        hit = gidx == i
        gsel = jnp.where(hit, 1.0, gsel)
        cur = jnp.where(hit, neg, cur)

    masked = jnp.concatenate(
        [jnp.where(gsel[g:g + 1, :] > 0.0, ch[g * gsz:(g + 1) * gsz, :], neg) for g in range(N_GROUPS)], axis=0)

    eiota = lax.broadcasted_iota(I32, (ne, tt), 0).astype(F32)
    sel = jnp.zeros((ne, tt), F32)
    cur = masked
    idx_rows, s_rows = [], []
    for _ in range(TOP_K):
        _, i = first_max(cur, eiota, float(ne))
        hit = eiota == i
        idx_rows.append(i)
        s_rows.append(jnp.sum(jnp.where(hit, sc, 0.0), axis=0, keepdims=True))
        sel = jnp.where(hit, 1.0, sel)
        cur = jnp.where(hit, neg, cur)
    idx = jnp.concatenate(idx_rows, axis=0)
    s = jnp.concatenate(s_rows, axis=0)
    w_ref[...] = s / jnp.sum(s, axis=0, keepdims=True) * ROUTED_SCALE
    eidx_ref[...] = idx.astype(I32)

    cw = 256
    a_i = lax.broadcasted_iota(I32, (cw, cw), 0)
    b_i = lax.broadcasted_iota(I32, (cw, cw), 1)
    upper = jnp.where(a_i < b_i, 1.0, 0.0).astype(BF16)
    carry = carry_ref[:, 0:1]
    pieces = []
    for c0 in range(0, tt, cw):
        sc_c = sel[:, c0:c0 + cw]
        pieces.append(_dot(sc_c.astype(BF16), upper) + carry)
        carry = carry + jnp.sum(sc_c, axis=1, keepdims=True)
    rank_full = jnp.concatenate(pieces, axis=1)
    carry_ref[...] = jnp.broadcast_to(carry, carry_ref.shape)
    cnt_ref[...] = jnp.broadcast_to(carry, cnt_ref.shape)
    rk = [jnp.sum(jnp.where(eiota == idx_rows[k], rank_full, 0.0), axis=0, keepdims=True) for k in range(TOP_K)]
    rank_ref[...] = jnp.concatenate(rk, axis=0).astype(I32)


def _route(logits_t, bias):
    ne, t = logits_t.shape
    tt = 1024
    return pl.pallas_call(
        _route_body,
        grid=(t // tt,),
        in_specs=[
            pl.BlockSpec((ne, tt), lambda i: (0, i)),
            pl.BlockSpec((ne, 1), lambda i: (0, 0)),
        ],
        out_specs=[
            pl.BlockSpec((TOP_K, tt), lambda i: (0, i)),
            pl.BlockSpec((TOP_K, tt), lambda i: (0, i)),
            pl.BlockSpec((TOP_K, tt), lambda i: (0, i)),
            pl.BlockSpec((ne, 128), lambda i: (0, 0)),
        ],
        out_shape=[
            jax.ShapeDtypeStruct((TOP_K, t), I32),
            jax.ShapeDtypeStruct((TOP_K, t), F32),
            jax.ShapeDtypeStruct((TOP_K, t), I32),
            jax.ShapeDtypeStruct((ne, 128), F32),
        ],
        scratch_shapes=[pltpu.VMEM((ne, 128), F32)],
        compiler_params=_cparams(("arbitrary",)),
        name="route",
    )(logits_t, bias.reshape(ne, 1))


def _slots_body(cnt_ref, eidx_ref, rank_ref, dest_ref, be_ref, first_ref, nxt_ref, nu_ref, pad_ref, off_smem):
    ne = cnt_ref.shape[0]
    nb = be_ref.shape[0]
    r_ = EXPERT_ROWS
    shift = r_.bit_length() - 1

    @pl.when(pl.program_id(0) == 0)
    def _():
        def per_expert(e, off):
            cnt = cnt_ref[e]
            n = lax.shift_right_logical(cnt + (r_ - 1), shift)
            off_smem[e] = off
            pad_ref[0, e] = off * r_ + cnt
            pad_ref[1, e] = n * r_ - cnt

            def fill(j, _):
                be_ref[off + j] = e
                first_ref[off + j] = jnp.where(j == 0, 1, 0)
                return 0
            lax.fori_loop(0, n, fill, 0)
            return off + n

        n_used = lax.fori_loop(0, ne, per_expert, 0)
        nu_ref[0] = n_used
        last_e = be_ref[n_used - 1]

        def tail(j, _):
            be_ref[j] = last_e
            first_ref[j] = 0
            return 0
        lax.fori_loop(n_used, nb, tail, 0)

        def back(i, nx):
            j = nb - 1 - i
            nxt_ref[j] = nx
            return jnp.where(first_ref[j] == 1, be_ref[j], nx)
        lax.fori_loop(0, nb, back, -1)

    ei = eidx_ref[...]
    acc = rank_ref[...]
    for e in range(ne):
        acc = acc + jnp.where(ei == e, off_smem[e] * r_, 0)
    dest_ref[...] = acc


def _slots(counts, eidx, rank, nb):
    k_, t = eidx.shape
    ne = counts.shape[0]
    tl = 1024
    smem = lambda: pl.BlockSpec(memory_space=pltpu.SMEM)
    chunk = lambda: pl.BlockSpec((k_, tl), lambda i: (0, i))
    return pl.pallas_call(
        _slots_body,
        grid=(t // tl,),
        in_specs=[smem(), chunk(), chunk()],
        out_specs=[chunk(), smem(), smem(), smem(), smem(), smem()],
        out_shape=[
            jax.ShapeDtypeStruct((k_, t), I32),
            jax.ShapeDtypeStruct((nb,), I32),
            jax.ShapeDtypeStruct((nb,), I32),
            jax.ShapeDtypeStruct((nb,), I32),
            jax.ShapeDtypeStruct((1,), I32),
            jax.ShapeDtypeStruct((2, ne), I32),
        ],
        scratch_shapes=[pltpu.SMEM((ne,), I32)],
        compiler_params=_cparams(("arbitrary",)),
        name="slots",
    )(counts, eidx, rank)


def _dispatch_body(pad_ref, nu_ref, dest_hbm, h2_hbm, x_hbm, idx_smem, buf, zero_blk, isem, insem, ssem, zsem):
    i = pl.program_id(0)
    n = pl.num_programs(0)
    tiles = x_hbm.shape[1]
    r_ = zero_blk.shape[0]
    n_blocks = x_hbm.shape[0] // r_
    tm = buf.shape[1] // tiles
    ne = pad_ref.shape[1]
    ring = buf.shape[0]
    slot = lax.rem(i, ring)
    nslot = lax.rem(i + 1, ring)

    def idx_copy(blk, s):
        cols = pl.ds(pl.multiple_of(blk * tm, tm), tm)
        return pltpu.make_async_copy(dest_hbm.at[:, cols], idx_smem.at[s], isem.at[s])

    def in_copy(blk, s):
        rows = pl.ds(pl.multiple_of(blk * tm * tiles, tm * tiles), tm * tiles)
        return pltpu.make_async_copy(h2_hbm.at[rows], buf.at[s], insem.at[s])

    def wait_scatter(s):
        span = x_hbm.at[pl.ds(0, TOP_K * tm)]
        pltpu.make_async_copy(span, span, ssem.at[s]).wait()

    def pad_fill(e, wait):
        start, length = pad_ref[0, e], pad_ref[1, e]
        size = r_ // 2
        while size >= 1:
            take = (length & size) != 0

            @pl.when(take)
            def _(start=start, size=size):
                cp = pltpu.make_async_copy(zero_blk.at[pl.ds(0, size)], x_hbm.at[pl.ds(start, size)], zsem)
                if wait:
                    cp.wait()
                else:
                    cp.start(priority=1)
            start = start + jnp.where(take, size, 0)
            size //= 2
        return 0

    def zero_block_copy(blk):
        return pltpu.make_async_copy(zero_blk, x_hbm.at[pl.ds(pl.multiple_of(blk * r_, r_), r_)], zsem)

    @pl.when(i == 0)
    def _():
        zero_blk[...] = jnp.zeros_like(zero_blk)
        idx_copy(0, 0).start()
        in_copy(0, 0).start()

        def blk_body(b, _):
            zero_block_copy(b).start(priority=1)
            return 0
        lax.fori_loop(nu_ref[0], n_blocks, blk_body, 0)
        lax.fori_loop(0, ne, lambda e, _: pad_fill(e, False), 0)

    @pl.when(i >= ring - 1)
    def _():
        wait_scatter(nslot)

    @pl.when(i + 1 < n)
    def _():
        in_copy(i + 1, nslot).start()
        idx_copy(i + 1, nslot).start()

    in_copy(i, slot).wait()
    idx_copy(i, slot).wait()
    for j in range(tm):
        for k in range(TOP_K):
            pltpu.make_async_copy(buf.at[slot, pl.ds(j * tiles, tiles)], x_hbm.at[idx_smem[slot, k, j]],
                                  ssem.at[slot]).start(priority=k % 2)

    @pl.when(i + 1 == n)
    def _():
        for back in range(ring - 1):
            @pl.when(i >= back)
            def _(back=back):
                wait_scatter(lax.rem(i + ring - back, ring))
        lax.fori_loop(0, ne, lambda e, _: pad_fill(e, True), 0)

        def blk_wait(b, _):
            zero_block_copy(0).wait()
            return 0
        lax.fori_loop(nu_ref[0], n_blocks, blk_wait, 0)


def _dispatch(pad, n_used, dest, h2p, n_slots, tm):
    k_, t = dest.shape
    tiles = h2p.shape[0] // t
    grid_spec = pltpu.PrefetchScalarGridSpec(
        num_scalar_prefetch=2,
        grid=(t // tm,),
        in_specs=[pl.BlockSpec(memory_space=pl.ANY), pl.BlockSpec(memory_space=pl.ANY)],
        out_specs=pl.BlockSpec(memory_space=pl.ANY),
        scratch_shapes=[
            pltpu.SMEM((DISPATCH_RING, k_, tm), I32),
            pltpu.VMEM((DISPATCH_RING, tm * tiles, 128), U32),
            pltpu.VMEM((EXPERT_ROWS, tiles, 128), U32),
            pltpu.SemaphoreType.DMA((DISPATCH_RING,)),
            pltpu.SemaphoreType.DMA((DISPATCH_RING,)),
            pltpu.SemaphoreType.DMA((DISPATCH_RING,)),
            pltpu.SemaphoreType.DMA,
        ],
    )
    return pl.pallas_call(
        _dispatch_body,
        grid_spec=grid_spec,
        out_shape=jax.ShapeDtypeStruct((n_slots, tiles, 128), U32),
        compiler_params=_cparams(("arbitrary",)),
        name="dispatch",
    )(pad, n_used, dest, h2p)


def _expert_body(be_ref, first_ref, nxt_ref, nu_ref, x_hbm, wg_hbm, wu_hbm, wd_hbm, y_ref,
                  xbuf, wg_f, wu_f, wd_f, wg_s, wu_s, wd_s, xsem, wsem):
    i = pl.program_id(0)
    n_used = nu_ref[0]
    r_ = EXPERT_ROWS
    tiles = xbuf.shape[1] // r_
    slot = i % 2
    nslot = 1 - slot

    def x_copy(blk, s):
        rows = pl.ds(pl.multiple_of(blk * r_ * tiles, r_ * tiles), r_ * tiles)
        return pltpu.make_async_copy(x_hbm.at[rows], xbuf.at[s], xsem.at[s])

    def weight_copies(e):
        return (pltpu.make_async_copy(wg_hbm.at[e], wg_f, wsem.at[0]),
                pltpu.make_async_copy(wu_hbm.at[e], wu_f, wsem.at[1]),
                pltpu.make_async_copy(wd_hbm.at[e], wd_f, wsem.at[2]))

    @pl.when(i == 0)
    def _():
        for cp in weight_copies(be_ref[0]):
            cp.start(priority=1)
        x_copy(0, 0).start()

    @pl.when(first_ref[i] == 1)
    def _():
        for cp in weight_copies(be_ref[i]):
            cp.wait()
        wg_s[...] = wg_f[...].astype(BF16)
        wu_s[...] = wu_f[...].astype(BF16)
        wd_s[...] = wd_f[...].astype(BF16)

        @pl.when(nxt_ref[i] >= 0)
        def _():
            for cp in weight_copies(nxt_ref[i]):
                cp.start(priority=1)

    @pl.when(i < n_used)
    def _():
        @pl.when(i + 1 < n_used)
        def _():
            x_copy(i + 1, nslot).start()

        x_copy(i, slot).wait()
        lo, hi = _unpack_bf16_pair(_load_row_tiles(xbuf, (slot,), 0, r_, tiles))
        lo = lo.astype(BF16)
        hi = hi.astype(BF16)
        half = lo.shape[-1]
        a = _dot(lo, wg_s[0:half, :]) + _dot(hi, wg_s[half:, :])
        u = _dot(lo, wu_s[0:half, :]) + _dot(hi, wu_s[half:, :])
        hid = (_silu(a) * u).astype(BF16)
        y_lo = _dot(hid, wd_s[:, 0:half])
        y_hi = _dot(hid, wd_s[:, half:])
        _store_row_tiles(y_ref, (), _pack_bf16_pair(y_lo, y_hi))

    @pl.when(i >= n_used)
    def _():
        y_ref[...] = jnp.zeros_like(y_ref)


def _experts(block_e, first, nxt, n_used, x_sorted, wg, wu, wd):
    nb = block_e.shape[0]
    ne, d, f = wg.shape
    r_ = EXPERT_ROWS
    tiles = d // 256
    hbm = lambda: pl.BlockSpec(memory_space=pl.ANY)
    grid_spec = pltpu.PrefetchScalarGridSpec(
        num_scalar_prefetch=4,
        grid=(nb,),
        in_specs=[hbm(), hbm(), hbm(), hbm()],
        out_specs=pl.BlockSpec((r_ * tiles, 128), lambda i, *_: (i, 0)),
        scratch_shapes=[
            pltpu.VMEM((2, r_ * tiles, 128), U32),
            pltpu.VMEM((d, f), F32),
            pltpu.VMEM((d, f), F32),
            pltpu.VMEM((f, d), F32),
            pltpu.VMEM((d, f), BF16),
            pltpu.VMEM((d, f), BF16),
            pltpu.VMEM((f, d), BF16),
            pltpu.SemaphoreType.DMA((2,)),
            pltpu.SemaphoreType.DMA((3,)),
        ],
    )
    return pl.pallas_call(
        _expert_body,
        grid_spec=grid_spec,
        out_shape=jax.ShapeDtypeStruct((nb * r_ * tiles, 128), U32),
        compiler_params=_cparams(("arbitrary",)),
        name="experts",
    )(block_e, first, nxt, n_used, x_sorted, wg, wu, wd)


def _combine_body(dest_hbm, y_hbm, w_ref, x1_ref, h2_ref, g2_ref, sg_ref, su_ref, sd_ref, fnw_ref, o_ref,
                  idx_smem, ybuf, isem, gsem):
    i = pl.program_id(0)
    n = pl.num_programs(0)
    tm = x1_ref.shape[0]
    tiles = y_hbm.shape[1]
    ring = ybuf.shape[0]
    la = ring - 1
    slot = lax.rem(i, ring)
    aslot = lax.rem(i + la, ring)

    def idx_copy(blk, s):
        cols = pl.ds(pl.multiple_of(blk * tm, tm), tm)
        return pltpu.make_async_copy(dest_hbm.at[:, cols], idx_smem.at[s], isem.at[s])

    def row_copy(s, k, j):
        return pltpu.make_async_copy(y_hbm.at[idx_smem[s, k, j]], ybuf.at[s, pl.ds((k * tm + j) * tiles, tiles)],
                                     gsem.at[s])

    def wait_rows(s):
        pltpu.make_async_copy(ybuf.at[s], ybuf.at[s], gsem.at[s]).wait()

    def ahead(step):
        return jnp.minimum(step + la, n - 1)

    @pl.when(i == 0)
    def _():
        for b in range(la):
            idx_copy(jnp.minimum(b, n - 1), b).start()
            idx_copy(jnp.minimum(b, n - 1), b).wait()

            def body(j, _, b=b):
                for k in range(TOP_K):
                    dst = pl.ds(pl.multiple_of((k * tm + j) * tiles, tiles), tiles)
                    pltpu.make_async_copy(y_hbm.at[idx_smem[b, k, j]], ybuf.at[b, dst], gsem.at[b]).start()
                return 0
            lax.fori_loop(0, tm, body, 0)
        idx_copy(ahead(0), la).start()

    idx_copy(ahead(i), aslot).wait()
    wait_rows(slot)

    w = w_ref[...]
    half = tiles * 128
    r_lo = [jnp.zeros((tm, 128), F32) for _ in range(tiles)]
    r_hi = [jnp.zeros((tm, 128), F32) for _ in range(tiles)]
    for k in range(TOP_K):
        for j in range(tm):
            row_copy(aslot, k, j).start(priority=j % 2)
        wk = w[:, k:k + 1]
        for kc in range(tiles):
            lo, hi = _unpack_bf16_pair(ybuf[slot, pl.ds(k * tm * tiles + kc, tm, stride=tiles), :])
            r_lo[kc] = r_lo[kc] + wk * lo
            r_hi[kc] = r_hi[kc] + wk * hi
    routed = jnp.concatenate(r_lo + r_hi, axis=-1)

    @pl.when(i + 1 < n)
    def _():
        idx_copy(ahead(i + 1), slot).start()

    @pl.when(i + 1 == n)
    def _():
        for b in range(1, ring):
            wait_rows(lax.rem(i + b, ring))

    x_lo, x_hi = _unpack_bf16_pair(_load_row_tiles(h2_ref, (), 0, tm, tiles))
    x_lo = x_lo.astype(BF16)
    x_hi = x_hi.astype(BF16)
    a = _dot(x_lo, sg_ref[0:half, :]) + _dot(x_hi, sg_ref[half:, :])
    u = _dot(x_lo, su_ref[0:half, :]) + _dot(x_hi, su_ref[half:, :])
    shared = _dot((_silu(a) * u).astype(BF16), sd_ref[...])

    x2 = x1_ref[...] + g2_ref[0, 0] * (routed + shared)
    ms = jnp.mean(x2 * x2, axis=-1, keepdims=True)
    o_ref[...] = x2 * lax.rsqrt(ms + EPS) * fnw_ref[...]


def _combine(dest, y_sorted, w_tok, x1, h2p, mod4, sg, su, sd, fnw, tiles_per_batch, tm):
    t, d = x1.shape
    half = d // 2
    nrow = TOP_K * tm
    res = lambda shape: pl.BlockSpec(shape, lambda i: (0,) * len(shape), pipeline_mode=pl.Buffered(1))
    return pl.pallas_call(
        _combine_body,
        grid=(t // tm,),
        in_specs=[
            pl.BlockSpec(memory_space=pl.ANY),
            pl.BlockSpec(memory_space=pl.ANY),
            pl.BlockSpec((tm, TOP_K), lambda i: (i, 0)),
            pl.BlockSpec((tm, d), lambda i: (i, 0)),
            pl.BlockSpec((tm * (half // 128), 128), lambda i: (i, 0)),
            pl.BlockSpec((1, 1, 1, d), lambda i: (i // tiles_per_batch, 5, 0, 0)),
            res(sg.shape), res(su.shape), res(sd.shape), res(fnw.shape),
        ],
        out_specs=pl.BlockSpec((tm, d), lambda i: (i, 0)),
        out_shape=jax.ShapeDtypeStruct((t, d), F32),
        scratch_shapes=[
            pltpu.SMEM((COMBINE_RING, TOP_K, tm), I32),
            pltpu.VMEM((COMBINE_RING, nrow * (half // 128), 128), U32),
            pltpu.SemaphoreType.DMA((COMBINE_RING,)),
            pltpu.SemaphoreType.DMA((COMBINE_RING,)),
        ],
        compiler_params=_cparams(("arbitrary",)),
        name="combine",
    )(dest, y_sorted, w_tok, x1, h2p, mod4, sg, su, sd, fnw)


def kernel(x, c, ctx, c_ctx, ada_w, ada_b, norm1_w, norm2_w, w_in, s5_lam_re, s5_lam_im, s5_log_dt,
           s5_b_re, s5_b_im, s5_c_re, s5_c_im, s5_d, s5_glu_w, gla_gk_up, gla_gk_b, gla_norm_w,
           w_s5_proj, w_gla_proj, w_out, router_w, router_bias, exp_w_gate, exp_w_up, exp_w_down,
           sh_w_gate, sh_w_up, sh_w_down, final_norm_w):
    depth = ada_w.shape[0]
    assert depth == 1, "single-layer block: context outputs are never consumed"
    b_, l_, d = x.shape
    lc = ctx.shape[1]
    s5w = s5_d.shape[1]
    kw = gla_gk_up.shape[-1]
    vw = w_gla_proj.shape[1]
    dk, dv = kw // GLA_HEADS, vw // GLA_HEADS
    rank2 = 2 * GLA_GATE_RANK
    li = 0

    c8 = jnp.zeros((8, d), F32).at[:b_].set(c).at[b_].set(c_ctx)
    mod4 = _ada(c8, ada_w[li], ada_b[li]).reshape(8, N_MOD, 1, d)

    cuts = [0, s5w, s5w + kw, s5w + 2 * kw, s5w + 2 * kw + vw, s5w + 2 * kw + 2 * vw]
    c_gkd = cuts[5]
    w = w_in[li]
    w_main = jnp.concatenate([w[:, :c_gkd].astype(BF16), w[:, c_gkd + rank2:].astype(BF16)], axis=1)
    w_ctx = jnp.concatenate([w[:, cuts[0]:cuts[1]].astype(BF16), w[:, cuts[2]:cuts[4]].astype(BF16)], axis=1)
    w_gkd = jnp.zeros((d, 128), F32).at[:, :rank2].set(w[:, c_gkd:c_gkd + rank2]).astype(BF16)
    nw1 = norm1_w[li].reshape(1, d)
    p_lat, gk_lat, v_lat = _inproj(x, mod4, lambda b: b, nw1, w_main, w_gkd, tm=1024, tn=1024, s5w=s5w)
    p_ctx, gk_ctx, v_ctx = _inproj(ctx, mod4, lambda b: b_, nw1, w_ctx, w_gkd, tm=lc, tn=w_ctx.shape[1] // 2,
                                   s5w=s5w)

    factors = _s5_factors(s5_lam_re[li], s5_lam_im[li], s5_log_dt[li], s5_b_re[li], s5_b_im[li],
                          s5_c_re[li], s5_c_im[li], s5_d[li])
    ys = _s5(v_ctx, v_lat, *factors)

    up_pad = jnp.zeros((2, 128, kw), F32)
    up_pad = up_pad.at[0, :GLA_GATE_RANK].set(gla_gk_up[li, 0]).at[1, GLA_GATE_RANK:rank2].set(gla_gk_up[li, 1])
    yg = _gla(p_lat, p_ctx, gk_lat, gk_ctx, up_pad.astype(BF16), gla_gk_b[li].reshape(2, 1, kw), dk, dv)

    rw = jnp.zeros((d, 128), F32).at[:, :N_EXPERTS].set(router_w[li])
    rw_hi = rw.astype(BF16)
    rw2 = jnp.stack([rw_hi, (rw - rw_hi.astype(F32)).astype(BF16)])
    x1, h2p, logits = _merge(
        ys, yg, p_lat, x, mod4, s5_glu_w[li].astype(BF16), w_s5_proj[li].astype(BF16),
        w_gla_proj[li].astype(BF16), w_out[li].astype(BF16), gla_norm_w[li].reshape(1, dv),
        norm2_w[li].reshape(1, d), rw2, dv, tm=256)

    t = b_ * l_
    logits_t = logits.reshape(t, 128)[:, :N_EXPERTS].T
    eidx, wts, rank, cnt = _route(logits_t, router_bias[li])
    nb = (t * TOP_K + N_EXPERTS * (EXPERT_ROWS - 1) + EXPERT_ROWS - 1) // EXPERT_ROWS
    dest, block_e, first, nxt, n_used, pad = _slots(cnt[:, 0].astype(I32), eidx, rank, nb)

    tiles = d // 256
    x_sorted = _dispatch(pad, n_used, dest, h2p.reshape(t * tiles, 128), nb * EXPERT_ROWS, tm=128)
    y_sorted = _experts(block_e, first, nxt, n_used, x_sorted.reshape(nb * EXPERT_ROWS * tiles, 128),
                         exp_w_gate[li], exp_w_up[li], exp_w_down[li])

    tm_c = 128
    out = _combine(dest, y_sorted.reshape(-1, tiles, 128), wts.T, x1.reshape(t, d), h2p.reshape(t * tiles, 128), mod4,
                   sh_w_gate[li].astype(BF16), sh_w_up[li].astype(BF16), sh_w_down[li].astype(BF16),
                   final_norm_w.reshape(1, d), l_ // tm_c, tm_c)
    return out.reshape(b_, l_, d)
```
